```python
import math
import jax
import jax.numpy as jnp
from jax import lax
import numpy as np

D_MODEL = 2048
BATCH = 4
SEQ = 2048
DEPTH = 1
DEC_BATCH = 128
DEC_SEQ = 4
PAST_LEN = 16384
PAGE_SIZE = 128

MIX_WIDTH = D_MODEL
W_A = MIX_WIDTH // 2
W_B = MIX_WIDTH - W_A
HGRN_EXPAND = 128
H_A = W_A // HGRN_EXPAND
DK = HGRN_EXPAND
DV = W_A // H_A
CHUNK = 64
S5_GROUP = 16
G_B = W_B // S5_GROUP
P_STATE = 64
D_FF = 4 * D_MODEL
IN_COLS = 4 * W_A + W_B
EPS = 1e-6

kernel_name = "hybrid_hgrn2_s5_decode_step"


def rmsnorm(x, g):
    xf = x.astype(jnp.float32)
    var = jnp.mean(xf * xf, axis=-1, keepdims=True)
    return xf * lax.rsqrt(var + EPS) * g.astype(jnp.float32)


def hgrn2_recurrence(q, k, v, logf, s0):
    b, t = q.shape[0], q.shape[1]
    c = min(CHUNK, t)
    n = -(-t // c)
    pad = n * c - t
    if pad:
        padw = ((0, 0), (0, pad), (0, 0), (0, 0))
        q, k, v, logf = [jnp.pad(a, padw) for a in (q, k, v, logf)]

    def to_chunks(a):
        return jnp.moveaxis(a.reshape(b, n, c, a.shape[2], a.shape[3]), 1, 0)

    qc, kc, vc, gc = map(to_chunks, (q, k, v, logf))
    causal = jnp.tril(jnp.ones((c, c), dtype=bool))[None, :, :, None, None]

    def step(s, inp):
        qi, ki, vi, gi = inp
        bcum = jnp.cumsum(gi, axis=1)
        o_inter = jnp.einsum('bthk,bhkv->bthv', qi * jnp.exp(bcum), s)
        diff = bcum[:, :, None] - bcum[:, None, :]
        decay = jnp.exp(jnp.where(causal, diff, -jnp.inf))
        scores = jnp.einsum('bthk,bshk,btshk->bhts', qi, ki, decay)
        o_intra = jnp.einsum('bhts,bshv->bthv', scores, vi)
        b_last = bcum[:, -1]
        k_dec = ki * jnp.exp(b_last[:, None] - bcum)
        s_new = jnp.exp(b_last)[..., None] * s + jnp.einsum('bshk,bshv->bhkv', k_dec, vi)
        return s_new, o_inter + o_intra

    s_final, oc = lax.scan(step, s0, (qc, kc, vc, gc))
    o = jnp.moveaxis(oc, 0, 1).reshape(b, n * c, H_A, DV)[:, :t]
    return o, s_final


def hgrn2_mixer(q_pre, f_pre, i_pre, g_pre, lb, norm_g, s0):
    bsz, t, _ = q_pre.shape
    q = jax.nn.silu(q_pre)
    f = lb + (1.0 - lb) * jax.nn.sigmoid(f_pre)
    k = 1.0 - f
    logf = jnp.log(f)
    o, s_new = hgrn2_recurrence(q.reshape(bsz, t, H_A, DK), k.reshape(bsz, t, H_A, DK),
                                i_pre.reshape(bsz, t, H_A, DV), logf.reshape(bsz, t, H_A, DK), s0)
    o = rmsnorm(o.reshape(bsz, t, W_A), norm_g) * jax.nn.silu(g_pre)
    return o, s_new


def s5_mixer(u, a_re, a_im, b_re, b_im, c_re, c_im, d, log_step, glu_w, glu_b, x0_re, x0_im):
    bsz, t, _ = u.shape
    f32 = jnp.float32
    a_re = a_re.astype(f32)
    a_im = a_im.astype(f32)
    dt = jnp.exp(log_step.astype(f32))[:, None]
    mag = jnp.exp(a_re * dt)
    lam_re, lam_im = mag * jnp.cos(a_im * dt), mag * jnp.sin(a_im * dt)
    den = a_re * a_re + a_im * a_im
    nr, ni = lam_re - 1.0, lam_im
    r_re = (nr * a_re + ni * a_im) / den
    r_im = (ni * a_re - nr * a_im) / den
    b_re = b_re.astype(f32)
    b_im = b_im.astype(f32)
    bb_re = r_re[..., None] * b_re - r_im[..., None] * b_im
    bb_im = r_re[..., None] * b_im + r_im[..., None] * b_re
    ug = u.reshape(bsz, t, G_B, S5_GROUP)
    bu_re = jnp.einsum('btgc,gpc->btgp', ug, bb_re)
    bu_im = jnp.einsum('btgc,gpc->btgp', ug, bb_im)
    bu_re = bu_re.at[:, 0].add(lam_re * x0_re - lam_im * x0_im)
    bu_im = bu_im.at[:, 0].add(lam_re * x0_im + lam_im * x0_re)
    a_el_re = jnp.broadcast_to(lam_re, bu_re.shape)
    a_el_im = jnp.broadcast_to(lam_im, bu_im.shape)

    def combine(e1, e2):
        a1r, a1i, b1r, b1i = e1
        a2r, a2i, b2r, b2i = e2
        return (a2r * a1r - a2i * a1i, a2r * a1i + a2i * a1r,
                a2r * b1r - a2i * b1i + b2r, a2r * b1i + a2i * b1r + b2i)

    _, _, xr, xi = lax.associative_scan(combine, (a_el_re, a_el_im, bu_re, bu_im), axis=1)
    y = (jnp.einsum('btgp,gcp->btgc', xr, c_re.astype(f32))
         - jnp.einsum('btgp,gcp->btgc', xi, c_im.astype(f32)))
    y = y.reshape(bsz, t, W_B) + d.astype(f32) * u
    y = jax.nn.gelu(y)
    y = y * jax.nn.sigmoid(y @ glu_w.astype(f32) + glu_b.astype(f32))
    return y, xr[:, -1], xi[:, -1]


def trunk_layer(x, s_hgrn0, x0_re, x0_im, lb, w_in, w_out, norm1_g, norm2_g, hgrn_norm_g,
                s5_a_re, s5_a_im, s5_b_re, s5_b_im, s5_c_re, s5_c_im, s5_d, s5_log_step,
                glu_w, glu_b, mlp_up, mlp_down):
    h = rmsnorm(x, norm1_g)
    proj = h @ w_in.astype(jnp.float32)
    q_pre, f_pre, i_pre, g_pre, u = jnp.split(proj, [W_A, 2 * W_A, 3 * W_A, 4 * W_A], axis=-1)
    o_a, s_a = hgrn2_mixer(q_pre, f_pre, i_pre, g_pre, lb, hgrn_norm_g, s_hgrn0.astype(jnp.float32))
    o_b, xr, xi = s5_mixer(u, s5_a_re, s5_a_im, s5_b_re, s5_b_im, s5_c_re, s5_c_im, s5_d,
                           s5_log_step, glu_w, glu_b,
                           x0_re.astype(jnp.float32), x0_im.astype(jnp.float32))
    mix = jnp.concatenate([o_a, o_b], axis=-1) @ w_out.astype(jnp.float32)
    x = x + mix.astype(x.dtype)
    h2 = rmsnorm(x, norm2_g)
    ff = jnp.square(jax.nn.relu(h2 @ mlp_up.astype(jnp.float32))) @ mlp_down.astype(jnp.float32)
    x = x + ff.astype(x.dtype)
    return x, s_a, xr, xi


def setup_inputs(seed: int = 0) -> dict:
    key = jax.random.key(seed)
    ks = jax.random.split(key, 32)
    f32 = jnp.float32
    nrm = lambda k, shape, s: jax.random.normal(k, shape, f32) * s
    n_idx = jnp.arange(P_STATE, dtype=f32)
    return {
        "x_prompt": nrm(ks[0], (BATCH, SEQ, D_MODEL), 1.0),
        "x_sample": nrm(ks[1], (DEC_BATCH, DEC_SEQ, D_MODEL), 1.0),
        "state_hgrn": nrm(ks[2], (DEPTH, DEC_BATCH, H_A, DK, DV), 0.5),
        "state_s5_re": nrm(ks[3], (DEPTH, DEC_BATCH, G_B, P_STATE), 0.1),
        "state_s5_im": nrm(ks[4], (DEPTH, DEC_BATCH, G_B, P_STATE), 0.1),
        "w_in": nrm(ks[5], (DEPTH, D_MODEL, IN_COLS), D_MODEL ** -0.5),
        "w_out": nrm(ks[6], (DEPTH, MIX_WIDTH, D_MODEL), MIX_WIDTH ** -0.5),
        "norm1_g": 1.0 + nrm(ks[7], (DEPTH, D_MODEL), 0.01),
        "norm2_g": 1.0 + nrm(ks[8], (DEPTH, D_MODEL), 0.01),
        "hgrn_lb_logits": nrm(ks[9], (DEPTH + 1, W_A), 0.1),
        "hgrn_norm_g": 1.0 + nrm(ks[10], (DEPTH, W_A), 0.01),
        "s5_a_re": -0.5 + nrm(ks[11], (DEPTH, G_B, P_STATE), 0.01),
        "s5_a_im": math.pi * n_idx + nrm(ks[12], (DEPTH, G_B, P_STATE), 0.01),
        "s5_b_re": nrm(ks[13], (DEPTH, G_B, P_STATE, S5_GROUP), (2 * S5_GROUP) ** -0.5),
        "s5_b_im": nrm(ks[14], (DEPTH, G_B, P_STATE, S5_GROUP), (2 * S5_GROUP) ** -0.5),
        "s5_c_re": nrm(ks[15], (DEPTH, G_B, S5_GROUP, P_STATE), P_STATE ** -0.5),
        "s5_c_im": nrm(ks[16], (DEPTH, G_B, S5_GROUP, P_STATE), P_STATE ** -0.5),
        "s5_d": nrm(ks[17], (DEPTH, W_B), 1.0),
        "s5_log_step": jax.random.uniform(ks[18], (DEPTH, G_B), f32, math.log(0.001), math.log(0.1)),
        "glu_w": nrm(ks[19], (DEPTH, W_B, W_B), W_B ** -0.5),
        "glu_b": nrm(ks[20], (DEPTH, W_B), 0.01),
        "mlp_up": nrm(ks[21], (DEPTH, D_MODEL, D_FF), D_MODEL ** -0.5),
        "mlp_down": nrm(ks[22], (DEPTH, D_FF, D_MODEL), D_FF ** -0.5),
        "final_norm_g": 1.0 + nrm(ks[23], (D_MODEL,), 0.01),
    }


def reference(x_prompt, x_sample, state_hgrn, state_s5_re, state_s5_im, w_in, w_out, norm1_g, norm2_g,
              hgrn_lb_logits, hgrn_norm_g, s5_a_re, s5_a_im, s5_b_re, s5_b_im, s5_c_re, s5_c_im, s5_d,
              s5_log_step, glu_w, glu_b, mlp_up, mlp_down, final_norm_g):
    f32 = jnp.float32
    lb_all = jnp.cumsum(jax.nn.softmax(hgrn_lb_logits.astype(f32), axis=0), axis=0)
    bp = x_prompt.shape[0]
    xp, xs = x_prompt, x_sample
    hp_list, rp_list, ip_list, hs_list, rs_list, is_list = [], [], [], [], [], []
    for l in range(DEPTH):
        shared = (lb_all[l], w_in[l], w_out[l], norm1_g[l], norm2_g[l], hgrn_norm_g[l],
                  s5_a_re[l], s5_a_im[l], s5_b_re[l], s5_b_im[l], s5_c_re[l], s5_c_im[l], s5_d[l],
                  s5_log_step[l], glu_w[l], glu_b[l], mlp_up[l], mlp_down[l])
        xp, sh_p, sr_p, si_p = trunk_layer(
            xp, jnp.zeros((bp, H_A, DK, DV), f32), jnp.zeros((bp, G_B, P_STATE), f32),
            jnp.zeros((bp, G_B, P_STATE), f32), *shared)
        xs, sh_s, sr_s, si_s = trunk_layer(xs, state_hgrn[l], state_s5_re[l], state_s5_im[l], *shared)
        hp_list.append(sh_p); rp_list.append(sr_p); ip_list.append(si_p)
        hs_list.append(sh_s); rs_list.append(sr_s); is_list.append(si_s)
    y_prompt = rmsnorm(xp, final_norm_g).astype(x_prompt.dtype)
    y_sample = rmsnorm(xs, final_norm_g).astype(x_sample.dtype)
    new_hgrn_prompt = jnp.stack(hp_list)
    new_s5_re_prompt = jnp.stack(rp_list)
    new_s5_im_prompt = jnp.stack(ip_list)
    new_hgrn_sample = jnp.stack(hs_list)
    new_s5_re_sample = jnp.stack(rs_list)
    new_s5_im_sample = jnp.stack(is_list)
    return (y_prompt, y_sample, new_hgrn_prompt, new_s5_re_prompt, new_s5_im_prompt,
            new_hgrn_sample, new_s5_re_sample, new_s5_im_sample)
```

```python
import functools

import jax
import jax.numpy as jnp
from jax import lax
from jax.experimental import pallas as pl
from jax.experimental.pallas import tpu as pltpu

F32 = jnp.float32
BF16 = jnp.bfloat16
EPS = 1e-6

LANES = 128
SUBLANES = 8
MXU_DIM = 256
VMEM_LIMIT = 56 * 1024 * 1024

HEAD_DIM = 128
S5_GROUP = 16
HGRN_CHUNK = 64
HGRN_SUB = 16
ROW_TILE = 512
NEG_BIG = -1e30


def _dot(a, b):
    return jnp.dot(a, b, preferred_element_type=F32)


def _dot_nt(a, b):
    return lax.dot_general(a, b, (((1,), (1,)), ((), ())), preferred_element_type=F32)


def _dot_tn(a, b):
    return lax.dot_general(a, b, (((0,), (0,)), ((), ())), preferred_element_type=F32)


def _split3(x):
    hi = x.astype(BF16)
    r1 = x - hi.astype(F32)
    mid = r1.astype(BF16)
    lo = (r1 - mid.astype(F32)).astype(BF16)
    return hi, mid, lo


def _params(*sem):
    return pltpu.CompilerParams(dimension_semantics=sem, vmem_limit_bytes=VMEM_LIMIT)


def _rmsnorm_rows(src_ref, g, dst_ref, rows, chunk=64):
    def body(c, carry):
        r0 = pl.multiple_of(c * chunk, chunk)
        x = src_ref[pl.ds(r0, chunk), :]
        var = jnp.mean(x * x, axis=-1, keepdims=True)
        dst_ref[pl.ds(r0, chunk), :] = (x * lax.rsqrt(var + EPS) * g).astype(dst_ref.dtype)
        return carry

    lax.fori_loop(0, rows // chunk, body, 0)


def _inproj_kernel(xp_ref, xs_ref, g_ref, w_ref, o_ref, h_ref, *, n_prompt_tiles, rows):
    i = pl.program_id(0)
    j = pl.program_id(1)

    @pl.when(jnp.logical_and(j == 0, i < n_prompt_tiles))
    def _():
        _rmsnorm_rows(xp_ref, g_ref[...], h_ref, rows)

    @pl.when(jnp.logical_and(j == 0, i >= n_prompt_tiles))
    def _():
        _rmsnorm_rows(xs_ref, g_ref[...], h_ref, rows)

    o_ref[...] = _dot(h_ref[...], w_ref[...])


def _inproj(xp, xs, g, w, *, tn=1024):
    mp, d = xp.shape
    ms = xs.shape[0]
    n = w.shape[1]
    tm = ROW_TILE
    assert mp % tm == 0 and ms % tm == 0 and n % tn == 0
    npt = mp // tm
    nt = npt + ms // tm
    return pl.pallas_call(
        functools.partial(_inproj_kernel, n_prompt_tiles=npt, rows=tm),
        grid=(nt, n // tn),
        in_specs=[
            pl.BlockSpec((tm, d), lambda i, j: (jnp.minimum(i, npt - 1), 0)),
            pl.BlockSpec((tm, d), lambda i, j: (jnp.maximum(i - npt, 0), 0)),
            pl.BlockSpec((1, d), lambda i, j: (0, 0)),
            pl.BlockSpec((d, tn), lambda i, j: (0, j)),
        ],
        out_specs=pl.BlockSpec((tm, tn), lambda i, j: (i, j)),
        out_shape=jax.ShapeDtypeStruct((mp + ms, n), F32),
        scratch_shapes=[pltpu.VMEM((tm, d), BF16)],
        compiler_params=_params("arbitrary", "arbitrary"),
        name="inproj",
    )(xp, xs, g.reshape(1, d), w)


def _lower_bound(logits, layer):
    m = jnp.max(logits, axis=0, keepdims=True)
    e = jnp.exp(logits - m)
    return jnp.sum(e[: layer + 1], axis=0, keepdims=True) / jnp.sum(e, axis=0, keepdims=True)


def _row_to_col(e_row):
    n = e_row.shape[1]
    hi, mid, lo = (p.astype(F32) for p in _split3(e_row))
    r = lax.broadcasted_iota(jnp.int32, (2 * SUBLANES, n), 0)
    pieces = jnp.where(r == 0, hi, jnp.where(r == 1, mid, jnp.where(r == 2, lo, 0.0)))
    return _dot_tn(pieces.astype(BF16), jnp.ones((2 * SUBLANES, LANES), BF16))


def _hgrn_gates(qp, fp, lb):
    q = qp * jax.nn.sigmoid(qp)
    f = lb + (1.0 - lb) * jax.nn.sigmoid(fp)
    return q, f, 1.0 - f, jnp.log(f)


def _hgrn_chunk(qp, fp, v, lb, s, tri):
    c = qp.shape[0]
    q, _, k, logf = _hgrn_gates(qp, fp, lb)
    hi, mid, lo = _split3(logf)
    b = _dot(tri, hi) + _dot(tri, mid) + _dot(tri, lo)
    b_last = b[c - 1 : c, :]
    v16 = v.astype(BF16)

    o = _dot((q * jnp.exp(b)).astype(BF16), s.astype(BF16))

    lane = lax.broadcasted_iota(jnp.int32, (HGRN_SUB, c), 1)
    row = lax.broadcasted_iota(jnp.int32, (c, 1), 0)
    a_rows = []
    for i in range(c // HGRN_SUB):
        r0 = i * HGRN_SUB
        if i == 0:
            a_i = jnp.zeros((HGRN_SUB, c), F32)
        else:
            bref = b[r0 - 1 : r0, :]
            qs = (q[r0 : r0 + HGRN_SUB] * jnp.exp(b[r0 : r0 + HGRN_SUB] - bref)).astype(BF16)
            ks = (k[:r0] * jnp.exp(bref - b[:r0])).astype(BF16)
            ks = jnp.concatenate([ks, jnp.zeros((c - r0, HEAD_DIM), BF16)], axis=0)
            a_i = _dot_nt(qs, ks)
        for sl in range(HGRN_SUB):
            g = r0 + sl
            lo_row = r0 + (sl // SUBLANES) * SUBLANES
            arg = jnp.where(row[lo_row : r0 + HGRN_SUB] >= g, b[lo_row : r0 + HGRN_SUB] - b[g : g + 1], NEG_BIG)
            z = q[lo_row : r0 + HGRN_SUB] * jnp.exp(arg) * k[g : g + 1]
            col = jnp.sum(z, axis=-1, keepdims=True)
            if lo_row != r0:
                col = jnp.concatenate([jnp.zeros((lo_row - r0, 1), F32), col], axis=0)
            a_i = jnp.where(lane == g, col, a_i)
        a_rows.append(a_i)
    a = jnp.concatenate(a_rows, axis=0)
    o = o + _dot(a.astype(BF16), v16)

    kd = (k * jnp.exp(b_last - b)).astype(BF16)
    s_new = _row_to_col(jnp.exp(b_last)) * s + _dot_tn(kd, v16)
    return o, s_new


def _tri(c):
    r = lax.broadcasted_iota(jnp.int32, (c, c), 0)
    cc = lax.broadcasted_iota(jnp.int32, (c, c), 1)
    return jnp.where(cc <= r, 1.0, 0.0).astype(BF16)


def _hgrn_prompt_kernel(q_ref, f_ref, v_ref, lbl_ref, o_ref, s_out_ref, s_ref, *, layer, heads, n_chunks):
    t = pl.program_id(1)

    @pl.when(t == 0)
    def _():
        s_ref[...] = jnp.zeros_like(s_ref)

    lb = _lower_bound(lbl_ref[...], layer)
    tri = _tri(HGRN_CHUNK)

    def body(c, carry):
        r0 = pl.multiple_of(c * HGRN_CHUNK, HGRN_CHUNK)
        rows = pl.ds(r0, HGRN_CHUNK)
        for h in range(heads):
            cols = slice(h * HEAD_DIM, (h + 1) * HEAD_DIM)
            o, s_new = _hgrn_chunk(q_ref[rows, cols], f_ref[rows, cols], v_ref[rows, cols], lb[:, cols], s_ref[h], tri)
            s_ref[h] = s_new
            o_ref[rows, cols] = o
        return carry

    lax.fori_loop(0, n_chunks, body, 0)

    @pl.when(t == pl.num_programs(1) - 1)
    def _():
        s_out_ref[0] = s_ref[...]


def _hgrn_prompt(proj, lb_logits, *, layer, batch, seq, w_a, tt=256):
    heads = w_a // HEAD_DIM
    assert seq % tt == 0 and tt % HGRN_CHUNK == 0
    nt = seq // tt
    col = lambda c: pl.BlockSpec((tt, w_a), lambda b, t: (b * nt + t, c))
    return pl.pallas_call(
        functools.partial(_hgrn_prompt_kernel, layer=layer, heads=heads, n_chunks=tt // HGRN_CHUNK),
        grid=(batch, nt),
        in_specs=[col(0), col(1), col(2), pl.BlockSpec(lb_logits.shape, lambda b, t: (0, 0))],
        out_specs=[
            pl.BlockSpec((tt, w_a), lambda b, t: (b * nt + t, 0)),
            pl.BlockSpec((1, heads, HEAD_DIM, HEAD_DIM), lambda b, t: (b, 0, 0, 0)),
        ],
        out_shape=[
            jax.ShapeDtypeStruct((batch * seq, w_a), F32),
            jax.ShapeDtypeStruct((batch, heads, HEAD_DIM, HEAD_DIM), F32),
        ],
        scratch_shapes=[pltpu.VMEM((heads, HEAD_DIM, HEAD_DIM), F32)],
        compiler_params=_params("arbitrary", "arbitrary"),
        name="hgrn_prompt",
    )(proj, proj, proj, lb_logits)


def _hgrn_sample_kernel(q_ref, f_ref, v_ref, lbl_ref, s_in_ref, o_ref, s_out_ref, *, layer, heads, seq):
    n_seq = SUBLANES // seq
    lb = _lower_bound(lbl_ref[...], layer)
    row = lax.broadcasted_iota(jnp.int32, (SUBLANES, 1), 0)
    pos = row % seq
    for h in range(heads):
        cols = slice(h * HEAD_DIM, (h + 1) * HEAD_DIM)
        q, _, k, logf = _hgrn_gates(q_ref[:, cols], f_ref[:, cols], lb[:, cols])
        v = v_ref[:, cols]
        b = logf
        shift = 1
        while shift < seq:
            b = b + jnp.where(pos >= shift, pltpu.roll(b, shift, 0), 0.0)
            shift *= 2
        o = jnp.zeros((SUBLANES, HEAD_DIM), F32)
        for sl in range(SUBLANES):
            lo_t, hi_t = sl, (sl // seq + 1) * seq
            mask = jnp.logical_and(row >= lo_t, row < hi_t)
            z = q * jnp.exp(jnp.where(mask, b - b[sl : sl + 1], NEG_BIG)) * k[sl : sl + 1]
            o = o + jnp.sum(z, axis=-1, keepdims=True) * v[sl : sl + 1]
        v16 = v.astype(BF16)
        for n in range(n_seq):
            own = jnp.logical_and(row >= n * seq, row < (n + 1) * seq)
            s = s_in_ref[0, n, h]
            b_last = b[(n + 1) * seq - 1 : (n + 1) * seq, :]
            qe = jnp.where(own, q * jnp.exp(b), 0.0).astype(BF16)
            o = o + _dot(qe, s.astype(BF16))
            kd = jnp.where(own, k * jnp.exp(jnp.where(own, b_last - b, 0.0)), 0.0).astype(BF16)
            s_out_ref[0, n, h] = _row_to_col(jnp.exp(b_last)) * s + _dot_tn(kd, v16)
        o_ref[:, cols] = o


def _hgrn_sample(proj, lb_logits, state, *, layer, row0, batch, seq, w_a):
    heads = w_a // HEAD_DIM
    assert SUBLANES % seq == 0 and row0 % SUBLANES == 0 and (batch * seq) % SUBLANES == 0
    n_seq = SUBLANES // seq
    t0 = row0 // SUBLANES
    col = lambda c: pl.BlockSpec((SUBLANES, w_a), lambda p: (t0 + p, c))
    st = pl.BlockSpec((1, n_seq, heads, HEAD_DIM, HEAD_DIM), lambda p: (layer, p, 0, 0, 0))
    st_out = pl.BlockSpec((1, n_seq, heads, HEAD_DIM, HEAD_DIM), lambda p: (0, p, 0, 0, 0))
    return pl.pallas_call(
        functools.partial(_hgrn_sample_kernel, layer=layer, heads=heads, seq=seq),
        grid=(batch // n_seq,),
        in_specs=[col(0), col(1), col(2), pl.BlockSpec(lb_logits.shape, lambda p: (0, 0)), st],
        out_specs=[pl.BlockSpec((SUBLANES, w_a), lambda p: (p, 0)), st_out],
        out_shape=[
            jax.ShapeDtypeStruct((batch * seq, w_a), F32),
            jax.ShapeDtypeStruct((1, batch, heads, HEAD_DIM, HEAD_DIM), F32),
        ],
        compiler_params=_params("arbitrary"),
        name="hgrn_sample",
    )(proj, proj, proj, lb_logits, state)


def _s5_disc_kernel(are_ref, aim_ref, ls_ref, bre_ref, bim_ref, lre_ref, lim_ref, bbre_ref, bbim_ref):
    a_re = are_ref[...]
    a_im = aim_ref[...]
    dt = jnp.exp(ls_ref[...])
    mag = jnp.exp(a_re * dt)
    lam_re = mag * jnp.cos(a_im * dt)
    lam_im = mag * jnp.sin(a_im * dt)
    den = a_re * a_re + a_im * a_im
    nr, ni = lam_re - 1.0, lam_im
    r_re = (nr * a_re + ni * a_im) / den
    r_im = (ni * a_re - nr * a_im) / den
    lre_ref[...] = lam_re
    lim_ref[...] = lam_im
    for c in range(bre_ref.shape[0]):
        b_re = bre_ref[c]
        b_im = bim_ref[c]
        bbre_ref[c] = r_re * b_re - r_im * b_im
        bbim_ref[c] = r_re * b_im + r_im * b_re


def _s5_discretise(a_re, a_im, log_step, b_re, b_im):
    g, p = a_re.shape
    c = b_re.shape[-1]
    bt = lambda x: jnp.transpose(x, (2, 0, 1))
    return pl.pallas_call(
        _s5_disc_kernel,
        out_shape=[
            jax.ShapeDtypeStruct((g, p), F32),
            jax.ShapeDtypeStruct((g, p), F32),
            jax.ShapeDtypeStruct((c, g, p), F32),
            jax.ShapeDtypeStruct((c, g, p), F32),
        ],
        name="s5_discretise",
    )(a_re, a_im, log_step.reshape(g, 1), bt(b_re), bt(b_im))


def _block_diag_in(bb):
    c, g, p = bb.shape
    gt = MXU_DIM // c
    x = bb.reshape(c, g // gt, gt, p)
    eye = jnp.eye(gt, dtype=bb.dtype)
    return jnp.einsum("cjgp,gh->jgchp", x, eye).reshape(g // gt, gt * c, gt * p)


def _block_diag_out(cc):
    g, c, p = cc.shape
    gt = MXU_DIM // c
    x = cc.reshape(g // gt, gt, c, p)
    eye = jnp.eye(gt, dtype=cc.dtype)
    return jnp.einsum("jgcp,gh->jgphc", x, eye).reshape(g // gt, gt * p, gt * c)


def _s5_in(ub, bre_ref, bim_ref, xr_ref, xi_ref):
    nj, kc, kp = bre_ref.shape
    spj = kp // LANES
    for j in range(nj):
        uj = ub[:, j * kc : (j + 1) * kc]
        for x_ref, w_ref in ((xr_ref, bre_ref), (xi_ref, bim_ref)):
            bu = _dot(uj, w_ref[j])
            for m in range(spj):
                x_ref[j * spj + m] = bu[:, m * LANES : (m + 1) * LANES]


def _s5_out(u, xr_ref, xi_ref, cre_ref, cim_ref, d):
    nj, kp, kc = cre_ref.shape
    spj = kp // LANES
    tiles = []
    for j in range(nj):
        xr = jnp.concatenate([xr_ref[j * spj + m] for m in range(spj)], axis=1).astype(BF16)
        xi = jnp.concatenate([xi_ref[j * spj + m] for m in range(spj)], axis=1).astype(BF16)
        y = _dot(xr, cre_ref[j]) - _dot(xi, cim_ref[j])
        tiles.append(y + d[:, j * kc : (j + 1) * kc] * u[:, j * kc : (j + 1) * kc])
    return tiles


def _s5_step(lr, li, xr, xi, bur, bui):
    return lr * xr - li * xi + bur, lr * xi + li * xr + bui


def _s5_prompt_kernel(*refs, nb, slabs_per_pass):
    u_refs = refs[:nb]
    (lre_ref, lim_ref, bre_ref, bim_ref, cre_ref, cim_ref, d_ref,
     y_ref, xr_out_ref, xi_out_ref, xr_ref, xi_ref, sr_ref, si_ref) = refs[nb:]
    ts = u_refs[0].shape[0]
    n_slabs = xr_ref.shape[0]
    step = pl.program_id(0)

    @pl.when(step == 0)
    def _():
        sr_ref[...] = jnp.zeros_like(sr_ref)
        si_ref[...] = jnp.zeros_like(si_ref)

    u = jnp.concatenate([r[...] for r in u_refs], axis=0)
    _s5_in(u.astype(BF16), bre_ref, bim_ref, xr_ref, xi_ref)

    for first in range(0, n_slabs, slabs_per_pass):
        slabs = range(first, first + slabs_per_pass)
        lam = [(lre_ref[m], lim_ref[m]) for m in slabs]

        def body(r, carry):
            rows = pl.ds(r, nb, stride=ts)
            out = []
            for (lr, li), (xr, xi), m in zip(lam, carry, slabs):
                nxr, nxi = _s5_step(lr, li, xr, xi, xr_ref[m, rows, :], xi_ref[m, rows, :])
                xr_ref[m, rows, :] = nxr
                xi_ref[m, rows, :] = nxi
                out.append((nxr, nxi))
            return tuple(out)

        final = lax.fori_loop(0, ts, body, tuple((sr_ref[m], si_ref[m]) for m in slabs), unroll=4)
        for (xr, xi), m in zip(final, slabs):
            sr_ref[m] = xr
            si_ref[m] = xi

    kc = cre_ref.shape[2]
    for j, y in enumerate(_s5_out(u, xr_ref, xi_ref, cre_ref, cim_ref, d_ref[...])):
        y_ref[:, :, j * kc : (j + 1) * kc] = y.reshape(nb, ts, kc)

    @pl.when(step == pl.num_programs(0) - 1)
    def _():
        for m in range(n_slabs):
            xr_out_ref[:, m * LANES : (m + 1) * LANES] = sr_ref[m]
            xi_out_ref[:, m * LANES : (m + 1) * LANES] = si_ref[m]


def _s5_weights_specs(ws):
    zero = lambda n: (lambda *_: (0,) * n)
    return [pl.BlockSpec(w.shape, zero(w.ndim)) for w in ws]


def _s5_prompt(proj, ws, *, nb, seq, ucol, w_b, ts=128, slabs_per_pass=8):
    n_slabs = ws[0].shape[0]
    gp = n_slabs * LANES
    assert seq % ts == 0 and n_slabs % slabs_per_pass == 0
    nblk = seq // ts
    u_spec = lambda b: pl.BlockSpec((ts, w_b), lambda s: (b * nblk + s, ucol))
    return pl.pallas_call(
        functools.partial(_s5_prompt_kernel, nb=nb, slabs_per_pass=slabs_per_pass),
        grid=(nblk,),
        in_specs=[u_spec(b) for b in range(nb)] + _s5_weights_specs(ws),
        out_specs=[
            pl.BlockSpec((nb, ts, w_b), lambda s: (0, s, 0)),
            pl.BlockSpec((nb, gp), lambda s: (0, 0)),
            pl.BlockSpec((nb, gp), lambda s: (0, 0)),
        ],
        out_shape=[
            jax.ShapeDtypeStruct((nb, seq, w_b), F32),
            jax.ShapeDtypeStruct((nb, gp), F32),
            jax.ShapeDtypeStruct((nb, gp), F32),
        ],
        scratch_shapes=[
            pltpu.VMEM((n_slabs, nb * ts, LANES), F32),
            pltpu.VMEM((n_slabs, nb * ts, LANES), F32),
            pltpu.VMEM((n_slabs, nb, LANES), F32),
            pltpu.VMEM((n_slabs, nb, LANES), F32),
        ],
        compiler_params=_params("arbitrary"),
        name="s5_prompt",
    )(*([proj] * nb), *ws)


def _s5_sample_kernel(u_ref, x0r_ref, x0i_ref, lre_ref, lim_ref, bre_ref, bim_ref, cre_ref, cim_ref, d_ref,
                      y_ref, xr_out_ref, xi_out_ref, xr_ref, xi_ref, *, seq):
    nb = x0r_ref.shape[0]
    u = u_ref[...]
    _s5_in(u.astype(BF16), bre_ref, bim_ref, xr_ref, xi_ref)
    for m in range(xr_ref.shape[0]):
        ls = slice(m * LANES, (m + 1) * LANES)
        lr, li = lre_ref[m], lim_ref[m]
        xr, xi = x0r_ref[:, ls], x0i_ref[:, ls]
        for r in range(seq):
            rows = pl.ds(r, nb, stride=seq)
            xr, xi = _s5_step(lr, li, xr, xi, xr_ref[m, rows, :], xi_ref[m, rows, :])
            xr_ref[m, rows, :] = xr
            xi_ref[m, rows, :] = xi
        xr_out_ref[:, ls] = xr
        xi_out_ref[:, ls] = xi
    kc = cre_ref.shape[2]
    for j, y in enumerate(_s5_out(u, xr_ref, xi_ref, cre_ref, cim_ref, d_ref[...])):
        y_ref[:, j * kc : (j + 1) * kc] = y


def _s5_sample(proj, x0_re, x0_im, ws, *, row0, ucol, seq, w_b, nb=64):
    batch, gp = x0_re.shape
    n_slabs = gp // LANES
    assert batch % nb == 0 and row0 % (nb * seq) == 0
    t0 = row0 // (nb * seq)
    return pl.pallas_call(
        functools.partial(_s5_sample_kernel, seq=seq),
        grid=(batch // nb,),
        in_specs=[
            pl.BlockSpec((nb * seq, w_b), lambda p: (t0 + p, ucol)),
            pl.BlockSpec((nb, gp), lambda p: (p, 0)),
            pl.BlockSpec((nb, gp), lambda p: (p, 0)),
        ] + _s5_weights_specs(ws),
        out_specs=[
            pl.BlockSpec((nb * seq, w_b), lambda p: (p, 0)),
            pl.BlockSpec((nb, gp), lambda p: (p, 0)),
            pl.BlockSpec((nb, gp), lambda p: (p, 0)),
        ],
        out_shape=[
            jax.ShapeDtypeStruct((batch * seq, w_b), F32),
            jax.ShapeDtypeStruct((batch, gp), F32),
            jax.ShapeDtypeStruct((batch, gp), F32),
        ],
        scratch_shapes=[pltpu.VMEM((n_slabs, nb * seq, LANES), F32), pltpu.VMEM((n_slabs, nb * seq, LANES), F32)],
        compiler_params=_params("arbitrary"),
        name="s5_sample",
    )(proj, x0_re, x0_im, *ws)


def _mix_prologue(o_ref, gate_ref, y_ref, ng_ref, gluw_ref, glub_ref, a_ref, rows, chunk=128):
    w_a = o_ref.shape[1]

    def body(c, carry):
        rs = pl.ds(pl.multiple_of(c * chunk, chunk), chunk)
        o = o_ref[rs, :]
        var = jnp.mean(o * o, axis=-1, keepdims=True)
        gate = gate_ref[rs, :]
        a_ref[rs, :w_a] = (o * lax.rsqrt(var + EPS) * ng_ref[...] * (gate * jax.nn.sigmoid(gate))).astype(BF16)
        y = jax.nn.gelu(y_ref[rs, :])
        z = _dot(y.astype(BF16), gluw_ref[...]) + glub_ref[...]
        a_ref[rs, w_a:] = (y * jax.nn.sigmoid(z)).astype(BF16)
        return carry

    lax.fori_loop(0, rows // chunk, body, 0)


def _mix_kernel(op_ref, os_ref, gate_ref, yp_ref, ys_ref, xp_ref, xs_ref, ng_ref, gluw_ref, glub_ref, wout_ref,
                o_ref, a_ref, *, n_prompt_tiles, rows):
    i = pl.program_id(0)
    j = pl.program_id(1)
    is_prompt = i < n_prompt_tiles

    @pl.when(jnp.logical_and(j == 0, is_prompt))
    def _():
        _mix_prologue(op_ref, gate_ref, yp_ref, ng_ref, gluw_ref, glub_ref, a_ref, rows)

    @pl.when(jnp.logical_and(j == 0, jnp.logical_not(is_prompt)))
    def _():
        _mix_prologue(os_ref, gate_ref, ys_ref, ng_ref, gluw_ref, glub_ref, a_ref, rows)

    mix = _dot(a_ref[...], wout_ref[...])

    @pl.when(is_prompt)
    def _():
        o_ref[...] = xp_ref[...] + mix

    @pl.when(jnp.logical_not(is_prompt))
    def _():
        o_ref[...] = xs_ref[...] + mix


def _mix(o_p, o_s, proj, y_p, y_s, xp, xs, norm_g, glu_w, glu_b, w_out, *, gate_col, tn=512):
    mp, w_a = o_p.shape
    ms = o_s.shape[0]
    w_b = y_p.shape[1]
    d = w_out.shape[1]
    tm = ROW_TILE
    assert mp % tm == 0 and ms % tm == 0 and d % tn == 0
    npt = mp // tm
    nt = npt + ms // tm
    pi = lambda i, j: (jnp.minimum(i, npt - 1), 0)
    si = lambda i, j: (jnp.maximum(i - npt, 0), 0)
    const = lambda i, j: (0, 0)
    return pl.pallas_call(
        functools.partial(_mix_kernel, n_prompt_tiles=npt, rows=tm),
        grid=(nt, d // tn),
        in_specs=[
            pl.BlockSpec((tm, w_a), pi),
            pl.BlockSpec((tm, w_a), si),
            pl.BlockSpec((tm, w_a), lambda i, j: (i, gate_col)),
            pl.BlockSpec((tm, w_b), pi),
            pl.BlockSpec((tm, w_b), si),
            pl.BlockSpec((tm, tn), lambda i, j: (jnp.minimum(i, npt - 1), j)),
            pl.BlockSpec((tm, tn), lambda i, j: (jnp.maximum(i - npt, 0), j)),
            pl.BlockSpec((1, w_a), const),
            pl.BlockSpec((w_b, w_b), const),
            pl.BlockSpec((1, w_b), const),
            pl.BlockSpec((w_a + w_b, tn), lambda i, j: (0, j)),
        ],
        out_specs=pl.BlockSpec((tm, tn), lambda i, j: (i, j)),
        out_shape=jax.ShapeDtypeStruct((mp + ms, d), F32),
        scratch_shapes=[pltpu.VMEM((tm, w_a + w_b), BF16)],
        compiler_params=_params("arbitrary", "arbitrary"),
        name="mix",
    )(o_p, o_s, proj, y_p, y_s, xp, xs, norm_g.reshape(1, w_a), glu_w, glu_b.reshape(1, w_b), w_out)


def _mlp_kernel(x_ref, g_ref, up_ref, down_ref, gf_ref, op_ref, os_ref, h_ref, acc_ref, *,
                n_prompt_tiles, rows, final_norm):
    i = pl.program_id(0)
    f = pl.program_id(1)

    @pl.when(f == 0)
    def _():
        _rmsnorm_rows(x_ref, g_ref[...], h_ref, rows)
        acc_ref[...] = x_ref[...]

    a = jnp.square(jnp.maximum(_dot(h_ref[...], up_ref[...]), 0.0)).astype(BF16)
    acc_ref[...] += _dot(a, down_ref[...])

    def finish(o_ref):
        if final_norm:
            _rmsnorm_rows(acc_ref, gf_ref[...], o_ref, rows)
        else:
            o_ref[...] = acc_ref[...]

    last = f == pl.num_programs(1) - 1

    @pl.when(jnp.logical_and(last, i < n_prompt_tiles))
    def _():
        finish(op_ref)

    @pl.when(jnp.logical_and(last, i >= n_prompt_tiles))
    def _():
        finish(os_ref)


def _mlp(x, g, up, down, gf, *, mp, final_norm, tf=512):
    m, d = x.shape
    dff = up.shape[1]
    tm = ROW_TILE
    assert mp % tm == 0 and m % tm == 0 and dff % tf == 0
    npt = mp // tm
    return pl.pallas_call(
        functools.partial(_mlp_kernel, n_prompt_tiles=npt, rows=tm, final_norm=final_norm),
        grid=(m // tm, dff // tf),
        in_specs=[
            pl.BlockSpec((tm, d), lambda i, f: (i, 0)),
            pl.BlockSpec((1, d), lambda i, f: (0, 0)),
            pl.BlockSpec((d, tf), lambda i, f: (0, f)),
            pl.BlockSpec((tf, d), lambda i, f: (f, 0)),
            pl.BlockSpec((1, d), lambda i, f: (0, 0)),
        ],
        out_specs=[
            pl.BlockSpec((tm, d), lambda i, f: (jnp.minimum(i, npt - 1), 0)),
            pl.BlockSpec((tm, d), lambda i, f: (jnp.maximum(i - npt, 0), 0)),
        ],
        out_shape=[jax.ShapeDtypeStruct((mp, d), F32), jax.ShapeDtypeStruct((m - mp, d), F32)],
        scratch_shapes=[pltpu.VMEM((tm, d), BF16), pltpu.VMEM((tm, d), F32)],
        compiler_params=_params("arbitrary", "arbitrary"),
        name="mlp",
    )(x, g.reshape(1, d), up, down, gf.reshape(1, d))


def kernel(x_prompt, x_sample, state_hgrn, state_s5_re, state_s5_im, w_in, w_out, norm1_g, norm2_g, hgrn_lb_logits, hgrn_norm_g, s5_a_re, s5_a_im, s5_b_re, s5_b_im, s5_c_re, s5_c_im, s5_d, s5_log_step, glu_w, glu_b, mlp_up, mlp_down, final_norm_g):
    bp, seq, d = x_prompt.shape
    bs, dseq, _ = x_sample.shape
    depth = w_in.shape[0]
    w_a = hgrn_norm_g.shape[1]
    w_b = s5_d.shape[1]
    g_b, p_state = s5_a_re.shape[1], s5_a_re.shape[2]
    heads = w_a // HEAD_DIM
    mp, ms = bp * seq, bs * dseq
    assert w_in.shape[2] == 4 * w_a + w_b and w_a == w_b and g_b * S5_GROUP == w_b

    xp = x_prompt.reshape(mp, d)
    xs = x_sample.reshape(ms, d)
    lb_logits = hgrn_lb_logits.astype(F32)
    new_states = []
    for l in range(depth):
        proj = _inproj(xp, xs, norm1_g[l], w_in[l].astype(BF16))

        o_p, sh_p = _hgrn_prompt(proj, lb_logits, layer=l, batch=bp, seq=seq, w_a=w_a)
        o_s, sh_s = _hgrn_sample(proj, lb_logits, state_hgrn, layer=l, row0=mp, batch=bs, seq=dseq, w_a=w_a)

        lam_re, lam_im, bb_re, bb_im = _s5_discretise(
            s5_a_re[l].astype(F32), s5_a_im[l].astype(F32), s5_log_step[l].astype(F32),
            s5_b_re[l].astype(F32), s5_b_im[l].astype(F32))
        ws = (
            lam_re.reshape(-1, 1, LANES), lam_im.reshape(-1, 1, LANES),
            _block_diag_in(bb_re).astype(BF16), _block_diag_in(bb_im).astype(BF16),
            _block_diag_out(s5_c_re[l].astype(F32)).astype(BF16), _block_diag_out(s5_c_im[l].astype(F32)).astype(BF16),
            s5_d[l].astype(F32).reshape(1, w_b),
        )
        ucol = (4 * w_a) // w_b
        y_p, sr_p, si_p = _s5_prompt(proj, ws, nb=bp, seq=seq, ucol=ucol, w_b=w_b)
        y_s, sr_s, si_s = _s5_sample(
            proj, state_s5_re[l].reshape(bs, g_b * p_state).astype(F32), state_s5_im[l].reshape(bs, g_b * p_state).astype(F32),
            ws, row0=mp, ucol=ucol, seq=dseq, w_b=w_b)

        x1 = _mix(o_p, o_s, proj, y_p.reshape(mp, w_b), y_s, xp, xs, hgrn_norm_g[l], glu_w[l].astype(BF16), glu_b[l],
                  w_out[l].astype(BF16), gate_col=3)
        last = l == depth - 1
        xp, xs = _mlp(x1, norm2_g[l], mlp_up[l].astype(BF16), mlp_down[l].astype(BF16), final_norm_g,
                      mp=mp, final_norm=last)
        new_states.append((sh_p, sr_p.reshape(bp, g_b, p_state), si_p.reshape(bp, g_b, p_state),
                           sh_s[0], sr_s.reshape(bs, g_b, p_state), si_s.reshape(bs, g_b, p_state)))

    y_prompt = xp.reshape(bp, seq, d).astype(x_prompt.dtype)
    y_sample = xs.reshape(bs, dseq, d).astype(x_sample.dtype)
    stacked = [jnp.stack([st[i] for st in new_states]) for i in range(6)]
    return (y_prompt, y_sample, *stacked)
```

```python
import functools

import jax
import jax.numpy as jnp
from jax import lax
from jax.experimental import pallas as pl
from jax.experimental.pallas import tpu as pltpu

F32 = jnp.float32
BF16 = jnp.bfloat16
EPS = 1e-6

LANES = 128
SUBLANES = 8
MXU_DIM = 256
VMEM_LIMIT = 56 * 1024 * 1024

HEAD_DIM = 128
S5_GROUP = 16
HGRN_CHUNK = 64
HGRN_SUB = 16
ROW_TILE = 1024
NEG_BIG = -1e30


def _dot(a, b):
    return jnp.dot(a, b, preferred_element_type=F32)


def _dot_nt(a, b):
    return lax.dot_general(a, b, (((1,), (1,)), ((), ())), preferred_element_type=F32)


def _dot_tn(a, b):
    return lax.dot_general(a, b, (((0,), (0,)), ((), ())), preferred_element_type=F32)


def _split3(x):
    hi = x.astype(BF16)
    r1 = x - hi.astype(F32)
    mid = r1.astype(BF16)
    lo = (r1 - mid.astype(F32)).astype(BF16)
    return hi, mid, lo


def _params(*sem):
    return pltpu.CompilerParams(dimension_semantics=sem, vmem_limit_bytes=VMEM_LIMIT)


def _rmsnorm_rows(src_ref, g, dst_ref, rows, chunk=64):
    def body(c, carry):
        r0 = pl.multiple_of(c * chunk, chunk)
        x = src_ref[pl.ds(r0, chunk), :]
        var = jnp.mean(x * x, axis=-1, keepdims=True)
        dst_ref[pl.ds(r0, chunk), :] = (x * lax.rsqrt(var + EPS) * g).astype(dst_ref.dtype)
        return carry

    lax.fori_loop(0, rows // chunk, body, 0)


def _inproj_kernel(x_ref, g_ref, w_ref, o_ref, h_ref, *, rows):
    @pl.when(pl.program_id(1) == 0)
    def _():
        _rmsnorm_rows(x_ref, g_ref[...], h_ref, rows)

    o_ref[...] = _dot(h_ref[...], w_ref[...])


def _row_tile(m):
    return ROW_TILE if m % ROW_TILE == 0 else ROW_TILE // 2


def _inproj(x, g, w, *, name, tn=1024):
    m, d = x.shape
    n = w.shape[1]
    tm = _row_tile(m)
    assert m % tm == 0 and n % tn == 0
    return pl.pallas_call(
        functools.partial(_inproj_kernel, rows=tm),
        grid=(m // tm, n // tn),
        in_specs=[
            pl.BlockSpec((tm, d), lambda i, j: (i, 0)),
            pl.BlockSpec((1, d), lambda i, j: (0, 0)),
            pl.BlockSpec((d, tn), lambda i, j: (0, j)),
        ],
        out_specs=pl.BlockSpec((tm, tn), lambda i, j: (i, j)),
        out_shape=jax.ShapeDtypeStruct((m, n), F32),
        scratch_shapes=[pltpu.VMEM((tm, d), BF16)],
        compiler_params=_params("arbitrary", "arbitrary"),
        name=name,
    )(x, g.reshape(1, d), w)


def _lower_bound(logits, layer):
    m = jnp.max(logits, axis=0, keepdims=True)
    e = jnp.exp(logits - m)
    return jnp.sum(e[: layer + 1], axis=0, keepdims=True) / jnp.sum(e, axis=0, keepdims=True)


def _row_to_col(e_row):
    n = e_row.shape[1]
    hi, mid, lo = (p.astype(F32) for p in _split3(e_row))
    r = lax.broadcasted_iota(jnp.int32, (2 * SUBLANES, n), 0)
    pieces = jnp.where(r == 0, hi, jnp.where(r == 1, mid, jnp.where(r == 2, lo, 0.0)))
    return _dot_tn(pieces.astype(BF16), jnp.ones((2 * SUBLANES, LANES), BF16))


def _hgrn_gates(qp, fp, lb):
    q = qp * jax.nn.sigmoid(qp)
    f = lb + (1.0 - lb) * jax.nn.sigmoid(fp)
    return q, f, 1.0 - f, jnp.log(f)


def _hgrn_chunk(qp, fp, v, lb, s, tri):
    c = qp.shape[0]
    q, _, k, logf = _hgrn_gates(qp, fp, lb)
    hi, mid, lo = _split3(logf)
    b = _dot(tri, hi) + _dot(tri, mid) + _dot(tri, lo)
    b_last = b[c - 1 : c, :]
    v16 = v.astype(BF16)

    o = _dot((q * jnp.exp(b)).astype(BF16), s.astype(BF16))

    lane = lax.broadcasted_iota(jnp.int32, (HGRN_SUB, c), 1)
    row = lax.broadcasted_iota(jnp.int32, (c, 1), 0)
    a_rows = []
    for i in range(c // HGRN_SUB):
        r0 = i * HGRN_SUB
        if i == 0:
            a_i = jnp.zeros((HGRN_SUB, c), F32)
        else:
            bref = b[r0 - 1 : r0, :]
            qs = (q[r0 : r0 + HGRN_SUB] * jnp.exp(b[r0 : r0 + HGRN_SUB] - bref)).astype(BF16)
            ks = (k[:r0] * jnp.exp(bref - b[:r0])).astype(BF16)
            ks = jnp.concatenate([ks, jnp.zeros((c - r0, HEAD_DIM), BF16)], axis=0)
            a_i = _dot_nt(qs, ks)
        for sl in range(HGRN_SUB):
            g = r0 + sl
            lo_row = r0 + (sl // SUBLANES) * SUBLANES
            arg = jnp.where(row[lo_row : r0 + HGRN_SUB] >= g, b[lo_row : r0 + HGRN_SUB] - b[g : g + 1], NEG_BIG)
            z = q[lo_row : r0 + HGRN_SUB] * jnp.exp(arg) * k[g : g + 1]
            col = jnp.sum(z, axis=-1, keepdims=True)
            if lo_row != r0:
                col = jnp.concatenate([jnp.zeros((lo_row - r0, 1), F32), col], axis=0)
            a_i = jnp.where(lane == g, col, a_i)
        a_rows.append(a_i)
    a = jnp.concatenate(a_rows, axis=0)
    o = o + _dot(a.astype(BF16), v16)

    kd = (k * jnp.exp(b_last - b)).astype(BF16)
    s_new = _row_to_col(jnp.exp(b_last)) * s + _dot_tn(kd, v16)
    return o, s_new


def _tri(c):
    r = lax.broadcasted_iota(jnp.int32, (c, c), 0)
    cc = lax.broadcasted_iota(jnp.int32, (c, c), 1)
    return jnp.where(cc <= r, 1.0, 0.0).astype(BF16)


def _hgrn_prompt_kernel(q_ref, f_ref, v_ref, lbl_ref, o_ref, s_out_ref, s_ref, *, layer, heads, n_chunks):
    t = pl.program_id(1)

    @pl.when(t == 0)
    def _():
        s_ref[...] = jnp.zeros_like(s_ref)

    lb = _lower_bound(lbl_ref[...], layer)
    tri = _tri(HGRN_CHUNK)

    def body(c, carry):
        r0 = pl.multiple_of(c * HGRN_CHUNK, HGRN_CHUNK)
        rows = pl.ds(r0, HGRN_CHUNK)
        for h in range(heads):
            cols = slice(h * HEAD_DIM, (h + 1) * HEAD_DIM)
            o, s_new = _hgrn_chunk(q_ref[rows, cols], f_ref[rows, cols], v_ref[rows, cols], lb[:, cols], s_ref[h], tri)
            s_ref[h] = s_new
            o_ref[rows, cols] = o
        return carry

    lax.fori_loop(0, n_chunks, body, 0)

    @pl.when(t == pl.num_programs(1) - 1)
    def _():
        s_out_ref[0] = s_ref[...]


def _hgrn_prompt(proj, lb_logits, *, layer, batch, seq, w_a, tt=256):
    heads = w_a // HEAD_DIM
    assert seq % tt == 0 and tt % HGRN_CHUNK == 0
    nt = seq // tt
    col = lambda c: pl.BlockSpec((tt, w_a), lambda b, t: (b * nt + t, c))
    return pl.pallas_call(
        functools.partial(_hgrn_prompt_kernel, layer=layer, heads=heads, n_chunks=tt // HGRN_CHUNK),
        grid=(batch, nt),
        in_specs=[col(0), col(1), col(2), pl.BlockSpec(lb_logits.shape, lambda b, t: (0, 0))],
        out_specs=[
            pl.BlockSpec((tt, w_a), lambda b, t: (b * nt + t, 0)),
            pl.BlockSpec((1, heads, HEAD_DIM, HEAD_DIM), lambda b, t: (b, 0, 0, 0)),
        ],
        out_shape=[
            jax.ShapeDtypeStruct((batch * seq, w_a), F32),
            jax.ShapeDtypeStruct((batch, heads, HEAD_DIM, HEAD_DIM), F32),
        ],
        scratch_shapes=[pltpu.VMEM((heads, HEAD_DIM, HEAD_DIM), F32)],
        compiler_params=_params("arbitrary", "arbitrary"),
        name="hgrn_prompt",
    )(proj, proj, proj, lb_logits)


def _hgrn_sample_kernel(q_ref, f_ref, v_ref, lbl_ref, s_in_ref, o_ref, s_out_ref, *, layer, heads, seq):
    n_seq = SUBLANES // seq
    lb = _lower_bound(lbl_ref[...], layer)
    row = lax.broadcasted_iota(jnp.int32, (SUBLANES, 1), 0)
    pos = row % seq
    for h in range(heads):
        cols = slice(h * HEAD_DIM, (h + 1) * HEAD_DIM)
        q, _, k, logf = _hgrn_gates(q_ref[:, cols], f_ref[:, cols], lb[:, cols])
        v = v_ref[:, cols]
        b = logf
        shift = 1
        while shift < seq:
            b = b + jnp.where(pos >= shift, pltpu.roll(b, shift, 0), 0.0)
            shift *= 2
        o = jnp.zeros((SUBLANES, HEAD_DIM), F32)
        for sl in range(SUBLANES):
            lo_t, hi_t = sl, (sl // seq + 1) * seq
            mask = jnp.logical_and(row >= lo_t, row < hi_t)
            z = q * jnp.exp(jnp.where(mask, b - b[sl : sl + 1], NEG_BIG)) * k[sl : sl + 1]
            o = o + jnp.sum(z, axis=-1, keepdims=True) * v[sl : sl + 1]
        v16 = v.astype(BF16)
        for n in range(n_seq):
            own = jnp.logical_and(row >= n * seq, row < (n + 1) * seq)
            s = s_in_ref[0, n, h]
            b_last = b[(n + 1) * seq - 1 : (n + 1) * seq, :]
            qe = jnp.where(own, q * jnp.exp(b), 0.0).astype(BF16)
            o = o + _dot(qe, s.astype(BF16))
            kd = jnp.where(own, k * jnp.exp(jnp.where(own, b_last - b, 0.0)), 0.0).astype(BF16)
            s_out_ref[0, n, h] = _row_to_col(jnp.exp(b_last)) * s + _dot_tn(kd, v16)
        o_ref[:, cols] = o


def _hgrn_sample(proj, lb_logits, state, *, layer, row0, batch, seq, w_a):
    heads = w_a // HEAD_DIM
    assert SUBLANES % seq == 0 and row0 % SUBLANES == 0 and (batch * seq) % SUBLANES == 0
    n_seq = SUBLANES // seq
    t0 = row0 // SUBLANES
    col = lambda c: pl.BlockSpec((SUBLANES, w_a), lambda p: (t0 + p, c))
    st = pl.BlockSpec((1, n_seq, heads, HEAD_DIM, HEAD_DIM), lambda p: (layer, p, 0, 0, 0))
    st_out = pl.BlockSpec((1, n_seq, heads, HEAD_DIM, HEAD_DIM), lambda p: (0, p, 0, 0, 0))
    return pl.pallas_call(
        functools.partial(_hgrn_sample_kernel, layer=layer, heads=heads, seq=seq),
        grid=(batch // n_seq,),
        in_specs=[col(0), col(1), col(2), pl.BlockSpec(lb_logits.shape, lambda p: (0, 0)), st],
        out_specs=[pl.BlockSpec((SUBLANES, w_a), lambda p: (p, 0)), st_out],
        out_shape=[
            jax.ShapeDtypeStruct((batch * seq, w_a), F32),
            jax.ShapeDtypeStruct((1, batch, heads, HEAD_DIM, HEAD_DIM), F32),
        ],
        compiler_params=_params("arbitrary"),
        name="hgrn_sample",
    )(proj, proj, proj, lb_logits, state)


def _s5_disc_kernel(are_ref, aim_ref, ls_ref, bre_ref, bim_ref, lre_ref, lim_ref, bbre_ref, bbim_ref):
    a_re = are_ref[...]
    a_im = aim_ref[...]
    dt = jnp.exp(ls_ref[...])
    mag = jnp.exp(a_re * dt)
    lam_re = mag * jnp.cos(a_im * dt)
    lam_im = mag * jnp.sin(a_im * dt)
    den = a_re * a_re + a_im * a_im
    nr, ni = lam_re - 1.0, lam_im
    r_re = (nr * a_re + ni * a_im) / den
    r_im = (ni * a_re - nr * a_im) / den
    lre_ref[...] = lam_re
    lim_ref[...] = lam_im
    for c in range(bre_ref.shape[0]):
        b_re = bre_ref[c]
        b_im = bim_ref[c]
        bbre_ref[c] = r_re * b_re - r_im * b_im
        bbim_ref[c] = r_re * b_im + r_im * b_re


def _s5_discretise(a_re, a_im, log_step, b_re, b_im):
    g, p = a_re.shape
    c = b_re.shape[-1]
    bt = lambda x: jnp.transpose(x, (2, 0, 1))
    return pl.pallas_call(
        _s5_disc_kernel,
        out_shape=[
            jax.ShapeDtypeStruct((g, p), F32),
            jax.ShapeDtypeStruct((g, p), F32),
            jax.ShapeDtypeStruct((c, g, p), F32),
            jax.ShapeDtypeStruct((c, g, p), F32),
        ],
        name="s5_discretise",
    )(a_re, a_im, log_step.reshape(g, 1), bt(b_re), bt(b_im))


def _block_diag_in(bb):
    c, g, p = bb.shape
    gt = MXU_DIM // c
    x = bb.reshape(c, g // gt, gt, p)
    eye = jnp.eye(gt, dtype=bb.dtype)
    return jnp.einsum("cjgp,gh->jgchp", x, eye).reshape(g // gt, gt * c, gt * p)


def _block_diag_out(cc):
    g, c, p = cc.shape
    gt = MXU_DIM // c
    x = cc.reshape(g // gt, gt, c, p)
    eye = jnp.eye(gt, dtype=cc.dtype)
    return jnp.einsum("jgcp,gh->jgphc", x, eye).reshape(g // gt, gt * p, gt * c)


def _s5_in(ub, bre_ref, bim_ref, xr_ref, xi_ref):
    nj, kc, kp = bre_ref.shape
    spj = kp // LANES
    for j in range(nj):
        uj = ub[:, j * kc : (j + 1) * kc]
        for x_ref, w_ref in ((xr_ref, bre_ref), (xi_ref, bim_ref)):
            bu = _dot(uj, w_ref[j])
            for m in range(spj):
                x_ref[j * spj + m] = bu[:, m * LANES : (m + 1) * LANES]


def _s5_out(u, xr_ref, xi_ref, cre_ref, cim_ref, d):
    nj, kp, kc = cre_ref.shape
    spj = kp // LANES
    tiles = []
    for j in range(nj):
        xr = jnp.concatenate([xr_ref[j * spj + m] for m in range(spj)], axis=1).astype(BF16)
        xi = jnp.concatenate([xi_ref[j * spj + m] for m in range(spj)], axis=1).astype(BF16)
        y = _dot(xr, cre_ref[j]) - _dot(xi, cim_ref[j])
        tiles.append(y + d[:, j * kc : (j + 1) * kc] * u[:, j * kc : (j + 1) * kc])
    return tiles


def _s5_step(lr, li, xr, xi, bur, bui):
    return lr * xr - li * xi + bur, lr * xi + li * xr + bui


def _s5_prompt_kernel(*refs, nb, slabs_per_pass):
    u_refs = refs[:nb]
    (lre_ref, lim_ref, bre_ref, bim_ref, cre_ref, cim_ref, d_ref,
     y_ref, xr_out_ref, xi_out_ref, xr_ref, xi_ref, sr_ref, si_ref) = refs[nb:]
    ts = u_refs[0].shape[0]
    n_slabs = xr_ref.shape[0]
    step = pl.program_id(0)

    @pl.when(step == 0)
    def _():
        sr_ref[...] = jnp.zeros_like(sr_ref)
        si_ref[...] = jnp.zeros_like(si_ref)

    u = jnp.concatenate([r[...] for r in u_refs], axis=0)
    _s5_in(u.astype(BF16), bre_ref, bim_ref, xr_ref, xi_ref)

    for first in range(0, n_slabs, slabs_per_pass):
        slabs = range(first, first + slabs_per_pass)
        lam = [(lre_ref[m], lim_ref[m]) for m in slabs]

        def body(r, carry):
            rows = pl.ds(r, nb, stride=ts)
            out = []
            for (lr, li), (xr, xi), m in zip(lam, carry, slabs):
                nxr, nxi = _s5_step(lr, li, xr, xi, xr_ref[m, rows, :], xi_ref[m, rows, :])
                xr_ref[m, rows, :] = nxr
                xi_ref[m, rows, :] = nxi
                out.append((nxr, nxi))
            return tuple(out)

        final = lax.fori_loop(0, ts, body, tuple((sr_ref[m], si_ref[m]) for m in slabs), unroll=4)
        for (xr, xi), m in zip(final, slabs):
            sr_ref[m] = xr
            si_ref[m] = xi

    kc = cre_ref.shape[2]
    for j, y in enumerate(_s5_out(u, xr_ref, xi_ref, cre_ref, cim_ref, d_ref[...])):
        y_ref[:, :, j * kc : (j + 1) * kc] = y.reshape(nb, ts, kc)

    @pl.when(step == pl.num_programs(0) - 1)
    def _():
        for m in range(n_slabs):
            xr_out_ref[:, m * LANES : (m + 1) * LANES] = sr_ref[m]
            xi_out_ref[:, m * LANES : (m + 1) * LANES] = si_ref[m]


def _s5_weights_specs(ws):
    zero = lambda n: (lambda *_: (0,) * n)
    return [pl.BlockSpec(w.shape, zero(w.ndim)) for w in ws]


def _s5_prompt(proj, ws, *, nb, seq, ucol, w_b, ts=128, slabs_per_pass=8):
    n_slabs = ws[0].shape[0]
    gp = n_slabs * LANES
    assert seq % ts == 0 and n_slabs % slabs_per_pass == 0
    nblk = seq // ts
    u_spec = lambda b: pl.BlockSpec((ts, w_b), lambda s: (b * nblk + s, ucol))
    return pl.pallas_call(
        functools.partial(_s5_prompt_kernel, nb=nb, slabs_per_pass=slabs_per_pass),
        grid=(nblk,),
        in_specs=[u_spec(b) for b in range(nb)] + _s5_weights_specs(ws),
        out_specs=[
            pl.BlockSpec((nb, ts, w_b), lambda s: (0, s, 0)),
            pl.BlockSpec((nb, gp), lambda s: (0, 0)),
            pl.BlockSpec((nb, gp), lambda s: (0, 0)),
        ],
        out_shape=[
            jax.ShapeDtypeStruct((nb, seq, w_b), F32),
            jax.ShapeDtypeStruct((nb, gp), F32),
            jax.ShapeDtypeStruct((nb, gp), F32),
        ],
        scratch_shapes=[
            pltpu.VMEM((n_slabs, nb * ts, LANES), F32),
            pltpu.VMEM((n_slabs, nb * ts, LANES), F32),
            pltpu.VMEM((n_slabs, nb, LANES), F32),
            pltpu.VMEM((n_slabs, nb, LANES), F32),
        ],
        compiler_params=_params("arbitrary"),
        name="s5_prompt",
    )(*([proj] * nb), *ws)


def _s5_sample_kernel(u_ref, x0r_ref, x0i_ref, lre_ref, lim_ref, bre_ref, bim_ref, cre_ref, cim_ref, d_ref,
                      y_ref, xr_out_ref, xi_out_ref, xr_ref, xi_ref, *, seq):
    nb = x0r_ref.shape[0]
    u = u_ref[...]
    _s5_in(u.astype(BF16), bre_ref, bim_ref, xr_ref, xi_ref)
    for m in range(xr_ref.shape[0]):
        ls = slice(m * LANES, (m + 1) * LANES)
        lr, li = lre_ref[m], lim_ref[m]
        xr, xi = x0r_ref[:, ls], x0i_ref[:, ls]
        for r in range(seq):
            rows = pl.ds(r, nb, stride=seq)
            xr, xi = _s5_step(lr, li, xr, xi, xr_ref[m, rows, :], xi_ref[m, rows, :])
            xr_ref[m, rows, :] = xr
            xi_ref[m, rows, :] = xi
        xr_out_ref[:, ls] = xr
        xi_out_ref[:, ls] = xi
    kc = cre_ref.shape[2]
    for j, y in enumerate(_s5_out(u, xr_ref, xi_ref, cre_ref, cim_ref, d_ref[...])):
        y_ref[:, j * kc : (j + 1) * kc] = y


def _s5_sample(proj, x0_re, x0_im, ws, *, row0, ucol, seq, w_b, nb=64):
    batch, gp = x0_re.shape
    n_slabs = gp // LANES
    assert batch % nb == 0 and row0 % (nb * seq) == 0
    t0 = row0 // (nb * seq)
    return pl.pallas_call(
        functools.partial(_s5_sample_kernel, seq=seq),
        grid=(batch // nb,),
        in_specs=[
            pl.BlockSpec((nb * seq, w_b), lambda p: (t0 + p, ucol)),
            pl.BlockSpec((nb, gp), lambda p: (p, 0)),
            pl.BlockSpec((nb, gp), lambda p: (p, 0)),
        ] + _s5_weights_specs(ws),
        out_specs=[
            pl.BlockSpec((nb * seq, w_b), lambda p: (p, 0)),
            pl.BlockSpec((nb, gp), lambda p: (p, 0)),
            pl.BlockSpec((nb, gp), lambda p: (p, 0)),
        ],
        out_shape=[
            jax.ShapeDtypeStruct((batch * seq, w_b), F32),
            jax.ShapeDtypeStruct((batch, gp), F32),
            jax.ShapeDtypeStruct((batch, gp), F32),
        ],
        scratch_shapes=[pltpu.VMEM((n_slabs, nb * seq, LANES), F32), pltpu.VMEM((n_slabs, nb * seq, LANES), F32)],
        compiler_params=_params("arbitrary"),
        name="s5_sample",
    )(proj, x0_re, x0_im, *ws)


def _mix_prologue(o_ref, gate_ref, y_ref, ng_ref, gluw_ref, glub_ref, a_ref, rows, chunk=128):
    w_a = o_ref.shape[1]

    def body(c, carry):
        rs = pl.ds(pl.multiple_of(c * chunk, chunk), chunk)
        o = o_ref[rs, :]
        var = jnp.mean(o * o, axis=-1, keepdims=True)
        gate = gate_ref[rs, :]
        a_ref[rs, :w_a] = (o * lax.rsqrt(var + EPS) * ng_ref[...] * (gate * jax.nn.sigmoid(gate))).astype(BF16)
        y = jax.nn.gelu(y_ref[rs, :])
        z = _dot(y.astype(BF16), gluw_ref[...]) + glub_ref[...]
        a_ref[rs, w_a:] = (y * jax.nn.sigmoid(z)).astype(BF16)
        return carry

    lax.fori_loop(0, rows // chunk, body, 0)


def _mix_kernel(o_ref, gate_ref, y_ref, x_ref, ng_ref, gluw_ref, glub_ref, wout_ref, out_ref, a_ref, *, rows):
    @pl.when(pl.program_id(1) == 0)
    def _():
        _mix_prologue(o_ref, gate_ref, y_ref, ng_ref, gluw_ref, glub_ref, a_ref, rows)

    out_ref[...] = x_ref[...] + _dot(a_ref[...], wout_ref[...])


def _mix(o, proj, y, x, norm_g, glu_w, glu_b, w_out, *, gate_col, name, tn=512):
    m, w_a = o.shape
    w_b = y.shape[1]
    d = w_out.shape[1]
    tm = _row_tile(m)
    assert m % tm == 0 and d % tn == 0
    row = lambda i, j: (i, 0)
    const = lambda i, j: (0, 0)
    return pl.pallas_call(
        functools.partial(_mix_kernel, rows=tm),
        grid=(m // tm, d // tn),
        in_specs=[
            pl.BlockSpec((tm, w_a), row),
            pl.BlockSpec((tm, w_a), lambda i, j: (i, gate_col)),
            pl.BlockSpec((tm, w_b), row),
            pl.BlockSpec((tm, tn), lambda i, j: (i, j)),
            pl.BlockSpec((1, w_a), const),
            pl.BlockSpec((w_b, w_b), const),
            pl.BlockSpec((1, w_b), const),
            pl.BlockSpec((w_a + w_b, tn), lambda i, j: (0, j)),
        ],
        out_specs=pl.BlockSpec((tm, tn), lambda i, j: (i, j)),
        out_shape=jax.ShapeDtypeStruct((m, d), F32),
        scratch_shapes=[pltpu.VMEM((tm, w_a + w_b), BF16)],
        compiler_params=_params("arbitrary", "arbitrary"),
        name=name,
    )(o, proj, y, x, norm_g.reshape(1, w_a), glu_w, glu_b.reshape(1, w_b), w_out)


def _mlp_kernel(x_ref, g_ref, up_ref, down_ref, gf_ref, o_ref, h_ref, *, rows, final_norm):
    f = pl.program_id(1)

    @pl.when(f == 0)
    def _():
        _rmsnorm_rows(x_ref, g_ref[...], h_ref, rows)
        o_ref[...] = x_ref[...]

    a = jnp.square(jnp.maximum(_dot(h_ref[...], up_ref[...]), 0.0)).astype(BF16)
    o_ref[...] += _dot(a, down_ref[...])

    if final_norm:
        @pl.when(f == pl.num_programs(1) - 1)
        def _():
            _rmsnorm_rows(o_ref, gf_ref[...], o_ref, rows)


def _mlp(x, g, up, down, gf, *, final_norm, name, tf=512):
    m, d = x.shape
    dff = up.shape[1]
    tm = _row_tile(m)
    assert m % tm == 0 and dff % tf == 0
    return pl.pallas_call(
        functools.partial(_mlp_kernel, rows=tm, final_norm=final_norm),
        grid=(m // tm, dff // tf),
        in_specs=[
            pl.BlockSpec((tm, d), lambda i, f: (i, 0)),
            pl.BlockSpec((1, d), lambda i, f: (0, 0)),
            pl.BlockSpec((d, tf), lambda i, f: (0, f)),
            pl.BlockSpec((tf, d), lambda i, f: (f, 0)),
            pl.BlockSpec((1, d), lambda i, f: (0, 0)),
        ],
        out_specs=pl.BlockSpec((tm, d), lambda i, f: (i, 0)),
        out_shape=jax.ShapeDtypeStruct((m, d), F32),
        scratch_shapes=[pltpu.VMEM((tm, d), BF16)],
        compiler_params=_params("arbitrary", "arbitrary"),
        name=name,
    )(x, g.reshape(1, d), up, down, gf.reshape(1, d))


def kernel(x_prompt, x_sample, state_hgrn, state_s5_re, state_s5_im, w_in, w_out, norm1_g, norm2_g, hgrn_lb_logits, hgrn_norm_g, s5_a_re, s5_a_im, s5_b_re, s5_b_im, s5_c_re, s5_c_im, s5_d, s5_log_step, glu_w, glu_b, mlp_up, mlp_down, final_norm_g):
    bp, seq, d = x_prompt.shape
    bs, dseq, _ = x_sample.shape
    depth = w_in.shape[0]
    w_a = hgrn_norm_g.shape[1]
    w_b = s5_d.shape[1]
    g_b, p_state = s5_a_re.shape[1], s5_a_re.shape[2]
    mp, ms = bp * seq, bs * dseq
    assert w_in.shape[2] == 4 * w_a + w_b and w_a == w_b and g_b * S5_GROUP == w_b
    gate_col, ucol = 3, (4 * w_a) // w_b

    xp = x_prompt.reshape(mp, d)
    xs = x_sample.reshape(ms, d)
    lb_logits = hgrn_lb_logits.astype(F32)
    new_states = []
    for l in range(depth):
        w_in_l, w_out_l, glu_w_l = w_in[l].astype(BF16), w_out[l].astype(BF16), glu_w[l].astype(BF16)
        up_l, down_l = mlp_up[l].astype(BF16), mlp_down[l].astype(BF16)
        proj_p = _inproj(xp, norm1_g[l], w_in_l, name="inproj_prompt")
        proj_s = _inproj(xs, norm1_g[l], w_in_l, name="inproj_sample")

        o_p, sh_p = _hgrn_prompt(proj_p, lb_logits, layer=l, batch=bp, seq=seq, w_a=w_a)
        o_s, sh_s = _hgrn_sample(proj_s, lb_logits, state_hgrn, layer=l, row0=0, batch=bs, seq=dseq, w_a=w_a)

        lam_re, lam_im, bb_re, bb_im = _s5_discretise(
            s5_a_re[l].astype(F32), s5_a_im[l].astype(F32), s5_log_step[l].astype(F32),
            s5_b_re[l].astype(F32), s5_b_im[l].astype(F32))
        ws = (
            lam_re.reshape(-1, 1, LANES), lam_im.reshape(-1, 1, LANES),
            _block_diag_in(bb_re).astype(BF16), _block_diag_in(bb_im).astype(BF16),
            _block_diag_out(s5_c_re[l].astype(F32)).astype(BF16), _block_diag_out(s5_c_im[l].astype(F32)).astype(BF16),
            s5_d[l].astype(F32).reshape(1, w_b),
        )
        y_p, sr_p, si_p = _s5_prompt(proj_p, ws, nb=bp, seq=seq, ucol=ucol, w_b=w_b)
        y_s, sr_s, si_s = _s5_sample(
            proj_s, state_s5_re[l].reshape(bs, g_b * p_state).astype(F32), state_s5_im[l].reshape(bs, g_b * p_state).astype(F32),
            ws, row0=0, ucol=ucol, seq=dseq, w_b=w_b)

        last = l == depth - 1
        groups = []
        for tag, x, o, proj, y in (("prompt", xp, o_p, proj_p, y_p.reshape(mp, w_b)), ("sample", xs, o_s, proj_s, y_s)):
            x1 = _mix(o, proj, y, x, hgrn_norm_g[l], glu_w_l, glu_b[l], w_out_l, gate_col=gate_col, name="mix_" + tag)
            groups.append(_mlp(x1, norm2_g[l], up_l, down_l, final_norm_g, final_norm=last, name="mlp_" + tag))
        xp, xs = groups
        new_states.append((sh_p, sr_p.reshape(bp, g_b, p_state), si_p.reshape(bp, g_b, p_state),
                           sh_s[0], sr_s.reshape(bs, g_b, p_state), si_s.reshape(bs, g_b, p_state)))

    y_prompt = xp.reshape(bp, seq, d).astype(x_prompt.dtype)
    y_sample = xs.reshape(bs, dseq, d).astype(x_sample.dtype)
    stacked = [jnp.stack([st[i] for st in new_states]) for i in range(6)]
    return (y_prompt, y_sample, *stacked)
```

```python
import functools

import jax
import jax.numpy as jnp
from jax import lax
from jax.experimental import pallas as pl
from jax.experimental.pallas import tpu as pltpu

F32 = jnp.float32
BF16 = jnp.bfloat16
EPS = 1e-6

LANES = 128
SUBLANES = 8
MXU_DIM = 256
VMEM_LIMIT = 56 * 1024 * 1024

HEAD_DIM = 128
S5_GROUP = 16
HGRN_CHUNK = 64
HGRN_SUB = 16
ROW_TILE = 1024
NEG_BIG = -1e30


def _dot(a, b):
    return jnp.dot(a, b, preferred_element_type=F32)


def _dot_nt(a, b):
    return lax.dot_general(a, b, (((1,), (1,)), ((), ())), preferred_element_type=F32)


def _dot_tn(a, b):
    return lax.dot_general(a, b, (((0,), (0,)), ((), ())), preferred_element_type=F32)


def _split3(x):
    hi = x.astype(BF16)
    r1 = x - hi.astype(F32)
    mid = r1.astype(BF16)
    lo = (r1 - mid.astype(F32)).astype(BF16)
    return hi, mid, lo


def _params(*sem):
    return pltpu.CompilerParams(dimension_semantics=sem, vmem_limit_bytes=VMEM_LIMIT)


def _rmsnorm_rows(src_ref, g, dst_ref, rows, chunk=64):
    def body(c, carry):
        r0 = pl.multiple_of(c * chunk, chunk)
        x = src_ref[pl.ds(r0, chunk), :]
        var = jnp.mean(x * x, axis=-1, keepdims=True)
        dst_ref[pl.ds(r0, chunk), :] = (x * lax.rsqrt(var + EPS) * g).astype(dst_ref.dtype)
        return carry

    lax.fori_loop(0, rows // chunk, body, 0)


def _inproj_kernel(x_ref, g_ref, w_ref, o_ref, h_ref, *, rows):
    @pl.when(pl.program_id(1) == 0)
    def _():
        _rmsnorm_rows(x_ref, g_ref[...], h_ref, rows)

    o_ref[...] = _dot(h_ref[...], w_ref[...])


def _row_tile(m):
    return ROW_TILE if m % ROW_TILE == 0 else ROW_TILE // 2


def _inproj(x, g, w, *, name, tn=1024):
    m, d = x.shape
    n = w.shape[1]
    tm = _row_tile(m)
    assert m % tm == 0 and n % tn == 0
    return pl.pallas_call(
        functools.partial(_inproj_kernel, rows=tm),
        grid=(m // tm, n // tn),
        in_specs=[
            pl.BlockSpec((tm, d), lambda i, j: (i, 0)),
            pl.BlockSpec((1, d), lambda i, j: (0, 0)),
            pl.BlockSpec((d, tn), lambda i, j: (0, j)),
        ],
        out_specs=pl.BlockSpec((tm, tn), lambda i, j: (i, j)),
        out_shape=jax.ShapeDtypeStruct((m, n), F32),
        scratch_shapes=[pltpu.VMEM((tm, d), BF16)],
        compiler_params=_params("arbitrary", "arbitrary"),
        name=name,
    )(x, g.reshape(1, d), w)


def _lower_bound(logits, layer):
    m = jnp.max(logits, axis=0, keepdims=True)
    e = jnp.exp(logits - m)
    return jnp.sum(e[: layer + 1], axis=0, keepdims=True) / jnp.sum(e, axis=0, keepdims=True)


def _row_to_col(e_row):
    n = e_row.shape[1]
    hi, mid, lo = (p.astype(F32) for p in _split3(e_row))
    r = lax.broadcasted_iota(jnp.int32, (2 * SUBLANES, n), 0)
    pieces = jnp.where(r == 0, hi, jnp.where(r == 1, mid, jnp.where(r == 2, lo, 0.0)))
    return _dot_tn(pieces.astype(BF16), jnp.ones((2 * SUBLANES, LANES), BF16))


def _hgrn_gates(qp, fp, lb):
    q = qp * jax.nn.sigmoid(qp)
    f = lb + (1.0 - lb) * jax.nn.sigmoid(fp)
    return q, f, 1.0 - f, jnp.log(f)


def _hgrn_chunk(qp, fp, v, lb, s, tri):
    c = qp.shape[0]
    q, _, k, logf = _hgrn_gates(qp, fp, lb)
    hi, mid, lo = _split3(logf)
    b = _dot(tri, hi) + _dot(tri, mid) + _dot(tri, lo)
    b_last = b[c - 1 : c, :]
    v16 = v.astype(BF16)

    o = _dot((q * jnp.exp(b)).astype(BF16), s.astype(BF16))

    lane = lax.broadcasted_iota(jnp.int32, (HGRN_SUB, c), 1)
    row = lax.broadcasted_iota(jnp.int32, (c, 1), 0)
    a_rows = []
    for i in range(c // HGRN_SUB):
        r0 = i * HGRN_SUB
        if i == 0:
            a_i = jnp.zeros((HGRN_SUB, c), F32)
        else:
            bref = b[r0 - 1 : r0, :]
            qs = (q[r0 : r0 + HGRN_SUB] * jnp.exp(b[r0 : r0 + HGRN_SUB] - bref)).astype(BF16)
            ks = (k[:r0] * jnp.exp(bref - b[:r0])).astype(BF16)
            ks = jnp.concatenate([ks, jnp.zeros((c - r0, HEAD_DIM), BF16)], axis=0)
            a_i = _dot_nt(qs, ks)
        for sl in range(HGRN_SUB):
            g = r0 + sl
            lo_row = r0 + (sl // SUBLANES) * SUBLANES
            arg = jnp.where(row[lo_row : r0 + HGRN_SUB] >= g, b[lo_row : r0 + HGRN_SUB] - b[g : g + 1], NEG_BIG)
            z = q[lo_row : r0 + HGRN_SUB] * jnp.exp(arg) * k[g : g + 1]
            col = jnp.sum(z, axis=-1, keepdims=True)
            if lo_row != r0:
                col = jnp.concatenate([jnp.zeros((lo_row - r0, 1), F32), col], axis=0)
            a_i = jnp.where(lane == g, col, a_i)
        a_rows.append(a_i)
    a = jnp.concatenate(a_rows, axis=0)
    o = o + _dot(a.astype(BF16), v16)

    kd = (k * jnp.exp(b_last - b)).astype(BF16)
    s_new = _row_to_col(jnp.exp(b_last)) * s + _dot_tn(kd, v16)
    return o, s_new


def _tri(c):
    r = lax.broadcasted_iota(jnp.int32, (c, c), 0)
    cc = lax.broadcasted_iota(jnp.int32, (c, c), 1)
    return jnp.where(cc <= r, 1.0, 0.0).astype(BF16)


def _hgrn_prompt_kernel(q_ref, f_ref, v_ref, lbl_ref, o_ref, s_out_ref, s_ref, *, layer, heads, n_chunks):
    t = pl.program_id(1)

    @pl.when(t == 0)
    def _():
        s_ref[...] = jnp.zeros_like(s_ref)

    lb = _lower_bound(lbl_ref[...], layer)
    tri = _tri(HGRN_CHUNK)

    def body(c, carry):
        r0 = pl.multiple_of(c * HGRN_CHUNK, HGRN_CHUNK)
        rows = pl.ds(r0, HGRN_CHUNK)
        for h in range(heads):
            cols = slice(h * HEAD_DIM, (h + 1) * HEAD_DIM)
            o, s_new = _hgrn_chunk(q_ref[rows, cols], f_ref[rows, cols], v_ref[rows, cols], lb[:, cols], s_ref[h], tri)
            s_ref[h] = s_new
            o_ref[rows, cols] = o
        return carry

    lax.fori_loop(0, n_chunks, body, 0)

    @pl.when(t == pl.num_programs(1) - 1)
    def _():
        s_out_ref[0] = s_ref[...]


def _hgrn_prompt(proj, lb_logits, *, layer, batch, seq, w_a, tt=256):
    heads = w_a // HEAD_DIM
    assert seq % tt == 0 and tt % HGRN_CHUNK == 0
    nt = seq // tt
    col = lambda c: pl.BlockSpec((tt, w_a), lambda b, t: (b * nt + t, c))
    return pl.pallas_call(
        functools.partial(_hgrn_prompt_kernel, layer=layer, heads=heads, n_chunks=tt // HGRN_CHUNK),
        grid=(batch, nt),
        in_specs=[col(0), col(1), col(2), pl.BlockSpec(lb_logits.shape, lambda b, t: (0, 0))],
        out_specs=[
            pl.BlockSpec((tt, w_a), lambda b, t: (b * nt + t, 0)),
            pl.BlockSpec((1, heads, HEAD_DIM, HEAD_DIM), lambda b, t: (b, 0, 0, 0)),
        ],
        out_shape=[
            jax.ShapeDtypeStruct((batch * seq, w_a), F32),
            jax.ShapeDtypeStruct((batch, heads, HEAD_DIM, HEAD_DIM), F32),
        ],
        scratch_shapes=[pltpu.VMEM((heads, HEAD_DIM, HEAD_DIM), F32)],
        compiler_params=_params("arbitrary", "arbitrary"),
        name="hgrn_prompt",
    )(proj, proj, proj, lb_logits)


def _hgrn_sample_kernel(q_ref, f_ref, v_ref, lbl_ref, s_in_ref, o_ref, s_out_ref, *, layer, heads, seq):
    n_seq = SUBLANES // seq
    lb = _lower_bound(lbl_ref[...], layer)
    row = lax.broadcasted_iota(jnp.int32, (SUBLANES, 1), 0)
    pos = row % seq
    for h in range(heads):
        cols = slice(h * HEAD_DIM, (h + 1) * HEAD_DIM)
        q, _, k, logf = _hgrn_gates(q_ref[:, cols], f_ref[:, cols], lb[:, cols])
        v = v_ref[:, cols]
        b = logf
        shift = 1
        while shift < seq:
            b = b + jnp.where(pos >= shift, pltpu.roll(b, shift, 0), 0.0)
            shift *= 2
        o = jnp.zeros((SUBLANES, HEAD_DIM), F32)
        for sl in range(SUBLANES):
            lo_t, hi_t = sl, (sl // seq + 1) * seq
            mask = jnp.logical_and(row >= lo_t, row < hi_t)
            z = q * jnp.exp(jnp.where(mask, b - b[sl : sl + 1], NEG_BIG)) * k[sl : sl + 1]
            o = o + jnp.sum(z, axis=-1, keepdims=True) * v[sl : sl + 1]
        v16 = v.astype(BF16)
        for n in range(n_seq):
            own = jnp.logical_and(row >= n * seq, row < (n + 1) * seq)
            s = s_in_ref[0, n, h]
            b_last = b[(n + 1) * seq - 1 : (n + 1) * seq, :]
            qe = jnp.where(own, q * jnp.exp(b), 0.0).astype(BF16)
            o = o + _dot(qe, s.astype(BF16))
            kd = jnp.where(own, k * jnp.exp(jnp.where(own, b_last - b, 0.0)), 0.0).astype(BF16)
            s_out_ref[0, n, h] = _row_to_col(jnp.exp(b_last)) * s + _dot_tn(kd, v16)
        o_ref[:, cols] = o


def _hgrn_sample(proj, lb_logits, state, *, layer, row0, batch, seq, w_a):
    heads = w_a // HEAD_DIM
    assert SUBLANES % seq == 0 and row0 % SUBLANES == 0 and (batch * seq) % SUBLANES == 0
    n_seq = SUBLANES // seq
    t0 = row0 // SUBLANES
    col = lambda c: pl.BlockSpec((SUBLANES, w_a), lambda p: (t0 + p, c))
    st = pl.BlockSpec((1, n_seq, heads, HEAD_DIM, HEAD_DIM), lambda p: (layer, p, 0, 0, 0))
    st_out = pl.BlockSpec((1, n_seq, heads, HEAD_DIM, HEAD_DIM), lambda p: (0, p, 0, 0, 0))
    return pl.pallas_call(
        functools.partial(_hgrn_sample_kernel, layer=layer, heads=heads, seq=seq),
        grid=(batch // n_seq,),
        in_specs=[col(0), col(1), col(2), pl.BlockSpec(lb_logits.shape, lambda p: (0, 0)), st],
        out_specs=[pl.BlockSpec((SUBLANES, w_a), lambda p: (p, 0)), st_out],
        out_shape=[
            jax.ShapeDtypeStruct((batch * seq, w_a), F32),
            jax.ShapeDtypeStruct((1, batch, heads, HEAD_DIM, HEAD_DIM), F32),
        ],
        compiler_params=_params("arbitrary"),
        name="hgrn_sample",
    )(proj, proj, proj, lb_logits, state)


def _s5_disc_kernel(are_ref, aim_ref, ls_ref, bre_ref, bim_ref, lre_ref, lim_ref, bbre_ref, bbim_ref):
    a_re = are_ref[...]
    a_im = aim_ref[...]
    dt = jnp.exp(ls_ref[...])
    mag = jnp.exp(a_re * dt)
    lam_re = mag * jnp.cos(a_im * dt)
    lam_im = mag * jnp.sin(a_im * dt)
    den = a_re * a_re + a_im * a_im
    nr, ni = lam_re - 1.0, lam_im
    r_re = (nr * a_re + ni * a_im) / den
    r_im = (ni * a_re - nr * a_im) / den
    lre_ref[...] = lam_re
    lim_ref[...] = lam_im
    for c in range(bre_ref.shape[0]):
        b_re = bre_ref[c]
        b_im = bim_ref[c]
        bbre_ref[c] = r_re * b_re - r_im * b_im
        bbim_ref[c] = r_re * b_im + r_im * b_re


def _s5_discretise(a_re, a_im, log_step, b_re, b_im):
    g, p = a_re.shape
    c = b_re.shape[-1]
    bt = lambda x: jnp.transpose(x, (2, 0, 1))
    return pl.pallas_call(
        _s5_disc_kernel,
        out_shape=[
            jax.ShapeDtypeStruct((g, p), F32),
            jax.ShapeDtypeStruct((g, p), F32),
            jax.ShapeDtypeStruct((c, g, p), F32),
            jax.ShapeDtypeStruct((c, g, p), F32),
        ],
        name="s5_discretise",
    )(a_re, a_im, log_step.reshape(g, 1), bt(b_re), bt(b_im))


def _block_diag_in(bb):
    c, g, p = bb.shape
    gt = MXU_DIM // c
    x = bb.reshape(c, g // gt, gt, p)
    eye = jnp.eye(gt, dtype=bb.dtype)
    return jnp.einsum("cjgp,gh->jgchp", x, eye).reshape(g // gt, gt * c, gt * p)


def _block_diag_out(cc):
    g, c, p = cc.shape
    gt = MXU_DIM // c
    x = cc.reshape(g // gt, gt, c, p)
    eye = jnp.eye(gt, dtype=cc.dtype)
    return jnp.einsum("jgcp,gh->jgphc", x, eye).reshape(g // gt, gt * p, gt * c)


def _s5_lanes(x_ref, lo, hi):
    if len(x_ref.shape) == 2:
        return x_ref[:, lo:hi]
    return jnp.concatenate([x_ref[m] for m in range(lo // LANES, hi // LANES)], axis=1)


def _s5_in(ub, bre_ref, bim_ref, xr_ref, xi_ref):
    nj, kc, kp = bre_ref.shape
    for j in range(nj):
        uj = ub[:, j * kc : (j + 1) * kc]
        for x_ref, w_ref in ((xr_ref, bre_ref), (xi_ref, bim_ref)):
            bu = _dot(uj, w_ref[j])
            if len(x_ref.shape) == 2:
                x_ref[:, j * kp : (j + 1) * kp] = bu
            else:
                for m in range(kp // LANES):
                    x_ref[j * kp // LANES + m] = bu[:, m * LANES : (m + 1) * LANES]


def _s5_out(xr_ref, xi_ref, cre_ref, cim_ref):
    nj, kp, kc = cre_ref.shape
    tiles = []
    for j in range(nj):
        xr = _s5_lanes(xr_ref, j * kp, (j + 1) * kp).astype(BF16)
        xi = _s5_lanes(xi_ref, j * kp, (j + 1) * kp).astype(BF16)
        tiles.append(_dot(xr, cre_ref[j]) - _dot(xi, cim_ref[j]))
    return tiles


def _s5_step(lr, li, xr, xi, bur, bui):
    return lr * xr - li * xi + bur, lr * xi + li * xr + bui


def _s5_lambda(lre_ref, lim_ref, lo, hi, rows):
    cat = lambda ref: jnp.concatenate([ref[m] for m in range(lo // LANES, hi // LANES)], axis=1)
    return jnp.broadcast_to(cat(lre_ref), (rows, hi - lo)), jnp.broadcast_to(cat(lim_ref), (rows, hi - lo))


def _to_time_major(x):
    n, ts, w = x.shape
    return jnp.swapaxes(x, 0, 1).reshape(ts * n, w)


def _from_time_major(x, n):
    rows, w = x.shape
    return jnp.swapaxes(x.reshape(rows // n, n, w), 0, 1)


def _s5_prompt_kernel(*refs, nv, lane_block):
    u_refs = refs[:nv]
    (lre_ref, lim_ref, bre_ref, bim_ref, cre_ref, cim_ref, d_ref,
     y_ref, xr_out_ref, xi_out_ref, xr_ref, xi_ref, sr_ref, si_ref) = refs[nv:]
    ts = u_refs[0].shape[0]
    gp = xr_ref.shape[1]
    step = pl.program_id(0)

    @pl.when(step == 0)
    def _():
        sr_ref[...] = jnp.zeros_like(sr_ref)
        si_ref[...] = jnp.zeros_like(si_ref)

    u = _to_time_major(jnp.stack([r[...] for r in u_refs], axis=0))
    _s5_in(u.astype(BF16), bre_ref, bim_ref, xr_ref, xi_ref)

    for lo in range(0, gp, lane_block):
        ls = slice(lo, lo + lane_block)
        lr, li = _s5_lambda(lre_ref, lim_ref, lo, lo + lane_block, nv)

        def body(r, carry):
            rows = pl.ds(pl.multiple_of(r * nv, nv), nv)
            xr, xi = _s5_step(lr, li, *carry, xr_ref[rows, ls], xi_ref[rows, ls])
            xr_ref[rows, ls] = xr
            xi_ref[rows, ls] = xi
            return xr, xi

        xr, xi = lax.fori_loop(0, ts, body, (sr_ref[:, ls], si_ref[:, ls]), unroll=4)
        sr_ref[:, ls] = xr
        si_ref[:, ls] = xi

    kc = cre_ref.shape[2]
    d = d_ref[...]
    for j, y in enumerate(_s5_out(xr_ref, xi_ref, cre_ref, cim_ref)):
        cols = slice(j * kc, (j + 1) * kc)
        y_ref[:, :, cols] = _from_time_major(y + d[:, cols] * u[:, cols], nv)

    @pl.when(step == pl.num_programs(0) - 1)
    def _():
        xr_out_ref[...] = sr_ref[...]
        xi_out_ref[...] = si_ref[...]


def _s5_carry_kernel(y_ref, fr_ref, fi_ref, lre_ref, lim_ref, cre_ref, cim_ref,
                     o_ref, xr_out_ref, xi_out_ref, zr_ref, zi_ref, sr_ref, si_ref, *, half_len, lane_block):
    nrow, gp = fr_ref.shape
    tf = y_ref.shape[2]
    step = pl.program_id(0)
    odd = lax.broadcasted_iota(jnp.int32, (nrow, 1), 0) % 2 == 1

    @pl.when(step == 0)
    def _():
        for lo in range(0, gp, lane_block):
            ls = slice(lo, lo + lane_block)
            pr, pi = _s5_lambda(lre_ref, lim_ref, lo, lo + lane_block, nrow)
            n = 1
            while n < half_len:
                pr, pi = pr * pr - pi * pi, 2.0 * pr * pi
                n *= 2
            fr, fi = fr_ref[:, ls], fi_ref[:, ls]
            fr = jnp.where(odd, pltpu.roll(fr, 1, 0), fr)
            fi = jnp.where(odd, pltpu.roll(fi, 1, 0), fi)
            sr_ref[:, ls] = jnp.where(odd, pr * fr - pi * fi, fr)
            si_ref[:, ls] = jnp.where(odd, pr * fi + pi * fr, fi)

    for lo in range(0, gp, lane_block):
        ls = slice(lo, lo + lane_block)
        lr, li = _s5_lambda(lre_ref, lim_ref, lo, lo + lane_block, nrow)

        def body(r, carry):
            rows = pl.ds(pl.multiple_of(r * nrow, nrow), nrow)
            zr, zi = carry
            zr, zi = lr * zr - li * zi, lr * zi + li * zr
            zr_ref[rows, ls] = zr
            zi_ref[rows, ls] = zi
            return zr, zi

        zr, zi = lax.fori_loop(0, tf, body, (sr_ref[:, ls], si_ref[:, ls]), unroll=4)
        sr_ref[:, ls] = zr
        si_ref[:, ls] = zi

    kc = cre_ref.shape[2]
    nseq = y_ref.shape[0]
    for j, y in enumerate(_s5_out(zr_ref, zi_ref, cre_ref, cim_ref)):
        cols = slice(j * kc, (j + 1) * kc)
        o_ref[:, :, :, cols] = y_ref[:, :, :, cols] + _from_time_major(y, nrow).reshape(nseq, nrow // nseq, tf, kc)

    @pl.when(step == pl.num_programs(0) - 1)
    def _():
        xr_out_ref[...] = fr_ref[...] + sr_ref[...]
        xi_out_ref[...] = fi_ref[...] + si_ref[...]


def _s5_weights_specs(ws):
    zero = lambda n: (lambda *_: (0,) * n)
    return [pl.BlockSpec(w.shape, zero(w.ndim)) for w in ws]


def _s5_prompt(proj, ws, *, nb, seq, ucol, w_b, ts=64, tf=64, lane_block=1024):
    lam_re, lam_im, bre, bim, cre, cim, d = ws
    gp = lam_re.shape[0] * LANES
    nv = SUBLANES
    assert nv == 2 * nb, "two pieces per sequence"
    plen, half_len = seq // 2, seq // 4
    assert plen % ts == 0 and half_len % tf == 0 and gp % lane_block == 0
    assert half_len & (half_len - 1) == 0, "repeated squaring needs a power of two"
    nblk = plen // ts
    u_spec = lambda n: pl.BlockSpec((ts, w_b), lambda s: (n * nblk + s, ucol))
    state = pl.BlockSpec((nv, gp), lambda s: (0, 0))
    state_shape = jax.ShapeDtypeStruct((nv, gp), F32)
    scratch = lambda steps: [pltpu.VMEM((nv * steps, gp), F32)] * 2 + [pltpu.VMEM((nv, gp), F32)] * 2
    y, fr, fi = pl.pallas_call(
        functools.partial(_s5_prompt_kernel, nv=nv, lane_block=lane_block),
        grid=(nblk,),
        in_specs=[u_spec(n) for n in range(nv)] + _s5_weights_specs(ws),
        out_specs=[pl.BlockSpec((nv, ts, w_b), lambda s: (0, s, 0)), state, state],
        out_shape=[jax.ShapeDtypeStruct((nv, plen, w_b), F32), state_shape, state_shape],
        scratch_shapes=scratch(ts),
        compiler_params=_params("arbitrary"),
        name="s5_prompt",
    )(*([proj] * nv), *ws)

    y5 = y.reshape(nb, 2, 2, half_len, w_b)
    y_spec = pl.BlockSpec((nb, None, 2, tf, w_b), lambda s: (0, 1, 0, s, 0))
    y5, xr, xi = pl.pallas_call(
        functools.partial(_s5_carry_kernel, half_len=half_len, lane_block=lane_block),
        grid=(half_len // tf,),
        in_specs=[y_spec, state, state] + _s5_weights_specs((lam_re, lam_im, cre, cim)),
        out_specs=[y_spec, state, state],
        out_shape=[jax.ShapeDtypeStruct(y5.shape, F32), state_shape, state_shape],
        scratch_shapes=scratch(tf),
        input_output_aliases={0: 0},
        compiler_params=_params("arbitrary"),
        name="s5_prompt_carry",
    )(y5, fr, fi, lam_re, lam_im, cre, cim)
    last = lambda x: x.reshape(nb, 2, gp)[:, 1]
    return y5.reshape(nb * seq, w_b), last(xr), last(xi)


def _s5_sample_kernel(u_ref, x0r_ref, x0i_ref, lre_ref, lim_ref, bre_ref, bim_ref, cre_ref, cim_ref, d_ref,
                      y_ref, xr_out_ref, xi_out_ref, xr_ref, xi_ref, *, seq):
    nb = x0r_ref.shape[0]
    u = u_ref[...]
    _s5_in(u.astype(BF16), bre_ref, bim_ref, xr_ref, xi_ref)
    for m in range(xr_ref.shape[0]):
        ls = slice(m * LANES, (m + 1) * LANES)
        lr, li = lre_ref[m], lim_ref[m]
        xr, xi = x0r_ref[:, ls], x0i_ref[:, ls]
        for r in range(seq):
            rows = pl.ds(r, nb, stride=seq)
            xr, xi = _s5_step(lr, li, xr, xi, xr_ref[m, rows, :], xi_ref[m, rows, :])
            xr_ref[m, rows, :] = xr
            xi_ref[m, rows, :] = xi
        xr_out_ref[:, ls] = xr
        xi_out_ref[:, ls] = xi
    kc = cre_ref.shape[2]
    d = d_ref[...]
    for j, y in enumerate(_s5_out(xr_ref, xi_ref, cre_ref, cim_ref)):
        cols = slice(j * kc, (j + 1) * kc)
        y_ref[:, cols] = y + d[:, cols] * u[:, cols]


def _s5_sample(proj, x0_re, x0_im, ws, *, row0, ucol, seq, w_b, nb=64):
    batch, gp = x0_re.shape
    n_slabs = gp // LANES
    assert batch % nb == 0 and row0 % (nb * seq) == 0
    t0 = row0 // (nb * seq)
    return pl.pallas_call(
        functools.partial(_s5_sample_kernel, seq=seq),
        grid=(batch // nb,),
        in_specs=[
            pl.BlockSpec((nb * seq, w_b), lambda p: (t0 + p, ucol)),
            pl.BlockSpec((nb, gp), lambda p: (p, 0)),
            pl.BlockSpec((nb, gp), lambda p: (p, 0)),
        ] + _s5_weights_specs(ws),
        out_specs=[
            pl.BlockSpec((nb * seq, w_b), lambda p: (p, 0)),
            pl.BlockSpec((nb, gp), lambda p: (p, 0)),
            pl.BlockSpec((nb, gp), lambda p: (p, 0)),
        ],
        out_shape=[
            jax.ShapeDtypeStruct((batch * seq, w_b), F32),
            jax.ShapeDtypeStruct((batch, gp), F32),
            jax.ShapeDtypeStruct((batch, gp), F32),
        ],
        scratch_shapes=[pltpu.VMEM((n_slabs, nb * seq, LANES), F32), pltpu.VMEM((n_slabs, nb * seq, LANES), F32)],
        compiler_params=_params("arbitrary"),
        name="s5_sample",
    )(proj, x0_re, x0_im, *ws)


def _mix_prologue(o_ref, gate_ref, y_ref, ng_ref, gluw_ref, glub_ref, a_ref, rows, chunk=128):
    w_a = o_ref.shape[1]

    def body(c, carry):
        rs = pl.ds(pl.multiple_of(c * chunk, chunk), chunk)
        o = o_ref[rs, :]
        var = jnp.mean(o * o, axis=-1, keepdims=True)
        gate = gate_ref[rs, :]
        a_ref[rs, :w_a] = (o * lax.rsqrt(var + EPS) * ng_ref[...] * (gate * jax.nn.sigmoid(gate))).astype(BF16)
        y = jax.nn.gelu(y_ref[rs, :])
        z = _dot(y.astype(BF16), gluw_ref[...]) + glub_ref[...]
        a_ref[rs, w_a:] = (y * jax.nn.sigmoid(z)).astype(BF16)
        return carry

    lax.fori_loop(0, rows // chunk, body, 0)


def _mix_kernel(o_ref, gate_ref, y_ref, x_ref, ng_ref, gluw_ref, glub_ref, wout_ref, out_ref, a_ref, *, rows):
    @pl.when(pl.program_id(1) == 0)
    def _():
        _mix_prologue(o_ref, gate_ref, y_ref, ng_ref, gluw_ref, glub_ref, a_ref, rows)

    out_ref[...] = x_ref[...] + _dot(a_ref[...], wout_ref[...])


def _mix(o, proj, y, x, norm_g, glu_w, glu_b, w_out, *, gate_col, name, tn=512):
    m, w_a = o.shape
    w_b = y.shape[1]
    d = w_out.shape[1]
    tm = _row_tile(m)
    assert m % tm == 0 and d % tn == 0
    row = lambda i, j: (i, 0)
    const = lambda i, j: (0, 0)
    return pl.pallas_call(
        functools.partial(_mix_kernel, rows=tm),
        grid=(m // tm, d // tn),
        in_specs=[
            pl.BlockSpec((tm, w_a), row),
            pl.BlockSpec((tm, w_a), lambda i, j: (i, gate_col)),
            pl.BlockSpec((tm, w_b), row),
            pl.BlockSpec((tm, tn), lambda i, j: (i, j)),
            pl.BlockSpec((1, w_a), const),
            pl.BlockSpec((w_b, w_b), const),
            pl.BlockSpec((1, w_b), const),
            pl.BlockSpec((w_a + w_b, tn), lambda i, j: (0, j)),
        ],
        out_specs=pl.BlockSpec((tm, tn), lambda i, j: (i, j)),
        out_shape=jax.ShapeDtypeStruct((m, d), F32),
        scratch_shapes=[pltpu.VMEM((tm, w_a + w_b), BF16)],
        compiler_params=_params("arbitrary", "arbitrary"),
        name=name,
    )(o, proj, y, x, norm_g.reshape(1, w_a), glu_w, glu_b.reshape(1, w_b), w_out)


def _mlp_kernel(x_ref, g_ref, up_ref, down_ref, gf_ref, o_ref, h_ref, *, rows, final_norm):
    f = pl.program_id(1)

    @pl.when(f == 0)
    def _():
        _rmsnorm_rows(x_ref, g_ref[...], h_ref, rows)
        o_ref[...] = x_ref[...]

    a = jnp.square(jnp.maximum(_dot(h_ref[...], up_ref[...]), 0.0)).astype(BF16)
    o_ref[...] += _dot(a, down_ref[...])

    if final_norm:
        @pl.when(f == pl.num_programs(1) - 1)
        def _():
            _rmsnorm_rows(o_ref, gf_ref[...], o_ref, rows)


def _mlp(x, g, up, down, gf, *, final_norm, name, tf=512):
    m, d = x.shape
    dff = up.shape[1]
    tm = _row_tile(m)
    assert m % tm == 0 and dff % tf == 0
    return pl.pallas_call(
        functools.partial(_mlp_kernel, rows=tm, final_norm=final_norm),
        grid=(m // tm, dff // tf),
        in_specs=[
            pl.BlockSpec((tm, d), lambda i, f: (i, 0)),
            pl.BlockSpec((1, d), lambda i, f: (0, 0)),
            pl.BlockSpec((d, tf), lambda i, f: (0, f)),
            pl.BlockSpec((tf, d), lambda i, f: (f, 0)),
            pl.BlockSpec((1, d), lambda i, f: (0, 0)),
        ],
        out_specs=pl.BlockSpec((tm, d), lambda i, f: (i, 0)),
        out_shape=jax.ShapeDtypeStruct((m, d), F32),
        scratch_shapes=[pltpu.VMEM((tm, d), BF16)],
        compiler_params=_params("arbitrary", "arbitrary"),
        name=name,
    )(x, g.reshape(1, d), up, down, gf.reshape(1, d))


def kernel(x_prompt, x_sample, state_hgrn, state_s5_re, state_s5_im, w_in, w_out, norm1_g, norm2_g, hgrn_lb_logits, hgrn_norm_g, s5_a_re, s5_a_im, s5_b_re, s5_b_im, s5_c_re, s5_c_im, s5_d, s5_log_step, glu_w, glu_b, mlp_up, mlp_down, final_norm_g):
    bp, seq, d = x_prompt.shape
    bs, dseq, _ = x_sample.shape
    depth = w_in.shape[0]
    w_a = hgrn_norm_g.shape[1]
    w_b = s5_d.shape[1]
    g_b, p_state = s5_a_re.shape[1], s5_a_re.shape[2]
    mp, ms = bp * seq, bs * dseq
    assert w_in.shape[2] == 4 * w_a + w_b and w_a == w_b and g_b * S5_GROUP == w_b
    gate_col, ucol = 3, (4 * w_a) // w_b

    xp = x_prompt.reshape(mp, d)
    xs = x_sample.reshape(ms, d)
    lb_logits = hgrn_lb_logits.astype(F32)
    new_states = []
    for l in range(depth):
        w_in_l, w_out_l, glu_w_l = w_in[l].astype(BF16), w_out[l].astype(BF16), glu_w[l].astype(BF16)
        up_l, down_l = mlp_up[l].astype(BF16), mlp_down[l].astype(BF16)
        proj_p = _inproj(xp, norm1_g[l], w_in_l, name="inproj_prompt")
        proj_s = _inproj(xs, norm1_g[l], w_in_l, name="inproj_sample")

        o_p, sh_p = _hgrn_prompt(proj_p, lb_logits, layer=l, batch=bp, seq=seq, w_a=w_a)
        o_s, sh_s = _hgrn_sample(proj_s, lb_logits, state_hgrn, layer=l, row0=0, batch=bs, seq=dseq, w_a=w_a)

        lam_re, lam_im, bb_re, bb_im = _s5_discretise(
            s5_a_re[l].astype(F32), s5_a_im[l].astype(F32), s5_log_step[l].astype(F32),
            s5_b_re[l].astype(F32), s5_b_im[l].astype(F32))
        ws = (
            lam_re.reshape(-1, 1, LANES), lam_im.reshape(-1, 1, LANES),
            _block_diag_in(bb_re).astype(BF16), _block_diag_in(bb_im).astype(BF16),
            _block_diag_out(s5_c_re[l].astype(F32)).astype(BF16), _block_diag_out(s5_c_im[l].astype(F32)).astype(BF16),
            s5_d[l].astype(F32).reshape(1, w_b),
        )
        y_p, sr_p, si_p = _s5_prompt(proj_p, ws, nb=bp, seq=seq, ucol=ucol, w_b=w_b)
        y_s, sr_s, si_s = _s5_sample(
            proj_s, state_s5_re[l].reshape(bs, g_b * p_state).astype(F32), state_s5_im[l].reshape(bs, g_b * p_state).astype(F32),
            ws, row0=0, ucol=ucol, seq=dseq, w_b=w_b)

        last = l == depth - 1
        groups = []
        for tag, x, o, proj, y in (("prompt", xp, o_p, proj_p, y_p), ("sample", xs, o_s, proj_s, y_s)):
            x1 = _mix(o, proj, y, x, hgrn_norm_g[l], glu_w_l, glu_b[l], w_out_l, gate_col=gate_col, name="mix_" + tag)
            groups.append(_mlp(x1, norm2_g[l], up_l, down_l, final_norm_g, final_norm=last, name="mlp_" + tag))
        xp, xs = groups
        new_states.append((sh_p, sr_p.reshape(bp, g_b, p_state), si_p.reshape(bp, g_b, p_state),
                           sh_s[0], sr_s.reshape(bs, g_b, p_state), si_s.reshape(bs, g_b, p_state)))

    y_prompt = xp.reshape(bp, seq, d).astype(x_prompt.dtype)
    y_sample = xs.reshape(bs, dseq, d).astype(x_sample.dtype)
    stacked = [jnp.stack([st[i] for st in new_states]) for i in range(6)]
    return (y_prompt, y_sample, *stacked)
```

```python
import functools

import jax
import jax.numpy as jnp
import numpy as np
from jax import lax
from jax.experimental import pallas as pl
from jax.experimental.pallas import tpu as pltpu

F32 = jnp.float32
BF16 = jnp.bfloat16
EPS = 1e-6

LANES = 128
SUBLANES = 8
MXU_DIM = 256
VMEM_LIMIT = 56 * 1024 * 1024

HEAD_DIM = 128
S5_GROUP = 16
HGRN_CHUNK = 64
ROW_TILE = 1024
NEG_BIG = -1e30


def _dot(a, b):
    return jnp.dot(a, b, preferred_element_type=F32)


def _dot_nt(a, b):
    return lax.dot_general(a, b, (((1,), (1,)), ((), ())), preferred_element_type=F32)


def _dot_tn(a, b):
    return lax.dot_general(a, b, (((0,), (0,)), ((), ())), preferred_element_type=F32)


def _split3(x):
    hi = x.astype(BF16)
    r1 = x - hi.astype(F32)
    mid = r1.astype(BF16)
    lo = (r1 - mid.astype(F32)).astype(BF16)
    return hi, mid, lo


def _params(*sem):
    return pltpu.CompilerParams(dimension_semantics=sem, vmem_limit_bytes=VMEM_LIMIT)


def _rmsnorm_rows(src_ref, g, dst_ref, rows, chunk=64):
    def body(c, carry):
        r0 = pl.multiple_of(c * chunk, chunk)
        x = src_ref[pl.ds(r0, chunk), :]
        var = jnp.mean(x * x, axis=-1, keepdims=True)
        dst_ref[pl.ds(r0, chunk), :] = (x * lax.rsqrt(var + EPS) * g).astype(dst_ref.dtype)
        return carry

    lax.fori_loop(0, rows // chunk, body, 0)


def _inproj_kernel(x_ref, g_ref, w_ref, o_ref, h_ref, *, rows):
    @pl.when(pl.program_id(1) == 0)
    def _():
        _rmsnorm_rows(x_ref, g_ref[...], h_ref, rows)

    o_ref[...] = _dot(h_ref[...], w_ref[...])


def _row_tile(m):
    return ROW_TILE if m % ROW_TILE == 0 else ROW_TILE // 2


def _inproj(x, g, w, *, name, tn=1024):
    m, d = x.shape
    n = w.shape[1]
    tm = _row_tile(m)
    assert m % tm == 0 and n % tn == 0
    return pl.pallas_call(
        functools.partial(_inproj_kernel, rows=tm),
        grid=(m // tm, n // tn),
        in_specs=[
            pl.BlockSpec((tm, d), lambda i, j: (i, 0)),
            pl.BlockSpec((1, d), lambda i, j: (0, 0)),
            pl.BlockSpec((d, tn), lambda i, j: (0, j)),
        ],
        out_specs=pl.BlockSpec((tm, tn), lambda i, j: (i, j)),
        out_shape=jax.ShapeDtypeStruct((m, n), F32),
        scratch_shapes=[pltpu.VMEM((tm, d), BF16)],
        compiler_params=_params("arbitrary", "arbitrary"),
        name=name,
    )(x, g.reshape(1, d), w)


def _lower_bound(logits, layer):
    m = jnp.max(logits, axis=0, keepdims=True)
    e = jnp.exp(logits - m)
    return jnp.sum(e[: layer + 1], axis=0, keepdims=True) / jnp.sum(e, axis=0, keepdims=True)


def _row_to_col(e_row):
    n = e_row.shape[1]
    hi, mid, lo = (p.astype(F32) for p in _split3(e_row))
    r = lax.broadcasted_iota(jnp.int32, (2 * SUBLANES, n), 0)
    pieces = jnp.where(r == 0, hi, jnp.where(r == 1, mid, jnp.where(r == 2, lo, 0.0)))
    return _dot_tn(pieces.astype(BF16), jnp.ones((2 * SUBLANES, LANES), BF16))


def _hgrn_gates(qp, fp, lb):
    q = qp * jax.nn.sigmoid(qp)
    f = lb + (1.0 - lb) * jax.nn.sigmoid(fp)
    return q, f, 1.0 - f, jnp.log(f)


def _hgrn_tables(c):
    t = np.arange(c)[:, None]
    j = np.arange(c)[None, :]
    sums, pairs = [j <= t], []
    bs = c
    while bs >= 2:
        hs = bs // 2
        ref = t - t % bs + hs - 1
        upper = t % bs >= hs
        sums.append(np.where(upper, (j > ref) & (j <= t), (j > t) & (j <= ref)))
        pairs.append((t // bs == j // bs) & upper & (j % bs < hs))
        bs = hs
    sums.append(j > t)
    pairs.append(t == j)
    sums = np.concatenate(sums, axis=0).astype(np.float32)
    return jnp.asarray(np.concatenate([sums] * 3, axis=1), BF16), jnp.asarray(np.stack(pairs), F32)


def _upper_q_lower_k(q, k, bs):
    c = q.shape[0]
    hs = bs // 2
    if hs % SUBLANES == 0:
        parts = []
        for r0 in range(0, c, bs):
            parts += [k[r0 : r0 + hs], q[r0 + hs : r0 + bs]]
        return jnp.concatenate(parts, axis=0)
    row = lax.broadcasted_iota(jnp.int32, (c, 1), 0)
    return jnp.where(row % bs >= hs, q, k)


def _hgrn_chunk(qp, fp, v, lb, st_ref, sums, pairs_ref, heads):
    c = qp.shape[0]
    levels = pairs_ref.shape[0] - 1
    q = qp * jax.nn.sigmoid(qp)
    f = lb + (1.0 - lb) * jax.nn.sigmoid(fp)
    k = 1.0 - f
    e = jnp.exp2(_dot(sums, jnp.concatenate(_split3(jnp.log2(f)), axis=0)))
    e_b, e_rest = e[:c], e[(levels + 1) * c :]
    qe = (q * e_b).astype(BF16)
    kd = (k * e_rest).astype(BF16)
    q16, k16, v16 = q.astype(BF16), k.astype(BF16), v.astype(BF16)
    ys = [(_upper_q_lower_k(q, k, c >> l) * e[(l + 1) * c : (l + 2) * c]).astype(BF16) for l in range(levels)]

    outs = []
    for h in range(heads):
        cols = slice(h * HEAD_DIM, (h + 1) * HEAD_DIM)
        a = _dot_nt(q16[:, cols], k16[:, cols]) * pairs_ref[levels]
        for l in range(levels):
            y = ys[l][:, cols]
            a = a + _dot_nt(y, y) * pairs_ref[l]
        st = st_ref[h]
        outs.append(_dot_nt(qe[:, cols], st.astype(BF16)) + _dot(a.astype(BF16), v16[:, cols]))
        st_ref[h] = e_b[c - 1 : c, cols] * st + _dot_tn(v16[:, cols], kd[:, cols])
    return jnp.concatenate(outs, axis=1)


def _hgrn_prompt_kernel(q_ref, f_ref, v_ref, lbl_ref, sums_ref, pairs_ref, o_ref, s_out_ref, st_ref, *,
                        layer, heads, n_chunks):
    t = pl.program_id(1)

    @pl.when(t == 0)
    def _():
        st_ref[...] = jnp.zeros_like(st_ref)

    lb = _lower_bound(lbl_ref[...], layer)

    def body(c, carry):
        rows = pl.ds(pl.multiple_of(c * HGRN_CHUNK, HGRN_CHUNK), HGRN_CHUNK)
        o_ref[rows, :] = _hgrn_chunk(q_ref[rows, :], f_ref[rows, :], v_ref[rows, :], lb, st_ref,
                                     sums_ref[...], pairs_ref, heads)
        return carry

    lax.fori_loop(0, n_chunks, body, 0)

    @pl.when(t == pl.num_programs(1) - 1)
    def _():
        for h in range(heads):
            s_out_ref[0, h] = st_ref[h].T


def _hgrn_prompt(proj, lb_logits, *, layer, batch, seq, w_a, tt=256):
    heads = w_a // HEAD_DIM
    assert seq % tt == 0 and tt % HGRN_CHUNK == 0
    nt = seq // tt
    sums, pairs = _hgrn_tables(HGRN_CHUNK)
    col = lambda c: pl.BlockSpec((tt, w_a), lambda b, t: (b * nt + t, c))
    const = lambda x: pl.BlockSpec(x.shape, lambda b, t: (0,) * x.ndim)
    return pl.pallas_call(
        functools.partial(_hgrn_prompt_kernel, layer=layer, heads=heads, n_chunks=tt // HGRN_CHUNK),
        grid=(batch, nt),
        in_specs=[col(0), col(1), col(2), const(lb_logits), const(sums), const(pairs)],
        out_specs=[
            pl.BlockSpec((tt, w_a), lambda b, t: (b * nt + t, 0)),
            pl.BlockSpec((1, heads, HEAD_DIM, HEAD_DIM), lambda b, t: (b, 0, 0, 0)),
        ],
        out_shape=[
            jax.ShapeDtypeStruct((batch * seq, w_a), F32),
            jax.ShapeDtypeStruct((batch, heads, HEAD_DIM, HEAD_DIM), F32),
        ],
        scratch_shapes=[pltpu.VMEM((heads, HEAD_DIM, HEAD_DIM), F32)],
        compiler_params=_params("arbitrary", "arbitrary"),
        name="hgrn_prompt",
    )(proj, proj, proj, lb_logits, sums, pairs)


def _hgrn_sample_kernel(q_ref, f_ref, v_ref, lbl_ref, s_in_ref, o_ref, s_out_ref, *, layer, heads, seq):
    n_seq = SUBLANES // seq
    lb = _lower_bound(lbl_ref[...], layer)
    row = lax.broadcasted_iota(jnp.int32, (SUBLANES, 1), 0)
    pos = row % seq
    for h in range(heads):
        cols = slice(h * HEAD_DIM, (h + 1) * HEAD_DIM)
        q, _, k, logf = _hgrn_gates(q_ref[:, cols], f_ref[:, cols], lb[:, cols])
        v = v_ref[:, cols]
        b = logf
        shift = 1
        while shift < seq:
            b = b + jnp.where(pos >= shift, pltpu.roll(b, shift, 0), 0.0)
            shift *= 2
        o = jnp.zeros((SUBLANES, HEAD_DIM), F32)
        for sl in range(SUBLANES):
            lo_t, hi_t = sl, (sl // seq + 1) * seq
            mask = jnp.logical_and(row >= lo_t, row < hi_t)
            z = q * jnp.exp(jnp.where(mask, b - b[sl : sl + 1], NEG_BIG)) * k[sl : sl + 1]
            o = o + jnp.sum(z, axis=-1, keepdims=True) * v[sl : sl + 1]
        v16 = v.astype(BF16)
        for n in range(n_seq):
            own = jnp.logical_and(row >= n * seq, row < (n + 1) * seq)
            s = s_in_ref[0, n, h]
            b_last = b[(n + 1) * seq - 1 : (n + 1) * seq, :]
            qe = jnp.where(own, q * jnp.exp(b), 0.0).astype(BF16)
            o = o + _dot(qe, s.astype(BF16))
            kd = jnp.where(own, k * jnp.exp(jnp.where(own, b_last - b, 0.0)), 0.0).astype(BF16)
            s_out_ref[0, n, h] = _row_to_col(jnp.exp(b_last)) * s + _dot_tn(kd, v16)
        o_ref[:, cols] = o


def _hgrn_sample(proj, lb_logits, state, *, layer, row0, batch, seq, w_a):
    heads = w_a // HEAD_DIM
    assert SUBLANES % seq == 0 and row0 % SUBLANES == 0 and (batch * seq) % SUBLANES == 0
    n_seq = SUBLANES // seq
    t0 = row0 // SUBLANES
    col = lambda c: pl.BlockSpec((SUBLANES, w_a), lambda p: (t0 + p, c))
    st = pl.BlockSpec((1, n_seq, heads, HEAD_DIM, HEAD_DIM), lambda p: (layer, p, 0, 0, 0))
    st_out = pl.BlockSpec((1, n_seq, heads, HEAD_DIM, HEAD_DIM), lambda p: (0, p, 0, 0, 0))
    return pl.pallas_call(
        functools.partial(_hgrn_sample_kernel, layer=layer, heads=heads, seq=seq),
        grid=(batch // n_seq,),
        in_specs=[col(0), col(1), col(2), pl.BlockSpec(lb_logits.shape, lambda p: (0, 0)), st],
        out_specs=[pl.BlockSpec((SUBLANES, w_a), lambda p: (p, 0)), st_out],
        out_shape=[
            jax.ShapeDtypeStruct((batch * seq, w_a), F32),
            jax.ShapeDtypeStruct((1, batch, heads, HEAD_DIM, HEAD_DIM), F32),
        ],
        compiler_params=_params("arbitrary"),
        name="hgrn_sample",
    )(proj, proj, proj, lb_logits, state)


def _s5_disc_kernel(are_ref, aim_ref, ls_ref, bre_ref, bim_ref, lre_ref, lim_ref, bbre_ref, bbim_ref):
    a_re = are_ref[...]
    a_im = aim_ref[...]
    dt = jnp.exp(ls_ref[...])
    mag = jnp.exp(a_re * dt)
    lam_re = mag * jnp.cos(a_im * dt)
    lam_im = mag * jnp.sin(a_im * dt)
    den = a_re * a_re + a_im * a_im
    nr, ni = lam_re - 1.0, lam_im
    r_re = (nr * a_re + ni * a_im) / den
    r_im = (ni * a_re - nr * a_im) / den
    lre_ref[...] = lam_re
    lim_ref[...] = lam_im
    for c in range(bre_ref.shape[0]):
        b_re = bre_ref[c]
        b_im = bim_ref[c]
        bbre_ref[c] = r_re * b_re - r_im * b_im
        bbim_ref[c] = r_re * b_im + r_im * b_re


def _s5_discretise(a_re, a_im, log_step, b_re, b_im):
    g, p = a_re.shape
    c = b_re.shape[-1]
    bt = lambda x: jnp.transpose(x, (2, 0, 1))
    return pl.pallas_call(
        _s5_disc_kernel,
        out_shape=[
            jax.ShapeDtypeStruct((g, p), F32),
            jax.ShapeDtypeStruct((g, p), F32),
            jax.ShapeDtypeStruct((c, g, p), F32),
            jax.ShapeDtypeStruct((c, g, p), F32),
        ],
        name="s5_discretise",
    )(a_re, a_im, log_step.reshape(g, 1), bt(b_re), bt(b_im))


def _block_diag_in(bb):
    c, g, p = bb.shape
    gt = MXU_DIM // c
    x = bb.reshape(c, g // gt, gt, p)
    eye = jnp.eye(gt, dtype=bb.dtype)
    return jnp.einsum("cjgp,gh->jgchp", x, eye).reshape(g // gt, gt * c, gt * p)


def _block_diag_out(cc):
    g, c, p = cc.shape
    gt = MXU_DIM // c
    x = cc.reshape(g // gt, gt, c, p)
    eye = jnp.eye(gt, dtype=cc.dtype)
    return jnp.einsum("jgcp,gh->jgphc", x, eye).reshape(g // gt, gt * p, gt * c)


def _s5_lanes(x_ref, lo, hi):
    if len(x_ref.shape) == 2:
        return x_ref[:, lo:hi]
    return jnp.concatenate([x_ref[m] for m in range(lo // LANES, hi // LANES)], axis=1)


def _s5_in(ub, bre_ref, bim_ref, xr_ref, xi_ref):
    nj, kc, kp = bre_ref.shape
    for j in range(nj):
        uj = ub[:, j * kc : (j + 1) * kc]
        for x_ref, w_ref in ((xr_ref, bre_ref), (xi_ref, bim_ref)):
            bu = _dot(uj, w_ref[j])
            if len(x_ref.shape) == 2:
                x_ref[:, j * kp : (j + 1) * kp] = bu
            else:
                for m in range(kp // LANES):
                    x_ref[j * kp // LANES + m] = bu[:, m * LANES : (m + 1) * LANES]


def _s5_out(xr_ref, xi_ref, cre_ref, cim_ref):
    nj, kp, kc = cre_ref.shape
    tiles = []
    for j in range(nj):
        xr = _s5_lanes(xr_ref, j * kp, (j + 1) * kp).astype(BF16)
        xi = _s5_lanes(xi_ref, j * kp, (j + 1) * kp).astype(BF16)
        tiles.append(_dot(xr, cre_ref[j]) - _dot(xi, cim_ref[j]))
    return tiles


def _s5_step(lr, li, xr, xi, bur, bui):
    return lr * xr - li * xi + bur, lr * xi + li * xr + bui


def _s5_lambda(lre_ref, lim_ref, lo, hi, rows):
    cat = lambda ref: jnp.concatenate([ref[m] for m in range(lo // LANES, hi // LANES)], axis=1)
    return jnp.broadcast_to(cat(lre_ref), (rows, hi - lo)), jnp.broadcast_to(cat(lim_ref), (rows, hi - lo))


def _to_time_major(x):
    n, ts, w = x.shape
    return jnp.swapaxes(x, 0, 1).reshape(ts * n, w)


def _from_time_major(x, n):
    rows, w = x.shape
    return jnp.swapaxes(x.reshape(rows // n, n, w), 0, 1)


def _s5_prompt_kernel(*refs, nv, lane_block):
    u_refs = refs[:nv]
    (lre_ref, lim_ref, bre_ref, bim_ref, cre_ref, cim_ref, d_ref,
     y_ref, xr_out_ref, xi_out_ref, xr_ref, xi_ref, sr_ref, si_ref) = refs[nv:]
    ts = u_refs[0].shape[0]
    gp = xr_ref.shape[1]
    step = pl.program_id(0)

    @pl.when(step == 0)
    def _():
        sr_ref[...] = jnp.zeros_like(sr_ref)
        si_ref[...] = jnp.zeros_like(si_ref)

    u = _to_time_major(jnp.stack([r[...] for r in u_refs], axis=0))
    _s5_in(u.astype(BF16), bre_ref, bim_ref, xr_ref, xi_ref)

    for lo in range(0, gp, lane_block):
        ls = slice(lo, lo + lane_block)
        lr, li = _s5_lambda(lre_ref, lim_ref, lo, lo + lane_block, nv)

        def body(r, carry):
            rows = pl.ds(pl.multiple_of(r * nv, nv), nv)
            xr, xi = _s5_step(lr, li, *carry, xr_ref[rows, ls], xi_ref[rows, ls])
            xr_ref[rows, ls] = xr
            xi_ref[rows, ls] = xi
            return xr, xi

        xr, xi = lax.fori_loop(0, ts, body, (sr_ref[:, ls], si_ref[:, ls]), unroll=4)
        sr_ref[:, ls] = xr
        si_ref[:, ls] = xi

    kc = cre_ref.shape[2]
    d = d_ref[...]
    for j, y in enumerate(_s5_out(xr_ref, xi_ref, cre_ref, cim_ref)):
        cols = slice(j * kc, (j + 1) * kc)
        y_ref[:, :, cols] = _from_time_major(y + d[:, cols] * u[:, cols], nv)

    @pl.when(step == pl.num_programs(0) - 1)
    def _():
        xr_out_ref[...] = sr_ref[...]
        xi_out_ref[...] = si_ref[...]


def _s5_carry_kernel(y_ref, fr_ref, fi_ref, lre_ref, lim_ref, cre_ref, cim_ref,
                     o_ref, xr_out_ref, xi_out_ref, zr_ref, zi_ref, sr_ref, si_ref, *, half_len, lane_block):
    nrow, gp = fr_ref.shape
    tf = y_ref.shape[2]
    step = pl.program_id(0)
    odd = lax.broadcasted_iota(jnp.int32, (nrow, 1), 0) % 2 == 1

    @pl.when(step == 0)
    def _():
        for lo in range(0, gp, lane_block):
            ls = slice(lo, lo + lane_block)
            pr, pi = _s5_lambda(lre_ref, lim_ref, lo, lo + lane_block, nrow)
            n = 1
            while n < half_len:
                pr, pi = pr * pr - pi * pi, 2.0 * pr * pi
                n *= 2
            fr, fi = fr_ref[:, ls], fi_ref[:, ls]
            fr = jnp.where(odd, pltpu.roll(fr, 1, 0), fr)
            fi = jnp.where(odd, pltpu.roll(fi, 1, 0), fi)
            sr_ref[:, ls] = jnp.where(odd, pr * fr - pi * fi, fr)
            si_ref[:, ls] = jnp.where(odd, pr * fi + pi * fr, fi)

    for lo in range(0, gp, lane_block):
        ls = slice(lo, lo + lane_block)
        lr, li = _s5_lambda(lre_ref, lim_ref, lo, lo + lane_block, nrow)

        def body(r, carry):
            rows = pl.ds(pl.multiple_of(r * nrow, nrow), nrow)
            zr, zi = carry
            zr, zi = lr * zr - li * zi, lr * zi + li * zr
            zr_ref[rows, ls] = zr
            zi_ref[rows, ls] = zi
            return zr, zi

        zr, zi = lax.fori_loop(0, tf, body, (sr_ref[:, ls], si_ref[:, ls]), unroll=4)
        sr_ref[:, ls] = zr
        si_ref[:, ls] = zi

    kc = cre_ref.shape[2]
    nseq = y_ref.shape[0]
    for j, y in enumerate(_s5_out(zr_ref, zi_ref, cre_ref, cim_ref)):
        cols = slice(j * kc, (j + 1) * kc)
        o_ref[:, :, :, cols] = y_ref[:, :, :, cols] + _from_time_major(y, nrow).reshape(nseq, nrow // nseq, tf, kc)

    @pl.when(step == pl.num_programs(0) - 1)
    def _():
        xr_out_ref[...] = fr_ref[...] + sr_ref[...]
        xi_out_ref[...] = fi_ref[...] + si_ref[...]


def _s5_weights_specs(ws):
    zero = lambda n: (lambda *_: (0,) * n)
    return [pl.BlockSpec(w.shape, zero(w.ndim)) for w in ws]


def _s5_prompt(proj, ws, *, nb, seq, ucol, w_b, ts=64, tf=64, lane_block=1024):
    lam_re, lam_im, bre, bim, cre, cim, d = ws
    gp = lam_re.shape[0] * LANES
    nv = SUBLANES
    assert nv == 2 * nb, "two pieces per sequence"
    plen, half_len = seq // 2, seq // 4
    assert plen % ts == 0 and half_len % tf == 0 and gp % lane_block == 0
    assert half_len & (half_len - 1) == 0, "repeated squaring needs a power of two"
    nblk = plen // ts
    u_spec = lambda n: pl.BlockSpec((ts, w_b), lambda s: (n * nblk + s, ucol))
    state = pl.BlockSpec((nv, gp), lambda s: (0, 0))
    state_shape = jax.ShapeDtypeStruct((nv, gp), F32)
    scratch = lambda steps: [pltpu.VMEM((nv * steps, gp), F32)] * 2 + [pltpu.VMEM((nv, gp), F32)] * 2
    y, fr, fi = pl.pallas_call(
        functools.partial(_s5_prompt_kernel, nv=nv, lane_block=lane_block),
        grid=(nblk,),
        in_specs=[u_spec(n) for n in range(nv)] + _s5_weights_specs(ws),
        out_specs=[pl.BlockSpec((nv, ts, w_b), lambda s: (0, s, 0)), state, state],
        out_shape=[jax.ShapeDtypeStruct((nv, plen, w_b), F32), state_shape, state_shape],
        scratch_shapes=scratch(ts),
        compiler_params=_params("arbitrary"),
        name="s5_prompt",
    )(*([proj] * nv), *ws)

    y5 = y.reshape(nb, 2, 2, half_len, w_b)
    y_spec = pl.BlockSpec((nb, None, 2, tf, w_b), lambda s: (0, 1, 0, s, 0))
    y5, xr, xi = pl.pallas_call(
        functools.partial(_s5_carry_kernel, half_len=half_len, lane_block=lane_block),
        grid=(half_len // tf,),
        in_specs=[y_spec, state, state] + _s5_weights_specs((lam_re, lam_im, cre, cim)),
        out_specs=[y_spec, state, state],
        out_shape=[jax.ShapeDtypeStruct(y5.shape, F32), state_shape, state_shape],
        scratch_shapes=scratch(tf),
        input_output_aliases={0: 0},
        compiler_params=_params("arbitrary"),
        name="s5_prompt_carry",
    )(y5, fr, fi, lam_re, lam_im, cre, cim)
    last = lambda x: x.reshape(nb, 2, gp)[:, 1]
    return y5.reshape(nb * seq, w_b), last(xr), last(xi)


def _s5_sample_kernel(u_ref, x0r_ref, x0i_ref, lre_ref, lim_ref, bre_ref, bim_ref, cre_ref, cim_ref, d_ref,
                      y_ref, xr_out_ref, xi_out_ref, xr_ref, xi_ref, *, seq):
    nb = x0r_ref.shape[0]
    u = u_ref[...]
    _s5_in(u.astype(BF16), bre_ref, bim_ref, xr_ref, xi_ref)
    for m in range(xr_ref.shape[0]):
        ls = slice(m * LANES, (m + 1) * LANES)
        lr, li = lre_ref[m], lim_ref[m]
        xr, xi = x0r_ref[:, ls], x0i_ref[:, ls]
        for r in range(seq):
            rows = pl.ds(r, nb, stride=seq)
            xr, xi = _s5_step(lr, li, xr, xi, xr_ref[m, rows, :], xi_ref[m, rows, :])
            xr_ref[m, rows, :] = xr
            xi_ref[m, rows, :] = xi
        xr_out_ref[:, ls] = xr
        xi_out_ref[:, ls] = xi
    kc = cre_ref.shape[2]
    d = d_ref[...]
    for j, y in enumerate(_s5_out(xr_ref, xi_ref, cre_ref, cim_ref)):
        cols = slice(j * kc, (j + 1) * kc)
        y_ref[:, cols] = y + d[:, cols] * u[:, cols]


def _s5_sample(proj, x0_re, x0_im, ws, *, row0, ucol, seq, w_b, nb=64):
    batch, gp = x0_re.shape
    n_slabs = gp // LANES
    assert batch % nb == 0 and row0 % (nb * seq) == 0
    t0 = row0 // (nb * seq)
    return pl.pallas_call(
        functools.partial(_s5_sample_kernel, seq=seq),
        grid=(batch // nb,),
        in_specs=[
            pl.BlockSpec((nb * seq, w_b), lambda p: (t0 + p, ucol)),
            pl.BlockSpec((nb, gp), lambda p: (p, 0)),
            pl.BlockSpec((nb, gp), lambda p: (p, 0)),
        ] + _s5_weights_specs(ws),
        out_specs=[
            pl.BlockSpec((nb * seq, w_b), lambda p: (p, 0)),
            pl.BlockSpec((nb, gp), lambda p: (p, 0)),
            pl.BlockSpec((nb, gp), lambda p: (p, 0)),
        ],
        out_shape=[
            jax.ShapeDtypeStruct((batch * seq, w_b), F32),
            jax.ShapeDtypeStruct((batch, gp), F32),
            jax.ShapeDtypeStruct((batch, gp), F32),
        ],
        scratch_shapes=[pltpu.VMEM((n_slabs, nb * seq, LANES), F32), pltpu.VMEM((n_slabs, nb * seq, LANES), F32)],
        compiler_params=_params("arbitrary"),
        name="s5_sample",
    )(proj, x0_re, x0_im, *ws)


def _mix_prologue(o_ref, gate_ref, y_ref, ng_ref, gluw_ref, glub_ref, a_ref, rows, chunk=128):
    w_a = o_ref.shape[1]

    def body(c, carry):
        rs = pl.ds(pl.multiple_of(c * chunk, chunk), chunk)
        o = o_ref[rs, :]
        var = jnp.mean(o * o, axis=-1, keepdims=True)
        gate = gate_ref[rs, :]
        a_ref[rs, :w_a] = (o * lax.rsqrt(var + EPS) * ng_ref[...] * (gate * jax.nn.sigmoid(gate))).astype(BF16)
        y = jax.nn.gelu(y_ref[rs, :])
        z = _dot(y.astype(BF16), gluw_ref[...]) + glub_ref[...]
        a_ref[rs, w_a:] = (y * jax.nn.sigmoid(z)).astype(BF16)
        return carry

    lax.fori_loop(0, rows // chunk, body, 0)


def _mix_kernel(o_ref, gate_ref, y_ref, x_ref, ng_ref, gluw_ref, glub_ref, wout_ref, out_ref, a_ref, *, rows):
    @pl.when(pl.program_id(1) == 0)
    def _():
        _mix_prologue(o_ref, gate_ref, y_ref, ng_ref, gluw_ref, glub_ref, a_ref, rows)

    out_ref[...] = x_ref[...] + _dot(a_ref[...], wout_ref[...])


def _mix(o, proj, y, x, norm_g, glu_w, glu_b, w_out, *, gate_col, name, tn=512):
    m, w_a = o.shape
    w_b = y.shape[1]
    d = w_out.shape[1]
    tm = _row_tile(m)
    assert m % tm == 0 and d % tn == 0
    row = lambda i, j: (i, 0)
    const = lambda i, j: (0, 0)
    return pl.pallas_call(
        functools.partial(_mix_kernel, rows=tm),
        grid=(m // tm, d // tn),
        in_specs=[
            pl.BlockSpec((tm, w_a), row),
            pl.BlockSpec((tm, w_a), lambda i, j: (i, gate_col)),
            pl.BlockSpec((tm, w_b), row),
            pl.BlockSpec((tm, tn), lambda i, j: (i, j)),
            pl.BlockSpec((1, w_a), const),
            pl.BlockSpec((w_b, w_b), const),
            pl.BlockSpec((1, w_b), const),
            pl.BlockSpec((w_a + w_b, tn), lambda i, j: (0, j)),
        ],
        out_specs=pl.BlockSpec((tm, tn), lambda i, j: (i, j)),
        out_shape=jax.ShapeDtypeStruct((m, d), F32),
        scratch_shapes=[pltpu.VMEM((tm, w_a + w_b), BF16)],
        compiler_params=_params("arbitrary", "arbitrary"),
        name=name,
    )(o, proj, y, x, norm_g.reshape(1, w_a), glu_w, glu_b.reshape(1, w_b), w_out)


def _mlp_kernel(x_ref, g_ref, up_ref, down_ref, gf_ref, o_ref, h_ref, *, rows, final_norm):
    f = pl.program_id(1)

    @pl.when(f == 0)
    def _():
        _rmsnorm_rows(x_ref, g_ref[...], h_ref, rows)
        o_ref[...] = x_ref[...]

    a = jnp.square(jnp.maximum(_dot(h_ref[...], up_ref[...]), 0.0)).astype(BF16)
    o_ref[...] += _dot(a, down_ref[...])

    if final_norm:
        @pl.when(f == pl.num_programs(1) - 1)
        def _():
            _rmsnorm_rows(o_ref, gf_ref[...], o_ref, rows)


def _mlp(x, g, up, down, gf, *, final_norm, name, tf=512):
    m, d = x.shape
    dff = up.shape[1]
    tm = _row_tile(m)
    assert m % tm == 0 and dff % tf == 0
    return pl.pallas_call(
        functools.partial(_mlp_kernel, rows=tm, final_norm=final_norm),
        grid=(m // tm, dff // tf),
        in_specs=[
            pl.BlockSpec((tm, d), lambda i, f: (i, 0)),
            pl.BlockSpec((1, d), lambda i, f: (0, 0)),
            pl.BlockSpec((d, tf), lambda i, f: (0, f)),
            pl.BlockSpec((tf, d), lambda i, f: (f, 0)),
            pl.BlockSpec((1, d), lambda i, f: (0, 0)),
        ],
        out_specs=pl.BlockSpec((tm, d), lambda i, f: (i, 0)),
        out_shape=jax.ShapeDtypeStruct((m, d), F32),
        scratch_shapes=[pltpu.VMEM((tm, d), BF16)],
        compiler_params=_params("arbitrary", "arbitrary"),
        name=name,
    )(x, g.reshape(1, d), up, down, gf.reshape(1, d))


def kernel(x_prompt, x_sample, state_hgrn, state_s5_re, state_s5_im, w_in, w_out, norm1_g, norm2_g, hgrn_lb_logits, hgrn_norm_g, s5_a_re, s5_a_im, s5_b_re, s5_b_im, s5_c_re, s5_c_im, s5_d, s5_log_step, glu_w, glu_b, mlp_up, mlp_down, final_norm_g):
    bp, seq, d = x_prompt.shape
    bs, dseq, _ = x_sample.shape
    depth = w_in.shape[0]
    w_a = hgrn_norm_g.shape[1]
    w_b = s5_d.shape[1]
    g_b, p_state = s5_a_re.shape[1], s5_a_re.shape[2]
    mp, ms = bp * seq, bs * dseq
    assert w_in.shape[2] == 4 * w_a + w_b and w_a == w_b and g_b * S5_GROUP == w_b
    gate_col, ucol = 3, (4 * w_a) // w_b

    xp = x_prompt.reshape(mp, d)
    xs = x_sample.reshape(ms, d)
    lb_logits = hgrn_lb_logits.astype(F32)
    new_states = []
    for l in range(depth):
        w_in_l, w_out_l, glu_w_l = w_in[l].astype(BF16), w_out[l].astype(BF16), glu_w[l].astype(BF16)
        up_l, down_l = mlp_up[l].astype(BF16), mlp_down[l].astype(BF16)
        proj_p = _inproj(xp, norm1_g[l], w_in_l, name="inproj_prompt")
        proj_s = _inproj(xs, norm1_g[l], w_in_l, name="inproj_sample")

        o_p, sh_p = _hgrn_prompt(proj_p, lb_logits, layer=l, batch=bp, seq=seq, w_a=w_a)
        o_s, sh_s = _hgrn_sample(proj_s, lb_logits, state_hgrn, layer=l, row0=0, batch=bs, seq=dseq, w_a=w_a)

        lam_re, lam_im, bb_re, bb_im = _s5_discretise(
            s5_a_re[l].astype(F32), s5_a_im[l].astype(F32), s5_log_step[l].astype(F32),
            s5_b_re[l].astype(F32), s5_b_im[l].astype(F32))
        ws = (
            lam_re.reshape(-1, 1, LANES), lam_im.reshape(-1, 1, LANES),
            _block_diag_in(bb_re).astype(BF16), _block_diag_in(bb_im).astype(BF16),
            _block_diag_out(s5_c_re[l].astype(F32)).astype(BF16), _block_diag_out(s5_c_im[l].astype(F32)).astype(BF16),
            s5_d[l].astype(F32).reshape(1, w_b),
        )
        y_p, sr_p, si_p = _s5_prompt(proj_p, ws, nb=bp, seq=seq, ucol=ucol, w_b=w_b)
        y_s, sr_s, si_s = _s5_sample(
            proj_s, state_s5_re[l].reshape(bs, g_b * p_state).astype(F32), state_s5_im[l].reshape(bs, g_b * p_state).astype(F32),
            ws, row0=0, ucol=ucol, seq=dseq, w_b=w_b)

        last = l == depth - 1
        groups = []
        for tag, x, o, proj, y in (("prompt", xp, o_p, proj_p, y_p), ("sample", xs, o_s, proj_s, y_s)):
            x1 = _mix(o, proj, y, x, hgrn_norm_g[l], glu_w_l, glu_b[l], w_out_l, gate_col=gate_col, name="mix_" + tag)
            groups.append(_mlp(x1, norm2_g[l], up_l, down_l, final_norm_g, final_norm=last, name="mlp_" + tag))
        xp, xs = groups
        new_states.append((sh_p, sr_p.reshape(bp, g_b, p_state), si_p.reshape(bp, g_b, p_state),
                           sh_s[0], sr_s.reshape(bs, g_b, p_state), si_s.reshape(bs, g_b, p_state)))

    y_prompt = xp.reshape(bp, seq, d).astype(x_prompt.dtype)
    y_sample = xs.reshape(bs, dseq, d).astype(x_sample.dtype)
    stacked = [jnp.stack([st[i] for st in new_states]) for i in range(6)]
    return (y_prompt, y_sample, *stacked)
```

```python
import functools

import jax
import jax.numpy as jnp
import numpy as np
from jax import lax
from jax.experimental import pallas as pl
from jax.experimental.pallas import tpu as pltpu

F32 = jnp.float32
BF16 = jnp.bfloat16
EPS = 1e-6

LANES = 128
SUBLANES = 8
MXU_DIM = 256
VMEM_LIMIT = 56 * 1024 * 1024

HEAD_DIM = 128
S5_GROUP = 16
HGRN_CHUNK = 64
ROW_TILE = 1024
NEG_BIG = -1e30


def _dot(a, b):
    return jnp.dot(a, b, preferred_element_type=F32)


def _dot_nt(a, b):
    return lax.dot_general(a, b, (((1,), (1,)), ((), ())), preferred_element_type=F32)


def _dot_tn(a, b):
    return lax.dot_general(a, b, (((0,), (0,)), ((), ())), preferred_element_type=F32)


def _split3(x):
    hi = x.astype(BF16)
    r1 = x - hi.astype(F32)
    mid = r1.astype(BF16)
    lo = (r1 - mid.astype(F32)).astype(BF16)
    return hi, mid, lo


def _params(*sem):
    return pltpu.CompilerParams(dimension_semantics=sem, vmem_limit_bytes=VMEM_LIMIT)


def _rmsnorm_rows(src_ref, g, dst_ref, rows, chunk=64):
    def body(c, carry):
        r0 = pl.multiple_of(c * chunk, chunk)
        x = src_ref[pl.ds(r0, chunk), :]
        var = jnp.mean(x * x, axis=-1, keepdims=True)
        dst_ref[pl.ds(r0, chunk), :] = (x * lax.rsqrt(var + EPS) * g).astype(dst_ref.dtype)
        return carry

    lax.fori_loop(0, rows // chunk, body, 0)


def _inproj_kernel(x_ref, g_ref, w_ref, o_ref, h_ref, *, rows):
    @pl.when(pl.program_id(1) == 0)
    def _():
        _rmsnorm_rows(x_ref, g_ref[...], h_ref, rows)

    o_ref[...] = _dot(h_ref[...], w_ref[...])


def _row_tile(m):
    return ROW_TILE if m % ROW_TILE == 0 else ROW_TILE // 2


def _inproj(x, g, w, *, name, tn=1024):
    m, d = x.shape
    n = w.shape[1]
    tm = _row_tile(m)
    assert m % tm == 0 and n % tn == 0
    return pl.pallas_call(
        functools.partial(_inproj_kernel, rows=tm),
        grid=(m // tm, n // tn),
        in_specs=[
            pl.BlockSpec((tm, d), lambda i, j: (i, 0)),
            pl.BlockSpec((1, d), lambda i, j: (0, 0)),
            pl.BlockSpec((d, tn), lambda i, j: (0, j)),
        ],
        out_specs=pl.BlockSpec((tm, tn), lambda i, j: (i, j)),
        out_shape=jax.ShapeDtypeStruct((m, n), F32),
        scratch_shapes=[pltpu.VMEM((tm, d), BF16)],
        compiler_params=_params("arbitrary", "arbitrary"),
        name=name,
    )(x, g.reshape(1, d), w)


def _lower_bound(logits, layer):
    m = jnp.max(logits, axis=0, keepdims=True)
    e = jnp.exp(logits - m)
    return jnp.sum(e[: layer + 1], axis=0, keepdims=True) / jnp.sum(e, axis=0, keepdims=True)


def _row_to_col(e_row):
    n = e_row.shape[1]
    hi, mid, lo = (p.astype(F32) for p in _split3(e_row))
    r = lax.broadcasted_iota(jnp.int32, (2 * SUBLANES, n), 0)
    pieces = jnp.where(r == 0, hi, jnp.where(r == 1, mid, jnp.where(r == 2, lo, 0.0)))
    return _dot_tn(pieces.astype(BF16), jnp.ones((2 * SUBLANES, LANES), BF16))


def _hgrn_gates(qp, fp, lb):
    q = qp * jax.nn.sigmoid(qp)
    f = lb + (1.0 - lb) * jax.nn.sigmoid(fp)
    return q, f, 1.0 - f, jnp.log(f)


def _hgrn_tables(c):
    t = np.arange(c)[:, None]
    j = np.arange(c)[None, :]
    sums, pairs = [j <= t], []
    bs = c
    while bs >= 2:
        hs = bs // 2
        ref = t - t % bs + hs - 1
        upper = t % bs >= hs
        sums.append(np.where(upper, (j > ref) & (j <= t), (j > t) & (j <= ref)))
        pairs.append((t // bs == j // bs) & upper & (j % bs < hs))
        bs = hs
    sums.append(j > t)
    pairs.append(t == j)
    sums = np.concatenate(sums, axis=0).astype(np.float32)
    return jnp.asarray(np.concatenate([sums] * 3, axis=1), BF16), jnp.asarray(np.stack(pairs), F32)


def _upper_q_lower_k(q, k, bs):
    c = q.shape[0]
    hs = bs // 2
    if hs % SUBLANES == 0:
        parts = []
        for r0 in range(0, c, bs):
            parts += [k[r0 : r0 + hs], q[r0 + hs : r0 + bs]]
        return jnp.concatenate(parts, axis=0)
    row = lax.broadcasted_iota(jnp.int32, (c, 1), 0)
    return jnp.where(row % bs >= hs, q, k)


def _hgrn_chunk(qp, fp, v, lb, st_ref, sums, pairs_ref, heads):
    c = qp.shape[0]
    levels = pairs_ref.shape[0] - 1
    q = qp * jax.nn.sigmoid(qp)
    f = lb + (1.0 - lb) * jax.nn.sigmoid(fp)
    k = 1.0 - f
    e = jnp.exp2(_dot(sums, jnp.concatenate(_split3(jnp.log2(f)), axis=0)))
    e_b, e_rest = e[:c], e[(levels + 1) * c :]
    qe = (q * e_b).astype(BF16)
    kd = (k * e_rest).astype(BF16)
    q16, k16, v16 = q.astype(BF16), k.astype(BF16), v.astype(BF16)
    ys = [(_upper_q_lower_k(q, k, c >> l) * e[(l + 1) * c : (l + 2) * c]).astype(BF16) for l in range(levels)]

    outs = []
    for h in range(heads):
        cols = slice(h * HEAD_DIM, (h + 1) * HEAD_DIM)
        a = _dot_nt(q16[:, cols], k16[:, cols]) * pairs_ref[levels]
        for l in range(levels):
            y = ys[l][:, cols]
            a = a + _dot_nt(y, y) * pairs_ref[l]
        st = st_ref[h]
        outs.append(_dot_nt(qe[:, cols], st.astype(BF16)) + _dot(a.astype(BF16), v16[:, cols]))
        st_ref[h] = e_b[c - 1 : c, cols] * st + _dot_tn(v16[:, cols], kd[:, cols])
    return jnp.concatenate(outs, axis=1)


def _hgrn_prompt_kernel(q_ref, f_ref, v_ref, lbl_ref, sums_ref, pairs_ref, o_ref, s_out_ref, st_ref, *,
                        layer, heads, n_chunks):
    t = pl.program_id(1)

    @pl.when(t == 0)
    def _():
        st_ref[...] = jnp.zeros_like(st_ref)

    lb = _lower_bound(lbl_ref[...], layer)
    n_seq = q_ref.shape[0]

    def body(c, carry):
        rows = pl.ds(pl.multiple_of(c * HGRN_CHUNK, HGRN_CHUNK), HGRN_CHUNK)
        for n in range(n_seq):
            o_ref[n, rows, :] = _hgrn_chunk(q_ref[n, rows, :], f_ref[n, rows, :], v_ref[n, rows, :], lb,
                                            st_ref.at[n], sums_ref[...], pairs_ref, heads)
        return carry

    lax.fori_loop(0, n_chunks, body, 0)

    @pl.when(t == pl.num_programs(1) - 1)
    def _():
        for n in range(n_seq):
            for h in range(heads):
                s_out_ref[n, h] = st_ref[n, h].T


def _hgrn_prompt(proj, lb_logits, *, layer, batch, seq, w_a, tt=256, n_seq=2):
    heads = w_a // HEAD_DIM
    assert seq % tt == 0 and tt % HGRN_CHUNK == 0 and batch % n_seq == 0
    sums, pairs = _hgrn_tables(HGRN_CHUNK)
    proj3 = proj.reshape(batch, seq, proj.shape[1])
    col = lambda c: pl.BlockSpec((n_seq, tt, w_a), lambda b, t: (b, t, c))
    const = lambda x: pl.BlockSpec(x.shape, lambda b, t: (0,) * x.ndim)
    o, s = pl.pallas_call(
        functools.partial(_hgrn_prompt_kernel, layer=layer, heads=heads, n_chunks=tt // HGRN_CHUNK),
        grid=(batch // n_seq, seq // tt),
        in_specs=[col(0), col(1), col(2), const(lb_logits), const(sums), const(pairs)],
        out_specs=[
            pl.BlockSpec((n_seq, tt, w_a), lambda b, t: (b, t, 0)),
            pl.BlockSpec((n_seq, heads, HEAD_DIM, HEAD_DIM), lambda b, t: (b, 0, 0, 0)),
        ],
        out_shape=[
            jax.ShapeDtypeStruct((batch, seq, w_a), F32),
            jax.ShapeDtypeStruct((batch, heads, HEAD_DIM, HEAD_DIM), F32),
        ],
        scratch_shapes=[pltpu.VMEM((n_seq, heads, HEAD_DIM, HEAD_DIM), F32)],
        compiler_params=_params("arbitrary", "arbitrary"),
        name="hgrn_prompt",
    )(proj3, proj3, proj3, lb_logits, sums, pairs)
    return o.reshape(batch * seq, w_a), s


def _hgrn_sample_kernel(q_ref, f_ref, v_ref, lbl_ref, s_in_ref, o_ref, s_out_ref, *, layer, heads, seq):
    n_seq = SUBLANES // seq
    lb = _lower_bound(lbl_ref[...], layer)
    row = lax.broadcasted_iota(jnp.int32, (SUBLANES, 1), 0)
    pos = row % seq
    for tile, h in ((tile, h) for tile in range(q_ref.shape[0] // SUBLANES) for h in range(heads)):
        cols = slice(h * HEAD_DIM, (h + 1) * HEAD_DIM)
        rows = slice(tile * SUBLANES, (tile + 1) * SUBLANES)
        q, _, k, logf = _hgrn_gates(q_ref[rows, cols], f_ref[rows, cols], lb[:, cols])
        v = v_ref[rows, cols]
        b = logf
        shift = 1
        while shift < seq:
            b = b + jnp.where(pos >= shift, pltpu.roll(b, shift, 0), 0.0)
            shift *= 2
        o = jnp.zeros((SUBLANES, HEAD_DIM), F32)
        for sl in range(SUBLANES):
            lo_t, hi_t = sl, (sl // seq + 1) * seq
            mask = jnp.logical_and(row >= lo_t, row < hi_t)
            z = q * jnp.exp(jnp.where(mask, b - b[sl : sl + 1], NEG_BIG)) * k[sl : sl + 1]
            o = o + jnp.sum(z, axis=-1, keepdims=True) * v[sl : sl + 1]
        v16 = v.astype(BF16)
        for n in range(n_seq):
            own = jnp.logical_and(row >= n * seq, row < (n + 1) * seq)
            s = s_in_ref[0, tile * n_seq + n, h]
            b_last = b[(n + 1) * seq - 1 : (n + 1) * seq, :]
            qe = jnp.where(own, q * jnp.exp(b), 0.0).astype(BF16)
            o = o + _dot(qe, s.astype(BF16))
            kd = jnp.where(own, k * jnp.exp(jnp.where(own, b_last - b, 0.0)), 0.0).astype(BF16)
            s_out_ref[0, tile * n_seq + n, h] = _row_to_col(jnp.exp(b_last)) * s + _dot_tn(kd, v16)
        o_ref[rows, cols] = o


def _hgrn_sample(proj, lb_logits, state, *, layer, row0, batch, seq, w_a, tiles=4):
    heads = w_a // HEAD_DIM
    rows = tiles * SUBLANES
    assert SUBLANES % seq == 0 and row0 % rows == 0 and (batch * seq) % rows == 0
    n_seq = rows // seq
    t0 = row0 // rows
    col = lambda c: pl.BlockSpec((rows, w_a), lambda p: (t0 + p, c))
    st = pl.BlockSpec((1, n_seq, heads, HEAD_DIM, HEAD_DIM), lambda p: (layer, p, 0, 0, 0))
    st_out = pl.BlockSpec((1, n_seq, heads, HEAD_DIM, HEAD_DIM), lambda p: (0, p, 0, 0, 0))
    return pl.pallas_call(
        functools.partial(_hgrn_sample_kernel, layer=layer, heads=heads, seq=seq),
        grid=(batch // n_seq,),
        in_specs=[col(0), col(1), col(2), pl.BlockSpec(lb_logits.shape, lambda p: (0, 0)), st],
        out_specs=[pl.BlockSpec((rows, w_a), lambda p: (p, 0)), st_out],
        out_shape=[
            jax.ShapeDtypeStruct((batch * seq, w_a), F32),
            jax.ShapeDtypeStruct((1, batch, heads, HEAD_DIM, HEAD_DIM), F32),
        ],
        compiler_params=_params("arbitrary"),
        name="hgrn_sample",
    )(proj, proj, proj, lb_logits, state)


def _s5_disc_kernel(are_ref, aim_ref, ls_ref, bre_ref, bim_ref, lre_ref, lim_ref, bbre_ref, bbim_ref):
    a_re = are_ref[...]
    a_im = aim_ref[...]
    dt = jnp.exp(ls_ref[...])
    mag = jnp.exp(a_re * dt)
    lam_re = mag * jnp.cos(a_im * dt)
    lam_im = mag * jnp.sin(a_im * dt)
    den = a_re * a_re + a_im * a_im
    nr, ni = lam_re - 1.0, lam_im
    r_re = (nr * a_re + ni * a_im) / den
    r_im = (ni * a_re - nr * a_im) / den
    lre_ref[...] = lam_re
    lim_ref[...] = lam_im
    for c in range(bre_ref.shape[0]):
        b_re = bre_ref[c]
        b_im = bim_ref[c]
        bbre_ref[c] = r_re * b_re - r_im * b_im
        bbim_ref[c] = r_re * b_im + r_im * b_re


def _s5_discretise(a_re, a_im, log_step, b_re, b_im):
    g, p = a_re.shape
    c = b_re.shape[-1]
    bt = lambda x: jnp.transpose(x, (2, 0, 1))
    return pl.pallas_call(
        _s5_disc_kernel,
        out_shape=[
            jax.ShapeDtypeStruct((g, p), F32),
            jax.ShapeDtypeStruct((g, p), F32),
            jax.ShapeDtypeStruct((c, g, p), F32),
            jax.ShapeDtypeStruct((c, g, p), F32),
        ],
        name="s5_discretise",
    )(a_re, a_im, log_step.reshape(g, 1), bt(b_re), bt(b_im))


def _block_diag_in(bb):
    c, g, p = bb.shape
    gt = MXU_DIM // c
    x = bb.reshape(c, g // gt, gt, p)
    eye = jnp.eye(gt, dtype=bb.dtype)
    return jnp.einsum("cjgp,gh->jgchp", x, eye).reshape(g // gt, gt * c, gt * p)


def _block_diag_out(cc):
    g, c, p = cc.shape
    gt = MXU_DIM // c
    x = cc.reshape(g // gt, gt, c, p)
    eye = jnp.eye(gt, dtype=cc.dtype)
    return jnp.einsum("jgcp,gh->jgphc", x, eye).reshape(g // gt, gt * p, gt * c)


def _s5_lanes(x_ref, lo, hi):
    if len(x_ref.shape) == 2:
        return x_ref[:, lo:hi]
    return jnp.concatenate([x_ref[m] for m in range(lo // LANES, hi // LANES)], axis=1)


def _s5_in(ub, bre_ref, bim_ref, xr_ref, xi_ref):
    nj, kc, kp = bre_ref.shape
    for j in range(nj):
        uj = ub[:, j * kc : (j + 1) * kc]
        for x_ref, w_ref in ((xr_ref, bre_ref), (xi_ref, bim_ref)):
            bu = _dot(uj, w_ref[j])
            if len(x_ref.shape) == 2:
                x_ref[:, j * kp : (j + 1) * kp] = bu
            else:
                for m in range(kp // LANES):
                    x_ref[j * kp // LANES + m] = bu[:, m * LANES : (m + 1) * LANES]


def _s5_out(xr_ref, xi_ref, cre_ref, cim_ref):
    nj, kp, kc = cre_ref.shape
    tiles = []
    for j in range(nj):
        xr = _s5_lanes(xr_ref, j * kp, (j + 1) * kp).astype(BF16)
        xi = _s5_lanes(xi_ref, j * kp, (j + 1) * kp).astype(BF16)
        tiles.append(_dot(xr, cre_ref[j]) - _dot(xi, cim_ref[j]))
    return tiles


def _s5_step(lr, li, xr, xi, bur, bui):
    return lr * xr - li * xi + bur, lr * xi + li * xr + bui


def _s5_lambda(lre_ref, lim_ref, lo, hi, rows):
    cat = lambda ref: jnp.concatenate([ref[m] for m in range(lo // LANES, hi // LANES)], axis=1)
    return jnp.broadcast_to(cat(lre_ref), (rows, hi - lo)), jnp.broadcast_to(cat(lim_ref), (rows, hi - lo))


def _to_time_major(x):
    n, ts, w = x.shape
    return jnp.swapaxes(x, 0, 1).reshape(ts * n, w)


def _from_time_major(x, n):
    rows, w = x.shape
    return jnp.swapaxes(x.reshape(rows // n, n, w), 0, 1)


def _s5_prompt_kernel(*refs, nv, lane_block):
    u_refs = refs[:nv]
    (lre_ref, lim_ref, bre_ref, bim_ref, cre_ref, cim_ref, d_ref,
     y_ref, xr_out_ref, xi_out_ref, xr_ref, xi_ref, sr_ref, si_ref) = refs[nv:]
    ts = u_refs[0].shape[0]
    gp = xr_ref.shape[1]
    step = pl.program_id(0)

    @pl.when(step == 0)
    def _():
        sr_ref[...] = jnp.zeros_like(sr_ref)
        si_ref[...] = jnp.zeros_like(si_ref)

    u = _to_time_major(jnp.stack([r[...] for r in u_refs], axis=0))
    _s5_in(u.astype(BF16), bre_ref, bim_ref, xr_ref, xi_ref)

    for lo in range(0, gp, lane_block):
        ls = slice(lo, lo + lane_block)
        lr, li = _s5_lambda(lre_ref, lim_ref, lo, lo + lane_block, nv)

        def body(r, carry):
            rows = pl.ds(pl.multiple_of(r * nv, nv), nv)
            xr, xi = _s5_step(lr, li, *carry, xr_ref[rows, ls], xi_ref[rows, ls])
            xr_ref[rows, ls] = xr
            xi_ref[rows, ls] = xi
            return xr, xi

        xr, xi = lax.fori_loop(0, ts, body, (sr_ref[:, ls], si_ref[:, ls]), unroll=4)
        sr_ref[:, ls] = xr
        si_ref[:, ls] = xi

    kc = cre_ref.shape[2]
    d = d_ref[...]
    for j, y in enumerate(_s5_out(xr_ref, xi_ref, cre_ref, cim_ref)):
        cols = slice(j * kc, (j + 1) * kc)
        y_ref[:, :, cols] = _from_time_major(y + d[:, cols] * u[:, cols], nv)

    @pl.when(step == pl.num_programs(0) - 1)
    def _():
        xr_out_ref[...] = sr_ref[...]
        xi_out_ref[...] = si_ref[...]


def _s5_carry_kernel(y_ref, fr_ref, fi_ref, lre_ref, lim_ref, cre_ref, cim_ref,
                     o_ref, xr_out_ref, xi_out_ref, zr_ref, zi_ref, sr_ref, si_ref, *, half_len, lane_block):
    nrow, gp = fr_ref.shape
    tf = y_ref.shape[2]
    step = pl.program_id(0)
    odd = lax.broadcasted_iota(jnp.int32, (nrow, 1), 0) % 2 == 1

    @pl.when(step == 0)
    def _():
        for lo in range(0, gp, lane_block):
            ls = slice(lo, lo + lane_block)
            pr, pi = _s5_lambda(lre_ref, lim_ref, lo, lo + lane_block, nrow)
            n = 1
            while n < half_len:
                pr, pi = pr * pr - pi * pi, 2.0 * pr * pi
                n *= 2
            fr, fi = fr_ref[:, ls], fi_ref[:, ls]
            fr = jnp.where(odd, pltpu.roll(fr, 1, 0), fr)
            fi = jnp.where(odd, pltpu.roll(fi, 1, 0), fi)
            sr_ref[:, ls] = jnp.where(odd, pr * fr - pi * fi, fr)
            si_ref[:, ls] = jnp.where(odd, pr * fi + pi * fr, fi)

    for lo in range(0, gp, lane_block):
        ls = slice(lo, lo + lane_block)
        lr, li = _s5_lambda(lre_ref, lim_ref, lo, lo + lane_block, nrow)

        def body(r, carry):
            rows = pl.ds(pl.multiple_of(r * nrow, nrow), nrow)
            zr, zi = carry
            zr, zi = lr * zr - li * zi, lr * zi + li * zr
            zr_ref[rows, ls] = zr
            zi_ref[rows, ls] = zi
            return zr, zi

        zr, zi = lax.fori_loop(0, tf, body, (sr_ref[:, ls], si_ref[:, ls]), unroll=4)
        sr_ref[:, ls] = zr
        si_ref[:, ls] = zi

    kc = cre_ref.shape[2]
    nseq = y_ref.shape[0]
    for j, y in enumerate(_s5_out(zr_ref, zi_ref, cre_ref, cim_ref)):
        cols = slice(j * kc, (j + 1) * kc)
        o_ref[:, :, :, cols] = y_ref[:, :, :, cols] + _from_time_major(y, nrow).reshape(nseq, nrow // nseq, tf, kc)

    @pl.when(step == pl.num_programs(0) - 1)
    def _():
        xr_out_ref[...] = fr_ref[...] + sr_ref[...]
        xi_out_ref[...] = fi_ref[...] + si_ref[...]


def _s5_weights_specs(ws):
    zero = lambda n: (lambda *_: (0,) * n)
    return [pl.BlockSpec(w.shape, zero(w.ndim)) for w in ws]


def _s5_prompt(proj, ws, *, nb, seq, ucol, w_b, ts=64, tf=64, lane_block=1024):
    lam_re, lam_im, bre, bim, cre, cim, d = ws
    gp = lam_re.shape[0] * LANES
    nv = SUBLANES
    assert nv == 2 * nb, "two pieces per sequence"
    plen, half_len = seq // 2, seq // 4
    assert plen % ts == 0 and half_len % tf == 0 and gp % lane_block == 0
    assert half_len & (half_len - 1) == 0, "repeated squaring needs a power of two"
    nblk = plen // ts
    u_spec = lambda n: pl.BlockSpec((ts, w_b), lambda s: (n * nblk + s, ucol))
    state = pl.BlockSpec((nv, gp), lambda s: (0, 0))
    state_shape = jax.ShapeDtypeStruct((nv, gp), F32)
    scratch = lambda steps: [pltpu.VMEM((nv * steps, gp), F32)] * 2 + [pltpu.VMEM((nv, gp), F32)] * 2
    y, fr, fi = pl.pallas_call(
        functools.partial(_s5_prompt_kernel, nv=nv, lane_block=lane_block),
        grid=(nblk,),
        in_specs=[u_spec(n) for n in range(nv)] + _s5_weights_specs(ws),
        out_specs=[pl.BlockSpec((nv, ts, w_b), lambda s: (0, s, 0)), state, state],
        out_shape=[jax.ShapeDtypeStruct((nv, plen, w_b), F32), state_shape, state_shape],
        scratch_shapes=scratch(ts),
        compiler_params=_params("arbitrary"),
        name="s5_prompt",
    )(*([proj] * nv), *ws)

    y5 = y.reshape(nb, 2, 2, half_len, w_b)
    y_spec = pl.BlockSpec((nb, None, 2, tf, w_b), lambda s: (0, 1, 0, s, 0))
    y5, xr, xi = pl.pallas_call(
        functools.partial(_s5_carry_kernel, half_len=half_len, lane_block=lane_block),
        grid=(half_len // tf,),
        in_specs=[y_spec, state, state] + _s5_weights_specs((lam_re, lam_im, cre, cim)),
        out_specs=[y_spec, state, state],
        out_shape=[jax.ShapeDtypeStruct(y5.shape, F32), state_shape, state_shape],
        scratch_shapes=scratch(tf),
        input_output_aliases={0: 0},
        compiler_params=_params("arbitrary"),
        name="s5_prompt_carry",
    )(y5, fr, fi, lam_re, lam_im, cre, cim)
    last = lambda x: x.reshape(nb, 2, gp)[:, 1]
    return y5.reshape(nb * seq, w_b), last(xr), last(xi)


def _s5_sample_kernel(u_ref, x0r_ref, x0i_ref, lre_ref, lim_ref, bre_ref, bim_ref, cre_ref, cim_ref, d_ref,
                      y_ref, xr_out_ref, xi_out_ref, xr_ref, xi_ref, *, seq):
    nb = x0r_ref.shape[0]
    u = u_ref[...]
    _s5_in(u.astype(BF16), bre_ref, bim_ref, xr_ref, xi_ref)
    for m in range(xr_ref.shape[0]):
        ls = slice(m * LANES, (m + 1) * LANES)
        lr, li = lre_ref[m], lim_ref[m]
        xr, xi = x0r_ref[:, ls], x0i_ref[:, ls]
        for r in range(seq):
            rows = pl.ds(r, nb, stride=seq)
            xr, xi = _s5_step(lr, li, xr, xi, xr_ref[m, rows, :], xi_ref[m, rows, :])
            xr_ref[m, rows, :] = xr
            xi_ref[m, rows, :] = xi
        xr_out_ref[:, ls] = xr
        xi_out_ref[:, ls] = xi
    kc = cre_ref.shape[2]
    d = d_ref[...]
    for j, y in enumerate(_s5_out(xr_ref, xi_ref, cre_ref, cim_ref)):
        cols = slice(j * kc, (j + 1) * kc)
        y_ref[:, cols] = y + d[:, cols] * u[:, cols]


def _s5_sample(proj, x0_re, x0_im, ws, *, row0, ucol, seq, w_b, nb=64):
    batch, gp = x0_re.shape
    n_slabs = gp // LANES
    assert batch % nb == 0 and row0 % (nb * seq) == 0
    t0 = row0 // (nb * seq)
    return pl.pallas_call(
        functools.partial(_s5_sample_kernel, seq=seq),
        grid=(batch // nb,),
        in_specs=[
            pl.BlockSpec((nb * seq, w_b), lambda p: (t0 + p, ucol)),
            pl.BlockSpec((nb, gp), lambda p: (p, 0)),
            pl.BlockSpec((nb, gp), lambda p: (p, 0)),
        ] + _s5_weights_specs(ws),
        out_specs=[
            pl.BlockSpec((nb * seq, w_b), lambda p: (p, 0)),
            pl.BlockSpec((nb, gp), lambda p: (p, 0)),
            pl.BlockSpec((nb, gp), lambda p: (p, 0)),
        ],
        out_shape=[
            jax.ShapeDtypeStruct((batch * seq, w_b), F32),
            jax.ShapeDtypeStruct((batch, gp), F32),
            jax.ShapeDtypeStruct((batch, gp), F32),
        ],
        scratch_shapes=[pltpu.VMEM((n_slabs, nb * seq, LANES), F32), pltpu.VMEM((n_slabs, nb * seq, LANES), F32)],
        compiler_params=_params("arbitrary"),
        name="s5_sample",
    )(proj, x0_re, x0_im, *ws)


def _mix_gates(o_ref, gate_ref, y_ref, ng_ref, gluw_ref, glub_ref, a_ref, rows):
    w_a = o_ref.shape[1]
    o = o_ref[rows, :]
    var = jnp.mean(o * o, axis=-1, keepdims=True)
    gate = gate_ref[rows, :]
    a_ref[rows, :w_a] = (o * lax.rsqrt(var + EPS) * ng_ref[...] * (gate * jax.nn.sigmoid(gate))).astype(BF16)
    y = jax.nn.gelu(y_ref[rows, :])
    z = _dot(y.astype(BF16), gluw_ref[...]) + glub_ref[...]
    a_ref[rows, w_a:] = (y * jax.nn.sigmoid(z)).astype(BF16)


def _mix_kernel(o_ref, gate_ref, y_ref, x_ref, ng_ref, gluw_ref, glub_ref, wout_ref, out_ref, a0_ref, a1_ref, *,
                n_tiles, n_cols):
    i = pl.program_id(0)
    j = pl.program_id(1)
    rows_per_step = a0_ref.shape[0] // n_cols
    rows = pl.ds(pl.multiple_of(j * rows_per_step, rows_per_step), rows_per_step)

    def step(a_gate, a_proj):
        pieces = out_ref.shape[1] // MXU_DIM
        sub = rows_per_step // pieces
        for p in range(pieces):
            if a_gate is not None:
                sub_rows = pl.ds(pl.multiple_of(j * rows_per_step + p * sub, sub), sub)
                _mix_gates(o_ref, gate_ref, y_ref, ng_ref, gluw_ref, glub_ref, a_gate, sub_rows)
            if a_proj is not None:
                cols = slice(p * MXU_DIM, (p + 1) * MXU_DIM)
                out_ref[:, cols] = x_ref[:, cols] + _dot(a_proj[...], wout_ref[:, cols])

    even = i % 2 == 0
    inner = jnp.logical_and(i > 0, i < n_tiles)
    pl.when(i == 0)(lambda: step(a0_ref, None))
    pl.when(jnp.logical_and(inner, even))(lambda: step(a0_ref, a1_ref))
    pl.when(jnp.logical_and(inner, jnp.logical_not(even)))(lambda: step(a1_ref, a0_ref))
    pl.when(jnp.logical_and(i == n_tiles, even))(lambda: step(None, a1_ref))
    pl.when(jnp.logical_and(i == n_tiles, jnp.logical_not(even)))(lambda: step(None, a0_ref))


def _mix(o, proj, y, x, norm_g, glu_w, glu_b, w_out, *, gate_col, name, tn=512):
    m, w_a = o.shape
    w_b = y.shape[1]
    d = w_out.shape[1]
    tm = _row_tile(m)
    nj = d // tn
    assert m % tm == 0 and d % tn == 0 and tm % (nj * SUBLANES) == 0
    n = m // tm
    nxt = lambda i: jnp.minimum(i, n - 1)
    cur = lambda i: jnp.maximum(i - 1, 0)
    const = lambda i, j: (0, 0)
    cur_tile = lambda i, j: (cur(i), jnp.where(i > 0, j, 0))
    return pl.pallas_call(
        functools.partial(_mix_kernel, n_tiles=n, n_cols=nj),
        grid=(n + 1, nj),
        in_specs=[
            pl.BlockSpec((tm, w_a), lambda i, j: (nxt(i), 0)),
            pl.BlockSpec((tm, w_a), lambda i, j: (nxt(i), gate_col)),
            pl.BlockSpec((tm, w_b), lambda i, j: (nxt(i), 0)),
            pl.BlockSpec((tm, tn), cur_tile),
            pl.BlockSpec((1, w_a), const),
            pl.BlockSpec((w_b, w_b), const),
            pl.BlockSpec((1, w_b), const),
            pl.BlockSpec((w_a + w_b, tn), lambda i, j: (0, j)),
        ],
        out_specs=pl.BlockSpec((tm, tn), cur_tile),
        out_shape=jax.ShapeDtypeStruct((m, d), F32),
        scratch_shapes=[pltpu.VMEM((tm, w_a + w_b), BF16)] * 2,
        compiler_params=_params("arbitrary", "arbitrary"),
        name=name,
    )(o, proj, y, x, norm_g.reshape(1, w_a), glu_w, glu_b.reshape(1, w_b), w_out)


def _mlp_kernel(x_ref, g_ref, up_ref, down_ref, gf_ref, o_ref, h_ref, *, rows, final_norm):
    f = pl.program_id(1)

    @pl.when(f == 0)
    def _():
        _rmsnorm_rows(x_ref, g_ref[...], h_ref, rows)
        o_ref[...] = x_ref[...]

    a = jnp.square(jnp.maximum(_dot(h_ref[...], up_ref[...]), 0.0)).astype(BF16)
    o_ref[...] += _dot(a, down_ref[...])

    if final_norm:
        @pl.when(f == pl.num_programs(1) - 1)
        def _():
            _rmsnorm_rows(o_ref, gf_ref[...], o_ref, rows)


def _mlp(x, g, up, down, gf, *, final_norm, name, tf=512):
    m, d = x.shape
    dff = up.shape[1]
    tm = _row_tile(m)
    assert m % tm == 0 and dff % tf == 0
    return pl.pallas_call(
        functools.partial(_mlp_kernel, rows=tm, final_norm=final_norm),
        grid=(m // tm, dff // tf),
        in_specs=[
            pl.BlockSpec((tm, d), lambda i, f: (i, 0)),
            pl.BlockSpec((1, d), lambda i, f: (0, 0)),
            pl.BlockSpec((d, tf), lambda i, f: (0, f)),
            pl.BlockSpec((tf, d), lambda i, f: (f, 0)),
            pl.BlockSpec((1, d), lambda i, f: (0, 0)),
        ],
        out_specs=pl.BlockSpec((tm, d), lambda i, f: (i, 0)),
        out_shape=jax.ShapeDtypeStruct((m, d), F32),
        scratch_shapes=[pltpu.VMEM((tm, d), BF16)],
        compiler_params=_params("arbitrary", "arbitrary"),
        name=name,
    )(x, g.reshape(1, d), up, down, gf.reshape(1, d))


def kernel(x_prompt, x_sample, state_hgrn, state_s5_re, state_s5_im, w_in, w_out, norm1_g, norm2_g, hgrn_lb_logits, hgrn_norm_g, s5_a_re, s5_a_im, s5_b_re, s5_b_im, s5_c_re, s5_c_im, s5_d, s5_log_step, glu_w, glu_b, mlp_up, mlp_down, final_norm_g):
    bp, seq, d = x_prompt.shape
    bs, dseq, _ = x_sample.shape
    depth = w_in.shape[0]
    w_a = hgrn_norm_g.shape[1]
    w_b = s5_d.shape[1]
    g_b, p_state = s5_a_re.shape[1], s5_a_re.shape[2]
    mp, ms = bp * seq, bs * dseq
    assert w_in.shape[2] == 4 * w_a + w_b and w_a == w_b and g_b * S5_GROUP == w_b
    gate_col, ucol = 3, (4 * w_a) // w_b

    xp = x_prompt.reshape(mp, d)
    xs = x_sample.reshape(ms, d)
    lb_logits = hgrn_lb_logits.astype(F32)
    new_states = []
    for l in range(depth):
        w_in_l, w_out_l, glu_w_l = w_in[l].astype(BF16), w_out[l].astype(BF16), glu_w[l].astype(BF16)
        up_l, down_l = mlp_up[l].astype(BF16), mlp_down[l].astype(BF16)
        proj_p = _inproj(xp, norm1_g[l], w_in_l, name="inproj_prompt")
        proj_s = _inproj(xs, norm1_g[l], w_in_l, name="inproj_sample")

        o_p, sh_p = _hgrn_prompt(proj_p, lb_logits, layer=l, batch=bp, seq=seq, w_a=w_a)
        o_s, sh_s = _hgrn_sample(proj_s, lb_logits, state_hgrn, layer=l, row0=0, batch=bs, seq=dseq, w_a=w_a)

        lam_re, lam_im, bb_re, bb_im = _s5_discretise(
            s5_a_re[l].astype(F32), s5_a_im[l].astype(F32), s5_log_step[l].astype(F32),
            s5_b_re[l].astype(F32), s5_b_im[l].astype(F32))
        ws = (
            lam_re.reshape(-1, 1, LANES), lam_im.reshape(-1, 1, LANES),
            _block_diag_in(bb_re).astype(BF16), _block_diag_in(bb_im).astype(BF16),
            _block_diag_out(s5_c_re[l].astype(F32)).astype(BF16), _block_diag_out(s5_c_im[l].astype(F32)).astype(BF16),
            s5_d[l].astype(F32).reshape(1, w_b),
        )
        y_p, sr_p, si_p = _s5_prompt(proj_p, ws, nb=bp, seq=seq, ucol=ucol, w_b=w_b)
        y_s, sr_s, si_s = _s5_sample(
            proj_s, state_s5_re[l].reshape(bs, g_b * p_state).astype(F32), state_s5_im[l].reshape(bs, g_b * p_state).astype(F32),
            ws, row0=0, ucol=ucol, seq=dseq, w_b=w_b)

        last = l == depth - 1
        groups = []
        for tag, x, o, proj, y in (("prompt", xp, o_p, proj_p, y_p), ("sample", xs, o_s, proj_s, y_s)):
            x1 = _mix(o, proj, y, x, hgrn_norm_g[l], glu_w_l, glu_b[l], w_out_l, gate_col=gate_col, name="mix_" + tag)
            groups.append(_mlp(x1, norm2_g[l], up_l, down_l, final_norm_g, final_norm=last, name="mlp_" + tag))
        xp, xs = groups
        new_states.append((sh_p, sr_p.reshape(bp, g_b, p_state), si_p.reshape(bp, g_b, p_state),
                           sh_s[0], sr_s.reshape(bs, g_b, p_state), si_s.reshape(bs, g_b, p_state)))

    y_prompt = xp.reshape(bp, seq, d).astype(x_prompt.dtype)
    y_sample = xs.reshape(bs, dseq, d).astype(x_sample.dtype)
    stacked = [jnp.stack([st[i] for st in new_states]) for i in range(6)]
    return (y_prompt, y_sample, *stacked)
```

```python
import functools

import jax
import jax.numpy as jnp
import numpy as np
from jax import lax
from jax.experimental import pallas as pl
from jax.experimental.pallas import tpu as pltpu

F32 = jnp.float32
BF16 = jnp.bfloat16
EPS = 1e-6

LANES = 128
SUBLANES = 8
MXU_DIM = 256
VMEM_LIMIT = 56 * 1024 * 1024

HEAD_DIM = 128
S5_GROUP = 16
HGRN_CHUNK = 64
ROW_TILE = 1024
NEG_BIG = -1e30


def _dot(a, b):
    return jnp.dot(a, b, preferred_element_type=F32)


def _dot_nt(a, b):
    return lax.dot_general(a, b, (((1,), (1,)), ((), ())), preferred_element_type=F32)


def _dot_tn(a, b):
    return lax.dot_general(a, b, (((0,), (0,)), ((), ())), preferred_element_type=F32)


def _split3(x):
    hi = x.astype(BF16)
    r1 = x - hi.astype(F32)
    mid = r1.astype(BF16)
    lo = (r1 - mid.astype(F32)).astype(BF16)
    return hi, mid, lo


def _params(*sem):
    return pltpu.CompilerParams(dimension_semantics=sem, vmem_limit_bytes=VMEM_LIMIT)


def _rmsnorm_rows(src_ref, g, dst_ref, rows, chunk=64):
    def body(c, carry):
        r0 = pl.multiple_of(c * chunk, chunk)
        x = src_ref[pl.ds(r0, chunk), :]
        var = jnp.mean(x * x, axis=-1, keepdims=True)
        dst_ref[pl.ds(r0, chunk), :] = (x * lax.rsqrt(var + EPS) * g).astype(dst_ref.dtype)
        return carry

    lax.fori_loop(0, rows // chunk, body, 0)


def _inproj_kernel(x_ref, g_ref, w_ref, o_ref, h_ref, *, rows):
    @pl.when(pl.program_id(1) == 0)
    def _():
        _rmsnorm_rows(x_ref, g_ref[...], h_ref, rows)

    o_ref[...] = _dot(h_ref[...], w_ref[...])


def _row_tile(m):
    return ROW_TILE if m % ROW_TILE == 0 else ROW_TILE // 2


def _inproj(x, g, w, *, name, tn=1024):
    m, d = x.shape
    n = w.shape[1]
    tm = _row_tile(m)
    assert m % tm == 0 and n % tn == 0
    return pl.pallas_call(
        functools.partial(_inproj_kernel, rows=tm),
        grid=(m // tm, n // tn),
        in_specs=[
            pl.BlockSpec((tm, d), lambda i, j: (i, 0)),
            pl.BlockSpec((1, d), lambda i, j: (0, 0)),
            pl.BlockSpec((d, tn), lambda i, j: (0, j)),
        ],
        out_specs=pl.BlockSpec((tm, tn), lambda i, j: (i, j)),
        out_shape=jax.ShapeDtypeStruct((m, n), F32),
        scratch_shapes=[pltpu.VMEM((tm, d), BF16)],
        compiler_params=_params("arbitrary", "arbitrary"),
        name=name,
    )(x, g.reshape(1, d), w)


def _lower_bound(logits, layer):
    m = jnp.max(logits, axis=0, keepdims=True)
    e = jnp.exp(logits - m)
    return jnp.sum(e[: layer + 1], axis=0, keepdims=True) / jnp.sum(e, axis=0, keepdims=True)


def _row_to_col(e_row):
    n = e_row.shape[1]
    hi, mid, lo = (p.astype(F32) for p in _split3(e_row))
    r = lax.broadcasted_iota(jnp.int32, (2 * SUBLANES, n), 0)
    pieces = jnp.where(r == 0, hi, jnp.where(r == 1, mid, jnp.where(r == 2, lo, 0.0)))
    return _dot_tn(pieces.astype(BF16), jnp.ones((2 * SUBLANES, LANES), BF16))


def _hgrn_gates(qp, fp, lb):
    q = qp * jax.nn.sigmoid(qp)
    f = lb + (1.0 - lb) * jax.nn.sigmoid(fp)
    return q, f, 1.0 - f, jnp.log(f)


def _hgrn_tables(c):
    t = np.arange(c)[:, None]
    j = np.arange(c)[None, :]
    sums, pairs = [j <= t], []
    bs = c
    while bs >= 2:
        hs = bs // 2
        ref = t - t % bs + hs - 1
        upper = t % bs >= hs
        sums.append(np.where(upper, (j > ref) & (j <= t), (j > t) & (j <= ref)))
        pairs.append((t // bs == j // bs) & upper & (j % bs < hs))
        bs = hs
    sums.append(j > t)
    pairs.append(t == j)
    sums = np.concatenate(sums, axis=0).astype(np.float32)
    return jnp.asarray(np.concatenate([sums] * 3, axis=1), BF16), jnp.asarray(np.stack(pairs), F32)


def _upper_q_lower_k(q, k, bs):
    c = q.shape[0]
    hs = bs // 2
    if hs % SUBLANES == 0:
        parts = []
        for r0 in range(0, c, bs):
            parts += [k[r0 : r0 + hs], q[r0 + hs : r0 + bs]]
        return jnp.concatenate(parts, axis=0)
    row = lax.broadcasted_iota(jnp.int32, (c, 1), 0)
    return jnp.where(row % bs >= hs, q, k)


def _hgrn_chunks(inputs, lb, st_ref, sums, pairs_ref, heads):
    levels = pairs_ref.shape[0] - 1
    prep = []
    for qp, fp, v in inputs:
        c = qp.shape[0]
        q = qp * jax.nn.sigmoid(qp)
        f = lb + (1.0 - lb) * jax.nn.sigmoid(fp)
        k = 1.0 - f
        e = jnp.exp2(_dot(sums, jnp.concatenate(_split3(jnp.log2(f)), axis=0)))
        e_b, e_rest = e[:c], e[(levels + 1) * c :]
        ys = [(_upper_q_lower_k(q, k, c >> l) * e[(l + 1) * c : (l + 2) * c]).astype(BF16) for l in range(levels)]
        prep.append(dict(ys=ys, q16=q.astype(BF16), k16=k.astype(BF16), v16=v.astype(BF16),
                         qe=(q * e_b).astype(BF16), kd=(k * e_rest).astype(BF16), e_last=e_b[c - 1 : c]))

    pairs = [(n, h) for n in range(len(inputs)) for h in range(heads)]
    head = lambda x, h: x[:, h * HEAD_DIM : (h + 1) * HEAD_DIM]
    scores = {}
    for n, h in pairs:
        p = prep[n]
        a = _dot_nt(head(p["q16"], h), head(p["k16"], h)) * pairs_ref[levels]
        for l in range(levels):
            y = head(p["ys"][l], h)
            a = a + _dot_nt(y, y) * pairs_ref[l]
        scores[n, h] = a.astype(BF16)
    outs = {}
    for n, h in pairs:
        p = prep[n]
        st = st_ref[n, h]
        outs[n, h] = _dot_nt(head(p["qe"], h), st.astype(BF16)) + _dot(scores[n, h], head(p["v16"], h))
        st_ref[n, h] = head(p["e_last"], h) * st + _dot_tn(head(p["v16"], h), head(p["kd"], h))
    return [jnp.concatenate([outs[n, h] for h in range(heads)], axis=1) for n in range(len(inputs))]


def _hgrn_prompt_kernel(q_ref, f_ref, v_ref, lbl_ref, sums_ref, pairs_ref, o_ref, s_out_ref, st_ref, *,
                        layer, heads, n_chunks):
    t = pl.program_id(1)

    @pl.when(t == 0)
    def _():
        st_ref[...] = jnp.zeros_like(st_ref)

    lb = _lower_bound(lbl_ref[...], layer)
    n_seq = q_ref.shape[0]

    def body(c, carry):
        rows = pl.ds(pl.multiple_of(c * HGRN_CHUNK, HGRN_CHUNK), HGRN_CHUNK)
        inputs = [(q_ref[n, rows, :], f_ref[n, rows, :], v_ref[n, rows, :]) for n in range(n_seq)]
        for n, o in enumerate(_hgrn_chunks(inputs, lb, st_ref, sums_ref[...], pairs_ref, heads)):
            o_ref[n, rows, :] = o
        return carry

    lax.fori_loop(0, n_chunks, body, 0)

    @pl.when(t == pl.num_programs(1) - 1)
    def _():
        for n in range(n_seq):
            for h in range(heads):
                s_out_ref[n, h] = st_ref[n, h].T


def _hgrn_prompt(proj, lb_logits, *, layer, batch, seq, w_a, tt=256, n_seq=2):
    heads = w_a // HEAD_DIM
    assert seq % tt == 0 and tt % HGRN_CHUNK == 0 and batch % n_seq == 0
    sums, pairs = _hgrn_tables(HGRN_CHUNK)
    proj3 = proj.reshape(batch, seq, proj.shape[1])
    col = lambda c: pl.BlockSpec((n_seq, tt, w_a), lambda b, t: (b, t, c))
    const = lambda x: pl.BlockSpec(x.shape, lambda b, t: (0,) * x.ndim)
    o, s = pl.pallas_call(
        functools.partial(_hgrn_prompt_kernel, layer=layer, heads=heads, n_chunks=tt // HGRN_CHUNK),
        grid=(batch // n_seq, seq // tt),
        in_specs=[col(0), col(1), col(2), const(lb_logits), const(sums), const(pairs)],
        out_specs=[
            pl.BlockSpec((n_seq, tt, w_a), lambda b, t: (b, t, 0)),
            pl.BlockSpec((n_seq, heads, HEAD_DIM, HEAD_DIM), lambda b, t: (b, 0, 0, 0)),
        ],
        out_shape=[
            jax.ShapeDtypeStruct((batch, seq, w_a), F32),
            jax.ShapeDtypeStruct((batch, heads, HEAD_DIM, HEAD_DIM), F32),
        ],
        scratch_shapes=[pltpu.VMEM((n_seq, heads, HEAD_DIM, HEAD_DIM), F32)],
        compiler_params=_params("arbitrary", "arbitrary"),
        name="hgrn_prompt",
    )(proj3, proj3, proj3, lb_logits, sums, pairs)
    return o.reshape(batch * seq, w_a), s


def _hgrn_sample_kernel(q_ref, f_ref, v_ref, lbl_ref, s_in_ref, o_ref, s_out_ref, *, layer, heads, seq):
    n_seq = SUBLANES // seq
    lb = _lower_bound(lbl_ref[...], layer)
    row = lax.broadcasted_iota(jnp.int32, (SUBLANES, 1), 0)
    pos = row % seq
    work = []
    for tile, h in ((tile, h) for tile in range(q_ref.shape[0] // SUBLANES) for h in range(heads)):
        cols = slice(h * HEAD_DIM, (h + 1) * HEAD_DIM)
        rows = slice(tile * SUBLANES, (tile + 1) * SUBLANES)
        q, _, k, logf = _hgrn_gates(q_ref[rows, cols], f_ref[rows, cols], lb[:, cols])
        v = v_ref[rows, cols]
        b = logf
        shift = 1
        while shift < seq:
            b = b + jnp.where(pos >= shift, pltpu.roll(b, shift, 0), 0.0)
            shift *= 2
        o = jnp.zeros((SUBLANES, HEAD_DIM), F32)
        for sl in range(SUBLANES):
            lo_t, hi_t = sl, (sl // seq + 1) * seq
            mask = jnp.logical_and(row >= lo_t, row < hi_t)
            z = q * jnp.exp(jnp.where(mask, b - b[sl : sl + 1], NEG_BIG)) * k[sl : sl + 1]
            o = o + jnp.sum(z, axis=-1, keepdims=True) * v[sl : sl + 1]
        per_seq = []
        for n in range(n_seq):
            own = jnp.logical_and(row >= n * seq, row < (n + 1) * seq)
            b_last = b[(n + 1) * seq - 1 : (n + 1) * seq, :]
            qe = jnp.where(own, q * jnp.exp(b), 0.0).astype(BF16)
            kd = jnp.where(own, k * jnp.exp(jnp.where(own, b_last - b, 0.0)), 0.0).astype(BF16)
            per_seq.append((tile * n_seq + n, qe, kd, jnp.exp(b_last)))
        work.append((rows, cols, h, o, v.astype(BF16), per_seq))
    for rows, cols, h, o, _, per_seq in work:
        for i, qe, _, _ in per_seq:
            o = o + _dot(qe, s_in_ref[0, i, h].astype(BF16))
        o_ref[rows, cols] = o
    for _, _, h, _, v16, per_seq in work:
        for i, _, kd, e_last in per_seq:
            s_out_ref[0, i, h] = _row_to_col(e_last) * s_in_ref[0, i, h] + _dot_tn(kd, v16)


def _hgrn_sample(proj, lb_logits, state, *, layer, row0, batch, seq, w_a, tiles=4):
    heads = w_a // HEAD_DIM
    rows = tiles * SUBLANES
    assert SUBLANES % seq == 0 and row0 % rows == 0 and (batch * seq) % rows == 0
    n_seq = rows // seq
    t0 = row0 // rows
    col = lambda c: pl.BlockSpec((rows, w_a), lambda p: (t0 + p, c))
    st = pl.BlockSpec((1, n_seq, heads, HEAD_DIM, HEAD_DIM), lambda p: (layer, p, 0, 0, 0))
    st_out = pl.BlockSpec((1, n_seq, heads, HEAD_DIM, HEAD_DIM), lambda p: (0, p, 0, 0, 0))
    return pl.pallas_call(
        functools.partial(_hgrn_sample_kernel, layer=layer, heads=heads, seq=seq),
        grid=(batch // n_seq,),
        in_specs=[col(0), col(1), col(2), pl.BlockSpec(lb_logits.shape, lambda p: (0, 0)), st],
        out_specs=[pl.BlockSpec((rows, w_a), lambda p: (p, 0)), st_out],
        out_shape=[
            jax.ShapeDtypeStruct((batch * seq, w_a), F32),
            jax.ShapeDtypeStruct((1, batch, heads, HEAD_DIM, HEAD_DIM), F32),
        ],
        compiler_params=_params("arbitrary"),
        name="hgrn_sample",
    )(proj, proj, proj, lb_logits, state)


def _s5_disc_kernel(are_ref, aim_ref, ls_ref, bre_ref, bim_ref, lre_ref, lim_ref, bbre_ref, bbim_ref):
    a_re = are_ref[...]
    a_im = aim_ref[...]
    dt = jnp.exp(ls_ref[...])
    mag = jnp.exp(a_re * dt)
    lam_re = mag * jnp.cos(a_im * dt)
    lam_im = mag * jnp.sin(a_im * dt)
    den = a_re * a_re + a_im * a_im
    nr, ni = lam_re - 1.0, lam_im
    r_re = (nr * a_re + ni * a_im) / den
    r_im = (ni * a_re - nr * a_im) / den
    lre_ref[...] = lam_re
    lim_ref[...] = lam_im
    for c in range(bre_ref.shape[0]):
        b_re = bre_ref[c]
        b_im = bim_ref[c]
        bbre_ref[c] = r_re * b_re - r_im * b_im
        bbim_ref[c] = r_re * b_im + r_im * b_re


def _s5_discretise(a_re, a_im, log_step, b_re, b_im):
    g, p = a_re.shape
    c = b_re.shape[-1]
    bt = lambda x: jnp.transpose(x, (2, 0, 1))
    return pl.pallas_call(
        _s5_disc_kernel,
        out_shape=[
            jax.ShapeDtypeStruct((g, p), F32),
            jax.ShapeDtypeStruct((g, p), F32),
            jax.ShapeDtypeStruct((c, g, p), F32),
            jax.ShapeDtypeStruct((c, g, p), F32),
        ],
        name="s5_discretise",
    )(a_re, a_im, log_step.reshape(g, 1), bt(b_re), bt(b_im))


def _block_diag_in(bb):
    c, g, p = bb.shape
    gt = MXU_DIM // c
    x = bb.reshape(c, g // gt, gt, p)
    eye = jnp.eye(gt, dtype=bb.dtype)
    return jnp.einsum("cjgp,gh->jgchp", x, eye).reshape(g // gt, gt * c, gt * p)


def _block_diag_out(cc):
    g, c, p = cc.shape
    gt = MXU_DIM // c
    x = cc.reshape(g // gt, gt, c, p)
    eye = jnp.eye(gt, dtype=cc.dtype)
    return jnp.einsum("jgcp,gh->jgphc", x, eye).reshape(g // gt, gt * p, gt * c)


def _s5_lanes(x_ref, lo, hi):
    if len(x_ref.shape) == 2:
        return x_ref[:, lo:hi]
    return jnp.concatenate([x_ref[m] for m in range(lo // LANES, hi // LANES)], axis=1)


def _s5_in(ub, bre_ref, bim_ref, xr_ref, xi_ref):
    nj, kc, kp = bre_ref.shape
    for j in range(nj):
        uj = ub[:, j * kc : (j + 1) * kc]
        for x_ref, w_ref in ((xr_ref, bre_ref), (xi_ref, bim_ref)):
            bu = _dot(uj, w_ref[j])
            if len(x_ref.shape) == 2:
                x_ref[:, j * kp : (j + 1) * kp] = bu
            else:
                for m in range(kp // LANES):
                    x_ref[j * kp // LANES + m] = bu[:, m * LANES : (m + 1) * LANES]


def _s5_out(xr_ref, xi_ref, cre_ref, cim_ref):
    nj, kp, kc = cre_ref.shape
    tiles = []
    for j in range(nj):
        xr = _s5_lanes(xr_ref, j * kp, (j + 1) * kp).astype(BF16)
        xi = _s5_lanes(xi_ref, j * kp, (j + 1) * kp).astype(BF16)
        tiles.append(_dot(xr, cre_ref[j]) - _dot(xi, cim_ref[j]))
    return tiles


def _s5_step(lr, li, xr, xi, bur, bui):
    return lr * xr - li * xi + bur, lr * xi + li * xr + bui


def _s5_lambda(lre_ref, lim_ref, lo, hi, rows):
    cat = lambda ref: jnp.concatenate([ref[m] for m in range(lo // LANES, hi // LANES)], axis=1)
    return jnp.broadcast_to(cat(lre_ref), (rows, hi - lo)), jnp.broadcast_to(cat(lim_ref), (rows, hi - lo))


def _to_time_major(x):
    n, ts, w = x.shape
    return jnp.swapaxes(x, 0, 1).reshape(ts * n, w)


def _from_time_major(x, n):
    rows, w = x.shape
    return jnp.swapaxes(x.reshape(rows // n, n, w), 0, 1)


def _s5_prompt_kernel(*refs, nv, lane_block):
    u_refs = refs[:nv]
    (lre_ref, lim_ref, bre_ref, bim_ref, cre_ref, cim_ref, d_ref,
     y_ref, xr_out_ref, xi_out_ref, xr_ref, xi_ref, sr_ref, si_ref) = refs[nv:]
    ts = u_refs[0].shape[0]
    gp = xr_ref.shape[1]
    step = pl.program_id(0)

    @pl.when(step == 0)
    def _():
        sr_ref[...] = jnp.zeros_like(sr_ref)
        si_ref[...] = jnp.zeros_like(si_ref)

    u = _to_time_major(jnp.stack([r[...] for r in u_refs], axis=0))
    _s5_in(u.astype(BF16), bre_ref, bim_ref, xr_ref, xi_ref)

    for lo in range(0, gp, lane_block):
        ls = slice(lo, lo + lane_block)
        lr, li = _s5_lambda(lre_ref, lim_ref, lo, lo + lane_block, nv)

        def body(r, carry):
            rows = pl.ds(pl.multiple_of(r * nv, nv), nv)
            xr, xi = _s5_step(lr, li, *carry, xr_ref[rows, ls], xi_ref[rows, ls])
            xr_ref[rows, ls] = xr
            xi_ref[rows, ls] = xi
            return xr, xi

        xr, xi = lax.fori_loop(0, ts, body, (sr_ref[:, ls], si_ref[:, ls]), unroll=4)
        sr_ref[:, ls] = xr
        si_ref[:, ls] = xi

    kc = cre_ref.shape[2]
    d = d_ref[...]
    for j, y in enumerate(_s5_out(xr_ref, xi_ref, cre_ref, cim_ref)):
        cols = slice(j * kc, (j + 1) * kc)
        y_ref[:, :, cols] = _from_time_major(y + d[:, cols] * u[:, cols], nv)

    @pl.when(step == pl.num_programs(0) - 1)
    def _():
        xr_out_ref[...] = sr_ref[...]
        xi_out_ref[...] = si_ref[...]


def _s5_carry_kernel(y_ref, fr_ref, fi_ref, lre_ref, lim_ref, cre_ref, cim_ref,
                     o_ref, xr_out_ref, xi_out_ref, zr_ref, zi_ref, sr_ref, si_ref, *, half_len, lane_block):
    nrow, gp = fr_ref.shape
    tf = y_ref.shape[2]
    step = pl.program_id(0)
    odd = lax.broadcasted_iota(jnp.int32, (nrow, 1), 0) % 2 == 1

    @pl.when(step == 0)
    def _():
        for lo in range(0, gp, lane_block):
            ls = slice(lo, lo + lane_block)
            pr, pi = _s5_lambda(lre_ref, lim_ref, lo, lo + lane_block, nrow)
            n = 1
            while n < half_len:
                pr, pi = pr * pr - pi * pi, 2.0 * pr * pi
                n *= 2
            fr, fi = fr_ref[:, ls], fi_ref[:, ls]
            fr = jnp.where(odd, pltpu.roll(fr, 1, 0), fr)
            fi = jnp.where(odd, pltpu.roll(fi, 1, 0), fi)
            sr_ref[:, ls] = jnp.where(odd, pr * fr - pi * fi, fr)
            si_ref[:, ls] = jnp.where(odd, pr * fi + pi * fr, fi)

    for lo in range(0, gp, lane_block):
        ls = slice(lo, lo + lane_block)
        lr, li = _s5_lambda(lre_ref, lim_ref, lo, lo + lane_block, nrow)

        def body(r, carry):
            rows = pl.ds(pl.multiple_of(r * nrow, nrow), nrow)
            zr, zi = carry
            zr, zi = lr * zr - li * zi, lr * zi + li * zr
            zr_ref[rows, ls] = zr
            zi_ref[rows, ls] = zi
            return zr, zi

        zr, zi = lax.fori_loop(0, tf, body, (sr_ref[:, ls], si_ref[:, ls]), unroll=4)
        sr_ref[:, ls] = zr
        si_ref[:, ls] = zi

    kc = cre_ref.shape[2]
    nseq = y_ref.shape[0]
    for j, y in enumerate(_s5_out(zr_ref, zi_ref, cre_ref, cim_ref)):
        cols = slice(j * kc, (j + 1) * kc)
        o_ref[:, :, :, cols] = y_ref[:, :, :, cols] + _from_time_major(y, nrow).reshape(nseq, nrow // nseq, tf, kc)

    @pl.when(step == pl.num_programs(0) - 1)
    def _():
        xr_out_ref[...] = fr_ref[...] + sr_ref[...]
        xi_out_ref[...] = fi_ref[...] + si_ref[...]


def _s5_weights_specs(ws):
    zero = lambda n: (lambda *_: (0,) * n)
    return [pl.BlockSpec(w.shape, zero(w.ndim)) for w in ws]


def _s5_prompt(proj, ws, *, nb, seq, ucol, w_b, ts=64, tf=64, lane_block=1024):
    lam_re, lam_im, bre, bim, cre, cim, d = ws
    gp = lam_re.shape[0] * LANES
    nv = SUBLANES
    assert nv == 2 * nb, "two pieces per sequence"
    plen, half_len = seq // 2, seq // 4
    assert plen % ts == 0 and half_len % tf == 0 and gp % lane_block == 0
    assert half_len & (half_len - 1) == 0, "repeated squaring needs a power of two"
    nblk = plen // ts
    u_spec = lambda n: pl.BlockSpec((ts, w_b), lambda s: (n * nblk + s, ucol))
    state = pl.BlockSpec((nv, gp), lambda s: (0, 0))
    state_shape = jax.ShapeDtypeStruct((nv, gp), F32)
    scratch = lambda steps: [pltpu.VMEM((nv * steps, gp), F32)] * 2 + [pltpu.VMEM((nv, gp), F32)] * 2
    y, fr, fi = pl.pallas_call(
        functools.partial(_s5_prompt_kernel, nv=nv, lane_block=lane_block),
        grid=(nblk,),
        in_specs=[u_spec(n) for n in range(nv)] + _s5_weights_specs(ws),
        out_specs=[pl.BlockSpec((nv, ts, w_b), lambda s: (0, s, 0)), state, state],
        out_shape=[jax.ShapeDtypeStruct((nv, plen, w_b), F32), state_shape, state_shape],
        scratch_shapes=scratch(ts),
        compiler_params=_params("arbitrary"),
        name="s5_prompt",
    )(*([proj] * nv), *ws)

    y5 = y.reshape(nb, 2, 2, half_len, w_b)
    y_spec = pl.BlockSpec((nb, None, 2, tf, w_b), lambda s: (0, 1, 0, s, 0))
    y5, xr, xi = pl.pallas_call(
        functools.partial(_s5_carry_kernel, half_len=half_len, lane_block=lane_block),
        grid=(half_len // tf,),
        in_specs=[y_spec, state, state] + _s5_weights_specs((lam_re, lam_im, cre, cim)),
        out_specs=[y_spec, state, state],
        out_shape=[jax.ShapeDtypeStruct(y5.shape, F32), state_shape, state_shape],
        scratch_shapes=scratch(tf),
        input_output_aliases={0: 0},
        compiler_params=_params("arbitrary"),
        name="s5_prompt_carry",
    )(y5, fr, fi, lam_re, lam_im, cre, cim)
    last = lambda x: x.reshape(nb, 2, gp)[:, 1]
    return y5.reshape(nb * seq, w_b), last(xr), last(xi)


def _s5_sample_kernel(u_ref, x0r_ref, x0i_ref, lre_ref, lim_ref, bre_ref, bim_ref, cre_ref, cim_ref, d_ref,
                      y_ref, xr_out_ref, xi_out_ref, xr_ref, xi_ref, *, seq):
    nb = x0r_ref.shape[0]
    u = u_ref[...]
    _s5_in(u.astype(BF16), bre_ref, bim_ref, xr_ref, xi_ref)
    for m in range(xr_ref.shape[0]):
        ls = slice(m * LANES, (m + 1) * LANES)
        lr, li = lre_ref[m], lim_ref[m]
        xr, xi = x0r_ref[:, ls], x0i_ref[:, ls]
        for r in range(seq):
            rows = pl.ds(r, nb, stride=seq)
            xr, xi = _s5_step(lr, li, xr, xi, xr_ref[m, rows, :], xi_ref[m, rows, :])
            xr_ref[m, rows, :] = xr
            xi_ref[m, rows, :] = xi
        xr_out_ref[:, ls] = xr
        xi_out_ref[:, ls] = xi
    kc = cre_ref.shape[2]
    d = d_ref[...]
    for j, y in enumerate(_s5_out(xr_ref, xi_ref, cre_ref, cim_ref)):
        cols = slice(j * kc, (j + 1) * kc)
        y_ref[:, cols] = y + d[:, cols] * u[:, cols]


def _s5_sample(proj, x0_re, x0_im, ws, *, row0, ucol, seq, w_b, nb=64):
    batch, gp = x0_re.shape
    n_slabs = gp // LANES
    assert batch % nb == 0 and row0 % (nb * seq) == 0
    t0 = row0 // (nb * seq)
    return pl.pallas_call(
        functools.partial(_s5_sample_kernel, seq=seq),
        grid=(batch // nb,),
        in_specs=[
            pl.BlockSpec((nb * seq, w_b), lambda p: (t0 + p, ucol)),
            pl.BlockSpec((nb, gp), lambda p: (p, 0)),
            pl.BlockSpec((nb, gp), lambda p: (p, 0)),
        ] + _s5_weights_specs(ws),
        out_specs=[
            pl.BlockSpec((nb * seq, w_b), lambda p: (p, 0)),
            pl.BlockSpec((nb, gp), lambda p: (p, 0)),
            pl.BlockSpec((nb, gp), lambda p: (p, 0)),
        ],
        out_shape=[
            jax.ShapeDtypeStruct((batch * seq, w_b), F32),
            jax.ShapeDtypeStruct((batch, gp), F32),
            jax.ShapeDtypeStruct((batch, gp), F32),
        ],
        scratch_shapes=[pltpu.VMEM((n_slabs, nb * seq, LANES), F32), pltpu.VMEM((n_slabs, nb * seq, LANES), F32)],
        compiler_params=_params("arbitrary"),
        name="s5_sample",
    )(proj, x0_re, x0_im, *ws)


def _mix_gates(o_ref, gate_ref, y_ref, ng_ref, gluw_ref, glub_ref, a_ref, rows):
    w_a = o_ref.shape[1]
    o = o_ref[rows, :]
    var = jnp.mean(o * o, axis=-1, keepdims=True)
    gate = gate_ref[rows, :]
    a_ref[rows, :w_a] = (o * lax.rsqrt(var + EPS) * ng_ref[...] * (gate * jax.nn.sigmoid(gate))).astype(BF16)
    y = jax.nn.gelu(y_ref[rows, :])
    z = _dot(y.astype(BF16), gluw_ref[...]) + glub_ref[...]
    a_ref[rows, w_a:] = (y * jax.nn.sigmoid(z)).astype(BF16)


def _mix_kernel(o_ref, gate_ref, y_ref, x_ref, ng_ref, gluw_ref, glub_ref, wout_ref, out_ref, a0_ref, a1_ref, *,
                n_tiles, n_cols):
    i = pl.program_id(0)
    j = pl.program_id(1)
    rows_per_step = a0_ref.shape[0] // n_cols
    rows = pl.ds(pl.multiple_of(j * rows_per_step, rows_per_step), rows_per_step)

    def step(a_gate, a_proj):
        pieces = out_ref.shape[1] // MXU_DIM
        sub = rows_per_step // pieces
        for p in range(pieces):
            if a_gate is not None:
                sub_rows = pl.ds(pl.multiple_of(j * rows_per_step + p * sub, sub), sub)
                _mix_gates(o_ref, gate_ref, y_ref, ng_ref, gluw_ref, glub_ref, a_gate, sub_rows)
            if a_proj is not None:
                cols = slice(p * MXU_DIM, (p + 1) * MXU_DIM)
                out_ref[:, cols] = x_ref[:, cols] + _dot(a_proj[...], wout_ref[:, cols])

    even = i % 2 == 0
    inner = jnp.logical_and(i > 0, i < n_tiles)
    pl.when(i == 0)(lambda: step(a0_ref, None))
    pl.when(jnp.logical_and(inner, even))(lambda: step(a0_ref, a1_ref))
    pl.when(jnp.logical_and(inner, jnp.logical_not(even)))(lambda: step(a1_ref, a0_ref))
    pl.when(jnp.logical_and(i == n_tiles, even))(lambda: step(None, a1_ref))
    pl.when(jnp.logical_and(i == n_tiles, jnp.logical_not(even)))(lambda: step(None, a0_ref))


def _mix(o, proj, y, x, norm_g, glu_w, glu_b, w_out, *, gate_col, name, tn=512):
    m, w_a = o.shape
    w_b = y.shape[1]
    d = w_out.shape[1]
    tm = _row_tile(m)
    nj = d // tn
    assert m % tm == 0 and d % tn == 0 and tm % (nj * SUBLANES) == 0
    n = m // tm
    nxt = lambda i: jnp.minimum(i, n - 1)
    cur = lambda i: jnp.maximum(i - 1, 0)
    const = lambda i, j: (0, 0)
    cur_tile = lambda i, j: (cur(i), jnp.where(i > 0, j, 0))
    return pl.pallas_call(
        functools.partial(_mix_kernel, n_tiles=n, n_cols=nj),
        grid=(n + 1, nj),
        in_specs=[
            pl.BlockSpec((tm, w_a), lambda i, j: (nxt(i), 0)),
            pl.BlockSpec((tm, w_a), lambda i, j: (nxt(i), gate_col)),
            pl.BlockSpec((tm, w_b), lambda i, j: (nxt(i), 0)),
            pl.BlockSpec((tm, tn), cur_tile),
            pl.BlockSpec((1, w_a), const),
            pl.BlockSpec((w_b, w_b), const),
            pl.BlockSpec((1, w_b), const),
            pl.BlockSpec((w_a + w_b, tn), lambda i, j: (0, j)),
        ],
        out_specs=pl.BlockSpec((tm, tn), cur_tile),
        out_shape=jax.ShapeDtypeStruct((m, d), F32),
        scratch_shapes=[pltpu.VMEM((tm, w_a + w_b), BF16)] * 2,
        compiler_params=_params("arbitrary", "arbitrary"),
        name=name,
    )(o, proj, y, x, norm_g.reshape(1, w_a), glu_w, glu_b.reshape(1, w_b), w_out)


def _mlp_kernel(x_ref, g_ref, up_ref, down_ref, gf_ref, o_ref, h_ref, *, rows, final_norm):
    f = pl.program_id(1)

    @pl.when(f == 0)
    def _():
        _rmsnorm_rows(x_ref, g_ref[...], h_ref, rows)
        o_ref[...] = x_ref[...]

    a = jnp.square(jnp.maximum(_dot(h_ref[...], up_ref[...]), 0.0)).astype(BF16)
    o_ref[...] += _dot(a, down_ref[...])

    if final_norm:
        @pl.when(f == pl.num_programs(1) - 1)
        def _():
            _rmsnorm_rows(o_ref, gf_ref[...], o_ref, rows)


def _mlp(x, g, up, down, gf, *, final_norm, name, tf=512):
    m, d = x.shape
    dff = up.shape[1]
    tm = _row_tile(m)
    assert m % tm == 0 and dff % tf == 0
    return pl.pallas_call(
        functools.partial(_mlp_kernel, rows=tm, final_norm=final_norm),
        grid=(m // tm, dff // tf),
        in_specs=[
            pl.BlockSpec((tm, d), lambda i, f: (i, 0)),
            pl.BlockSpec((1, d), lambda i, f: (0, 0)),
            pl.BlockSpec((d, tf), lambda i, f: (0, f)),
            pl.BlockSpec((tf, d), lambda i, f: (f, 0)),
            pl.BlockSpec((1, d), lambda i, f: (0, 0)),
        ],
        out_specs=pl.BlockSpec((tm, d), lambda i, f: (i, 0)),
        out_shape=jax.ShapeDtypeStruct((m, d), F32),
        scratch_shapes=[pltpu.VMEM((tm, d), BF16)],
        compiler_params=_params("arbitrary", "arbitrary"),
        name=name,
    )(x, g.reshape(1, d), up, down, gf.reshape(1, d))


def kernel(x_prompt, x_sample, state_hgrn, state_s5_re, state_s5_im, w_in, w_out, norm1_g, norm2_g, hgrn_lb_logits, hgrn_norm_g, s5_a_re, s5_a_im, s5_b_re, s5_b_im, s5_c_re, s5_c_im, s5_d, s5_log_step, glu_w, glu_b, mlp_up, mlp_down, final_norm_g):
    bp, seq, d = x_prompt.shape
    bs, dseq, _ = x_sample.shape
    depth = w_in.shape[0]
    w_a = hgrn_norm_g.shape[1]
    w_b = s5_d.shape[1]
    g_b, p_state = s5_a_re.shape[1], s5_a_re.shape[2]
    mp, ms = bp * seq, bs * dseq
    assert w_in.shape[2] == 4 * w_a + w_b and w_a == w_b and g_b * S5_GROUP == w_b
    gate_col, ucol = 3, (4 * w_a) // w_b

    xp = x_prompt.reshape(mp, d)
    xs = x_sample.reshape(ms, d)
    lb_logits = hgrn_lb_logits.astype(F32)
    new_states = []
    for l in range(depth):
        w_in_l, w_out_l, glu_w_l = w_in[l].astype(BF16), w_out[l].astype(BF16), glu_w[l].astype(BF16)
        up_l, down_l = mlp_up[l].astype(BF16), mlp_down[l].astype(BF16)
        proj_p = _inproj(xp, norm1_g[l], w_in_l, name="inproj_prompt")
        proj_s = _inproj(xs, norm1_g[l], w_in_l, name="inproj_sample")

        o_p, sh_p = _hgrn_prompt(proj_p, lb_logits, layer=l, batch=bp, seq=seq, w_a=w_a)
        o_s, sh_s = _hgrn_sample(proj_s, lb_logits, state_hgrn, layer=l, row0=0, batch=bs, seq=dseq, w_a=w_a)

        lam_re, lam_im, bb_re, bb_im = _s5_discretise(
            s5_a_re[l].astype(F32), s5_a_im[l].astype(F32), s5_log_step[l].astype(F32),
            s5_b_re[l].astype(F32), s5_b_im[l].astype(F32))
        ws = (
            lam_re.reshape(-1, 1, LANES), lam_im.reshape(-1, 1, LANES),
            _block_diag_in(bb_re).astype(BF16), _block_diag_in(bb_im).astype(BF16),
            _block_diag_out(s5_c_re[l].astype(F32)).astype(BF16), _block_diag_out(s5_c_im[l].astype(F32)).astype(BF16),
            s5_d[l].astype(F32).reshape(1, w_b),
        )
        y_p, sr_p, si_p = _s5_prompt(proj_p, ws, nb=bp, seq=seq, ucol=ucol, w_b=w_b)
        y_s, sr_s, si_s = _s5_sample(
            proj_s, state_s5_re[l].reshape(bs, g_b * p_state).astype(F32), state_s5_im[l].reshape(bs, g_b * p_state).astype(F32),
            ws, row0=0, ucol=ucol, seq=dseq, w_b=w_b)

        last = l == depth - 1
        groups = []
        for tag, x, o, proj, y in (("prompt", xp, o_p, proj_p, y_p), ("sample", xs, o_s, proj_s, y_s)):
            x1 = _mix(o, proj, y, x, hgrn_norm_g[l], glu_w_l, glu_b[l], w_out_l, gate_col=gate_col, name="mix_" + tag)
            groups.append(_mlp(x1, norm2_g[l], up_l, down_l, final_norm_g, final_norm=last, name="mlp_" + tag))
        xp, xs = groups
        new_states.append((sh_p, sr_p.reshape(bp, g_b, p_state), si_p.reshape(bp, g_b, p_state),
                           sh_s[0], sr_s.reshape(bs, g_b, p_state), si_s.reshape(bs, g_b, p_state)))

    y_prompt = xp.reshape(bp, seq, d).astype(x_prompt.dtype)
    y_sample = xs.reshape(bs, dseq, d).astype(x_sample.dtype)
    stacked = [jnp.stack([st[i] for st in new_states]) for i in range(6)]
    return (y_prompt, y_sample, *stacked)
```

```python
import functools

import jax
import jax.numpy as jnp
import numpy as np
from jax import lax
from jax.experimental import pallas as pl
from jax.experimental.pallas import tpu as pltpu

F32 = jnp.float32
BF16 = jnp.bfloat16
EPS = 1e-6

LANES = 128
SUBLANES = 8
MXU_DIM = 256
VMEM_LIMIT = 56 * 1024 * 1024

HEAD_DIM = 128
S5_GROUP = 16
HGRN_CHUNK = 64
ROW_TILE = 1024
NEG_BIG = -1e30


def _dot(a, b):
    return jnp.dot(a, b, preferred_element_type=F32)


def _dot_nt(a, b):
    return lax.dot_general(a, b, (((1,), (1,)), ((), ())), preferred_element_type=F32)


def _dot_tn(a, b):
    return lax.dot_general(a, b, (((0,), (0,)), ((), ())), preferred_element_type=F32)


def _split3(x):
    hi = x.astype(BF16)
    r1 = x - hi.astype(F32)
    mid = r1.astype(BF16)
    lo = (r1 - mid.astype(F32)).astype(BF16)
    return hi, mid, lo


def _params(*sem):
    return pltpu.CompilerParams(dimension_semantics=sem, vmem_limit_bytes=VMEM_LIMIT)


def _rmsnorm_rows(src_ref, g, dst_ref, rows, chunk=64):
    def body(c, carry):
        r0 = pl.multiple_of(c * chunk, chunk)
        x = src_ref[pl.ds(r0, chunk), :]
        var = jnp.mean(x * x, axis=-1, keepdims=True)
        dst_ref[pl.ds(r0, chunk), :] = (x * lax.rsqrt(var + EPS) * g).astype(dst_ref.dtype)
        return carry

    lax.fori_loop(0, rows // chunk, body, 0)


def _bf16_weights(w_ref, copy_ref, at=(Ellipsis,)):
    w = w_ref[at]
    if copy_ref is None:
        return w
    w = w.astype(BF16)

    @pl.when(pl.program_id(0) == 0)
    def _():
        copy_ref[at] = w

    return w


def _weight_copy(w, block, index_map, last_index):
    if w.dtype == BF16:
        return [], []
    once = lambda i, j: jax.tree.map(lambda a, b: jnp.where(i == 0, a, b), index_map(i, j), last_index)
    return [pl.BlockSpec(block, once)], [jax.ShapeDtypeStruct(w.shape, BF16)]


def _inproj_kernel(x_ref, g_ref, w_ref, o_ref, *rest, rows):
    *w16_ref, h_ref = rest

    @pl.when(pl.program_id(1) == 0)
    def _():
        _rmsnorm_rows(x_ref, g_ref[...], h_ref, rows)

    o_ref[...] = _dot(h_ref[...], _bf16_weights(w_ref, *(w16_ref or [None])))


def _row_tile(m):
    return ROW_TILE if m % ROW_TILE == 0 else ROW_TILE // 2


def _inproj(x, g, w, *, name, tn=1024):
    m, d = x.shape
    n = w.shape[1]
    tm = _row_tile(m)
    assert m % tm == 0 and n % tn == 0
    w_spec = ((d, tn), lambda i, j: (0, j))
    copy_specs, copy_shapes = _weight_copy(w, *w_spec, (0, n // tn - 1))
    return pl.pallas_call(
        functools.partial(_inproj_kernel, rows=tm),
        grid=(m // tm, n // tn),
        in_specs=[
            pl.BlockSpec((tm, d), lambda i, j: (i, 0)),
            pl.BlockSpec((1, d), lambda i, j: (0, 0)),
            pl.BlockSpec(*w_spec),
        ],
        out_specs=[pl.BlockSpec((tm, tn), lambda i, j: (i, j))] + copy_specs,
        out_shape=[jax.ShapeDtypeStruct((m, n), F32)] + copy_shapes,
        scratch_shapes=[pltpu.VMEM((tm, d), BF16)],
        compiler_params=_params("arbitrary", "arbitrary"),
        name=name,
    )(x, g.reshape(1, d), w)


def _lower_bound(logits, layer):
    m = jnp.max(logits, axis=0, keepdims=True)
    e = jnp.exp(logits - m)
    return jnp.sum(e[: layer + 1], axis=0, keepdims=True) / jnp.sum(e, axis=0, keepdims=True)


def _row_to_col(e_row):
    n = e_row.shape[1]
    hi, mid, lo = (p.astype(F32) for p in _split3(e_row))
    r = lax.broadcasted_iota(jnp.int32, (2 * SUBLANES, n), 0)
    pieces = jnp.where(r == 0, hi, jnp.where(r == 1, mid, jnp.where(r == 2, lo, 0.0)))
    return _dot_tn(pieces.astype(BF16), jnp.ones((2 * SUBLANES, LANES), BF16))


def _hgrn_gates(qp, fp, lb):
    q = qp * jax.nn.sigmoid(qp)
    f = lb + (1.0 - lb) * jax.nn.sigmoid(fp)
    return q, f, 1.0 - f, jnp.log(f)


def _hgrn_tables(c):
    t = np.arange(c)[:, None]
    j = np.arange(c)[None, :]
    sums, pairs = [j <= t], []
    bs = c
    while bs >= 2:
        hs = bs // 2
        ref = t - t % bs + hs - 1
        upper = t % bs >= hs
        sums.append(np.where(upper, (j > ref) & (j <= t), (j > t) & (j <= ref)))
        pairs.append((t // bs == j // bs) & upper & (j % bs < hs))
        bs = hs
    sums.append(j > t)
    pairs.append(t == j)
    sums = np.concatenate(sums, axis=0).astype(np.float32)
    return jnp.asarray(np.concatenate([sums] * 3, axis=1), BF16), jnp.asarray(np.stack(pairs), F32)


def _upper_q_lower_k(q, k, bs):
    c = q.shape[0]
    hs = bs // 2
    if hs % SUBLANES == 0:
        parts = []
        for r0 in range(0, c, bs):
            parts += [k[r0 : r0 + hs], q[r0 + hs : r0 + bs]]
        return jnp.concatenate(parts, axis=0)
    row = lax.broadcasted_iota(jnp.int32, (c, 1), 0)
    return jnp.where(row % bs >= hs, q, k)


def _hgrn_chunks(inputs, lb, st_ref, sums, pairs_ref, heads):
    levels = pairs_ref.shape[0] - 1
    prep = []
    for qp, fp, v in inputs:
        c = qp.shape[0]
        q = qp * jax.nn.sigmoid(qp)
        f = lb + (1.0 - lb) * jax.nn.sigmoid(fp)
        k = 1.0 - f
        e = jnp.exp2(_dot(sums, jnp.concatenate(_split3(jnp.log2(f)), axis=0)))
        e_b, e_rest = e[:c], e[(levels + 1) * c :]
        ys = [(_upper_q_lower_k(q, k, c >> l) * e[(l + 1) * c : (l + 2) * c]).astype(BF16) for l in range(levels)]
        prep.append(dict(ys=ys, q16=q.astype(BF16), k16=k.astype(BF16), v16=v.astype(BF16),
                         qe=(q * e_b).astype(BF16), kd=(k * e_rest).astype(BF16), e_last=e_b[c - 1 : c]))

    pairs = [(n, h) for n in range(len(inputs)) for h in range(heads)]
    head = lambda x, h: x[:, h * HEAD_DIM : (h + 1) * HEAD_DIM]
    scores = {}
    for n, h in pairs:
        p = prep[n]
        a = _dot_nt(head(p["q16"], h), head(p["k16"], h)) * pairs_ref[levels]
        for l in range(levels):
            y = head(p["ys"][l], h)
            a = a + _dot_nt(y, y) * pairs_ref[l]
        scores[n, h] = a.astype(BF16)
    outs = {}
    for n, h in pairs:
        p = prep[n]
        st = st_ref[n, h]
        outs[n, h] = _dot_nt(head(p["qe"], h), st.astype(BF16)) + _dot(scores[n, h], head(p["v16"], h))
        st_ref[n, h] = head(p["e_last"], h) * st + _dot_tn(head(p["v16"], h), head(p["kd"], h))
    return [jnp.concatenate([outs[n, h] for h in range(heads)], axis=1) for n in range(len(inputs))]


def _hgrn_prompt_kernel(q_ref, f_ref, v_ref, lbl_ref, sums_ref, pairs_ref, o_ref, s_out_ref, st_ref, *,
                        layer, heads, n_chunks):
    t = pl.program_id(1)

    @pl.when(t == 0)
    def _():
        st_ref[...] = jnp.zeros_like(st_ref)

    lb = _lower_bound(lbl_ref[...], layer)
    n_seq = q_ref.shape[0]

    def body(c, carry):
        rows = pl.ds(pl.multiple_of(c * HGRN_CHUNK, HGRN_CHUNK), HGRN_CHUNK)
        inputs = [(q_ref[n, rows, :], f_ref[n, rows, :], v_ref[n, rows, :]) for n in range(n_seq)]
        for n, o in enumerate(_hgrn_chunks(inputs, lb, st_ref, sums_ref[...], pairs_ref, heads)):
            o_ref[n, rows, :] = o
        return carry

    lax.fori_loop(0, n_chunks, body, 0)

    @pl.when(t == pl.num_programs(1) - 1)
    def _():
        for n in range(n_seq):
            for h in range(heads):
                s_out_ref[n, h] = st_ref[n, h].T


def _hgrn_prompt(proj, lb_logits, *, layer, batch, seq, w_a, tt=256, n_seq=2):
    heads = w_a // HEAD_DIM
    assert seq % tt == 0 and tt % HGRN_CHUNK == 0 and batch % n_seq == 0
    sums, pairs = _hgrn_tables(HGRN_CHUNK)
    proj3 = proj.reshape(batch, seq, proj.shape[1])
    col = lambda c: pl.BlockSpec((n_seq, tt, w_a), lambda b, t: (b, t, c))
    const = lambda x: pl.BlockSpec(x.shape, lambda b, t: (0,) * x.ndim)
    o, s = pl.pallas_call(
        functools.partial(_hgrn_prompt_kernel, layer=layer, heads=heads, n_chunks=tt // HGRN_CHUNK),
        grid=(batch // n_seq, seq // tt),
        in_specs=[col(0), col(1), col(2), const(lb_logits), const(sums), const(pairs)],
        out_specs=[
            pl.BlockSpec((n_seq, tt, w_a), lambda b, t: (b, t, 0)),
            pl.BlockSpec((n_seq, heads, HEAD_DIM, HEAD_DIM), lambda b, t: (b, 0, 0, 0)),
        ],
        out_shape=[
            jax.ShapeDtypeStruct((batch, seq, w_a), F32),
            jax.ShapeDtypeStruct((batch, heads, HEAD_DIM, HEAD_DIM), F32),
        ],
        scratch_shapes=[pltpu.VMEM((n_seq, heads, HEAD_DIM, HEAD_DIM), F32)],
        compiler_params=_params("arbitrary", "arbitrary"),
        name="hgrn_prompt",
    )(proj3, proj3, proj3, lb_logits, sums, pairs)
    return o.reshape(batch * seq, w_a), s


def _hgrn_sample_kernel(q_ref, f_ref, v_ref, lbl_ref, s_in_ref, o_ref, s_out_ref, *, layer, heads, seq):
    n_seq = SUBLANES // seq
    lb = _lower_bound(lbl_ref[...], layer)
    row = lax.broadcasted_iota(jnp.int32, (SUBLANES, 1), 0)
    pos = row % seq
    work = []
    for tile, h in ((tile, h) for tile in range(q_ref.shape[0] // SUBLANES) for h in range(heads)):
        cols = slice(h * HEAD_DIM, (h + 1) * HEAD_DIM)
        rows = slice(tile * SUBLANES, (tile + 1) * SUBLANES)
        q, _, k, logf = _hgrn_gates(q_ref[rows, cols], f_ref[rows, cols], lb[:, cols])
        v = v_ref[rows, cols]
        b = logf
        shift = 1
        while shift < seq:
            b = b + jnp.where(pos >= shift, pltpu.roll(b, shift, 0), 0.0)
            shift *= 2
        o = jnp.zeros((SUBLANES, HEAD_DIM), F32)
        for sl in range(SUBLANES):
            lo_t, hi_t = sl, (sl // seq + 1) * seq
            mask = jnp.logical_and(row >= lo_t, row < hi_t)
            z = q * jnp.exp(jnp.where(mask, b - b[sl : sl + 1], NEG_BIG)) * k[sl : sl + 1]
            o = o + jnp.sum(z, axis=-1, keepdims=True) * v[sl : sl + 1]
        per_seq = []
        for n in range(n_seq):
            own = jnp.logical_and(row >= n * seq, row < (n + 1) * seq)
            b_last = b[(n + 1) * seq - 1 : (n + 1) * seq, :]
            qe = jnp.where(own, q * jnp.exp(b), 0.0).astype(BF16)
            kd = jnp.where(own, k * jnp.exp(jnp.where(own, b_last - b, 0.0)), 0.0).astype(BF16)
            per_seq.append((tile * n_seq + n, qe, kd, jnp.exp(b_last)))
        work.append((rows, cols, h, o, v.astype(BF16), per_seq))
    for rows, cols, h, o, _, per_seq in work:
        for i, qe, _, _ in per_seq:
            o = o + _dot(qe, s_in_ref[0, i, h].astype(BF16))
        o_ref[rows, cols] = o
    for _, _, h, _, v16, per_seq in work:
        for i, _, kd, e_last in per_seq:
            s_out_ref[0, i, h] = _row_to_col(e_last) * s_in_ref[0, i, h] + _dot_tn(kd, v16)


def _hgrn_sample(proj, lb_logits, state, *, layer, row0, batch, seq, w_a, tiles=4):
    heads = w_a // HEAD_DIM
    rows = tiles * SUBLANES
    assert SUBLANES % seq == 0 and row0 % rows == 0 and (batch * seq) % rows == 0
    n_seq = rows // seq
    t0 = row0 // rows
    col = lambda c: pl.BlockSpec((rows, w_a), lambda p: (t0 + p, c))
    st = pl.BlockSpec((1, n_seq, heads, HEAD_DIM, HEAD_DIM), lambda p: (layer, p, 0, 0, 0))
    st_out = pl.BlockSpec((1, n_seq, heads, HEAD_DIM, HEAD_DIM), lambda p: (0, p, 0, 0, 0))
    return pl.pallas_call(
        functools.partial(_hgrn_sample_kernel, layer=layer, heads=heads, seq=seq),
        grid=(batch // n_seq,),
        in_specs=[col(0), col(1), col(2), pl.BlockSpec(lb_logits.shape, lambda p: (0, 0)), st],
        out_specs=[pl.BlockSpec((rows, w_a), lambda p: (p, 0)), st_out],
        out_shape=[
            jax.ShapeDtypeStruct((batch * seq, w_a), F32),
            jax.ShapeDtypeStruct((1, batch, heads, HEAD_DIM, HEAD_DIM), F32),
        ],
        compiler_params=_params("arbitrary"),
        name="hgrn_sample",
    )(proj, proj, proj, lb_logits, state)


def _s5_disc_kernel(are_ref, aim_ref, ls_ref, bre_ref, bim_ref, lre_ref, lim_ref, bbre_ref, bbim_ref):
    a_re = are_ref[...]
    a_im = aim_ref[...]
    dt = jnp.exp(ls_ref[...])
    mag = jnp.exp(a_re * dt)
    lam_re = mag * jnp.cos(a_im * dt)
    lam_im = mag * jnp.sin(a_im * dt)
    den = a_re * a_re + a_im * a_im
    nr, ni = lam_re - 1.0, lam_im
    r_re = (nr * a_re + ni * a_im) / den
    r_im = (ni * a_re - nr * a_im) / den
    lre_ref[...] = lam_re
    lim_ref[...] = lam_im
    for c in range(bre_ref.shape[0]):
        b_re = bre_ref[c]
        b_im = bim_ref[c]
        bbre_ref[c] = r_re * b_re - r_im * b_im
        bbim_ref[c] = r_re * b_im + r_im * b_re


def _s5_discretise(a_re, a_im, log_step, b_re, b_im):
    g, p = a_re.shape
    c = b_re.shape[-1]
    bt = lambda x: jnp.transpose(x, (2, 0, 1))
    return pl.pallas_call(
        _s5_disc_kernel,
        out_shape=[
            jax.ShapeDtypeStruct((g, p), F32),
            jax.ShapeDtypeStruct((g, p), F32),
            jax.ShapeDtypeStruct((c, g, p), F32),
            jax.ShapeDtypeStruct((c, g, p), F32),
        ],
        name="s5_discretise",
    )(a_re, a_im, log_step.reshape(g, 1), bt(b_re), bt(b_im))


def _block_diag_in(bb):
    c, g, p = bb.shape
    gt = MXU_DIM // c
    x = bb.reshape(c, g // gt, gt, p)
    eye = jnp.eye(gt, dtype=bb.dtype)
    return jnp.einsum("cjgp,gh->jgchp", x, eye).reshape(g // gt, gt * c, gt * p)


def _block_diag_out(cc):
    g, c, p = cc.shape
    gt = MXU_DIM // c
    x = cc.reshape(g // gt, gt, c, p)
    eye = jnp.eye(gt, dtype=cc.dtype)
    return jnp.einsum("jgcp,gh->jgphc", x, eye).reshape(g // gt, gt * p, gt * c)


def _s5_lanes(x_ref, lo, hi):
    if len(x_ref.shape) == 2:
        return x_ref[:, lo:hi]
    return jnp.concatenate([x_ref[m] for m in range(lo // LANES, hi // LANES)], axis=1)


def _s5_in(ub, bre_ref, bim_ref, xr_ref, xi_ref):
    nj, kc, kp = bre_ref.shape
    for j in range(nj):
        uj = ub[:, j * kc : (j + 1) * kc]
        for x_ref, w_ref in ((xr_ref, bre_ref), (xi_ref, bim_ref)):
            bu = _dot(uj, w_ref[j])
            if len(x_ref.shape) == 2:
                x_ref[:, j * kp : (j + 1) * kp] = bu
            else:
                for m in range(kp // LANES):
                    x_ref[j * kp // LANES + m] = bu[:, m * LANES : (m + 1) * LANES]


def _s5_out(xr_ref, xi_ref, cre_ref, cim_ref):
    nj, kp, kc = cre_ref.shape
    tiles = []
    for j in range(nj):
        xr = _s5_lanes(xr_ref, j * kp, (j + 1) * kp).astype(BF16)
        xi = _s5_lanes(xi_ref, j * kp, (j + 1) * kp).astype(BF16)
        tiles.append(_dot(xr, cre_ref[j]) - _dot(xi, cim_ref[j]))
    return tiles


def _s5_step(lr, li, xr, xi, bur, bui):
    return lr * xr - li * xi + bur, lr * xi + li * xr + bui


def _s5_lambda(lre_ref, lim_ref, lo, hi, rows):
    cat = lambda ref: jnp.concatenate([ref[m] for m in range(lo // LANES, hi // LANES)], axis=1)
    return jnp.broadcast_to(cat(lre_ref), (rows, hi - lo)), jnp.broadcast_to(cat(lim_ref), (rows, hi - lo))


def _to_time_major(x):
    n, ts, w = x.shape
    return jnp.swapaxes(x, 0, 1).reshape(ts * n, w)


def _from_time_major(x, n):
    rows, w = x.shape
    return jnp.swapaxes(x.reshape(rows // n, n, w), 0, 1)


def _s5_prompt_kernel(*refs, nv, lane_block):
    u_refs = refs[:nv]
    (lre_ref, lim_ref, bre_ref, bim_ref, cre_ref, cim_ref, d_ref,
     y_ref, xr_out_ref, xi_out_ref, xr_ref, xi_ref, sr_ref, si_ref) = refs[nv:]
    ts = u_refs[0].shape[0]
    gp = xr_ref.shape[1]
    step = pl.program_id(0)

    @pl.when(step == 0)
    def _():
        sr_ref[...] = jnp.zeros_like(sr_ref)
        si_ref[...] = jnp.zeros_like(si_ref)

    u = _to_time_major(jnp.stack([r[...] for r in u_refs], axis=0))
    _s5_in(u.astype(BF16), bre_ref, bim_ref, xr_ref, xi_ref)

    for lo in range(0, gp, lane_block):
        ls = slice(lo, lo + lane_block)
        lr, li = _s5_lambda(lre_ref, lim_ref, lo, lo + lane_block, nv)

        def body(r, carry):
            rows = pl.ds(pl.multiple_of(r * nv, nv), nv)
            xr, xi = _s5_step(lr, li, *carry, xr_ref[rows, ls], xi_ref[rows, ls])
            xr_ref[rows, ls] = xr
            xi_ref[rows, ls] = xi
            return xr, xi

        xr, xi = lax.fori_loop(0, ts, body, (sr_ref[:, ls], si_ref[:, ls]), unroll=4)
        sr_ref[:, ls] = xr
        si_ref[:, ls] = xi

    kc = cre_ref.shape[2]
    d = d_ref[...]
    for j, y in enumerate(_s5_out(xr_ref, xi_ref, cre_ref, cim_ref)):
        cols = slice(j * kc, (j + 1) * kc)
        y_ref[:, :, cols] = _from_time_major(y + d[:, cols] * u[:, cols], nv)

    @pl.when(step == pl.num_programs(0) - 1)
    def _():
        xr_out_ref[...] = sr_ref[...]
        xi_out_ref[...] = si_ref[...]


def _s5_carry_kernel(y_ref, fr_ref, fi_ref, lre_ref, lim_ref, cre_ref, cim_ref,
                     o_ref, xr_out_ref, xi_out_ref, zr_ref, zi_ref, sr_ref, si_ref, *, half_len, lane_block):
    nrow, gp = fr_ref.shape
    tf = y_ref.shape[2]
    step = pl.program_id(0)
    odd = lax.broadcasted_iota(jnp.int32, (nrow, 1), 0) % 2 == 1

    @pl.when(step == 0)
    def _():
        for lo in range(0, gp, lane_block):
            ls = slice(lo, lo + lane_block)
            pr, pi = _s5_lambda(lre_ref, lim_ref, lo, lo + lane_block, nrow)
            n = 1
            while n < half_len:
                pr, pi = pr * pr - pi * pi, 2.0 * pr * pi
                n *= 2
            fr, fi = fr_ref[:, ls], fi_ref[:, ls]
            fr = jnp.where(odd, pltpu.roll(fr, 1, 0), fr)
            fi = jnp.where(odd, pltpu.roll(fi, 1, 0), fi)
            sr_ref[:, ls] = jnp.where(odd, pr * fr - pi * fi, fr)
            si_ref[:, ls] = jnp.where(odd, pr * fi + pi * fr, fi)

    for lo in range(0, gp, lane_block):
        ls = slice(lo, lo + lane_block)
        lr, li = _s5_lambda(lre_ref, lim_ref, lo, lo + lane_block, nrow)

        def body(r, carry):
            rows = pl.ds(pl.multiple_of(r * nrow, nrow), nrow)
            zr, zi = carry
            zr, zi = lr * zr - li * zi, lr * zi + li * zr
            zr_ref[rows, ls] = zr
            zi_ref[rows, ls] = zi
            return zr, zi

        zr, zi = lax.fori_loop(0, tf, body, (sr_ref[:, ls], si_ref[:, ls]), unroll=4)
        sr_ref[:, ls] = zr
        si_ref[:, ls] = zi

    kc = cre_ref.shape[2]
    nseq = y_ref.shape[0]
    for j, y in enumerate(_s5_out(zr_ref, zi_ref, cre_ref, cim_ref)):
        cols = slice(j * kc, (j + 1) * kc)
        o_ref[:, :, :, cols] = y_ref[:, :, :, cols] + _from_time_major(y, nrow).reshape(nseq, nrow // nseq, tf, kc)

    @pl.when(step == pl.num_programs(0) - 1)
    def _():
        xr_out_ref[...] = fr_ref[...] + sr_ref[...]
        xi_out_ref[...] = fi_ref[...] + si_ref[...]


def _s5_weights_specs(ws):
    zero = lambda n: (lambda *_: (0,) * n)
    return [pl.BlockSpec(w.shape, zero(w.ndim)) for w in ws]


def _s5_prompt(proj, ws, *, nb, seq, ucol, w_b, ts=64, tf=64, lane_block=1024):
    lam_re, lam_im, bre, bim, cre, cim, d = ws
    gp = lam_re.shape[0] * LANES
    nv = SUBLANES
    assert nv == 2 * nb, "two pieces per sequence"
    plen, half_len = seq // 2, seq // 4
    assert plen % ts == 0 and half_len % tf == 0 and gp % lane_block == 0
    assert half_len & (half_len - 1) == 0, "repeated squaring needs a power of two"
    nblk = plen // ts
    u_spec = lambda n: pl.BlockSpec((ts, w_b), lambda s: (n * nblk + s, ucol))
    state = pl.BlockSpec((nv, gp), lambda s: (0, 0))
    state_shape = jax.ShapeDtypeStruct((nv, gp), F32)
    scratch = lambda steps: [pltpu.VMEM((nv * steps, gp), F32)] * 2 + [pltpu.VMEM((nv, gp), F32)] * 2
    y, fr, fi = pl.pallas_call(
        functools.partial(_s5_prompt_kernel, nv=nv, lane_block=lane_block),
        grid=(nblk,),
        in_specs=[u_spec(n) for n in range(nv)] + _s5_weights_specs(ws),
        out_specs=[pl.BlockSpec((nv, ts, w_b), lambda s: (0, s, 0)), state, state],
        out_shape=[jax.ShapeDtypeStruct((nv, plen, w_b), F32), state_shape, state_shape],
        scratch_shapes=scratch(ts),
        compiler_params=_params("arbitrary"),
        name="s5_prompt",
    )(*([proj] * nv), *ws)

    y5 = y.reshape(nb, 2, 2, half_len, w_b)
    y_spec = pl.BlockSpec((nb, None, 2, tf, w_b), lambda s: (0, 1, 0, s, 0))
    y5, xr, xi = pl.pallas_call(
        functools.partial(_s5_carry_kernel, half_len=half_len, lane_block=lane_block),
        grid=(half_len // tf,),
        in_specs=[y_spec, state, state] + _s5_weights_specs((lam_re, lam_im, cre, cim)),
        out_specs=[y_spec, state, state],
        out_shape=[jax.ShapeDtypeStruct(y5.shape, F32), state_shape, state_shape],
        scratch_shapes=scratch(tf),
        input_output_aliases={0: 0},
        compiler_params=_params("arbitrary"),
        name="s5_prompt_carry",
    )(y5, fr, fi, lam_re, lam_im, cre, cim)
    last = lambda x: x.reshape(nb, 2, gp)[:, 1]
    return y5.reshape(nb * seq, w_b), last(xr), last(xi)


def _s5_sample_kernel(u_ref, x0r_ref, x0i_ref, lre_ref, lim_ref, bre_ref, bim_ref, cre_ref, cim_ref, d_ref,
                      y_ref, xr_out_ref, xi_out_ref, xr_ref, xi_ref, *, seq):
    nb = x0r_ref.shape[0]
    u = u_ref[...]
    _s5_in(u.astype(BF16), bre_ref, bim_ref, xr_ref, xi_ref)
    for m in range(xr_ref.shape[0]):
        ls = slice(m * LANES, (m + 1) * LANES)
        lr, li = lre_ref[m], lim_ref[m]
        xr, xi = x0r_ref[:, ls], x0i_ref[:, ls]
        for r in range(seq):
            rows = pl.ds(r, nb, stride=seq)
            xr, xi = _s5_step(lr, li, xr, xi, xr_ref[m, rows, :], xi_ref[m, rows, :])
            xr_ref[m, rows, :] = xr
            xi_ref[m, rows, :] = xi
        xr_out_ref[:, ls] = xr
        xi_out_ref[:, ls] = xi
    kc = cre_ref.shape[2]
    d = d_ref[...]
    for j, y in enumerate(_s5_out(xr_ref, xi_ref, cre_ref, cim_ref)):
        cols = slice(j * kc, (j + 1) * kc)
        y_ref[:, cols] = y + d[:, cols] * u[:, cols]


def _s5_sample(proj, x0_re, x0_im, ws, *, row0, ucol, seq, w_b, nb=64):
    batch, gp = x0_re.shape
    n_slabs = gp // LANES
    assert batch % nb == 0 and row0 % (nb * seq) == 0
    t0 = row0 // (nb * seq)
    return pl.pallas_call(
        functools.partial(_s5_sample_kernel, seq=seq),
        grid=(batch // nb,),
        in_specs=[
            pl.BlockSpec((nb * seq, w_b), lambda p: (t0 + p, ucol)),
            pl.BlockSpec((nb, gp), lambda p: (p, 0)),
            pl.BlockSpec((nb, gp), lambda p: (p, 0)),
        ] + _s5_weights_specs(ws),
        out_specs=[
            pl.BlockSpec((nb * seq, w_b), lambda p: (p, 0)),
            pl.BlockSpec((nb, gp), lambda p: (p, 0)),
            pl.BlockSpec((nb, gp), lambda p: (p, 0)),
        ],
        out_shape=[
            jax.ShapeDtypeStruct((batch * seq, w_b), F32),
            jax.ShapeDtypeStruct((batch, gp), F32),
            jax.ShapeDtypeStruct((batch, gp), F32),
        ],
        scratch_shapes=[pltpu.VMEM((n_slabs, nb * seq, LANES), F32), pltpu.VMEM((n_slabs, nb * seq, LANES), F32)],
        compiler_params=_params("arbitrary"),
        name="s5_sample",
    )(proj, x0_re, x0_im, *ws)


def _mix_gates(o_ref, gate_ref, y_ref, ng_ref, gluw_ref, glub_ref, a_ref, rows):
    w_a = o_ref.shape[1]
    o = o_ref[rows, :]
    var = jnp.mean(o * o, axis=-1, keepdims=True)
    gate = gate_ref[rows, :]
    a_ref[rows, :w_a] = (o * lax.rsqrt(var + EPS) * ng_ref[...] * (gate * jax.nn.sigmoid(gate))).astype(BF16)
    y = jax.nn.gelu(y_ref[rows, :])
    z = _dot(y.astype(BF16), gluw_ref[...]) + glub_ref[...]
    a_ref[rows, w_a:] = (y * jax.nn.sigmoid(z)).astype(BF16)


def _mix_kernel(o_ref, gate_ref, y_ref, x_ref, ng_ref, gluw_ref, glub_ref, wout_ref, out_ref, *rest,
                n_tiles, n_cols):
    *copies, a0_ref, a1_ref = rest
    glu16_ref, wout16_ref = copies or (None, None)
    i = pl.program_id(0)
    j = pl.program_id(1)
    rows_per_step = a0_ref.shape[0] // n_cols

    if glu16_ref is not None:
        @pl.when(jnp.logical_and(i == 0, j == 0))
        def _():
            _bf16_weights(gluw_ref, glu16_ref)
        gluw_ref = glu16_ref

    def step(a_gate, a_proj):
        pieces = out_ref.shape[1] // MXU_DIM
        sub = rows_per_step // pieces
        for p in range(pieces):
            cols = (slice(None), slice(p * MXU_DIM, (p + 1) * MXU_DIM))
            if a_gate is not None:
                sub_rows = pl.ds(pl.multiple_of(j * rows_per_step + p * sub, sub), sub)
                _mix_gates(o_ref, gate_ref, y_ref, ng_ref, gluw_ref, glub_ref, a_gate, sub_rows)
            w = _bf16_weights(wout_ref, wout16_ref, cols)
            if a_proj is not None:
                out_ref[cols] = x_ref[cols] + _dot(a_proj[...], w)

    even = i % 2 == 0
    inner = jnp.logical_and(i > 0, i < n_tiles)
    pl.when(i == 0)(lambda: step(a0_ref, None))
    pl.when(jnp.logical_and(inner, even))(lambda: step(a0_ref, a1_ref))
    pl.when(jnp.logical_and(inner, jnp.logical_not(even)))(lambda: step(a1_ref, a0_ref))
    pl.when(jnp.logical_and(i == n_tiles, even))(lambda: step(None, a1_ref))
    pl.when(jnp.logical_and(i == n_tiles, jnp.logical_not(even)))(lambda: step(None, a0_ref))


def _mix(o, proj, y, x, norm_g, glu_w, glu_b, w_out, *, gate_col, name, tn=512):
    m, w_a = o.shape
    w_b = y.shape[1]
    d = w_out.shape[1]
    tm = _row_tile(m)
    nj = d // tn
    assert m % tm == 0 and d % tn == 0 and tm % (nj * SUBLANES) == 0
    n = m // tm
    nxt = lambda i: jnp.minimum(i, n - 1)
    cur = lambda i: jnp.maximum(i - 1, 0)
    const = lambda i, j: (0, 0)
    cur_tile = lambda i, j: (cur(i), jnp.where(i > 0, j, 0))
    glu_spec = ((w_b, w_b), const)
    wout_spec = ((w_a + w_b, tn), lambda i, j: (0, j))
    assert glu_w.dtype == w_out.dtype
    copy_specs, copy_shapes = (a + b for a, b in zip(_weight_copy(glu_w, *glu_spec, (0, 0)),
                                                     _weight_copy(w_out, *wout_spec, (0, nj - 1))))
    return pl.pallas_call(
        functools.partial(_mix_kernel, n_tiles=n, n_cols=nj),
        grid=(n + 1, nj),
        in_specs=[
            pl.BlockSpec((tm, w_a), lambda i, j: (nxt(i), 0)),
            pl.BlockSpec((tm, w_a), lambda i, j: (nxt(i), gate_col)),
            pl.BlockSpec((tm, w_b), lambda i, j: (nxt(i), 0)),
            pl.BlockSpec((tm, tn), cur_tile),
            pl.BlockSpec((1, w_a), const),
            pl.BlockSpec(*glu_spec),
            pl.BlockSpec((1, w_b), const),
            pl.BlockSpec(*wout_spec),
        ],
        out_specs=[pl.BlockSpec((tm, tn), cur_tile)] + copy_specs,
        out_shape=[jax.ShapeDtypeStruct((m, d), F32)] + copy_shapes,
        scratch_shapes=[pltpu.VMEM((tm, w_a + w_b), BF16)] * 2,
        compiler_params=_params("arbitrary", "arbitrary"),
        name=name,
    )(o, proj, y, x, norm_g.reshape(1, w_a), glu_w, glu_b.reshape(1, w_b), w_out)


def _mlp_kernel(x_ref, g_ref, up_ref, down_ref, gf_ref, o_ref, *rest, rows, final_norm):
    *copies, h_ref = rest
    up16_ref, down16_ref = copies or (None, None)
    f = pl.program_id(1)

    @pl.when(f == 0)
    def _():
        _rmsnorm_rows(x_ref, g_ref[...], h_ref, rows)
        o_ref[...] = x_ref[...]

    a = jnp.square(jnp.maximum(_dot(h_ref[...], _bf16_weights(up_ref, up16_ref)), 0.0)).astype(BF16)
    o_ref[...] += _dot(a, _bf16_weights(down_ref, down16_ref))

    if final_norm:
        @pl.when(f == pl.num_programs(1) - 1)
        def _():
            _rmsnorm_rows(o_ref, gf_ref[...], o_ref, rows)


def _mlp(x, g, up, down, gf, *, final_norm, name, tf=512):
    m, d = x.shape
    dff = up.shape[1]
    tm = _row_tile(m)
    assert m % tm == 0 and dff % tf == 0 and up.dtype == down.dtype
    up_spec = ((d, tf), lambda i, f: (0, f))
    down_spec = ((tf, d), lambda i, f: (f, 0))
    nf = dff // tf
    copy_specs, copy_shapes = (a + b for a, b in zip(_weight_copy(up, *up_spec, (0, nf - 1)),
                                                     _weight_copy(down, *down_spec, (nf - 1, 0))))
    return pl.pallas_call(
        functools.partial(_mlp_kernel, rows=tm, final_norm=final_norm),
        grid=(m // tm, dff // tf),
        in_specs=[
            pl.BlockSpec((tm, d), lambda i, f: (i, 0)),
            pl.BlockSpec((1, d), lambda i, f: (0, 0)),
            pl.BlockSpec(*up_spec),
            pl.BlockSpec(*down_spec),
            pl.BlockSpec((1, d), lambda i, f: (0, 0)),
        ],
        out_specs=[pl.BlockSpec((tm, d), lambda i, f: (i, 0))] + copy_specs,
        out_shape=[jax.ShapeDtypeStruct((m, d), F32)] + copy_shapes,
        scratch_shapes=[pltpu.VMEM((tm, d), BF16)],
        compiler_params=_params("arbitrary", "arbitrary"),
        name=name,
    )(x, g.reshape(1, d), up, down, gf.reshape(1, d))


def kernel(x_prompt, x_sample, state_hgrn, state_s5_re, state_s5_im, w_in, w_out, norm1_g, norm2_g, hgrn_lb_logits, hgrn_norm_g, s5_a_re, s5_a_im, s5_b_re, s5_b_im, s5_c_re, s5_c_im, s5_d, s5_log_step, glu_w, glu_b, mlp_up, mlp_down, final_norm_g):
    bp, seq, d = x_prompt.shape
    bs, dseq, _ = x_sample.shape
    depth = w_in.shape[0]
    w_a = hgrn_norm_g.shape[1]
    w_b = s5_d.shape[1]
    g_b, p_state = s5_a_re.shape[1], s5_a_re.shape[2]
    mp, ms = bp * seq, bs * dseq
    assert w_in.shape[2] == 4 * w_a + w_b and w_a == w_b and g_b * S5_GROUP == w_b
    gate_col, ucol = 3, (4 * w_a) // w_b

    xp = x_prompt.reshape(mp, d)
    xs = x_sample.reshape(ms, d)
    lb_logits = hgrn_lb_logits.astype(F32)
    new_states = []
    for l in range(depth):
        proj_s, w_in16 = _inproj(xs, norm1_g[l], w_in[l].astype(F32), name="inproj_sample")
        proj_p, = _inproj(xp, norm1_g[l], w_in16, name="inproj_prompt")

        o_p, sh_p = _hgrn_prompt(proj_p, lb_logits, layer=l, batch=bp, seq=seq, w_a=w_a)
        o_s, sh_s = _hgrn_sample(proj_s, lb_logits, state_hgrn, layer=l, row0=0, batch=bs, seq=dseq, w_a=w_a)

        lam_re, lam_im, bb_re, bb_im = _s5_discretise(
            s5_a_re[l].astype(F32), s5_a_im[l].astype(F32), s5_log_step[l].astype(F32),
            s5_b_re[l].astype(F32), s5_b_im[l].astype(F32))
        ws = (
            lam_re.reshape(-1, 1, LANES), lam_im.reshape(-1, 1, LANES),
            _block_diag_in(bb_re).astype(BF16), _block_diag_in(bb_im).astype(BF16),
            _block_diag_out(s5_c_re[l].astype(F32)).astype(BF16), _block_diag_out(s5_c_im[l].astype(F32)).astype(BF16),
            s5_d[l].astype(F32).reshape(1, w_b),
        )
        y_p, sr_p, si_p = _s5_prompt(proj_p, ws, nb=bp, seq=seq, ucol=ucol, w_b=w_b)
        y_s, sr_s, si_s = _s5_sample(
            proj_s, state_s5_re[l].reshape(bs, g_b * p_state).astype(F32), state_s5_im[l].reshape(bs, g_b * p_state).astype(F32),
            ws, row0=0, ucol=ucol, seq=dseq, w_b=w_b)

        last = l == depth - 1
        x1_s, glu16, w_out16 = _mix(o_s, proj_s, y_s, xs, hgrn_norm_g[l], glu_w[l].astype(F32), glu_b[l],
                                    w_out[l].astype(F32), gate_col=gate_col, name="mix_sample")
        xs, up16, down16 = _mlp(x1_s, norm2_g[l], mlp_up[l].astype(F32), mlp_down[l].astype(F32), final_norm_g,
                                final_norm=last, name="mlp_sample")
        x1_p, = _mix(o_p, proj_p, y_p, xp, hgrn_norm_g[l], glu16, glu_b[l], w_out16, gate_col=gate_col,
                     name="mix_prompt")
        xp, = _mlp(x1_p, norm2_g[l], up16, down16, final_norm_g, final_norm=last, name="mlp_prompt")
        new_states.append((sh_p, sr_p.reshape(bp, g_b, p_state), si_p.reshape(bp, g_b, p_state),
                           sh_s[0], sr_s.reshape(bs, g_b, p_state), si_s.reshape(bs, g_b, p_state)))

    y_prompt = xp.reshape(bp, seq, d).astype(x_prompt.dtype)
    y_sample = xs.reshape(bs, dseq, d).astype(x_sample.dtype)
    stacked = [jnp.stack([st[i] for st in new_states]) for i in range(6)]
    return (y_prompt, y_sample, *stacked)
```

```python
import functools

import jax
import jax.numpy as jnp
import numpy as np
from jax import lax
from jax.experimental import pallas as pl
from jax.experimental.pallas import tpu as pltpu

F32 = jnp.float32
BF16 = jnp.bfloat16
EPS = 1e-6

LANES = 128
SUBLANES = 8
MXU_DIM = 256
VMEM_LIMIT = 56 * 1024 * 1024

HEAD_DIM = 128
S5_GROUP = 16
HGRN_CHUNK = 64
ROW_TILE = 1024
NEG_BIG = -1e30


def _dot(a, b):
    return jnp.dot(a, b, preferred_element_type=F32)


def _dot_nt(a, b):
    return lax.dot_general(a, b, (((1,), (1,)), ((), ())), preferred_element_type=F32)


def _dot_tn(a, b):
    return lax.dot_general(a, b, (((0,), (0,)), ((), ())), preferred_element_type=F32)


def _split3(x):
    hi = x.astype(BF16)
    r1 = x - hi.astype(F32)
    mid = r1.astype(BF16)
    lo = (r1 - mid.astype(F32)).astype(BF16)
    return hi, mid, lo


def _params(*sem):
    return pltpu.CompilerParams(dimension_semantics=sem, vmem_limit_bytes=VMEM_LIMIT)


def _rmsnorm_rows(src_ref, g, dst_ref, rows, chunk=64):
    def body(c, carry):
        r0 = pl.multiple_of(c * chunk, chunk)
        x = src_ref[pl.ds(r0, chunk), :]
        var = jnp.mean(x * x, axis=-1, keepdims=True)
        dst_ref[pl.ds(r0, chunk), :] = (x * lax.rsqrt(var + EPS) * g).astype(dst_ref.dtype)
        return carry

    lax.fori_loop(0, rows // chunk, body, 0)


def _bf16_weights(w_ref, copy_ref, at=(Ellipsis,)):
    w = w_ref[at]
    if copy_ref is None:
        return w
    w = w.astype(BF16)

    @pl.when(pl.program_id(0) == 0)
    def _():
        copy_ref[at] = w

    return w


def _weight_copy(w, block, index_map, last_index):
    if w.dtype == BF16:
        return [], []
    once = lambda i, j: jax.tree.map(lambda a, b: jnp.where(i == 0, a, b), index_map(i, j), last_index)
    return [pl.BlockSpec(block, once)], [jax.ShapeDtypeStruct(w.shape, BF16)]


def _inproj_kernel(x_ref, g_ref, w_ref, o_ref, *rest, rows):
    *w16_ref, h_ref = rest

    @pl.when(pl.program_id(1) == 0)
    def _():
        _rmsnorm_rows(x_ref, g_ref[...], h_ref, rows)

    o_ref[...] = _dot(h_ref[...], _bf16_weights(w_ref, *(w16_ref or [None])))


def _row_tile(m):
    return ROW_TILE if m % ROW_TILE == 0 else ROW_TILE // 2


def _inproj(x, g, w, *, name, tn=1024):
    m, d = x.shape
    n = w.shape[1]
    tm = _row_tile(m)
    assert m % tm == 0 and n % tn == 0
    w_spec = ((d, tn), lambda i, j: (0, j))
    copy_specs, copy_shapes = _weight_copy(w, *w_spec, (0, n // tn - 1))
    return pl.pallas_call(
        functools.partial(_inproj_kernel, rows=tm),
        grid=(m // tm, n // tn),
        in_specs=[
            pl.BlockSpec((tm, d), lambda i, j: (i, 0)),
            pl.BlockSpec((1, d), lambda i, j: (0, 0)),
            pl.BlockSpec(*w_spec),
        ],
        out_specs=[pl.BlockSpec((tm, tn), lambda i, j: (i, j))] + copy_specs,
        out_shape=[jax.ShapeDtypeStruct((m, n), F32)] + copy_shapes,
        scratch_shapes=[pltpu.VMEM((tm, d), BF16)],
        compiler_params=_params("arbitrary", "arbitrary"),
        name=name,
    )(x, g.reshape(1, d), w)


def _lower_bound(logits, layer):
    m = jnp.max(logits, axis=0, keepdims=True)
    e = jnp.exp(logits - m)
    return jnp.sum(e[: layer + 1], axis=0, keepdims=True) / jnp.sum(e, axis=0, keepdims=True)


def _row_to_col(e_row):
    n = e_row.shape[1]
    hi, mid, lo = (p.astype(F32) for p in _split3(e_row))
    r = lax.broadcasted_iota(jnp.int32, (2 * SUBLANES, n), 0)
    pieces = jnp.where(r == 0, hi, jnp.where(r == 1, mid, jnp.where(r == 2, lo, 0.0)))
    return _dot_tn(pieces.astype(BF16), jnp.ones((2 * SUBLANES, LANES), BF16))


def _hgrn_gates(qp, fp, lb):
    q = qp * jax.nn.sigmoid(qp)
    f = lb + (1.0 - lb) * jax.nn.sigmoid(fp)
    return q, f, 1.0 - f, jnp.log(f)


def _hgrn_tables(c):
    t = np.arange(c)[:, None]
    j = np.arange(c)[None, :]
    sums, pairs = [j <= t], []
    bs = c
    while bs >= 2:
        hs = bs // 2
        ref = t - t % bs + hs - 1
        upper = t % bs >= hs
        sums.append(np.where(upper, (j > ref) & (j <= t), (j > t) & (j <= ref)))
        pairs.append((t // bs == j // bs) & upper & (j % bs < hs))
        bs = hs
    sums.append(j > t)
    pairs.append(t == j)
    sums = np.concatenate(sums, axis=0).astype(np.float32)
    return jnp.asarray(np.concatenate([sums] * 3, axis=1), BF16), jnp.asarray(np.stack(pairs), F32)


def _upper_q_lower_k(q, k, bs):
    c = q.shape[0]
    hs = bs // 2
    if hs % SUBLANES == 0:
        parts = []
        for r0 in range(0, c, bs):
            parts += [k[r0 : r0 + hs], q[r0 + hs : r0 + bs]]
        return jnp.concatenate(parts, axis=0)
    row = lax.broadcasted_iota(jnp.int32, (c, 1), 0)
    return jnp.where(row % bs >= hs, q, k)


def _hgrn_chunks(inputs, lb, st_ref, sums, pairs_ref, heads):
    levels = pairs_ref.shape[0] - 1
    prep = []
    for qp, fp, v in inputs:
        c = qp.shape[0]
        q = qp * jax.nn.sigmoid(qp)
        f = lb + (1.0 - lb) * jax.nn.sigmoid(fp)
        k = 1.0 - f
        e = jnp.exp2(_dot(sums, jnp.concatenate(_split3(jnp.log2(f)), axis=0)))
        e_b, e_rest = e[:c], e[(levels + 1) * c :]
        ys = [(_upper_q_lower_k(q, k, c >> l) * e[(l + 1) * c : (l + 2) * c]).astype(BF16) for l in range(levels)]
        prep.append(dict(ys=ys, q16=q.astype(BF16), k16=k.astype(BF16), v16=v.astype(BF16),
                         qe=(q * e_b).astype(BF16), kd=(k * e_rest).astype(BF16), e_last=e_b[c - 1 : c]))

    pairs = [(n, h) for n in range(len(inputs)) for h in range(heads)]
    head = lambda x, h: x[:, h * HEAD_DIM : (h + 1) * HEAD_DIM]
    scores = {}
    for n, h in pairs:
        p = prep[n]
        a = _dot_nt(head(p["q16"], h), head(p["k16"], h)) * pairs_ref[levels]
        for l in range(levels):
            y = head(p["ys"][l], h)
            a = a + _dot_nt(y, y) * pairs_ref[l]
        scores[n, h] = a.astype(BF16)
    outs = {}
    for n, h in pairs:
        p = prep[n]
        st = st_ref[n, h]
        outs[n, h] = _dot_nt(head(p["qe"], h), st.astype(BF16)) + _dot(scores[n, h], head(p["v16"], h))
        st_ref[n, h] = head(p["e_last"], h) * st + _dot_tn(head(p["v16"], h), head(p["kd"], h))
    return [jnp.concatenate([outs[n, h] for h in range(heads)], axis=1) for n in range(len(inputs))]


def _hgrn_prompt_kernel(q_ref, f_ref, v_ref, lbl_ref, sums_ref, pairs_ref, o_ref, s_out_ref, st_ref, *,
                        layer, heads, n_chunks):
    t = pl.program_id(1)

    @pl.when(t == 0)
    def _():
        st_ref[...] = jnp.zeros_like(st_ref)

    lb = _lower_bound(lbl_ref[...], layer)
    n_seq = q_ref.shape[0]

    def body(c, carry):
        rows = pl.ds(pl.multiple_of(c * HGRN_CHUNK, HGRN_CHUNK), HGRN_CHUNK)
        inputs = [(q_ref[n, rows, :], f_ref[n, rows, :], v_ref[n, rows, :]) for n in range(n_seq)]
        for n, o in enumerate(_hgrn_chunks(inputs, lb, st_ref, sums_ref[...], pairs_ref, heads)):
            o_ref[n, rows, :] = o
        return carry

    lax.fori_loop(0, n_chunks, body, 0)

    @pl.when(t == pl.num_programs(1) - 1)
    def _():
        for n in range(n_seq):
            for h in range(heads):
                s_out_ref[n, h] = st_ref[n, h].T


def _hgrn_prompt(proj, lb_logits, *, layer, batch, seq, w_a, tt=256, n_seq=2):
    heads = w_a // HEAD_DIM
    assert seq % tt == 0 and tt % HGRN_CHUNK == 0 and batch % n_seq == 0
    sums, pairs = _hgrn_tables(HGRN_CHUNK)
    proj3 = proj.reshape(batch, seq, proj.shape[1])
    col = lambda c: pl.BlockSpec((n_seq, tt, w_a), lambda b, t: (b, t, c))
    const = lambda x: pl.BlockSpec(x.shape, lambda b, t: (0,) * x.ndim)
    o, s = pl.pallas_call(
        functools.partial(_hgrn_prompt_kernel, layer=layer, heads=heads, n_chunks=tt // HGRN_CHUNK),
        grid=(batch // n_seq, seq // tt),
        in_specs=[col(0), col(1), col(2), const(lb_logits), const(sums), const(pairs)],
        out_specs=[
            pl.BlockSpec((n_seq, tt, w_a), lambda b, t: (b, t, 0)),
            pl.BlockSpec((n_seq, heads, HEAD_DIM, HEAD_DIM), lambda b, t: (b, 0, 0, 0)),
        ],
        out_shape=[
            jax.ShapeDtypeStruct((batch, seq, w_a), F32),
            jax.ShapeDtypeStruct((batch, heads, HEAD_DIM, HEAD_DIM), F32),
        ],
        scratch_shapes=[pltpu.VMEM((n_seq, heads, HEAD_DIM, HEAD_DIM), F32)],
        compiler_params=_params("arbitrary", "arbitrary"),
        name="hgrn_prompt",
    )(proj3, proj3, proj3, lb_logits, sums, pairs)
    return o.reshape(batch * seq, w_a), s


def _hgrn_sample_kernel(q_ref, f_ref, v_ref, lbl_ref, s_in_ref, o_ref, s_out_ref, *, layer, heads, seq):
    n_seq = SUBLANES // seq
    lb = _lower_bound(lbl_ref[...], layer)
    row = lax.broadcasted_iota(jnp.int32, (SUBLANES, 1), 0)
    pos = row % seq
    work = []
    for tile, h in ((tile, h) for tile in range(q_ref.shape[0] // SUBLANES) for h in range(heads)):
        cols = slice(h * HEAD_DIM, (h + 1) * HEAD_DIM)
        rows = slice(tile * SUBLANES, (tile + 1) * SUBLANES)
        q, _, k, logf = _hgrn_gates(q_ref[rows, cols], f_ref[rows, cols], lb[:, cols])
        v = v_ref[rows, cols]
        b = logf
        shift = 1
        while shift < seq:
            b = b + jnp.where(pos >= shift, pltpu.roll(b, shift, 0), 0.0)
            shift *= 2
        o = jnp.zeros((SUBLANES, HEAD_DIM), F32)
        for sl in range(SUBLANES):
            lo_t, hi_t = sl, (sl // seq + 1) * seq
            mask = jnp.logical_and(row >= lo_t, row < hi_t)
            z = q * jnp.exp(jnp.where(mask, b - b[sl : sl + 1], NEG_BIG)) * k[sl : sl + 1]
            o = o + jnp.sum(z, axis=-1, keepdims=True) * v[sl : sl + 1]
        per_seq = []
        for n in range(n_seq):
            own = jnp.logical_and(row >= n * seq, row < (n + 1) * seq)
            b_last = b[(n + 1) * seq - 1 : (n + 1) * seq, :]
            qe = jnp.where(own, q * jnp.exp(b), 0.0).astype(BF16)
            kd = jnp.where(own, k * jnp.exp(jnp.where(own, b_last - b, 0.0)), 0.0).astype(BF16)
            per_seq.append((tile * n_seq + n, qe, kd, jnp.exp(b_last)))
        work.append((rows, cols, h, o, v.astype(BF16), per_seq))
    for rows, cols, h, o, _, per_seq in work:
        for i, qe, _, _ in per_seq:
            o = o + _dot(qe, s_in_ref[0, i, h].astype(BF16))
        o_ref[rows, cols] = o
    for _, _, h, _, v16, per_seq in work:
        for i, _, kd, e_last in per_seq:
            s_out_ref[0, i, h] = _row_to_col(e_last) * s_in_ref[0, i, h] + _dot_tn(kd, v16)


def _hgrn_sample(proj, lb_logits, state, *, layer, row0, batch, seq, w_a, tiles=4):
    heads = w_a // HEAD_DIM
    rows = tiles * SUBLANES
    assert SUBLANES % seq == 0 and row0 % rows == 0 and (batch * seq) % rows == 0
    n_seq = rows // seq
    t0 = row0 // rows
    col = lambda c: pl.BlockSpec((rows, w_a), lambda p: (t0 + p, c))
    st = pl.BlockSpec((1, n_seq, heads, HEAD_DIM, HEAD_DIM), lambda p: (layer, p, 0, 0, 0))
    st_out = pl.BlockSpec((1, n_seq, heads, HEAD_DIM, HEAD_DIM), lambda p: (0, p, 0, 0, 0))
    return pl.pallas_call(
        functools.partial(_hgrn_sample_kernel, layer=layer, heads=heads, seq=seq),
        grid=(batch // n_seq,),
        in_specs=[col(0), col(1), col(2), pl.BlockSpec(lb_logits.shape, lambda p: (0, 0)), st],
        out_specs=[pl.BlockSpec((rows, w_a), lambda p: (p, 0)), st_out],
        out_shape=[
            jax.ShapeDtypeStruct((batch * seq, w_a), F32),
            jax.ShapeDtypeStruct((1, batch, heads, HEAD_DIM, HEAD_DIM), F32),
        ],
        compiler_params=_params("arbitrary"),
        name="hgrn_sample",
    )(proj, proj, proj, lb_logits, state)


def _s5_weights_kernel(are_ref, aim_ref, ls_ref, bre_ref, bim_ref, cre_ref, cim_ref, rep_ref, tile_ref, same_ref,
                       lre_ref, lim_ref, bbre_ref, bbim_ref, ccre_ref, ccim_ref):
    a_re = are_ref[...]
    a_im = aim_ref[...]
    dt = jnp.exp(ls_ref[...])
    mag = jnp.exp(a_re * dt)
    lam_re = mag * jnp.cos(a_im * dt)
    lam_im = mag * jnp.sin(a_im * dt)
    den = a_re * a_re + a_im * a_im
    nr, ni = lam_re - 1.0, lam_im
    r_re = (nr * a_re + ni * a_im) / den
    r_im = (ni * a_re - nr * a_im) / den
    lre_ref[...] = lam_re
    lim_ref[...] = lam_im
    b_re = bre_ref[...]
    b_im = bim_ref[...]
    same = same_ref[...]
    bbre_ref[0] = (_dot(rep_ref[...], (r_re * b_re - r_im * b_im).astype(BF16)) * same).astype(BF16)
    bbim_ref[0] = (_dot(rep_ref[...], (r_re * b_im + r_im * b_re).astype(BF16)) * same).astype(BF16)
    ccre_ref[0] = (_dot(cre_ref[...].astype(BF16), tile_ref[...]) * same).astype(BF16)
    ccim_ref[0] = (_dot(cim_ref[...].astype(BF16), tile_ref[...]) * same).astype(BF16)


def _s5_weights(a_re, a_im, log_step, b_re, b_im, c_re, c_im):
    g, p = a_re.shape
    c = b_re.shape[-1]
    gt = MXU_DIM // c
    nj, kc, kp = g // gt, gt * c, gt * p
    flat = lambda x: x.reshape(1, g * p)
    chan = lambda x: jnp.transpose(x, (2, 0, 1)).reshape(c, g * p)
    row_g, lane_g = np.arange(kc)[:, None] // c, np.arange(kp)[None, :] // p
    rep = jnp.asarray(np.arange(kc)[:, None] % c == np.arange(c)[None, :], BF16)
    tile = jnp.asarray(np.arange(p)[:, None] == np.arange(kp)[None, :] % p, BF16)
    same = jnp.asarray(row_g == lane_g, F32)
    lanes = lambda rows: pl.BlockSpec((rows, kp), lambda j: (0, j))
    const = lambda x: pl.BlockSpec(x.shape, lambda j: (0, 0))
    tiles = pl.BlockSpec((1, kc, kp), lambda j: (j, 0, 0))
    tiles_shape = jax.ShapeDtypeStruct((nj, kc, kp), BF16)
    lam_shape = jax.ShapeDtypeStruct((1, g * p), F32)
    lam_re, lam_im, *mats = pl.pallas_call(
        _s5_weights_kernel,
        grid=(nj,),
        in_specs=[lanes(1), lanes(1), lanes(1), lanes(c), lanes(c),
                  pl.BlockSpec((kc, p), lambda j: (j, 0)), pl.BlockSpec((kc, p), lambda j: (j, 0)),
                  const(rep), const(tile), const(same)],
        out_specs=[lanes(1), lanes(1), tiles, tiles, tiles, tiles],
        out_shape=[lam_shape, lam_shape] + [tiles_shape] * 4,
        compiler_params=_params("arbitrary"),
        name="s5_weights",
    )(flat(a_re), flat(a_im), flat(jnp.repeat(log_step, p)), chan(b_re), chan(b_im),
      c_re.reshape(g * c, p), c_im.reshape(g * c, p), rep, tile, same)
    return (lam_re.reshape(-1, 1, LANES), lam_im.reshape(-1, 1, LANES), *mats)


def _s5_lanes(x_ref, lo, hi):
    if len(x_ref.shape) == 2:
        return x_ref[:, lo:hi]
    return jnp.concatenate([x_ref[m] for m in range(lo // LANES, hi // LANES)], axis=1)


def _s5_in(ub, bre_ref, bim_ref, xr_ref, xi_ref):
    nj, kc, kp = bre_ref.shape
    for j in range(nj):
        uj = ub[:, j * kc : (j + 1) * kc]
        for x_ref, w_ref in ((xr_ref, bre_ref), (xi_ref, bim_ref)):
            bu = _dot(uj, w_ref[j])
            if len(x_ref.shape) == 2:
                x_ref[:, j * kp : (j + 1) * kp] = bu
            else:
                for m in range(kp // LANES):
                    x_ref[j * kp // LANES + m] = bu[:, m * LANES : (m + 1) * LANES]


def _s5_out(xr_ref, xi_ref, cre_ref, cim_ref):
    nj, kc, kp = cre_ref.shape
    tiles = []
    for j in range(nj):
        xr = _s5_lanes(xr_ref, j * kp, (j + 1) * kp).astype(BF16)
        xi = _s5_lanes(xi_ref, j * kp, (j + 1) * kp).astype(BF16)
        tiles.append(_dot_nt(xr, cre_ref[j]) - _dot_nt(xi, cim_ref[j]))
    return tiles


def _s5_step(lr, li, xr, xi, bur, bui):
    return lr * xr - li * xi + bur, lr * xi + li * xr + bui


def _s5_lambda(lre_ref, lim_ref, lo, hi, rows):
    cat = lambda ref: jnp.concatenate([ref[m] for m in range(lo // LANES, hi // LANES)], axis=1)
    return jnp.broadcast_to(cat(lre_ref), (rows, hi - lo)), jnp.broadcast_to(cat(lim_ref), (rows, hi - lo))


def _to_time_major(x):
    n, ts, w = x.shape
    return jnp.swapaxes(x, 0, 1).reshape(ts * n, w)


def _from_time_major(x, n):
    rows, w = x.shape
    return jnp.swapaxes(x.reshape(rows // n, n, w), 0, 1)


def _s5_prompt_kernel(*refs, nv, lane_block):
    u_refs = refs[:nv]
    (lre_ref, lim_ref, bre_ref, bim_ref, cre_ref, cim_ref, d_ref,
     y_ref, xr_out_ref, xi_out_ref, xr_ref, xi_ref, sr_ref, si_ref) = refs[nv:]
    ts = u_refs[0].shape[0]
    gp = xr_ref.shape[1]
    step = pl.program_id(0)

    @pl.when(step == 0)
    def _():
        sr_ref[...] = jnp.zeros_like(sr_ref)
        si_ref[...] = jnp.zeros_like(si_ref)

    u = _to_time_major(jnp.stack([r[...] for r in u_refs], axis=0))
    _s5_in(u.astype(BF16), bre_ref, bim_ref, xr_ref, xi_ref)

    for lo in range(0, gp, lane_block):
        ls = slice(lo, lo + lane_block)
        lr, li = _s5_lambda(lre_ref, lim_ref, lo, lo + lane_block, nv)

        xr, xi = sr_ref[:, ls], si_ref[:, ls]
        for r in range(ts):
            rows = slice(r * nv, (r + 1) * nv)
            xr, xi = _s5_step(lr, li, xr, xi, xr_ref[rows, ls], xi_ref[rows, ls])
            xr_ref[rows, ls] = xr
            xi_ref[rows, ls] = xi
        sr_ref[:, ls] = xr
        si_ref[:, ls] = xi

    kc = cre_ref.shape[1]
    d = d_ref[...]
    for j, y in enumerate(_s5_out(xr_ref, xi_ref, cre_ref, cim_ref)):
        cols = slice(j * kc, (j + 1) * kc)
        y_ref[:, :, cols] = _from_time_major(y + d[:, cols] * u[:, cols], nv)

    @pl.when(step == pl.num_programs(0) - 1)
    def _():
        xr_out_ref[...] = sr_ref[...]
        xi_out_ref[...] = si_ref[...]


def _s5_carry_kernel(y_ref, fr_ref, fi_ref, lre_ref, lim_ref, cre_ref, cim_ref,
                     o_ref, xr_out_ref, xi_out_ref, zr_ref, zi_ref, sr_ref, si_ref, *, half_len, lane_block):
    nrow, gp = fr_ref.shape
    tf = y_ref.shape[2]
    step = pl.program_id(0)
    odd = lax.broadcasted_iota(jnp.int32, (nrow, 1), 0) % 2 == 1

    @pl.when(step == 0)
    def _():
        for lo in range(0, gp, lane_block):
            ls = slice(lo, lo + lane_block)
            pr, pi = _s5_lambda(lre_ref, lim_ref, lo, lo + lane_block, nrow)
            n = 1
            while n < half_len:
                pr, pi = pr * pr - pi * pi, 2.0 * pr * pi
                n *= 2
            fr, fi = fr_ref[:, ls], fi_ref[:, ls]
            fr = jnp.where(odd, pltpu.roll(fr, 1, 0), fr)
            fi = jnp.where(odd, pltpu.roll(fi, 1, 0), fi)
            sr_ref[:, ls] = jnp.where(odd, pr * fr - pi * fi, fr)
            si_ref[:, ls] = jnp.where(odd, pr * fi + pi * fr, fi)

    for lo in range(0, gp, lane_block):
        ls = slice(lo, lo + lane_block)
        lr, li = _s5_lambda(lre_ref, lim_ref, lo, lo + lane_block, nrow)

        zr, zi = sr_ref[:, ls], si_ref[:, ls]
        for r in range(tf):
            rows = slice(r * nrow, (r + 1) * nrow)
            zr, zi = lr * zr - li * zi, lr * zi + li * zr
            zr_ref[rows, ls] = zr
            zi_ref[rows, ls] = zi
        sr_ref[:, ls] = zr
        si_ref[:, ls] = zi

    kc = cre_ref.shape[1]
    nseq = y_ref.shape[0]
    for j, y in enumerate(_s5_out(zr_ref, zi_ref, cre_ref, cim_ref)):
        cols = slice(j * kc, (j + 1) * kc)
        o_ref[:, :, :, cols] = y_ref[:, :, :, cols] + _from_time_major(y, nrow).reshape(nseq, nrow // nseq, tf, kc)

    @pl.when(step == pl.num_programs(0) - 1)
    def _():
        xr_out_ref[...] = fr_ref[...] + sr_ref[...]
        xi_out_ref[...] = fi_ref[...] + si_ref[...]


def _s5_weights_specs(ws):
    zero = lambda n: (lambda *_: (0,) * n)
    return [pl.BlockSpec(w.shape, zero(w.ndim)) for w in ws]


def _s5_prompt(proj, ws, *, nb, seq, ucol, w_b, ts=64, tf=64, lane_block=1024):
    lam_re, lam_im, bre, bim, cre, cim, d = ws
    gp = lam_re.shape[0] * LANES
    nv = SUBLANES
    assert nv == 2 * nb, "two pieces per sequence"
    plen, half_len = seq // 2, seq // 4
    assert plen % ts == 0 and half_len % tf == 0 and gp % lane_block == 0
    assert half_len & (half_len - 1) == 0, "repeated squaring needs a power of two"
    nblk = plen // ts
    u_spec = lambda n: pl.BlockSpec((ts, w_b), lambda s: (n * nblk + s, ucol))
    state = pl.BlockSpec((nv, gp), lambda s: (0, 0))
    state_shape = jax.ShapeDtypeStruct((nv, gp), F32)
    scratch = lambda steps: [pltpu.VMEM((nv * steps, gp), F32)] * 2 + [pltpu.VMEM((nv, gp), F32)] * 2
    y, fr, fi = pl.pallas_call(
        functools.partial(_s5_prompt_kernel, nv=nv, lane_block=lane_block),
        grid=(nblk,),
        in_specs=[u_spec(n) for n in range(nv)] + _s5_weights_specs(ws),
        out_specs=[pl.BlockSpec((nv, ts, w_b), lambda s: (0, s, 0)), state, state],
        out_shape=[jax.ShapeDtypeStruct((nv, plen, w_b), F32), state_shape, state_shape],
        scratch_shapes=scratch(ts),
        compiler_params=_params("arbitrary"),
        name="s5_prompt",
    )(*([proj] * nv), *ws)

    y5 = y.reshape(nb, 2, 2, half_len, w_b)
    y_spec = pl.BlockSpec((nb, None, 2, tf, w_b), lambda s: (0, 1, 0, s, 0))
    y5, xr, xi = pl.pallas_call(
        functools.partial(_s5_carry_kernel, half_len=half_len, lane_block=lane_block),
        grid=(half_len // tf,),
        in_specs=[y_spec, state, state] + _s5_weights_specs((lam_re, lam_im, cre, cim)),
        out_specs=[y_spec, state, state],
        out_shape=[jax.ShapeDtypeStruct(y5.shape, F32), state_shape, state_shape],
        scratch_shapes=scratch(tf),
        input_output_aliases={0: 0},
        compiler_params=_params("arbitrary"),
        name="s5_prompt_carry",
    )(y5, fr, fi, lam_re, lam_im, cre, cim)
    last = lambda x: x.reshape(nb, 2, gp)[:, 1]
    return y5.reshape(nb * seq, w_b), last(xr), last(xi)


def _s5_sample_kernel(u_ref, x0r_ref, x0i_ref, lre_ref, lim_ref, bre_ref, bim_ref, cre_ref, cim_ref, d_ref,
                      y_ref, xr_out_ref, xi_out_ref, xr_ref, xi_ref, *, seq):
    nb = x0r_ref.shape[0]
    u = u_ref[...]
    _s5_in(u.astype(BF16), bre_ref, bim_ref, xr_ref, xi_ref)
    for m in range(xr_ref.shape[0]):
        ls = slice(m * LANES, (m + 1) * LANES)
        lr, li = lre_ref[m], lim_ref[m]
        xr, xi = x0r_ref[:, ls], x0i_ref[:, ls]
        for r in range(seq):
            rows = pl.ds(r, nb, stride=seq)
            xr, xi = _s5_step(lr, li, xr, xi, xr_ref[m, rows, :], xi_ref[m, rows, :])
            xr_ref[m, rows, :] = xr
            xi_ref[m, rows, :] = xi
        xr_out_ref[:, ls] = xr
        xi_out_ref[:, ls] = xi
    kc = cre_ref.shape[1]
    d = d_ref[...]
    for j, y in enumerate(_s5_out(xr_ref, xi_ref, cre_ref, cim_ref)):
        cols = slice(j * kc, (j + 1) * kc)
        y_ref[:, cols] = y + d[:, cols] * u[:, cols]


def _s5_sample(proj, x0_re, x0_im, ws, *, row0, ucol, seq, w_b, nb=64):
    batch, gp = x0_re.shape
    n_slabs = gp // LANES
    assert batch % nb == 0 and row0 % (nb * seq) == 0
    t0 = row0 // (nb * seq)
    return pl.pallas_call(
        functools.partial(_s5_sample_kernel, seq=seq),
        grid=(batch // nb,),
        in_specs=[
            pl.BlockSpec((nb * seq, w_b), lambda p: (t0 + p, ucol)),
            pl.BlockSpec((nb, gp), lambda p: (p, 0)),
            pl.BlockSpec((nb, gp), lambda p: (p, 0)),
        ] + _s5_weights_specs(ws),
        out_specs=[
            pl.BlockSpec((nb * seq, w_b), lambda p: (p, 0)),
            pl.BlockSpec((nb, gp), lambda p: (p, 0)),
            pl.BlockSpec((nb, gp), lambda p: (p, 0)),
        ],
        out_shape=[
            jax.ShapeDtypeStruct((batch * seq, w_b), F32),
            jax.ShapeDtypeStruct((batch, gp), F32),
            jax.ShapeDtypeStruct((batch, gp), F32),
        ],
        scratch_shapes=[pltpu.VMEM((n_slabs, nb * seq, LANES), F32), pltpu.VMEM((n_slabs, nb * seq, LANES), F32)],
        compiler_params=_params("arbitrary"),
        name="s5_sample",
    )(proj, x0_re, x0_im, *ws)


def _mix_gates(o_ref, gate_ref, y_ref, ng_ref, gluw_ref, glub_ref, a_ref, rows):
    w_a = o_ref.shape[1]
    o = o_ref[rows, :]
    var = jnp.mean(o * o, axis=-1, keepdims=True)
    gate = gate_ref[rows, :]
    a_ref[rows, :w_a] = (o * lax.rsqrt(var + EPS) * ng_ref[...] * (gate * jax.nn.sigmoid(gate))).astype(BF16)
    y = jax.nn.gelu(y_ref[rows, :])
    z = _dot(y.astype(BF16), gluw_ref[...]) + glub_ref[...]
    a_ref[rows, w_a:] = (y * jax.nn.sigmoid(z)).astype(BF16)


def _mix_kernel(o_ref, gate_ref, y_ref, x_ref, ng_ref, gluw_ref, glub_ref, wout_ref, out_ref, *rest,
                n_tiles, n_cols):
    *copies, a0_ref, a1_ref = rest
    glu16_ref, wout16_ref = copies or (None, None)
    i = pl.program_id(0)
    j = pl.program_id(1)
    rows_per_step = a0_ref.shape[0] // n_cols

    if glu16_ref is not None:
        @pl.when(jnp.logical_and(i == 0, j == 0))
        def _():
            _bf16_weights(gluw_ref, glu16_ref)
        gluw_ref = glu16_ref

    def step(a_gate, a_proj):
        pieces = out_ref.shape[1] // MXU_DIM
        sub = rows_per_step // pieces
        for p in range(pieces):
            cols = (slice(None), slice(p * MXU_DIM, (p + 1) * MXU_DIM))
            if a_gate is not None:
                sub_rows = pl.ds(pl.multiple_of(j * rows_per_step + p * sub, sub), sub)
                _mix_gates(o_ref, gate_ref, y_ref, ng_ref, gluw_ref, glub_ref, a_gate, sub_rows)
            w = _bf16_weights(wout_ref, wout16_ref, cols)
            if a_proj is not None:
                out_ref[cols] = x_ref[cols] + _dot(a_proj[...], w)

    even = i % 2 == 0
    inner = jnp.logical_and(i > 0, i < n_tiles)
    pl.when(i == 0)(lambda: step(a0_ref, None))
    pl.when(jnp.logical_and(inner, even))(lambda: step(a0_ref, a1_ref))
    pl.when(jnp.logical_and(inner, jnp.logical_not(even)))(lambda: step(a1_ref, a0_ref))
    pl.when(jnp.logical_and(i == n_tiles, even))(lambda: step(None, a1_ref))
    pl.when(jnp.logical_and(i == n_tiles, jnp.logical_not(even)))(lambda: step(None, a0_ref))


def _mix(o, proj, y, x, norm_g, glu_w, glu_b, w_out, *, gate_col, name, tn=512):
    m, w_a = o.shape
    w_b = y.shape[1]
    d = w_out.shape[1]
    tm = _row_tile(m)
    nj = d // tn
    assert m % tm == 0 and d % tn == 0 and tm % (nj * SUBLANES) == 0
    n = m // tm
    nxt = lambda i: jnp.minimum(i, n - 1)
    cur = lambda i: jnp.maximum(i - 1, 0)
    const = lambda i, j: (0, 0)
    cur_tile = lambda i, j: (cur(i), jnp.where(i > 0, j, 0))
    glu_spec = ((w_b, w_b), const)
    wout_spec = ((w_a + w_b, tn), lambda i, j: (0, j))
    assert glu_w.dtype == w_out.dtype
    copy_specs, copy_shapes = (a + b for a, b in zip(_weight_copy(glu_w, *glu_spec, (0, 0)),
                                                     _weight_copy(w_out, *wout_spec, (0, nj - 1))))
    return pl.pallas_call(
        functools.partial(_mix_kernel, n_tiles=n, n_cols=nj),
        grid=(n + 1, nj),
        in_specs=[
            pl.BlockSpec((tm, w_a), lambda i, j: (nxt(i), 0)),
            pl.BlockSpec((tm, w_a), lambda i, j: (nxt(i), gate_col)),
            pl.BlockSpec((tm, w_b), lambda i, j: (nxt(i), 0)),
            pl.BlockSpec((tm, tn), cur_tile),
            pl.BlockSpec((1, w_a), const),
            pl.BlockSpec(*glu_spec),
            pl.BlockSpec((1, w_b), const),
            pl.BlockSpec(*wout_spec),
        ],
        out_specs=[pl.BlockSpec((tm, tn), cur_tile)] + copy_specs,
        out_shape=[jax.ShapeDtypeStruct((m, d), F32)] + copy_shapes,
        scratch_shapes=[pltpu.VMEM((tm, w_a + w_b), BF16)] * 2,
        compiler_params=_params("arbitrary", "arbitrary"),
        name=name,
    )(o, proj, y, x, norm_g.reshape(1, w_a), glu_w, glu_b.reshape(1, w_b), w_out)


def _mlp_kernel(x_ref, g_ref, up_ref, down_ref, gf_ref, o_ref, *rest, rows, final_norm):
    *copies, h_ref = rest
    up16_ref, down16_ref = copies or (None, None)
    f = pl.program_id(1)

    @pl.when(f == 0)
    def _():
        _rmsnorm_rows(x_ref, g_ref[...], h_ref, rows)
        o_ref[...] = x_ref[...]

    a = jnp.square(jnp.maximum(_dot(h_ref[...], _bf16_weights(up_ref, up16_ref)), 0.0)).astype(BF16)
    o_ref[...] += _dot(a, _bf16_weights(down_ref, down16_ref))

    if final_norm:
        @pl.when(f == pl.num_programs(1) - 1)
        def _():
            _rmsnorm_rows(o_ref, gf_ref[...], o_ref, rows)


def _mlp(x, g, up, down, gf, *, final_norm, name, tf=512):
    m, d = x.shape
    dff = up.shape[1]
    tm = _row_tile(m)
    assert m % tm == 0 and dff % tf == 0 and up.dtype == down.dtype
    up_spec = ((d, tf), lambda i, f: (0, f))
    down_spec = ((tf, d), lambda i, f: (f, 0))
    nf = dff // tf
    copy_specs, copy_shapes = (a + b for a, b in zip(_weight_copy(up, *up_spec, (0, nf - 1)),
                                                     _weight_copy(down, *down_spec, (nf - 1, 0))))
    return pl.pallas_call(
        functools.partial(_mlp_kernel, rows=tm, final_norm=final_norm),
        grid=(m // tm, dff // tf),
        in_specs=[
            pl.BlockSpec((tm, d), lambda i, f: (i, 0)),
            pl.BlockSpec((1, d), lambda i, f: (0, 0)),
            pl.BlockSpec(*up_spec),
            pl.BlockSpec(*down_spec),
            pl.BlockSpec((1, d), lambda i, f: (0, 0)),
        ],
        out_specs=[pl.BlockSpec((tm, d), lambda i, f: (i, 0))] + copy_specs,
        out_shape=[jax.ShapeDtypeStruct((m, d), F32)] + copy_shapes,
        scratch_shapes=[pltpu.VMEM((tm, d), BF16)],
        compiler_params=_params("arbitrary", "arbitrary"),
        name=name,
    )(x, g.reshape(1, d), up, down, gf.reshape(1, d))


def kernel(x_prompt, x_sample, state_hgrn, state_s5_re, state_s5_im, w_in, w_out, norm1_g, norm2_g, hgrn_lb_logits, hgrn_norm_g, s5_a_re, s5_a_im, s5_b_re, s5_b_im, s5_c_re, s5_c_im, s5_d, s5_log_step, glu_w, glu_b, mlp_up, mlp_down, final_norm_g):
    bp, seq, d = x_prompt.shape
    bs, dseq, _ = x_sample.shape
    depth = w_in.shape[0]
    w_a = hgrn_norm_g.shape[1]
    w_b = s5_d.shape[1]
    g_b, p_state = s5_a_re.shape[1], s5_a_re.shape[2]
    mp, ms = bp * seq, bs * dseq
    assert w_in.shape[2] == 4 * w_a + w_b and w_a == w_b and g_b * S5_GROUP == w_b
    gate_col, ucol = 3, (4 * w_a) // w_b

    xp = x_prompt.reshape(mp, d)
    xs = x_sample.reshape(ms, d)
    lb_logits = hgrn_lb_logits.astype(F32)
    new_states = []
    for l in range(depth):
        proj_s, w_in16 = _inproj(xs, norm1_g[l], w_in[l].astype(F32), name="inproj_sample")
        proj_p, = _inproj(xp, norm1_g[l], w_in16, name="inproj_prompt")

        o_p, sh_p = _hgrn_prompt(proj_p, lb_logits, layer=l, batch=bp, seq=seq, w_a=w_a)
        o_s, sh_s = _hgrn_sample(proj_s, lb_logits, state_hgrn, layer=l, row0=0, batch=bs, seq=dseq, w_a=w_a)

        ws = (*_s5_weights(*(w[l].astype(F32) for w in (s5_a_re, s5_a_im, s5_log_step, s5_b_re, s5_b_im,
                                                        s5_c_re, s5_c_im))),
              s5_d[l].astype(F32).reshape(1, w_b))
        y_p, sr_p, si_p = _s5_prompt(proj_p, ws, nb=bp, seq=seq, ucol=ucol, w_b=w_b)
        y_s, sr_s, si_s = _s5_sample(
            proj_s, state_s5_re[l].reshape(bs, g_b * p_state).astype(F32), state_s5_im[l].reshape(bs, g_b * p_state).astype(F32),
            ws, row0=0, ucol=ucol, seq=dseq, w_b=w_b)

        last = l == depth - 1
        x1_s, glu16, w_out16 = _mix(o_s, proj_s, y_s, xs, hgrn_norm_g[l], glu_w[l].astype(F32), glu_b[l],
                                    w_out[l].astype(F32), gate_col=gate_col, name="mix_sample")
        xs, up16, down16 = _mlp(x1_s, norm2_g[l], mlp_up[l].astype(F32), mlp_down[l].astype(F32), final_norm_g,
                                final_norm=last, name="mlp_sample")
        x1_p, = _mix(o_p, proj_p, y_p, xp, hgrn_norm_g[l], glu16, glu_b[l], w_out16, gate_col=gate_col,
                     name="mix_prompt")
        xp, = _mlp(x1_p, norm2_g[l], up16, down16, final_norm_g, final_norm=last, name="mlp_prompt")
        new_states.append((sh_p, sr_p.reshape(bp, g_b, p_state), si_p.reshape(bp, g_b, p_state),
                           sh_s[0], sr_s.reshape(bs, g_b, p_state), si_s.reshape(bs, g_b, p_state)))

    y_prompt = xp.reshape(bp, seq, d).astype(x_prompt.dtype)
    y_sample = xs.reshape(bs, dseq, d).astype(x_sample.dtype)
    stacked = [jnp.stack([st[i] for st in new_states]) for i in range(6)]
    return (y_prompt, y_sample, *stacked)
```

```python
import functools

import jax
import jax.numpy as jnp
import numpy as np
from jax import lax
from jax.experimental import pallas as pl
from jax.experimental.pallas import tpu as pltpu

F32 = jnp.float32
BF16 = jnp.bfloat16
EPS = 1e-6

LANES = 128
SUBLANES = 8
MXU_DIM = 256
VMEM_LIMIT = 56 * 1024 * 1024

HEAD_DIM = 128
S5_GROUP = 16
HGRN_CHUNK = 64
ROW_TILE = 1024
NEG_BIG = -1e30


def _dot(a, b):
    return jnp.dot(a, b, preferred_element_type=F32)


def _dot_nt(a, b):
    return lax.dot_general(a, b, (((1,), (1,)), ((), ())), preferred_element_type=F32)


def _dot_tn(a, b):
    return lax.dot_general(a, b, (((0,), (0,)), ((), ())), preferred_element_type=F32)


def _split3(x):
    hi = x.astype(BF16)
    r1 = x - hi.astype(F32)
    mid = r1.astype(BF16)
    lo = (r1 - mid.astype(F32)).astype(BF16)
    return hi, mid, lo


def _params(*sem):
    return pltpu.CompilerParams(dimension_semantics=sem, vmem_limit_bytes=VMEM_LIMIT)


def _rmsnorm_rows(src_ref, g, dst_ref, rows, chunk=64):
    def body(c, carry):
        r0 = pl.multiple_of(c * chunk, chunk)
        x = src_ref[pl.ds(r0, chunk), :]
        var = jnp.mean(x * x, axis=-1, keepdims=True)
        dst_ref[pl.ds(r0, chunk), :] = (x * lax.rsqrt(var + EPS) * g).astype(dst_ref.dtype)
        return carry

    lax.fori_loop(0, rows // chunk, body, 0)


def _bf16_weights(w_ref, copy_ref, at=(Ellipsis,), first_row=0):
    w = w_ref[at]
    if copy_ref is None:
        return w
    w = w.astype(BF16)

    @pl.when(pl.program_id(0) == first_row)
    def _():
        copy_ref[at] = w

    return w


def _weight_copy(w, block, index_map, last_index, first_row=0):
    if w.dtype == BF16:
        return [], []

    def once(i, j):
        during = index_map(jnp.full_like(i, first_row), j)
        before = index_map(jnp.full_like(i, first_row), jnp.zeros_like(j))
        pick = lambda b, d, a: jnp.where(i < first_row, b, jnp.where(i == first_row, d, a))
        return jax.tree.map(pick, before, during, last_index)

    return [pl.BlockSpec(block, once)], [jax.ShapeDtypeStruct(w.shape, BF16)]


def _lead_in_maps(n):
    nxt = lambda i: jnp.minimum(i, n - 1)
    cur = lambda i: jnp.maximum(i - 1, 0)
    col = lambda i, j: jnp.where(i > 0, j, 0)
    return nxt, cur, col


def _by_parity(i, n_tiles, step, buf0, buf1):
    even = i % 2 == 0
    inner = jnp.logical_and(i > 0, i < n_tiles)
    pl.when(i == 0)(lambda: step(buf0, None))
    pl.when(jnp.logical_and(inner, even))(lambda: step(buf0, buf1))
    pl.when(jnp.logical_and(inner, jnp.logical_not(even)))(lambda: step(buf1, buf0))
    pl.when(jnp.logical_and(i == n_tiles, even))(lambda: step(None, buf1))
    pl.when(jnp.logical_and(i == n_tiles, jnp.logical_not(even)))(lambda: step(None, buf0))


def _inproj_kernel(x_ref, g_ref, w_ref, o_ref, *rest, n_tiles, n_cols):
    *w16_ref, h0_ref, h1_ref = rest
    j = pl.program_id(1)
    rows_per_step = h0_ref.shape[0] // n_cols
    rows = pl.ds(pl.multiple_of(j * rows_per_step, rows_per_step), rows_per_step)

    def step(h_norm, h_proj):
        if h_norm is not None:
            x = x_ref[rows, :]
            var = jnp.mean(x * x, axis=-1, keepdims=True)
            h_norm[rows, :] = (x * lax.rsqrt(var + EPS) * g_ref[...]).astype(BF16)
        if h_proj is not None:
            o_ref[...] = _dot(h_proj[...], _bf16_weights(w_ref, *(w16_ref or [None]), first_row=1))

    _by_parity(pl.program_id(0), n_tiles, step, h0_ref, h1_ref)


def _row_tile(m):
    return ROW_TILE if m % ROW_TILE == 0 else ROW_TILE // 2


def _inproj(x, g, w, *, name, tn=1280):
    m, d = x.shape
    n_out = w.shape[1]
    tm = _row_tile(m)
    if w.dtype != BF16:
        tn //= 2
    nj = n_out // tn
    assert m % tm == 0 and n_out % tn == 0 and tm % (nj * SUBLANES) == 0
    n = m // tm
    nxt, cur, col = _lead_in_maps(n)
    w_spec = ((d, tn), lambda i, j: (0, col(i, j)))
    copy_specs, copy_shapes = _weight_copy(w, *w_spec, (0, nj - 1), first_row=1)
    return pl.pallas_call(
        functools.partial(_inproj_kernel, n_tiles=n, n_cols=nj),
        grid=(n + 1, nj),
        in_specs=[
            pl.BlockSpec((tm, d), lambda i, j: (nxt(i), 0)),
            pl.BlockSpec((1, d), lambda i, j: (0, 0)),
            pl.BlockSpec(*w_spec),
        ],
        out_specs=[pl.BlockSpec((tm, tn), lambda i, j: (cur(i), col(i, j)))] + copy_specs,
        out_shape=[jax.ShapeDtypeStruct((m, n_out), F32)] + copy_shapes,
        scratch_shapes=[pltpu.VMEM((tm, d), BF16)] * 2,
        compiler_params=_params("arbitrary", "arbitrary"),
        name=name,
    )(x, g.reshape(1, d), w)


def _lower_bound(logits, layer):
    m = jnp.max(logits, axis=0, keepdims=True)
    e = jnp.exp(logits - m)
    return jnp.sum(e[: layer + 1], axis=0, keepdims=True) / jnp.sum(e, axis=0, keepdims=True)


def _row_to_col(e_row):
    n = e_row.shape[1]
    hi, mid, lo = (p.astype(F32) for p in _split3(e_row))
    r = lax.broadcasted_iota(jnp.int32, (2 * SUBLANES, n), 0)
    pieces = jnp.where(r == 0, hi, jnp.where(r == 1, mid, jnp.where(r == 2, lo, 0.0)))
    return _dot_tn(pieces.astype(BF16), jnp.ones((2 * SUBLANES, LANES), BF16))


def _hgrn_gates(qp, fp, lb):
    q = qp * jax.nn.sigmoid(qp)
    f = lb + (1.0 - lb) * jax.nn.sigmoid(fp)
    return q, f, 1.0 - f, jnp.log(f)


def _hgrn_tables(c):
    t = np.arange(c)[:, None]
    j = np.arange(c)[None, :]
    sums, pairs = [j <= t], []
    bs = c
    while bs >= 2:
        hs = bs // 2
        ref = t - t % bs + hs - 1
        upper = t % bs >= hs
        sums.append(np.where(upper, (j > ref) & (j <= t), (j > t) & (j <= ref)))
        pairs.append((t // bs == j // bs) & upper & (j % bs < hs))
        bs = hs
    sums.append(j > t)
    pairs.append(t == j)
    sums = np.concatenate(sums, axis=0).astype(np.float32)
    return jnp.asarray(np.concatenate([sums] * 3, axis=1), BF16), jnp.asarray(np.stack(pairs), F32)


def _upper_q_lower_k(q, k, bs):
    c = q.shape[0]
    hs = bs // 2
    if hs % SUBLANES == 0:
        parts = []
        for r0 in range(0, c, bs):
            parts += [k[r0 : r0 + hs], q[r0 + hs : r0 + bs]]
        return jnp.concatenate(parts, axis=0)
    row = lax.broadcasted_iota(jnp.int32, (c, 1), 0)
    return jnp.where(row % bs >= hs, q, k)


def _hgrn_prep(qp, fp, v, lb, sums, levels):
    c = qp.shape[0]
    q = qp * jax.nn.sigmoid(qp)
    f = lb + (1.0 - lb) * jax.nn.sigmoid(fp)
    k = 1.0 - f
    e = jnp.exp2(_dot(sums, jnp.concatenate(_split3(jnp.log2(f)), axis=0)))
    e_b, e_rest = e[:c], e[(levels + 1) * c :]
    ys = [(_upper_q_lower_k(q, k, c >> l) * e[(l + 1) * c : (l + 2) * c]).astype(BF16) for l in range(levels)]
    return dict(ys=ys, q16=q.astype(BF16), k16=k.astype(BF16), v16=v.astype(BF16),
                qe=(q * e_b).astype(BF16), kd=(k * e_rest).astype(BF16), e_last=e_b[c - 1 : c])


def _hgrn_apply(prep, st_ref, pairs_ref, heads):
    levels = pairs_ref.shape[0] - 1
    pairs = [(n, h) for n in range(len(prep)) for h in range(heads)]
    head = lambda x, h: x[:, h * HEAD_DIM : (h + 1) * HEAD_DIM]
    scores = {}
    for n, h in pairs:
        p = prep[n]
        a = _dot_nt(head(p["q16"], h), head(p["k16"], h)) * pairs_ref[levels]
        for l in range(levels):
            y = head(p["ys"][l], h)
            a = a + _dot_nt(y, y) * pairs_ref[l]
        scores[n, h] = a.astype(BF16)
    outs = {}
    for n, h in pairs:
        p = prep[n]
        st = st_ref[n, h]
        outs[n, h] = _dot_nt(head(p["qe"], h), st.astype(BF16)) + _dot(scores[n, h], head(p["v16"], h))
        st_ref[n, h] = head(p["e_last"], h) * st + _dot_tn(head(p["v16"], h), head(p["kd"], h))
    return [jnp.concatenate([outs[n, h] for h in range(heads)], axis=1) for n in range(len(prep))]


def _hgrn_prompt_kernel(q_ref, f_ref, v_ref, lbl_ref, sums_ref, pairs_ref, o_ref, s_out_ref, st_ref, *,
                        layer, heads, n_chunks):
    t = pl.program_id(1)

    @pl.when(t == 0)
    def _():
        st_ref[...] = jnp.zeros_like(st_ref)

    lb = _lower_bound(lbl_ref[...], layer)
    n_seq = q_ref.shape[0]
    levels = pairs_ref.shape[0] - 1

    def prep(c):
        rows = slice(c * HGRN_CHUNK, (c + 1) * HGRN_CHUNK)
        return [_hgrn_prep(q_ref[n, rows, :], f_ref[n, rows, :], v_ref[n, rows, :], lb, sums_ref[...], levels)
                for n in range(n_seq)]

    nxt = prep(0)
    for c in range(n_chunks):
        cur, nxt = nxt, (prep(c + 1) if c + 1 < n_chunks else None)
        for n, o in enumerate(_hgrn_apply(cur, st_ref, pairs_ref, heads)):
            o_ref[n, c * HGRN_CHUNK : (c + 1) * HGRN_CHUNK, :] = o

    @pl.when(t == pl.num_programs(1) - 1)
    def _():
        for n in range(n_seq):
            for h in range(heads):
                s_out_ref[n, h] = st_ref[n, h].T


def _hgrn_prompt(proj, lb_logits, *, layer, batch, seq, w_a, tt=256, n_seq=2):
    heads = w_a // HEAD_DIM
    assert seq % tt == 0 and tt % HGRN_CHUNK == 0 and batch % n_seq == 0
    sums, pairs = _hgrn_tables(HGRN_CHUNK)
    proj3 = proj.reshape(batch, seq, proj.shape[1])
    col = lambda c: pl.BlockSpec((n_seq, tt, w_a), lambda b, t: (b, t, c))
    const = lambda x: pl.BlockSpec(x.shape, lambda b, t: (0,) * x.ndim)
    o, s = pl.pallas_call(
        functools.partial(_hgrn_prompt_kernel, layer=layer, heads=heads, n_chunks=tt // HGRN_CHUNK),
        grid=(batch // n_seq, seq // tt),
        in_specs=[col(0), col(1), col(2), const(lb_logits), const(sums), const(pairs)],
        out_specs=[
            pl.BlockSpec((n_seq, tt, w_a), lambda b, t: (b, t, 0)),
            pl.BlockSpec((n_seq, heads, HEAD_DIM, HEAD_DIM), lambda b, t: (b, 0, 0, 0)),
        ],
        out_shape=[
            jax.ShapeDtypeStruct((batch, seq, w_a), F32),
            jax.ShapeDtypeStruct((batch, heads, HEAD_DIM, HEAD_DIM), F32),
        ],
        scratch_shapes=[pltpu.VMEM((n_seq, heads, HEAD_DIM, HEAD_DIM), F32)],
        compiler_params=_params("arbitrary", "arbitrary"),
        name="hgrn_prompt",
    )(proj3, proj3, proj3, lb_logits, sums, pairs)
    return o.reshape(batch * seq, w_a), s


def _hgrn_sample_kernel(q_ref, f_ref, v_ref, lbl_ref, s_in_ref, o_ref, s_out_ref, *, layer, heads, seq):
    n_seq = SUBLANES // seq
    lb = _lower_bound(lbl_ref[...], layer)
    row = lax.broadcasted_iota(jnp.int32, (SUBLANES, 1), 0)
    pos = row % seq
    work = []
    for tile, h in ((tile, h) for tile in range(q_ref.shape[0] // SUBLANES) for h in range(heads)):
        cols = slice(h * HEAD_DIM, (h + 1) * HEAD_DIM)
        rows = slice(tile * SUBLANES, (tile + 1) * SUBLANES)
        q, _, k, logf = _hgrn_gates(q_ref[rows, cols], f_ref[rows, cols], lb[:, cols])
        v = v_ref[rows, cols]
        b = logf
        shift = 1
        while shift < seq:
            b = b + jnp.where(pos >= shift, pltpu.roll(b, shift, 0), 0.0)
            shift *= 2
        o = jnp.zeros((SUBLANES, HEAD_DIM), F32)
        for sl in range(SUBLANES):
            lo_t, hi_t = sl, (sl // seq + 1) * seq
            mask = jnp.logical_and(row >= lo_t, row < hi_t)
            z = q * jnp.exp(jnp.where(mask, b - b[sl : sl + 1], NEG_BIG)) * k[sl : sl + 1]
            o = o + jnp.sum(z, axis=-1, keepdims=True) * v[sl : sl + 1]
        per_seq = []
        for n in range(n_seq):
            own = jnp.logical_and(row >= n * seq, row < (n + 1) * seq)
            b_last = b[(n + 1) * seq - 1 : (n + 1) * seq, :]
            qe = jnp.where(own, q * jnp.exp(b), 0.0).astype(BF16)
            kd = jnp.where(own, k * jnp.exp(jnp.where(own, b_last - b, 0.0)), 0.0).astype(BF16)
            per_seq.append((tile * n_seq + n, qe, kd, jnp.exp(b_last)))
        work.append((rows, cols, h, o, v.astype(BF16), per_seq))
    for rows, cols, h, o, _, per_seq in work:
        for i, qe, _, _ in per_seq:
            o = o + _dot(qe, s_in_ref[0, i, h].astype(BF16))
        o_ref[rows, cols] = o
    for _, _, h, _, v16, per_seq in work:
        for i, _, kd, e_last in per_seq:
            s_out_ref[0, i, h] = _row_to_col(e_last) * s_in_ref[0, i, h] + _dot_tn(kd, v16)


def _hgrn_sample(proj, lb_logits, state, *, layer, row0, batch, seq, w_a, tiles=4):
    heads = w_a // HEAD_DIM
    rows = tiles * SUBLANES
    assert SUBLANES % seq == 0 and row0 % rows == 0 and (batch * seq) % rows == 0
    n_seq = rows // seq
    t0 = row0 // rows
    col = lambda c: pl.BlockSpec((rows, w_a), lambda p: (t0 + p, c))
    st = pl.BlockSpec((1, n_seq, heads, HEAD_DIM, HEAD_DIM), lambda p: (layer, p, 0, 0, 0))
    st_out = pl.BlockSpec((1, n_seq, heads, HEAD_DIM, HEAD_DIM), lambda p: (0, p, 0, 0, 0))
    return pl.pallas_call(
        functools.partial(_hgrn_sample_kernel, layer=layer, heads=heads, seq=seq),
        grid=(batch // n_seq,),
        in_specs=[col(0), col(1), col(2), pl.BlockSpec(lb_logits.shape, lambda p: (0, 0)), st],
        out_specs=[pl.BlockSpec((rows, w_a), lambda p: (p, 0)), st_out],
        out_shape=[
            jax.ShapeDtypeStruct((batch * seq, w_a), F32),
            jax.ShapeDtypeStruct((1, batch, heads, HEAD_DIM, HEAD_DIM), F32),
        ],
        compiler_params=_params("arbitrary"),
        name="hgrn_sample",
    )(proj, proj, proj, lb_logits, state)


def _s5_weights_kernel(are_ref, aim_ref, ls_ref, bre_ref, bim_ref, cre_ref, cim_ref, rep_ref, tile_ref, same_ref,
                       lre_ref, lim_ref, bbre_ref, bbim_ref, ccre_ref, ccim_ref):
    a_re = are_ref[...]
    a_im = aim_ref[...]
    dt = jnp.exp(ls_ref[...])
    mag = jnp.exp(a_re * dt)
    lam_re = mag * jnp.cos(a_im * dt)
    lam_im = mag * jnp.sin(a_im * dt)
    den = a_re * a_re + a_im * a_im
    nr, ni = lam_re - 1.0, lam_im
    r_re = (nr * a_re + ni * a_im) / den
    r_im = (ni * a_re - nr * a_im) / den
    lre_ref[...] = lam_re
    lim_ref[...] = lam_im
    b_re = bre_ref[...]
    b_im = bim_ref[...]
    same = same_ref[...]
    bbre_ref[0] = (_dot(rep_ref[...], (r_re * b_re - r_im * b_im).astype(BF16)) * same).astype(BF16)
    bbim_ref[0] = (_dot(rep_ref[...], (r_re * b_im + r_im * b_re).astype(BF16)) * same).astype(BF16)
    ccre_ref[0] = (_dot(cre_ref[...].astype(BF16), tile_ref[...]) * same).astype(BF16)
    ccim_ref[0] = (_dot(cim_ref[...].astype(BF16), tile_ref[...]) * same).astype(BF16)


def _s5_weights(a_re, a_im, log_step, b_re, b_im, c_re, c_im):
    g, p = a_re.shape
    c = b_re.shape[-1]
    gt = MXU_DIM // c
    nj, kc, kp = g // gt, gt * c, gt * p
    flat = lambda x: x.reshape(1, g * p)
    chan = lambda x: jnp.transpose(x, (2, 0, 1)).reshape(c, g * p)
    row_g, lane_g = np.arange(kc)[:, None] // c, np.arange(kp)[None, :] // p
    rep = jnp.asarray(np.arange(kc)[:, None] % c == np.arange(c)[None, :], BF16)
    tile = jnp.asarray(np.arange(p)[:, None] == np.arange(kp)[None, :] % p, BF16)
    same = jnp.asarray(row_g == lane_g, F32)
    lanes = lambda rows: pl.BlockSpec((rows, kp), lambda j: (0, j))
    const = lambda x: pl.BlockSpec(x.shape, lambda j: (0, 0))
    tiles = pl.BlockSpec((1, kc, kp), lambda j: (j, 0, 0))
    tiles_shape = jax.ShapeDtypeStruct((nj, kc, kp), BF16)
    lam_shape = jax.ShapeDtypeStruct((1, g * p), F32)
    lam_re, lam_im, *mats = pl.pallas_call(
        _s5_weights_kernel,
        grid=(nj,),
        in_specs=[lanes(1), lanes(1), lanes(1), lanes(c), lanes(c),
                  pl.BlockSpec((kc, p), lambda j: (j, 0)), pl.BlockSpec((kc, p), lambda j: (j, 0)),
                  const(rep), const(tile), const(same)],
        out_specs=[lanes(1), lanes(1), tiles, tiles, tiles, tiles],
        out_shape=[lam_shape, lam_shape] + [tiles_shape] * 4,
        compiler_params=_params("arbitrary"),
        name="s5_weights",
    )(flat(a_re), flat(a_im), flat(jnp.repeat(log_step, p)), chan(b_re), chan(b_im),
      c_re.reshape(g * c, p), c_im.reshape(g * c, p), rep, tile, same)
    return (lam_re.reshape(-1, 1, LANES), lam_im.reshape(-1, 1, LANES), *mats)


def _s5_lanes(x_ref, lo, hi):
    if len(x_ref.shape) == 2:
        return x_ref[:, lo:hi]
    return jnp.concatenate([x_ref[m] for m in range(lo // LANES, hi // LANES)], axis=1)


def _s5_in(ub, bre_ref, bim_ref, xr_ref, xi_ref):
    nj, kc, kp = bre_ref.shape
    for j in range(nj):
        uj = ub[:, j * kc : (j + 1) * kc]
        for x_ref, w_ref in ((xr_ref, bre_ref), (xi_ref, bim_ref)):
            bu = _dot(uj, w_ref[j])
            if len(x_ref.shape) == 2:
                x_ref[:, j * kp : (j + 1) * kp] = bu
            else:
                for m in range(kp // LANES):
                    x_ref[j * kp // LANES + m] = bu[:, m * LANES : (m + 1) * LANES]


def _s5_out(xr_ref, xi_ref, cre_ref, cim_ref):
    nj, kc, kp = cre_ref.shape
    tiles = []
    for j in range(nj):
        xr = _s5_lanes(xr_ref, j * kp, (j + 1) * kp).astype(BF16)
        xi = _s5_lanes(xi_ref, j * kp, (j + 1) * kp).astype(BF16)
        tiles.append(_dot_nt(xr, cre_ref[j]) - _dot_nt(xi, cim_ref[j]))
    return tiles


def _s5_step(lr, li, xr, xi, bur, bui):
    return lr * xr - li * xi + bur, lr * xi + li * xr + bui


def _s5_lambda(lre_ref, lim_ref, lo, hi, rows):
    cat = lambda ref: jnp.concatenate([ref[m] for m in range(lo // LANES, hi // LANES)], axis=1)
    return jnp.broadcast_to(cat(lre_ref), (rows, hi - lo)), jnp.broadcast_to(cat(lim_ref), (rows, hi - lo))


def _to_time_major(x):
    n, ts, w = x.shape
    return jnp.swapaxes(x, 0, 1).reshape(ts * n, w)


def _from_time_major(x, n):
    rows, w = x.shape
    return jnp.swapaxes(x.reshape(rows // n, n, w), 0, 1)


def _s5_prompt_kernel(*refs, nv, lane_block):
    u_refs = refs[:nv]
    (lre_ref, lim_ref, bre_ref, bim_ref, cre_ref, cim_ref, d_ref,
     y_ref, xr_out_ref, xi_out_ref, xr_ref, xi_ref, sr_ref, si_ref) = refs[nv:]
    ts = u_refs[0].shape[0]
    gp = xr_ref.shape[1]
    step = pl.program_id(0)

    @pl.when(step == 0)
    def _():
        sr_ref[...] = jnp.zeros_like(sr_ref)
        si_ref[...] = jnp.zeros_like(si_ref)

    u = _to_time_major(jnp.stack([r[...] for r in u_refs], axis=0))
    _s5_in(u.astype(BF16), bre_ref, bim_ref, xr_ref, xi_ref)

    for lo in range(0, gp, lane_block):
        ls = slice(lo, lo + lane_block)
        lr, li = _s5_lambda(lre_ref, lim_ref, lo, lo + lane_block, nv)

        xr, xi = sr_ref[:, ls], si_ref[:, ls]
        for r in range(ts):
            rows = slice(r * nv, (r + 1) * nv)
            xr, xi = _s5_step(lr, li, xr, xi, xr_ref[rows, ls], xi_ref[rows, ls])
            xr_ref[rows, ls] = xr
            xi_ref[rows, ls] = xi
        sr_ref[:, ls] = xr
        si_ref[:, ls] = xi

    kc = cre_ref.shape[1]
    d = d_ref[...]
    for j, y in enumerate(_s5_out(xr_ref, xi_ref, cre_ref, cim_ref)):
        cols = slice(j * kc, (j + 1) * kc)
        y_ref[:, :, cols] = _from_time_major(y + d[:, cols] * u[:, cols], nv)

    @pl.when(step == pl.num_programs(0) - 1)
    def _():
        xr_out_ref[...] = sr_ref[...]
        xi_out_ref[...] = si_ref[...]


def _s5_carry_kernel(y_ref, fr_ref, fi_ref, lre_ref, lim_ref, cre_ref, cim_ref,
                     o_ref, xr_out_ref, xi_out_ref, zr_ref, zi_ref, sr_ref, si_ref, *, half_len, lane_block):
    nrow, gp = fr_ref.shape
    tf = y_ref.shape[2]
    step = pl.program_id(0)
    odd = lax.broadcasted_iota(jnp.int32, (nrow, 1), 0) % 2 == 1

    @pl.when(step == 0)
    def _():
        for lo in range(0, gp, lane_block):
            ls = slice(lo, lo + lane_block)
            pr, pi = _s5_lambda(lre_ref, lim_ref, lo, lo + lane_block, nrow)
            n = 1
            while n < half_len:
                pr, pi = pr * pr - pi * pi, 2.0 * pr * pi
                n *= 2
            fr, fi = fr_ref[:, ls], fi_ref[:, ls]
            fr = jnp.where(odd, pltpu.roll(fr, 1, 0), fr)
            fi = jnp.where(odd, pltpu.roll(fi, 1, 0), fi)
            sr_ref[:, ls] = jnp.where(odd, pr * fr - pi * fi, fr)
            si_ref[:, ls] = jnp.where(odd, pr * fi + pi * fr, fi)

    for lo in range(0, gp, lane_block):
        ls = slice(lo, lo + lane_block)
        lr, li = _s5_lambda(lre_ref, lim_ref, lo, lo + lane_block, nrow)

        zr, zi = sr_ref[:, ls], si_ref[:, ls]
        for r in range(tf):
            rows = slice(r * nrow, (r + 1) * nrow)
            zr, zi = lr * zr - li * zi, lr * zi + li * zr
            zr_ref[rows, ls] = zr
            zi_ref[rows, ls] = zi
        sr_ref[:, ls] = zr
        si_ref[:, ls] = zi

    kc = cre_ref.shape[1]
    nseq = y_ref.shape[0]
    for j, y in enumerate(_s5_out(zr_ref, zi_ref, cre_ref, cim_ref)):
        cols = slice(j * kc, (j + 1) * kc)
        o_ref[:, :, :, cols] = y_ref[:, :, :, cols] + _from_time_major(y, nrow).reshape(nseq, nrow // nseq, tf, kc)

    @pl.when(step == pl.num_programs(0) - 1)
    def _():
        xr_out_ref[...] = fr_ref[...] + sr_ref[...]
        xi_out_ref[...] = fi_ref[...] + si_ref[...]


def _s5_weights_specs(ws):
    zero = lambda n: (lambda *_: (0,) * n)
    return [pl.BlockSpec(w.shape, zero(w.ndim)) for w in ws]


def _s5_prompt(proj, ws, *, nb, seq, ucol, w_b, ts=64, tf=64, lane_block=1024):
    lam_re, lam_im, bre, bim, cre, cim, d = ws
    gp = lam_re.shape[0] * LANES
    nv = SUBLANES
    assert nv == 2 * nb, "two pieces per sequence"
    plen, half_len = seq // 2, seq // 4
    assert plen % ts == 0 and half_len % tf == 0 and gp % lane_block == 0
    assert half_len & (half_len - 1) == 0, "repeated squaring needs a power of two"
    nblk = plen // ts
    u_spec = lambda n: pl.BlockSpec((ts, w_b), lambda s: (n * nblk + s, ucol))
    state = pl.BlockSpec((nv, gp), lambda s: (0, 0))
    state_shape = jax.ShapeDtypeStruct((nv, gp), F32)
    scratch = lambda steps: [pltpu.VMEM((nv * steps, gp), F32)] * 2 + [pltpu.VMEM((nv, gp), F32)] * 2
    y, fr, fi = pl.pallas_call(
        functools.partial(_s5_prompt_kernel, nv=nv, lane_block=lane_block),
        grid=(nblk,),
        in_specs=[u_spec(n) for n in range(nv)] + _s5_weights_specs(ws),
        out_specs=[pl.BlockSpec((nv, ts, w_b), lambda s: (0, s, 0)), state, state],
        out_shape=[jax.ShapeDtypeStruct((nv, plen, w_b), F32), state_shape, state_shape],
        scratch_shapes=scratch(ts),
        compiler_params=_params("arbitrary"),
        name="s5_prompt",
    )(*([proj] * nv), *ws)

    y5 = y.reshape(nb, 2, 2, half_len, w_b)
    y_spec = pl.BlockSpec((nb, None, 2, tf, w_b), lambda s: (0, 1, 0, s, 0))
    y5, xr, xi = pl.pallas_call(
        functools.partial(_s5_carry_kernel, half_len=half_len, lane_block=lane_block),
        grid=(half_len // tf,),
        in_specs=[y_spec, state, state] + _s5_weights_specs((lam_re, lam_im, cre, cim)),
        out_specs=[y_spec, state, state],
        out_shape=[jax.ShapeDtypeStruct(y5.shape, F32), state_shape, state_shape],
        scratch_shapes=scratch(tf),
        input_output_aliases={0: 0},
        compiler_params=_params("arbitrary"),
        name="s5_prompt_carry",
    )(y5, fr, fi, lam_re, lam_im, cre, cim)
    last = lambda x: x.reshape(nb, 2, gp)[:, 1]
    return y5.reshape(nb * seq, w_b), last(xr), last(xi)


def _s5_sample_kernel(u_ref, x0r_ref, x0i_ref, lre_ref, lim_ref, bre_ref, bim_ref, cre_ref, cim_ref, d_ref,
                      y_ref, xr_out_ref, xi_out_ref, xr_ref, xi_ref, *, seq):
    nb = x0r_ref.shape[0]
    u = u_ref[...]
    _s5_in(u.astype(BF16), bre_ref, bim_ref, xr_ref, xi_ref)
    for m in range(xr_ref.shape[0]):
        ls = slice(m * LANES, (m + 1) * LANES)
        lr, li = lre_ref[m], lim_ref[m]
        xr, xi = x0r_ref[:, ls], x0i_ref[:, ls]
        for r in range(seq):
            rows = pl.ds(r, nb, stride=seq)
            xr, xi = _s5_step(lr, li, xr, xi, xr_ref[m, rows, :], xi_ref[m, rows, :])
            xr_ref[m, rows, :] = xr
            xi_ref[m, rows, :] = xi
        xr_out_ref[:, ls] = xr
        xi_out_ref[:, ls] = xi
    kc = cre_ref.shape[1]
    d = d_ref[...]
    for j, y in enumerate(_s5_out(xr_ref, xi_ref, cre_ref, cim_ref)):
        cols = slice(j * kc, (j + 1) * kc)
        y_ref[:, cols] = y + d[:, cols] * u[:, cols]


def _s5_sample(proj, x0_re, x0_im, ws, *, row0, ucol, seq, w_b, nb=64):
    batch, gp = x0_re.shape
    n_slabs = gp // LANES
    assert batch % nb == 0 and row0 % (nb * seq) == 0
    t0 = row0 // (nb * seq)
    return pl.pallas_call(
        functools.partial(_s5_sample_kernel, seq=seq),
        grid=(batch // nb,),
        in_specs=[
            pl.BlockSpec((nb * seq, w_b), lambda p: (t0 + p, ucol)),
            pl.BlockSpec((nb, gp), lambda p: (p, 0)),
            pl.BlockSpec((nb, gp), lambda p: (p, 0)),
        ] + _s5_weights_specs(ws),
        out_specs=[
            pl.BlockSpec((nb * seq, w_b), lambda p: (p, 0)),
            pl.BlockSpec((nb, gp), lambda p: (p, 0)),
            pl.BlockSpec((nb, gp), lambda p: (p, 0)),
        ],
        out_shape=[
            jax.ShapeDtypeStruct((batch * seq, w_b), F32),
            jax.ShapeDtypeStruct((batch, gp), F32),
            jax.ShapeDtypeStruct((batch, gp), F32),
        ],
        scratch_shapes=[pltpu.VMEM((n_slabs, nb * seq, LANES), F32), pltpu.VMEM((n_slabs, nb * seq, LANES), F32)],
        compiler_params=_params("arbitrary"),
        name="s5_sample",
    )(proj, x0_re, x0_im, *ws)


def _mix_gates(o_ref, gate_ref, y_ref, ng_ref, gluw_ref, glub_ref, a_ref, rows):
    w_a = o_ref.shape[1]
    o = o_ref[rows, :]
    var = jnp.mean(o * o, axis=-1, keepdims=True)
    gate = gate_ref[rows, :]
    a_ref[rows, :w_a] = (o * lax.rsqrt(var + EPS) * ng_ref[...] * (gate * jax.nn.sigmoid(gate))).astype(BF16)
    y = jax.nn.gelu(y_ref[rows, :])
    z = _dot(y.astype(BF16), gluw_ref[...]) + glub_ref[...]
    a_ref[rows, w_a:] = (y * jax.nn.sigmoid(z)).astype(BF16)


def _mix_kernel(o_ref, gate_ref, y_ref, x_ref, ng_ref, gluw_ref, glub_ref, wout_ref, out_ref, *rest,
                n_tiles, n_cols):
    *copies, a0_ref, a1_ref = rest
    glu16_ref, wout16_ref = copies or (None, None)
    i = pl.program_id(0)
    j = pl.program_id(1)
    rows_per_step = a0_ref.shape[0] // n_cols

    if glu16_ref is not None:
        @pl.when(jnp.logical_and(i == 0, j == 0))
        def _():
            _bf16_weights(gluw_ref, glu16_ref)
        gluw_ref = glu16_ref

    def step(a_gate, a_proj):
        pieces = out_ref.shape[1] // MXU_DIM
        sub = rows_per_step // pieces
        for p in range(pieces):
            cols = (slice(None), slice(p * MXU_DIM, (p + 1) * MXU_DIM))
            if a_gate is not None:
                sub_rows = pl.ds(pl.multiple_of(j * rows_per_step + p * sub, sub), sub)
                _mix_gates(o_ref, gate_ref, y_ref, ng_ref, gluw_ref, glub_ref, a_gate, sub_rows)
            if a_proj is not None:
                w = _bf16_weights(wout_ref, wout16_ref, cols, first_row=1)
                out_ref[cols] = x_ref[cols] + _dot(a_proj[...], w)

    _by_parity(i, n_tiles, step, a0_ref, a1_ref)


def _mix(o, proj, y, x, norm_g, glu_w, glu_b, w_out, *, gate_col, name, tn=512):
    m, w_a = o.shape
    w_b = y.shape[1]
    d = w_out.shape[1]
    tm = _row_tile(m)
    nj = d // tn
    assert m % tm == 0 and d % tn == 0 and tm % (nj * SUBLANES) == 0
    n = m // tm
    nxt, cur, col = _lead_in_maps(n)
    const = lambda i, j: (0, 0)
    cur_tile = lambda i, j: (cur(i), col(i, j))
    glu_spec = ((w_b, w_b), const)
    wout_spec = ((w_a + w_b, tn), lambda i, j: (0, col(i, j)))
    assert glu_w.dtype == w_out.dtype
    copy_specs, copy_shapes = (a + b for a, b in zip(_weight_copy(glu_w, *glu_spec, (0, 0)),
                                                     _weight_copy(w_out, *wout_spec, (0, nj - 1), first_row=1)))
    return pl.pallas_call(
        functools.partial(_mix_kernel, n_tiles=n, n_cols=nj),
        grid=(n + 1, nj),
        in_specs=[
            pl.BlockSpec((tm, w_a), lambda i, j: (nxt(i), 0)),
            pl.BlockSpec((tm, w_a), lambda i, j: (nxt(i), gate_col)),
            pl.BlockSpec((tm, w_b), lambda i, j: (nxt(i), 0)),
            pl.BlockSpec((tm, tn), cur_tile),
            pl.BlockSpec((1, w_a), const),
            pl.BlockSpec(*glu_spec),
            pl.BlockSpec((1, w_b), const),
            pl.BlockSpec(*wout_spec),
        ],
        out_specs=[pl.BlockSpec((tm, tn), cur_tile)] + copy_specs,
        out_shape=[jax.ShapeDtypeStruct((m, d), F32)] + copy_shapes,
        scratch_shapes=[pltpu.VMEM((tm, w_a + w_b), BF16)] * 2,
        compiler_params=_params("arbitrary", "arbitrary"),
        name=name,
    )(o, proj, y, x, norm_g.reshape(1, w_a), glu_w, glu_b.reshape(1, w_b), w_out)


def _mlp_kernel(x_ref, g_ref, up_ref, down_ref, gf_ref, o_ref, *rest, rows, final_norm):
    *copies, h_ref = rest
    up16_ref, down16_ref = copies or (None, None)
    f = pl.program_id(1)

    @pl.when(f == 0)
    def _():
        _rmsnorm_rows(x_ref, g_ref[...], h_ref, rows)
        o_ref[...] = x_ref[...]

    a = jnp.square(jnp.maximum(_dot(h_ref[...], _bf16_weights(up_ref, up16_ref)), 0.0)).astype(BF16)
    o_ref[...] += _dot(a, _bf16_weights(down_ref, down16_ref))

    if final_norm:
        @pl.when(f == pl.num_programs(1) - 1)
        def _():
            _rmsnorm_rows(o_ref, gf_ref[...], o_ref, rows)


def _mlp(x, g, up, down, gf, *, final_norm, name, tf=512):
    m, d = x.shape
    dff = up.shape[1]
    tm = _row_tile(m)
    assert m % tm == 0 and dff % tf == 0 and up.dtype == down.dtype
    up_spec = ((d, tf), lambda i, f: (0, f))
    down_spec = ((tf, d), lambda i, f: (f, 0))
    nf = dff // tf
    copy_specs, copy_shapes = (a + b for a, b in zip(_weight_copy(up, *up_spec, (0, nf - 1)),
                                                     _weight_copy(down, *down_spec, (nf - 1, 0))))
    return pl.pallas_call(
        functools.partial(_mlp_kernel, rows=tm, final_norm=final_norm),
        grid=(m // tm, dff // tf),
        in_specs=[
            pl.BlockSpec((tm, d), lambda i, f: (i, 0)),
            pl.BlockSpec((1, d), lambda i, f: (0, 0)),
            pl.BlockSpec(*up_spec),
            pl.BlockSpec(*down_spec),
            pl.BlockSpec((1, d), lambda i, f: (0, 0)),
        ],
        out_specs=[pl.BlockSpec((tm, d), lambda i, f: (i, 0))] + copy_specs,
        out_shape=[jax.ShapeDtypeStruct((m, d), F32)] + copy_shapes,
        scratch_shapes=[pltpu.VMEM((tm, d), BF16)],
        compiler_params=_params("arbitrary", "arbitrary"),
        name=name,
    )(x, g.reshape(1, d), up, down, gf.reshape(1, d))


def kernel(x_prompt, x_sample, state_hgrn, state_s5_re, state_s5_im, w_in, w_out, norm1_g, norm2_g, hgrn_lb_logits, hgrn_norm_g, s5_a_re, s5_a_im, s5_b_re, s5_b_im, s5_c_re, s5_c_im, s5_d, s5_log_step, glu_w, glu_b, mlp_up, mlp_down, final_norm_g):
    bp, seq, d = x_prompt.shape
    bs, dseq, _ = x_sample.shape
    depth = w_in.shape[0]
    w_a = hgrn_norm_g.shape[1]
    w_b = s5_d.shape[1]
    g_b, p_state = s5_a_re.shape[1], s5_a_re.shape[2]
    mp, ms = bp * seq, bs * dseq
    assert w_in.shape[2] == 4 * w_a + w_b and w_a == w_b and g_b * S5_GROUP == w_b
    gate_col, ucol = 3, (4 * w_a) // w_b

    xp = x_prompt.reshape(mp, d)
    xs = x_sample.reshape(ms, d)
    lb_logits = hgrn_lb_logits.astype(F32)
    new_states = []
    for l in range(depth):
        proj_s, w_in16 = _inproj(xs, norm1_g[l], w_in[l].astype(F32), name="inproj_sample")
        proj_p, = _inproj(xp, norm1_g[l], w_in16, name="inproj_prompt")

        o_p, sh_p = _hgrn_prompt(proj_p, lb_logits, layer=l, batch=bp, seq=seq, w_a=w_a)
        o_s, sh_s = _hgrn_sample(proj_s, lb_logits, state_hgrn, layer=l, row0=0, batch=bs, seq=dseq, w_a=w_a)

        ws = (*_s5_weights(*(w[l].astype(F32) for w in (s5_a_re, s5_a_im, s5_log_step, s5_b_re, s5_b_im,
                                                        s5_c_re, s5_c_im))),
              s5_d[l].astype(F32).reshape(1, w_b))
        y_p, sr_p, si_p = _s5_prompt(proj_p, ws, nb=bp, seq=seq, ucol=ucol, w_b=w_b)
        y_s, sr_s, si_s = _s5_sample(
            proj_s, state_s5_re[l].reshape(bs, g_b * p_state).astype(F32), state_s5_im[l].reshape(bs, g_b * p_state).astype(F32),
            ws, row0=0, ucol=ucol, seq=dseq, w_b=w_b)

        last = l == depth - 1
        x1_s, glu16, w_out16 = _mix(o_s, proj_s, y_s, xs, hgrn_norm_g[l], glu_w[l].astype(F32), glu_b[l],
                                    w_out[l].astype(F32), gate_col=gate_col, name="mix_sample")
        xs, up16, down16 = _mlp(x1_s, norm2_g[l], mlp_up[l].astype(F32), mlp_down[l].astype(F32), final_norm_g,
                                final_norm=last, name="mlp_sample")
        x1_p, = _mix(o_p, proj_p, y_p, xp, hgrn_norm_g[l], glu16, glu_b[l], w_out16, gate_col=gate_col,
                     name="mix_prompt")
        xp, = _mlp(x1_p, norm2_g[l], up16, down16, final_norm_g, final_norm=last, name="mlp_prompt")
        new_states.append((sh_p, sr_p.reshape(bp, g_b, p_state), si_p.reshape(bp, g_b, p_state),
                           sh_s[0], sr_s.reshape(bs, g_b, p_state), si_s.reshape(bs, g_b, p_state)))

    y_prompt = xp.reshape(bp, seq, d).astype(x_prompt.dtype)
    y_sample = xs.reshape(bs, dseq, d).astype(x_sample.dtype)
    stacked = [jnp.stack([st[i] for st in new_states]) for i in range(6)]
    return (y_prompt, y_sample, *stacked)
```

```python
import functools

import jax
import jax.numpy as jnp
import numpy as np
from jax import lax
from jax.experimental import pallas as pl
from jax.experimental.pallas import tpu as pltpu

F32 = jnp.float32
BF16 = jnp.bfloat16
EPS = 1e-6

LANES = 128
SUBLANES = 8
MXU_DIM = 256
VMEM_LIMIT = 56 * 1024 * 1024

HEAD_DIM = 128
S5_GROUP = 16
HGRN_CHUNK = 64
ROW_TILE = 1024
NEG_BIG = -1e30


def _dot(a, b):
    return jnp.dot(a, b, preferred_element_type=F32)


def _dot_nt(a, b):
    return lax.dot_general(a, b, (((1,), (1,)), ((), ())), preferred_element_type=F32)


def _dot_tn(a, b):
    return lax.dot_general(a, b, (((0,), (0,)), ((), ())), preferred_element_type=F32)


def _split3(x):
    hi = x.astype(BF16)
    r1 = x - hi.astype(F32)
    mid = r1.astype(BF16)
    lo = (r1 - mid.astype(F32)).astype(BF16)
    return hi, mid, lo


def _params(*sem):
    return pltpu.CompilerParams(dimension_semantics=sem, vmem_limit_bytes=VMEM_LIMIT)


def _rmsnorm_rows(src_ref, g, dst_ref, rows, chunk=256):
    def body(c, carry):
        r0 = pl.multiple_of(c * chunk, chunk)
        x = src_ref[pl.ds(r0, chunk), :]
        var = jnp.mean(x * x, axis=-1, keepdims=True)
        dst_ref[pl.ds(r0, chunk), :] = (x * lax.rsqrt(var + EPS) * g).astype(dst_ref.dtype)
        return carry

    lax.fori_loop(0, rows // chunk, body, 0)


def _bf16_weights(w_ref, copy_ref, at=(Ellipsis,), first_row=0):
    w = w_ref[at]
    if copy_ref is None:
        return w
    w = w.astype(BF16)

    @pl.when(pl.program_id(0) == first_row)
    def _():
        copy_ref[at] = w

    return w


def _weight_copy(w, block, index_map, last_index, first_row=0):
    if w.dtype == BF16:
        return [], []

    def once(i, j):
        during = index_map(jnp.full_like(i, first_row), j)
        before = index_map(jnp.full_like(i, first_row), jnp.zeros_like(j))
        pick = lambda b, d, a: jnp.where(i < first_row, b, jnp.where(i == first_row, d, a))
        return jax.tree.map(pick, before, during, last_index)

    return [pl.BlockSpec(block, once)], [jax.ShapeDtypeStruct(w.shape, BF16)]


def _lead_in_maps(n):
    nxt = lambda i: jnp.minimum(i, n - 1)
    cur = lambda i: jnp.maximum(i - 1, 0)
    col = lambda i, j: jnp.where(i > 0, j, 0)
    return nxt, cur, col


def _by_parity(i, n_tiles, step, buf0, buf1):
    even = i % 2 == 0
    inner = jnp.logical_and(i > 0, i < n_tiles)
    pl.when(i == 0)(lambda: step(buf0, None))
    pl.when(jnp.logical_and(inner, even))(lambda: step(buf0, buf1))
    pl.when(jnp.logical_and(inner, jnp.logical_not(even)))(lambda: step(buf1, buf0))
    pl.when(jnp.logical_and(i == n_tiles, even))(lambda: step(None, buf1))
    pl.when(jnp.logical_and(i == n_tiles, jnp.logical_not(even)))(lambda: step(None, buf0))


def _inproj_kernel(x_ref, g_ref, w_ref, o_ref, *rest, n_tiles, n_cols):
    *w16_ref, h0_ref, h1_ref = rest
    j = pl.program_id(1)
    rows_per_step = h0_ref.shape[0] // n_cols
    rows = pl.ds(pl.multiple_of(j * rows_per_step, rows_per_step), rows_per_step)

    def step(h_norm, h_proj):
        if h_norm is not None:
            x = x_ref[rows, :]
            var = jnp.mean(x * x, axis=-1, keepdims=True)
            h_norm[rows, :] = (x * lax.rsqrt(var + EPS) * g_ref[...]).astype(BF16)
        if h_proj is not None:
            o_ref[...] = _dot(h_proj[...], _bf16_weights(w_ref, *(w16_ref or [None]), first_row=1))

    _by_parity(pl.program_id(0), n_tiles, step, h0_ref, h1_ref)


def _row_tile(m):
    return ROW_TILE if m % ROW_TILE == 0 else ROW_TILE // 2


def _inproj(x, g, w, *, name, tn=1280):
    m, d = x.shape
    n_out = w.shape[1]
    tm = _row_tile(m)
    if w.dtype != BF16:
        tn //= 2
    nj = n_out // tn
    assert m % tm == 0 and n_out % tn == 0 and tm % (nj * SUBLANES) == 0
    n = m // tm
    nxt, cur, col = _lead_in_maps(n)
    w_spec = ((d, tn), lambda i, j: (0, col(i, j)))
    copy_specs, copy_shapes = _weight_copy(w, *w_spec, (0, nj - 1), first_row=1)
    return pl.pallas_call(
        functools.partial(_inproj_kernel, n_tiles=n, n_cols=nj),
        grid=(n + 1, nj),
        in_specs=[
            pl.BlockSpec((tm, d), lambda i, j: (nxt(i), 0)),
            pl.BlockSpec((1, d), lambda i, j: (0, 0)),
            pl.BlockSpec(*w_spec),
        ],
        out_specs=[pl.BlockSpec((tm, tn), lambda i, j: (cur(i), col(i, j)))] + copy_specs,
        out_shape=[jax.ShapeDtypeStruct((m, n_out), F32)] + copy_shapes,
        scratch_shapes=[pltpu.VMEM((tm, d), BF16)] * 2,
        compiler_params=_params("arbitrary", "arbitrary"),
        name=name,
    )(x, g.reshape(1, d), w)


def _lower_bound(logits, layer):
    m = jnp.max(logits, axis=0, keepdims=True)
    e = jnp.exp(logits - m)
    return jnp.sum(e[: layer + 1], axis=0, keepdims=True) / jnp.sum(e, axis=0, keepdims=True)


def _row_to_col(e_row):
    n = e_row.shape[1]
    hi, mid, lo = (p.astype(F32) for p in _split3(e_row))
    r = lax.broadcasted_iota(jnp.int32, (2 * SUBLANES, n), 0)
    pieces = jnp.where(r == 0, hi, jnp.where(r == 1, mid, jnp.where(r == 2, lo, 0.0)))
    return _dot_tn(pieces.astype(BF16), jnp.ones((2 * SUBLANES, LANES), BF16))


def _hgrn_gates(qp, fp, lb):
    q = qp * jax.nn.sigmoid(qp)
    f = lb + (1.0 - lb) * jax.nn.sigmoid(fp)
    return q, f, 1.0 - f, jnp.log(f)


def _hgrn_tables(c):
    t = np.arange(c)[:, None]
    j = np.arange(c)[None, :]
    sums, pairs = [j <= t], []
    bs = c
    while bs >= 2:
        hs = bs // 2
        ref = t - t % bs + hs - 1
        upper = t % bs >= hs
        sums.append(np.where(upper, (j > ref) & (j <= t), (j > t) & (j <= ref)))
        pairs.append((t // bs == j // bs) & upper & (j % bs < hs))
        bs = hs
    sums.append(j > t)
    pairs.append(t == j)
    sums = np.concatenate(sums, axis=0).astype(np.float32)
    return jnp.asarray(np.concatenate([sums] * 3, axis=1), BF16), jnp.asarray(np.stack(pairs), F32)


def _upper_q_lower_k(q, k, bs):
    c = q.shape[0]
    hs = bs // 2
    if hs % SUBLANES == 0:
        parts = []
        for r0 in range(0, c, bs):
            parts += [k[r0 : r0 + hs], q[r0 + hs : r0 + bs]]
        return jnp.concatenate(parts, axis=0)
    row = lax.broadcasted_iota(jnp.int32, (c, 1), 0)
    return jnp.where(row % bs >= hs, q, k)


def _hgrn_prep(qp, fp, v, lb, sums, levels):
    c = qp.shape[0]
    q = qp * jax.nn.sigmoid(qp)
    f = lb + (1.0 - lb) * jax.nn.sigmoid(fp)
    k = 1.0 - f
    e = jnp.exp2(_dot(sums, jnp.concatenate(_split3(jnp.log2(f)), axis=0)))
    e_b, e_rest = e[:c], e[(levels + 1) * c :]
    ys = [(_upper_q_lower_k(q, k, c >> l) * e[(l + 1) * c : (l + 2) * c]).astype(BF16) for l in range(levels)]
    return dict(ys=ys, q16=q.astype(BF16), k16=k.astype(BF16), v16=v.astype(BF16),
                qe=(q * e_b).astype(BF16), kd=(k * e_rest).astype(BF16), e_last=e_b[c - 1 : c])


def _hgrn_apply(prep, st_ref, pairs_ref, heads):
    levels = pairs_ref.shape[0] - 1
    pairs = [(n, h) for n in range(len(prep)) for h in range(heads)]
    head = lambda x, h: x[:, h * HEAD_DIM : (h + 1) * HEAD_DIM]
    scores = {}
    for n, h in pairs:
        p = prep[n]
        a = _dot_nt(head(p["q16"], h), head(p["k16"], h)) * pairs_ref[levels]
        for l in range(levels):
            y = head(p["ys"][l], h)
            a = a + _dot_nt(y, y) * pairs_ref[l]
        scores[n, h] = a.astype(BF16)
    outs = {}
    for n, h in pairs:
        p = prep[n]
        st = st_ref[n, h]
        outs[n, h] = _dot_nt(head(p["qe"], h), st.astype(BF16)) + _dot(scores[n, h], head(p["v16"], h))
        st_ref[n, h] = head(p["e_last"], h) * st + _dot_tn(head(p["v16"], h), head(p["kd"], h))
    return [jnp.concatenate([outs[n, h] for h in range(heads)], axis=1) for n in range(len(prep))]


def _hgrn_prompt_kernel(q_ref, f_ref, v_ref, lbl_ref, sums_ref, pairs_ref, o_ref, s_out_ref, st_ref, *,
                        layer, heads, n_chunks):
    t = pl.program_id(1)

    @pl.when(t == 0)
    def _():
        st_ref[...] = jnp.zeros_like(st_ref)

    lb = _lower_bound(lbl_ref[...], layer)
    n_seq = q_ref.shape[0]
    levels = pairs_ref.shape[0] - 1

    def prep(c):
        rows = slice(c * HGRN_CHUNK, (c + 1) * HGRN_CHUNK)
        return [_hgrn_prep(q_ref[n, rows, :], f_ref[n, rows, :], v_ref[n, rows, :], lb, sums_ref[...], levels)
                for n in range(n_seq)]

    nxt = prep(0)
    for c in range(n_chunks):
        cur, nxt = nxt, (prep(c + 1) if c + 1 < n_chunks else None)
        for n, o in enumerate(_hgrn_apply(cur, st_ref, pairs_ref, heads)):
            o_ref[n, c * HGRN_CHUNK : (c + 1) * HGRN_CHUNK, :] = o

    @pl.when(t == pl.num_programs(1) - 1)
    def _():
        for n in range(n_seq):
            for h in range(heads):
                s_out_ref[n, h] = st_ref[n, h].T


def _hgrn_prompt(proj, lb_logits, *, layer, batch, seq, w_a, tt=256, n_seq=2):
    heads = w_a // HEAD_DIM
    assert seq % tt == 0 and tt % HGRN_CHUNK == 0 and batch % n_seq == 0
    sums, pairs = _hgrn_tables(HGRN_CHUNK)
    proj3 = proj.reshape(batch, seq, proj.shape[1])
    col = lambda c: pl.BlockSpec((n_seq, tt, w_a), lambda b, t: (b, t, c))
    const = lambda x: pl.BlockSpec(x.shape, lambda b, t: (0,) * x.ndim)
    o, s = pl.pallas_call(
        functools.partial(_hgrn_prompt_kernel, layer=layer, heads=heads, n_chunks=tt // HGRN_CHUNK),
        grid=(batch // n_seq, seq // tt),
        in_specs=[col(0), col(1), col(2), const(lb_logits), const(sums), const(pairs)],
        out_specs=[
            pl.BlockSpec((n_seq, tt, w_a), lambda b, t: (b, t, 0)),
            pl.BlockSpec((n_seq, heads, HEAD_DIM, HEAD_DIM), lambda b, t: (b, 0, 0, 0)),
        ],
        out_shape=[
            jax.ShapeDtypeStruct((batch, seq, w_a), F32),
            jax.ShapeDtypeStruct((batch, heads, HEAD_DIM, HEAD_DIM), F32),
        ],
        scratch_shapes=[pltpu.VMEM((n_seq, heads, HEAD_DIM, HEAD_DIM), F32)],
        compiler_params=_params("arbitrary", "arbitrary"),
        name="hgrn_prompt",
    )(proj3, proj3, proj3, lb_logits, sums, pairs)
    return o.reshape(batch * seq, w_a), s


def _hgrn_sample_kernel(q_ref, f_ref, v_ref, lbl_ref, s_in_ref, o_ref, s_out_ref, *, layer, heads, seq):
    n_seq = SUBLANES // seq
    lb = _lower_bound(lbl_ref[...], layer)
    row = lax.broadcasted_iota(jnp.int32, (SUBLANES, 1), 0)
    pos = row % seq
    work = []
    for tile, h in ((tile, h) for tile in range(q_ref.shape[0] // SUBLANES) for h in range(heads)):
        cols = slice(h * HEAD_DIM, (h + 1) * HEAD_DIM)
        rows = slice(tile * SUBLANES, (tile + 1) * SUBLANES)
        q, _, k, logf = _hgrn_gates(q_ref[rows, cols], f_ref[rows, cols], lb[:, cols])
        v = v_ref[rows, cols]
        b = logf
        shift = 1
        while shift < seq:
            b = b + jnp.where(pos >= shift, pltpu.roll(b, shift, 0), 0.0)
            shift *= 2
        o = jnp.zeros((SUBLANES, HEAD_DIM), F32)
        for sl in range(SUBLANES):
            lo_t, hi_t = sl, (sl // seq + 1) * seq
            mask = jnp.logical_and(row >= lo_t, row < hi_t)
            z = q * jnp.exp(jnp.where(mask, b - b[sl : sl + 1], NEG_BIG)) * k[sl : sl + 1]
            o = o + jnp.sum(z, axis=-1, keepdims=True) * v[sl : sl + 1]
        per_seq = []
        for n in range(n_seq):
            own = jnp.logical_and(row >= n * seq, row < (n + 1) * seq)
            b_last = b[(n + 1) * seq - 1 : (n + 1) * seq, :]
            qe = jnp.where(own, q * jnp.exp(b), 0.0).astype(BF16)
            kd = jnp.where(own, k * jnp.exp(jnp.where(own, b_last - b, 0.0)), 0.0).astype(BF16)
            per_seq.append((tile * n_seq + n, qe, kd, jnp.exp(b_last)))
        work.append((rows, cols, h, o, v.astype(BF16), per_seq))
    for rows, cols, h, o, _, per_seq in work:
        for i, qe, _, _ in per_seq:
            o = o + _dot(qe, s_in_ref[0, i, h].astype(BF16))
        o_ref[rows, cols] = o
    for _, _, h, _, v16, per_seq in work:
        for i, _, kd, e_last in per_seq:
            s_out_ref[0, i, h] = _row_to_col(e_last) * s_in_ref[0, i, h] + _dot_tn(kd, v16)


def _hgrn_sample(proj, lb_logits, state, *, layer, row0, batch, seq, w_a, tiles=8):
    heads = w_a // HEAD_DIM
    rows = tiles * SUBLANES
    assert SUBLANES % seq == 0 and row0 % rows == 0 and (batch * seq) % rows == 0
    n_seq = rows // seq
    t0 = row0 // rows
    col = lambda c: pl.BlockSpec((rows, w_a), lambda p: (t0 + p, c))
    st = pl.BlockSpec((1, n_seq, heads, HEAD_DIM, HEAD_DIM), lambda p: (layer, p, 0, 0, 0))
    st_out = pl.BlockSpec((1, n_seq, heads, HEAD_DIM, HEAD_DIM), lambda p: (0, p, 0, 0, 0))
    return pl.pallas_call(
        functools.partial(_hgrn_sample_kernel, layer=layer, heads=heads, seq=seq),
        grid=(batch // n_seq,),
        in_specs=[col(0), col(1), col(2), pl.BlockSpec(lb_logits.shape, lambda p: (0, 0)), st],
        out_specs=[pl.BlockSpec((rows, w_a), lambda p: (p, 0)), st_out],
        out_shape=[
            jax.ShapeDtypeStruct((batch * seq, w_a), F32),
            jax.ShapeDtypeStruct((1, batch, heads, HEAD_DIM, HEAD_DIM), F32),
        ],
        compiler_params=_params("arbitrary"),
        name="hgrn_sample",
    )(proj, proj, proj, lb_logits, state)


def _s5_weights_kernel(are_ref, aim_ref, ls_ref, bre_ref, bim_ref, cre_ref, cim_ref, rep_ref, tile_ref, same_ref,
                       lre_ref, lim_ref, bbre_ref, bbim_ref, ccre_ref, ccim_ref):
    a_re = are_ref[...]
    a_im = aim_ref[...]
    dt = jnp.exp(ls_ref[...])
    mag = jnp.exp(a_re * dt)
    lam_re = mag * jnp.cos(a_im * dt)
    lam_im = mag * jnp.sin(a_im * dt)
    den = a_re * a_re + a_im * a_im
    nr, ni = lam_re - 1.0, lam_im
    r_re = (nr * a_re + ni * a_im) / den
    r_im = (ni * a_re - nr * a_im) / den
    lre_ref[...] = lam_re
    lim_ref[...] = lam_im
    b_re = bre_ref[...]
    b_im = bim_ref[...]
    same = same_ref[...]
    bbre_ref[0] = (_dot(rep_ref[...], (r_re * b_re - r_im * b_im).astype(BF16)) * same).astype(BF16)
    bbim_ref[0] = (_dot(rep_ref[...], (r_re * b_im + r_im * b_re).astype(BF16)) * same).astype(BF16)
    ccre_ref[0] = (_dot(cre_ref[...].astype(BF16), tile_ref[...]) * same).astype(BF16)
    ccim_ref[0] = (_dot(cim_ref[...].astype(BF16), tile_ref[...]) * same).astype(BF16)


def _s5_weights(a_re, a_im, log_step, b_re, b_im, c_re, c_im):
    g, p = a_re.shape
    c = b_re.shape[-1]
    gt = MXU_DIM // c
    nj, kc, kp = g // gt, gt * c, gt * p
    flat = lambda x: x.reshape(1, g * p)
    chan = lambda x: jnp.transpose(x, (2, 0, 1)).reshape(c, g * p)
    row_g, lane_g = np.arange(kc)[:, None] // c, np.arange(kp)[None, :] // p
    rep = jnp.asarray(np.arange(kc)[:, None] % c == np.arange(c)[None, :], BF16)
    tile = jnp.asarray(np.arange(p)[:, None] == np.arange(kp)[None, :] % p, BF16)
    same = jnp.asarray(row_g == lane_g, F32)
    lanes = lambda rows: pl.BlockSpec((rows, kp), lambda j: (0, j))
    const = lambda x: pl.BlockSpec(x.shape, lambda j: (0, 0))
    tiles = pl.BlockSpec((1, kc, kp), lambda j: (j, 0, 0))
    tiles_shape = jax.ShapeDtypeStruct((nj, kc, kp), BF16)
    lam_shape = jax.ShapeDtypeStruct((1, g * p), F32)
    lam_re, lam_im, *mats = pl.pallas_call(
        _s5_weights_kernel,
        grid=(nj,),
        in_specs=[lanes(1), lanes(1), lanes(1), lanes(c), lanes(c),
                  pl.BlockSpec((kc, p), lambda j: (j, 0)), pl.BlockSpec((kc, p), lambda j: (j, 0)),
                  const(rep), const(tile), const(same)],
        out_specs=[lanes(1), lanes(1), tiles, tiles, tiles, tiles],
        out_shape=[lam_shape, lam_shape] + [tiles_shape] * 4,
        compiler_params=_params("arbitrary"),
        name="s5_weights",
    )(flat(a_re), flat(a_im), flat(jnp.repeat(log_step, p)), chan(b_re), chan(b_im),
      c_re.reshape(g * c, p), c_im.reshape(g * c, p), rep, tile, same)
    return (lam_re.reshape(-1, 1, LANES), lam_im.reshape(-1, 1, LANES), *mats)


def _s5_lanes(x_ref, lo, hi):
    if len(x_ref.shape) == 2:
        return x_ref[:, lo:hi]
    return jnp.concatenate([x_ref[m] for m in range(lo // LANES, hi // LANES)], axis=1)


def _s5_in(ub, bre_ref, bim_ref, xr_ref, xi_ref):
    nj, kc, kp = bre_ref.shape
    for j in range(nj):
        uj = ub[:, j * kc : (j + 1) * kc]
        for x_ref, w_ref in ((xr_ref, bre_ref), (xi_ref, bim_ref)):
            bu = _dot(uj, w_ref[j])
            if len(x_ref.shape) == 2:
                x_ref[:, j * kp : (j + 1) * kp] = bu
            else:
                for m in range(kp // LANES):
                    x_ref[j * kp // LANES + m] = bu[:, m * LANES : (m + 1) * LANES]


def _s5_out(xr_ref, xi_ref, cre_ref, cim_ref):
    nj, kc, kp = cre_ref.shape
    tiles = []
    for j in range(nj):
        xr = _s5_lanes(xr_ref, j * kp, (j + 1) * kp).astype(BF16)
        xi = _s5_lanes(xi_ref, j * kp, (j + 1) * kp).astype(BF16)
        tiles.append(_dot_nt(xr, cre_ref[j]) - _dot_nt(xi, cim_ref[j]))
    return tiles


def _s5_step(lr, li, xr, xi, bur, bui):
    return lr * xr - li * xi + bur, lr * xi + li * xr + bui


def _s5_lambda(lre_ref, lim_ref, lo, hi, rows):
    cat = lambda ref: jnp.concatenate([ref[m] for m in range(lo // LANES, hi // LANES)], axis=1)
    return jnp.broadcast_to(cat(lre_ref), (rows, hi - lo)), jnp.broadcast_to(cat(lim_ref), (rows, hi - lo))


def _to_time_major(x):
    n, ts, w = x.shape
    return jnp.swapaxes(x, 0, 1).reshape(ts * n, w)


def _from_time_major(x, n):
    rows, w = x.shape
    return jnp.swapaxes(x.reshape(rows // n, n, w), 0, 1)


def _s5_prompt_kernel(*refs, nv, lane_block):
    u_refs = refs[:nv]
    (lre_ref, lim_ref, bre_ref, bim_ref, cre_ref, cim_ref, d_ref,
     y_ref, xr_out_ref, xi_out_ref, xr_ref, xi_ref, sr_ref, si_ref) = refs[nv:]
    ts = u_refs[0].shape[0]
    gp = xr_ref.shape[1]
    step = pl.program_id(0)

    @pl.when(step == 0)
    def _():
        sr_ref[...] = jnp.zeros_like(sr_ref)
        si_ref[...] = jnp.zeros_like(si_ref)

    u = _to_time_major(jnp.stack([r[...] for r in u_refs], axis=0))
    _s5_in(u.astype(BF16), bre_ref, bim_ref, xr_ref, xi_ref)

    for lo in range(0, gp, lane_block):
        ls = slice(lo, lo + lane_block)
        lr, li = _s5_lambda(lre_ref, lim_ref, lo, lo + lane_block, nv)

        xr, xi = sr_ref[:, ls], si_ref[:, ls]
        for r in range(ts):
            rows = slice(r * nv, (r + 1) * nv)
            xr, xi = _s5_step(lr, li, xr, xi, xr_ref[rows, ls], xi_ref[rows, ls])
            xr_ref[rows, ls] = xr
            xi_ref[rows, ls] = xi
        sr_ref[:, ls] = xr
        si_ref[:, ls] = xi

    kc = cre_ref.shape[1]
    d = d_ref[...]
    for j, y in enumerate(_s5_out(xr_ref, xi_ref, cre_ref, cim_ref)):
        cols = slice(j * kc, (j + 1) * kc)
        y_ref[:, :, cols] = _from_time_major(y + d[:, cols] * u[:, cols], nv)

    @pl.when(step == pl.num_programs(0) - 1)
    def _():
        xr_out_ref[...] = sr_ref[...]
        xi_out_ref[...] = si_ref[...]


def _s5_carry_kernel(y_ref, fr_ref, fi_ref, lre_ref, lim_ref, cre_ref, cim_ref,
                     o_ref, xr_out_ref, xi_out_ref, zr_ref, zi_ref, sr_ref, si_ref, *, half_len, lane_block):
    nrow, gp = fr_ref.shape
    tf = y_ref.shape[2]
    step = pl.program_id(0)
    odd = lax.broadcasted_iota(jnp.int32, (nrow, 1), 0) % 2 == 1

    @pl.when(step == 0)
    def _():
        for lo in range(0, gp, lane_block):
            ls = slice(lo, lo + lane_block)
            pr, pi = _s5_lambda(lre_ref, lim_ref, lo, lo + lane_block, nrow)
            n = 1
            while n < half_len:
                pr, pi = pr * pr - pi * pi, 2.0 * pr * pi
                n *= 2
            fr, fi = fr_ref[:, ls], fi_ref[:, ls]
            fr = jnp.where(odd, pltpu.roll(fr, 1, 0), fr)
            fi = jnp.where(odd, pltpu.roll(fi, 1, 0), fi)
            sr_ref[:, ls] = jnp.where(odd, pr * fr - pi * fi, fr)
            si_ref[:, ls] = jnp.where(odd, pr * fi + pi * fr, fi)

    for lo in range(0, gp, lane_block):
        ls = slice(lo, lo + lane_block)
        lr, li = _s5_lambda(lre_ref, lim_ref, lo, lo + lane_block, nrow)

        zr, zi = sr_ref[:, ls], si_ref[:, ls]
        for r in range(tf):
            rows = slice(r * nrow, (r + 1) * nrow)
            zr, zi = lr * zr - li * zi, lr * zi + li * zr
            zr_ref[rows, ls] = zr
            zi_ref[rows, ls] = zi
        sr_ref[:, ls] = zr
        si_ref[:, ls] = zi

    kc = cre_ref.shape[1]
    nseq = y_ref.shape[0]
    for j, y in enumerate(_s5_out(zr_ref, zi_ref, cre_ref, cim_ref)):
        cols = slice(j * kc, (j + 1) * kc)
        o_ref[:, :, :, cols] = y_ref[:, :, :, cols] + _from_time_major(y, nrow).reshape(nseq, nrow // nseq, tf, kc)

    @pl.when(step == pl.num_programs(0) - 1)
    def _():
        xr_out_ref[...] = fr_ref[...] + sr_ref[...]
        xi_out_ref[...] = fi_ref[...] + si_ref[...]


def _s5_weights_specs(ws):
    zero = lambda n: (lambda *_: (0,) * n)
    return [pl.BlockSpec(w.shape, zero(w.ndim)) for w in ws]


def _s5_prompt(proj, ws, *, nb, seq, ucol, w_b, ts=64, tf=64, lane_block=1024):
    lam_re, lam_im, bre, bim, cre, cim, d = ws
    gp = lam_re.shape[0] * LANES
    nv = SUBLANES
    assert nv == 2 * nb, "two pieces per sequence"
    plen, half_len = seq // 2, seq // 4
    assert plen % ts == 0 and half_len % tf == 0 and gp % lane_block == 0
    assert half_len & (half_len - 1) == 0, "repeated squaring needs a power of two"
    nblk = plen // ts
    u_spec = lambda n: pl.BlockSpec((ts, w_b), lambda s: (n * nblk + s, ucol))
    state = pl.BlockSpec((nv, gp), lambda s: (0, 0))
    state_shape = jax.ShapeDtypeStruct((nv, gp), F32)
    scratch = lambda steps: [pltpu.VMEM((nv * steps, gp), F32)] * 2 + [pltpu.VMEM((nv, gp), F32)] * 2
    y, fr, fi = pl.pallas_call(
        functools.partial(_s5_prompt_kernel, nv=nv, lane_block=lane_block),
        grid=(nblk,),
        in_specs=[u_spec(n) for n in range(nv)] + _s5_weights_specs(ws),
        out_specs=[pl.BlockSpec((nv, ts, w_b), lambda s: (0, s, 0)), state, state],
        out_shape=[jax.ShapeDtypeStruct((nv, plen, w_b), F32), state_shape, state_shape],
        scratch_shapes=scratch(ts),
        compiler_params=_params("arbitrary"),
        name="s5_prompt",
    )(*([proj] * nv), *ws)

    y5 = y.reshape(nb, 2, 2, half_len, w_b)
    y_spec = pl.BlockSpec((nb, None, 2, tf, w_b), lambda s: (0, 1, 0, s, 0))
    y5, xr, xi = pl.pallas_call(
        functools.partial(_s5_carry_kernel, half_len=half_len, lane_block=lane_block),
        grid=(half_len // tf,),
        in_specs=[y_spec, state, state] + _s5_weights_specs((lam_re, lam_im, cre, cim)),
        out_specs=[y_spec, state, state],
        out_shape=[jax.ShapeDtypeStruct(y5.shape, F32), state_shape, state_shape],
        scratch_shapes=scratch(tf),
        input_output_aliases={0: 0},
        compiler_params=_params("arbitrary"),
        name="s5_prompt_carry",
    )(y5, fr, fi, lam_re, lam_im, cre, cim)
    last = lambda x: x.reshape(nb, 2, gp)[:, 1]
    return y5.reshape(nb * seq, w_b), last(xr), last(xi)


def _s5_sample_kernel(u_ref, x0r_ref, x0i_ref, lre_ref, lim_ref, bre_ref, bim_ref, cre_ref, cim_ref, d_ref,
                      y_ref, xr_out_ref, xi_out_ref, xr_ref, xi_ref, *, seq):
    nb = x0r_ref.shape[0]
    u = u_ref[...]
    _s5_in(u.astype(BF16), bre_ref, bim_ref, xr_ref, xi_ref)
    for m in range(xr_ref.shape[0]):
        ls = slice(m * LANES, (m + 1) * LANES)
        lr, li = lre_ref[m], lim_ref[m]
        xr, xi = x0r_ref[:, ls], x0i_ref[:, ls]
        for r in range(seq):
            rows = pl.ds(r, nb, stride=seq)
            xr, xi = _s5_step(lr, li, xr, xi, xr_ref[m, rows, :], xi_ref[m, rows, :])
            xr_ref[m, rows, :] = xr
            xi_ref[m, rows, :] = xi
        xr_out_ref[:, ls] = xr
        xi_out_ref[:, ls] = xi
    kc = cre_ref.shape[1]
    d = d_ref[...]
    for j, y in enumerate(_s5_out(xr_ref, xi_ref, cre_ref, cim_ref)):
        cols = slice(j * kc, (j + 1) * kc)
        y_ref[:, cols] = y + d[:, cols] * u[:, cols]


def _s5_sample(proj, x0_re, x0_im, ws, *, row0, ucol, seq, w_b, nb=64):
    batch, gp = x0_re.shape
    n_slabs = gp // LANES
    assert batch % nb == 0 and row0 % (nb * seq) == 0
    t0 = row0 // (nb * seq)
    return pl.pallas_call(
        functools.partial(_s5_sample_kernel, seq=seq),
        grid=(batch // nb,),
        in_specs=[
            pl.BlockSpec((nb * seq, w_b), lambda p: (t0 + p, ucol)),
            pl.BlockSpec((nb, gp), lambda p: (p, 0)),
            pl.BlockSpec((nb, gp), lambda p: (p, 0)),
        ] + _s5_weights_specs(ws),
        out_specs=[
            pl.BlockSpec((nb * seq, w_b), lambda p: (p, 0)),
            pl.BlockSpec((nb, gp), lambda p: (p, 0)),
            pl.BlockSpec((nb, gp), lambda p: (p, 0)),
        ],
        out_shape=[
            jax.ShapeDtypeStruct((batch * seq, w_b), F32),
            jax.ShapeDtypeStruct((batch, gp), F32),
            jax.ShapeDtypeStruct((batch, gp), F32),
        ],
        scratch_shapes=[pltpu.VMEM((n_slabs, nb * seq, LANES), F32), pltpu.VMEM((n_slabs, nb * seq, LANES), F32)],
        compiler_params=_params("arbitrary"),
        name="s5_sample",
    )(proj, x0_re, x0_im, *ws)


def _mix_gates(o_ref, gate_ref, y_ref, ng_ref, gluw_ref, glub_ref, a_ref, rows):
    w_a = o_ref.shape[1]
    o = o_ref[rows, :]
    var = jnp.mean(o * o, axis=-1, keepdims=True)
    gate = gate_ref[rows, :]
    a_ref[rows, :w_a] = (o * lax.rsqrt(var + EPS) * ng_ref[...] * (gate * jax.nn.sigmoid(gate))).astype(BF16)
    y = jax.nn.gelu(y_ref[rows, :])
    z = _dot(y.astype(BF16), gluw_ref[...]) + glub_ref[...]
    a_ref[rows, w_a:] = (y * jax.nn.sigmoid(z)).astype(BF16)


def _mix_kernel(o_ref, gate_ref, y_ref, x_ref, ng_ref, gluw_ref, glub_ref, wout_ref, out_ref, *rest,
                n_tiles, n_cols):
    *copies, a0_ref, a1_ref = rest
    glu16_ref, wout16_ref = copies or (None, None)
    i = pl.program_id(0)
    j = pl.program_id(1)
    rows_per_step = a0_ref.shape[0] // n_cols

    if glu16_ref is not None:
        @pl.when(jnp.logical_and(i == 0, j == 0))
        def _():
            _bf16_weights(gluw_ref, glu16_ref)
        gluw_ref = glu16_ref

    def step(a_gate, a_proj):
        pieces = out_ref.shape[1] // MXU_DIM
        sub = rows_per_step // pieces
        for p in range(pieces):
            cols = (slice(None), slice(p * MXU_DIM, (p + 1) * MXU_DIM))
            if a_gate is not None:
                sub_rows = pl.ds(pl.multiple_of(j * rows_per_step + p * sub, sub), sub)
                _mix_gates(o_ref, gate_ref, y_ref, ng_ref, gluw_ref, glub_ref, a_gate, sub_rows)
            if a_proj is not None:
                w = _bf16_weights(wout_ref, wout16_ref, cols, first_row=1)
                out_ref[cols] = x_ref[cols] + _dot(a_proj[...], w)

    _by_parity(i, n_tiles, step, a0_ref, a1_ref)


def _mix(o, proj, y, x, norm_g, glu_w, glu_b, w_out, *, gate_col, name, tn=512):
    m, w_a = o.shape
    w_b = y.shape[1]
    d = w_out.shape[1]
    tm = _row_tile(m)
    nj = d // tn
    assert m % tm == 0 and d % tn == 0 and tm % (nj * SUBLANES) == 0
    n = m // tm
    nxt, cur, col = _lead_in_maps(n)
    const = lambda i, j: (0, 0)
    cur_tile = lambda i, j: (cur(i), col(i, j))
    glu_spec = ((w_b, w_b), const)
    wout_spec = ((w_a + w_b, tn), lambda i, j: (0, col(i, j)))
    assert glu_w.dtype == w_out.dtype
    copy_specs, copy_shapes = (a + b for a, b in zip(_weight_copy(glu_w, *glu_spec, (0, 0)),
                                                     _weight_copy(w_out, *wout_spec, (0, nj - 1), first_row=1)))
    return pl.pallas_call(
        functools.partial(_mix_kernel, n_tiles=n, n_cols=nj),
        grid=(n + 1, nj),
        in_specs=[
            pl.BlockSpec((tm, w_a), lambda i, j: (nxt(i), 0)),
            pl.BlockSpec((tm, w_a), lambda i, j: (nxt(i), gate_col)),
            pl.BlockSpec((tm, w_b), lambda i, j: (nxt(i), 0)),
            pl.BlockSpec((tm, tn), cur_tile),
            pl.BlockSpec((1, w_a), const),
            pl.BlockSpec(*glu_spec),
            pl.BlockSpec((1, w_b), const),
            pl.BlockSpec(*wout_spec),
        ],
        out_specs=[pl.BlockSpec((tm, tn), cur_tile)] + copy_specs,
        out_shape=[jax.ShapeDtypeStruct((m, d), F32)] + copy_shapes,
        scratch_shapes=[pltpu.VMEM((tm, w_a + w_b), BF16)] * 2,
        compiler_params=_params("arbitrary", "arbitrary"),
        name=name,
    )(o, proj, y, x, norm_g.reshape(1, w_a), glu_w, glu_b.reshape(1, w_b), w_out)


def _mlp_kernel(x_ref, g_ref, up_ref, down_ref, gf_ref, o_ref, *rest, rows, final_norm):
    *copies, h_ref = rest
    up16_ref, down16_ref = copies or (None, None)
    f = pl.program_id(1)

    @pl.when(f == 0)
    def _():
        _rmsnorm_rows(x_ref, g_ref[...], h_ref, rows)
        o_ref[...] = x_ref[...]

    a = jnp.square(jnp.maximum(_dot(h_ref[...], _bf16_weights(up_ref, up16_ref)), 0.0)).astype(BF16)
    o_ref[...] += _dot(a, _bf16_weights(down_ref, down16_ref))

    if final_norm:
        @pl.when(f == pl.num_programs(1) - 1)
        def _():
            _rmsnorm_rows(o_ref, gf_ref[...], o_ref, rows)


def _mlp(x, g, up, down, gf, *, final_norm, name, tf=512):
    m, d = x.shape
    dff = up.shape[1]
    tm = _row_tile(m)
    assert m % tm == 0 and dff % tf == 0 and up.dtype == down.dtype
    up_spec = ((d, tf), lambda i, f: (0, f))
    down_spec = ((tf, d), lambda i, f: (f, 0))
    nf = dff // tf
    copy_specs, copy_shapes = (a + b for a, b in zip(_weight_copy(up, *up_spec, (0, nf - 1)),
                                                     _weight_copy(down, *down_spec, (nf - 1, 0))))
    return pl.pallas_call(
        functools.partial(_mlp_kernel, rows=tm, final_norm=final_norm),
        grid=(m // tm, dff // tf),
        in_specs=[
            pl.BlockSpec((tm, d), lambda i, f: (i, 0)),
            pl.BlockSpec((1, d), lambda i, f: (0, 0)),
            pl.BlockSpec(*up_spec),
            pl.BlockSpec(*down_spec),
            pl.BlockSpec((1, d), lambda i, f: (0, 0)),
        ],
        out_specs=[pl.BlockSpec((tm, d), lambda i, f: (i, 0))] + copy_specs,
        out_shape=[jax.ShapeDtypeStruct((m, d), F32)] + copy_shapes,
        scratch_shapes=[pltpu.VMEM((tm, d), BF16)],
        compiler_params=_params("arbitrary", "arbitrary"),
        name=name,
    )(x, g.reshape(1, d), up, down, gf.reshape(1, d))


def kernel(x_prompt, x_sample, state_hgrn, state_s5_re, state_s5_im, w_in, w_out, norm1_g, norm2_g, hgrn_lb_logits, hgrn_norm_g, s5_a_re, s5_a_im, s5_b_re, s5_b_im, s5_c_re, s5_c_im, s5_d, s5_log_step, glu_w, glu_b, mlp_up, mlp_down, final_norm_g):
    bp, seq, d = x_prompt.shape
    bs, dseq, _ = x_sample.shape
    depth = w_in.shape[0]
    w_a = hgrn_norm_g.shape[1]
    w_b = s5_d.shape[1]
    g_b, p_state = s5_a_re.shape[1], s5_a_re.shape[2]
    mp, ms = bp * seq, bs * dseq
    assert w_in.shape[2] == 4 * w_a + w_b and w_a == w_b and g_b * S5_GROUP == w_b
    gate_col, ucol = 3, (4 * w_a) // w_b

    xp = x_prompt.reshape(mp, d)
    xs = x_sample.reshape(ms, d)
    lb_logits = hgrn_lb_logits.astype(F32)
    new_states = []
    for l in range(depth):
        proj_s, w_in16 = _inproj(xs, norm1_g[l], w_in[l].astype(F32), name="inproj_sample")
        proj_p, = _inproj(xp, norm1_g[l], w_in16, name="inproj_prompt")

        o_p, sh_p = _hgrn_prompt(proj_p, lb_logits, layer=l, batch=bp, seq=seq, w_a=w_a)
        o_s, sh_s = _hgrn_sample(proj_s, lb_logits, state_hgrn, layer=l, row0=0, batch=bs, seq=dseq, w_a=w_a)

        ws = (*_s5_weights(*(w[l].astype(F32) for w in (s5_a_re, s5_a_im, s5_log_step, s5_b_re, s5_b_im,
                                                        s5_c_re, s5_c_im))),
              s5_d[l].astype(F32).reshape(1, w_b))
        y_p, sr_p, si_p = _s5_prompt(proj_p, ws, nb=bp, seq=seq, ucol=ucol, w_b=w_b)
        y_s, sr_s, si_s = _s5_sample(
            proj_s, state_s5_re[l].reshape(bs, g_b * p_state).astype(F32), state_s5_im[l].reshape(bs, g_b * p_state).astype(F32),
            ws, row0=0, ucol=ucol, seq=dseq, w_b=w_b)

        last = l == depth - 1
        x1_s, glu16, w_out16 = _mix(o_s, proj_s, y_s, xs, hgrn_norm_g[l], glu_w[l].astype(F32), glu_b[l],
                                    w_out[l].astype(F32), gate_col=gate_col, name="mix_sample")
        xs, up16, down16 = _mlp(x1_s, norm2_g[l], mlp_up[l].astype(F32), mlp_down[l].astype(F32), final_norm_g,
                                final_norm=last, name="mlp_sample")
        x1_p, = _mix(o_p, proj_p, y_p, xp, hgrn_norm_g[l], glu16, glu_b[l], w_out16, gate_col=gate_col,
                     name="mix_prompt")
        xp, = _mlp(x1_p, norm2_g[l], up16, down16, final_norm_g, final_norm=last, name="mlp_prompt")
        new_states.append((sh_p, sr_p.reshape(bp, g_b, p_state), si_p.reshape(bp, g_b, p_state),
                           sh_s[0], sr_s.reshape(bs, g_b, p_state), si_s.reshape(bs, g_b, p_state)))

    y_prompt = xp.reshape(bp, seq, d).astype(x_prompt.dtype)
    y_sample = xs.reshape(bs, dseq, d).astype(x_sample.dtype)
    stacked = [jnp.stack([st[i] for st in new_states]) for i in range(6)]
    return (y_prompt, y_sample, *stacked)
```

```python
import functools

import jax
import jax.numpy as jnp
import numpy as np
from jax import lax
from jax.experimental import pallas as pl
from jax.experimental.pallas import tpu as pltpu

F32 = jnp.float32
BF16 = jnp.bfloat16
EPS = 1e-6

LANES = 128
SUBLANES = 8
MXU_DIM = 256
VMEM_LIMIT = 56 * 1024 * 1024

HEAD_DIM = 128
S5_GROUP = 16
HGRN_CHUNK = 64
ROW_TILE = 1024
NEG_BIG = -1e30


def _dot(a, b):
    return jnp.dot(a, b, preferred_element_type=F32)


def _dot_nt(a, b):
    return lax.dot_general(a, b, (((1,), (1,)), ((), ())), preferred_element_type=F32)


def _dot_tn(a, b):
    return lax.dot_general(a, b, (((0,), (0,)), ((), ())), preferred_element_type=F32)


def _split3(x):
    hi = x.astype(BF16)
    r1 = x - hi.astype(F32)
    mid = r1.astype(BF16)
    lo = (r1 - mid.astype(F32)).astype(BF16)
    return hi, mid, lo


def _params(*sem):
    return pltpu.CompilerParams(dimension_semantics=sem, vmem_limit_bytes=VMEM_LIMIT)


def _rmsnorm_rows(src_ref, g, dst_ref, rows, chunk=256):
    def body(c, carry):
        r0 = pl.multiple_of(c * chunk, chunk)
        x = src_ref[pl.ds(r0, chunk), :]
        var = jnp.mean(x * x, axis=-1, keepdims=True)
        dst_ref[pl.ds(r0, chunk), :] = (x * lax.rsqrt(var + EPS) * g).astype(dst_ref.dtype)
        return carry

    lax.fori_loop(0, rows // chunk, body, 0)


def _bf16_weights(w_ref, copy_ref, at=(Ellipsis,), first_row=0):
    w = w_ref[at]
    if copy_ref is None:
        return w
    w = w.astype(BF16)

    @pl.when(pl.program_id(0) == first_row)
    def _():
        copy_ref[at] = w

    return w


def _weight_copy(w, block, index_map, last_index, first_row=0):
    if w.dtype == BF16:
        return [], []

    def once(i, j):
        during = index_map(jnp.full_like(i, first_row), j)
        before = index_map(jnp.full_like(i, first_row), jnp.zeros_like(j))
        pick = lambda b, d, a: jnp.where(i < first_row, b, jnp.where(i == first_row, d, a))
        return jax.tree.map(pick, before, during, last_index)

    return [pl.BlockSpec(block, once)], [jax.ShapeDtypeStruct(w.shape, BF16)]


def _lead_in_maps(n, nj):
    nxt = lambda i: jnp.minimum(i, n - 1)
    cur = lambda i: jnp.maximum(i - 1, 0)
    col = lambda i, j: jnp.where(i > 0, j, 0)
    row_slice = lambda i, j: nxt(i) * nj + jnp.where(i < n, j, nj - 1)
    return nxt, cur, col, row_slice


def _by_parity(i, n_tiles, step, buf0, buf1):
    even = i % 2 == 0
    inner = jnp.logical_and(i > 0, i < n_tiles)
    pl.when(i == 0)(lambda: step(buf0, None))
    pl.when(jnp.logical_and(inner, even))(lambda: step(buf0, buf1))
    pl.when(jnp.logical_and(inner, jnp.logical_not(even)))(lambda: step(buf1, buf0))
    pl.when(jnp.logical_and(i == n_tiles, even))(lambda: step(None, buf1))
    pl.when(jnp.logical_and(i == n_tiles, jnp.logical_not(even)))(lambda: step(None, buf0))


def _inproj_kernel(x_ref, g_ref, w_ref, o_ref, *rest, n_tiles, n_cols):
    *w16_ref, h0_ref, h1_ref = rest
    j = pl.program_id(1)
    rows_per_step = h0_ref.shape[0] // n_cols
    rows = pl.ds(pl.multiple_of(j * rows_per_step, rows_per_step), rows_per_step)

    def step(h_norm, h_proj):
        if h_norm is not None:
            x = x_ref[...]
            var = jnp.mean(x * x, axis=-1, keepdims=True)
            h_norm[rows, :] = (x * lax.rsqrt(var + EPS) * g_ref[...]).astype(BF16)
        if h_proj is not None:
            o_ref[...] = _dot(h_proj[...], _bf16_weights(w_ref, *(w16_ref or [None]), first_row=1))

    _by_parity(pl.program_id(0), n_tiles, step, h0_ref, h1_ref)


def _row_tile(m):
    return ROW_TILE if m % ROW_TILE == 0 else ROW_TILE // 2


def _inproj(x, g, w, *, name, tn=1280):
    m, d = x.shape
    n_out = w.shape[1]
    tm = _row_tile(m)
    if w.dtype != BF16:
        tn //= 2
    nj = n_out // tn
    assert m % tm == 0 and n_out % tn == 0 and tm % (nj * SUBLANES) == 0
    n = m // tm
    nxt, cur, col, row_slice = _lead_in_maps(n, nj)
    w_spec = ((d, tn), lambda i, j: (0, col(i, j)))
    copy_specs, copy_shapes = _weight_copy(w, *w_spec, (0, nj - 1), first_row=1)
    return pl.pallas_call(
        functools.partial(_inproj_kernel, n_tiles=n, n_cols=nj),
        grid=(n + 1, nj),
        in_specs=[
            pl.BlockSpec((tm // nj, d), lambda i, j: (row_slice(i, j), 0)),
            pl.BlockSpec((1, d), lambda i, j: (0, 0)),
            pl.BlockSpec(*w_spec),
        ],
        out_specs=[pl.BlockSpec((tm, tn), lambda i, j: (cur(i), col(i, j)))] + copy_specs,
        out_shape=[jax.ShapeDtypeStruct((m, n_out), F32)] + copy_shapes,
        scratch_shapes=[pltpu.VMEM((tm, d), BF16)] * 2,
        compiler_params=_params("arbitrary", "arbitrary"),
        name=name,
    )(x, g.reshape(1, d), w)


def _lower_bound(logits, layer):
    m = jnp.max(logits, axis=0, keepdims=True)
    e = jnp.exp(logits - m)
    return jnp.sum(e[: layer + 1], axis=0, keepdims=True) / jnp.sum(e, axis=0, keepdims=True)


def _row_to_col(e_row):
    n = e_row.shape[1]
    hi, mid, lo = (p.astype(F32) for p in _split3(e_row))
    r = lax.broadcasted_iota(jnp.int32, (2 * SUBLANES, n), 0)
    pieces = jnp.where(r == 0, hi, jnp.where(r == 1, mid, jnp.where(r == 2, lo, 0.0)))
    return _dot_tn(pieces.astype(BF16), jnp.ones((2 * SUBLANES, LANES), BF16))


def _hgrn_gates(qp, fp, lb):
    q = qp * jax.nn.sigmoid(qp)
    f = lb + (1.0 - lb) * jax.nn.sigmoid(fp)
    return q, f, 1.0 - f, jnp.log(f)


def _hgrn_tables(c):
    t = np.arange(c)[:, None]
    j = np.arange(c)[None, :]
    sums, pairs = [j <= t], []
    bs = c
    while bs >= 2:
        hs = bs // 2
        ref = t - t % bs + hs - 1
        upper = t % bs >= hs
        sums.append(np.where(upper, (j > ref) & (j <= t), (j > t) & (j <= ref)))
        pairs.append((t // bs == j // bs) & upper & (j % bs < hs))
        bs = hs
    sums.append(j > t)
    pairs.append(t == j)
    sums = np.concatenate(sums, axis=0).astype(np.float32)
    return jnp.asarray(np.concatenate([sums] * 3, axis=1), BF16), jnp.asarray(np.stack(pairs), F32)


def _upper_q_lower_k(q, k, bs):
    c = q.shape[0]
    hs = bs // 2
    if hs % SUBLANES == 0:
        parts = []
        for r0 in range(0, c, bs):
            parts += [k[r0 : r0 + hs], q[r0 + hs : r0 + bs]]
        return jnp.concatenate(parts, axis=0)
    row = lax.broadcasted_iota(jnp.int32, (c, 1), 0)
    return jnp.where(row % bs >= hs, q, k)


def _hgrn_prep(qp, fp, v, lb, sums, levels):
    c = qp.shape[0]
    q = qp * jax.nn.sigmoid(qp)
    f = lb + (1.0 - lb) * jax.nn.sigmoid(fp)
    k = 1.0 - f
    e = jnp.exp2(_dot(sums, jnp.concatenate(_split3(jnp.log2(f)), axis=0)))
    e_b, e_rest = e[:c], e[(levels + 1) * c :]
    ys = [(_upper_q_lower_k(q, k, c >> l) * e[(l + 1) * c : (l + 2) * c]).astype(BF16) for l in range(levels)]
    return dict(ys=ys, q16=q.astype(BF16), k16=k.astype(BF16), v16=v.astype(BF16),
                qe=(q * e_b).astype(BF16), kd=(k * e_rest).astype(BF16), e_last=e_b[c - 1 : c])


def _hgrn_apply(prep, st_ref, pairs_ref, heads):
    levels = pairs_ref.shape[0] - 1
    pairs = [(n, h) for n in range(len(prep)) for h in range(heads)]
    head = lambda x, h: x[:, h * HEAD_DIM : (h + 1) * HEAD_DIM]
    scores = {}
    for n, h in pairs:
        p = prep[n]
        a = _dot_nt(head(p["q16"], h), head(p["k16"], h)) * pairs_ref[levels]
        for l in range(levels):
            y = head(p["ys"][l], h)
            a = a + _dot_nt(y, y) * pairs_ref[l]
        scores[n, h] = a.astype(BF16)
    outs = {}
    for n, h in pairs:
        p = prep[n]
        st = st_ref[n, h]
        outs[n, h] = _dot_nt(head(p["qe"], h), st.astype(BF16)) + _dot(scores[n, h], head(p["v16"], h))
        st_ref[n, h] = head(p["e_last"], h) * st + _dot_tn(head(p["v16"], h), head(p["kd"], h))
    return [jnp.concatenate([outs[n, h] for h in range(heads)], axis=1) for n in range(len(prep))]


def _hgrn_prompt_kernel(q_ref, f_ref, v_ref, lbl_ref, sums_ref, pairs_ref, o_ref, s_out_ref, st_ref, *,
                        layer, heads, n_chunks):
    t = pl.program_id(1)

    @pl.when(t == 0)
    def _():
        st_ref[...] = jnp.zeros_like(st_ref)

    lb = _lower_bound(lbl_ref[...], layer)
    n_seq = q_ref.shape[0]
    levels = pairs_ref.shape[0] - 1

    def prep(c):
        rows = slice(c * HGRN_CHUNK, (c + 1) * HGRN_CHUNK)
        return [_hgrn_prep(q_ref[n, rows, :], f_ref[n, rows, :], v_ref[n, rows, :], lb, sums_ref[...], levels)
                for n in range(n_seq)]

    nxt = prep(0)
    for c in range(n_chunks):
        cur, nxt = nxt, (prep(c + 1) if c + 1 < n_chunks else None)
        for n, o in enumerate(_hgrn_apply(cur, st_ref, pairs_ref, heads)):
            o_ref[n, c * HGRN_CHUNK : (c + 1) * HGRN_CHUNK, :] = o

    @pl.when(t == pl.num_programs(1) - 1)
    def _():
        for n in range(n_seq):
            for h in range(heads):
                s_out_ref[n, h] = st_ref[n, h].T


def _hgrn_prompt(proj, lb_logits, *, layer, batch, seq, w_a, tt=256, n_seq=2):
    heads = w_a // HEAD_DIM
    assert seq % tt == 0 and tt % HGRN_CHUNK == 0 and batch % n_seq == 0
    sums, pairs = _hgrn_tables(HGRN_CHUNK)
    proj3 = proj.reshape(batch, seq, proj.shape[1])
    col = lambda c: pl.BlockSpec((n_seq, tt, w_a), lambda b, t: (b, t, c))
    const = lambda x: pl.BlockSpec(x.shape, lambda b, t: (0,) * x.ndim)
    o, s = pl.pallas_call(
        functools.partial(_hgrn_prompt_kernel, layer=layer, heads=heads, n_chunks=tt // HGRN_CHUNK),
        grid=(batch // n_seq, seq // tt),
        in_specs=[col(0), col(1), col(2), const(lb_logits), const(sums), const(pairs)],
        out_specs=[
            pl.BlockSpec((n_seq, tt, w_a), lambda b, t: (b, t, 0)),
            pl.BlockSpec((n_seq, heads, HEAD_DIM, HEAD_DIM), lambda b, t: (b, 0, 0, 0)),
        ],
        out_shape=[
            jax.ShapeDtypeStruct((batch, seq, w_a), F32),
            jax.ShapeDtypeStruct((batch, heads, HEAD_DIM, HEAD_DIM), F32),
        ],
        scratch_shapes=[pltpu.VMEM((n_seq, heads, HEAD_DIM, HEAD_DIM), F32)],
        compiler_params=_params("arbitrary", "arbitrary"),
        name="hgrn_prompt",
    )(proj3, proj3, proj3, lb_logits, sums, pairs)
    return o.reshape(batch * seq, w_a), s


def _hgrn_sample_kernel(q_ref, f_ref, v_ref, lbl_ref, s_in_ref, o_ref, s_out_ref, *, layer, heads, seq):
    n_seq = SUBLANES // seq
    lb = _lower_bound(lbl_ref[...], layer)
    row = lax.broadcasted_iota(jnp.int32, (SUBLANES, 1), 0)
    pos = row % seq
    work = []
    for tile, h in ((tile, h) for tile in range(q_ref.shape[0] // SUBLANES) for h in range(heads)):
        cols = slice(h * HEAD_DIM, (h + 1) * HEAD_DIM)
        rows = slice(tile * SUBLANES, (tile + 1) * SUBLANES)
        q, _, k, logf = _hgrn_gates(q_ref[rows, cols], f_ref[rows, cols], lb[:, cols])
        v = v_ref[rows, cols]
        b = logf
        shift = 1
        while shift < seq:
            b = b + jnp.where(pos >= shift, pltpu.roll(b, shift, 0), 0.0)
            shift *= 2
        o = jnp.zeros((SUBLANES, HEAD_DIM), F32)
        for sl in range(SUBLANES):
            lo_t, hi_t = sl, (sl // seq + 1) * seq
            mask = jnp.logical_and(row >= lo_t, row < hi_t)
            z = q * jnp.exp(jnp.where(mask, b - b[sl : sl + 1], NEG_BIG)) * k[sl : sl + 1]
            o = o + jnp.sum(z, axis=-1, keepdims=True) * v[sl : sl + 1]
        per_seq = []
        for n in range(n_seq):
            own = jnp.logical_and(row >= n * seq, row < (n + 1) * seq)
            b_last = b[(n + 1) * seq - 1 : (n + 1) * seq, :]
            qe = jnp.where(own, q * jnp.exp(b), 0.0).astype(BF16)
            kd = jnp.where(own, k * jnp.exp(jnp.where(own, b_last - b, 0.0)), 0.0).astype(BF16)
            per_seq.append((tile * n_seq + n, qe, kd, jnp.exp(b_last)))
        work.append((rows, cols, h, o, v.astype(BF16), per_seq))
    for rows, cols, h, o, _, per_seq in work:
        for i, qe, _, _ in per_seq:
            o = o + _dot(qe, s_in_ref[0, i, h].astype(BF16))
        o_ref[rows, cols] = o
    for _, _, h, _, v16, per_seq in work:
        for i, _, kd, e_last in per_seq:
            s_out_ref[0, i, h] = _row_to_col(e_last) * s_in_ref[0, i, h] + _dot_tn(kd, v16)


def _hgrn_sample(proj, lb_logits, state, *, layer, row0, batch, seq, w_a, tiles=8):
    heads = w_a // HEAD_DIM
    rows = tiles * SUBLANES
    assert SUBLANES % seq == 0 and row0 % rows == 0 and (batch * seq) % rows == 0
    n_seq = rows // seq
    t0 = row0 // rows
    col = lambda c: pl.BlockSpec((rows, w_a), lambda p: (t0 + p, c))
    st = pl.BlockSpec((1, n_seq, heads, HEAD_DIM, HEAD_DIM), lambda p: (layer, p, 0, 0, 0))
    st_out = pl.BlockSpec((1, n_seq, heads, HEAD_DIM, HEAD_DIM), lambda p: (0, p, 0, 0, 0))
    return pl.pallas_call(
        functools.partial(_hgrn_sample_kernel, layer=layer, heads=heads, seq=seq),
        grid=(batch // n_seq,),
        in_specs=[col(0), col(1), col(2), pl.BlockSpec(lb_logits.shape, lambda p: (0, 0)), st],
        out_specs=[pl.BlockSpec((rows, w_a), lambda p: (p, 0)), st_out],
        out_shape=[
            jax.ShapeDtypeStruct((batch * seq, w_a), F32),
            jax.ShapeDtypeStruct((1, batch, heads, HEAD_DIM, HEAD_DIM), F32),
        ],
        compiler_params=_params("arbitrary"),
        name="hgrn_sample",
    )(proj, proj, proj, lb_logits, state)


def _s5_weights_kernel(are_ref, aim_ref, ls_ref, bre_ref, bim_ref, cre_ref, cim_ref, rep_ref, tile_ref, same_ref,
                       lre_ref, lim_ref, bbre_ref, bbim_ref, ccre_ref, ccim_ref):
    a_re = are_ref[...]
    a_im = aim_ref[...]
    dt = jnp.exp(ls_ref[...])
    mag = jnp.exp(a_re * dt)
    lam_re = mag * jnp.cos(a_im * dt)
    lam_im = mag * jnp.sin(a_im * dt)
    den = a_re * a_re + a_im * a_im
    nr, ni = lam_re - 1.0, lam_im
    r_re = (nr * a_re + ni * a_im) / den
    r_im = (ni * a_re - nr * a_im) / den
    lre_ref[...] = lam_re
    lim_ref[...] = lam_im
    b_re = bre_ref[...]
    b_im = bim_ref[...]
    same = same_ref[...]
    bbre_ref[0] = (_dot(rep_ref[...], (r_re * b_re - r_im * b_im).astype(BF16)) * same).astype(BF16)
    bbim_ref[0] = (_dot(rep_ref[...], (r_re * b_im + r_im * b_re).astype(BF16)) * same).astype(BF16)
    ccre_ref[0] = (_dot(cre_ref[...].astype(BF16), tile_ref[...]) * same).astype(BF16)
    ccim_ref[0] = (_dot(cim_ref[...].astype(BF16), tile_ref[...]) * same).astype(BF16)


def _s5_weights(a_re, a_im, log_step, b_re, b_im, c_re, c_im):
    g, p = a_re.shape
    c = b_re.shape[-1]
    gt = MXU_DIM // c
    nj, kc, kp = g // gt, gt * c, gt * p
    flat = lambda x: x.reshape(1, g * p)
    chan = lambda x: jnp.transpose(x, (2, 0, 1)).reshape(c, g * p)
    row_g, lane_g = np.arange(kc)[:, None] // c, np.arange(kp)[None, :] // p
    rep = jnp.asarray(np.arange(kc)[:, None] % c == np.arange(c)[None, :], BF16)
    tile = jnp.asarray(np.arange(p)[:, None] == np.arange(kp)[None, :] % p, BF16)
    same = jnp.asarray(row_g == lane_g, F32)
    lanes = lambda rows: pl.BlockSpec((rows, kp), lambda j: (0, j))
    const = lambda x: pl.BlockSpec(x.shape, lambda j: (0, 0))
    tiles = pl.BlockSpec((1, kc, kp), lambda j: (j, 0, 0))
    tiles_shape = jax.ShapeDtypeStruct((nj, kc, kp), BF16)
    lam_shape = jax.ShapeDtypeStruct((1, g * p), F32)
    lam_re, lam_im, *mats = pl.pallas_call(
        _s5_weights_kernel,
        grid=(nj,),
        in_specs=[lanes(1), lanes(1), lanes(1), lanes(c), lanes(c),
                  pl.BlockSpec((kc, p), lambda j: (j, 0)), pl.BlockSpec((kc, p), lambda j: (j, 0)),
                  const(rep), const(tile), const(same)],
        out_specs=[lanes(1), lanes(1), tiles, tiles, tiles, tiles],
        out_shape=[lam_shape, lam_shape] + [tiles_shape] * 4,
        compiler_params=_params("arbitrary"),
        name="s5_weights",
    )(flat(a_re), flat(a_im), flat(jnp.repeat(log_step, p)), chan(b_re), chan(b_im),
      c_re.reshape(g * c, p), c_im.reshape(g * c, p), rep, tile, same)
    return (lam_re.reshape(-1, 1, LANES), lam_im.reshape(-1, 1, LANES), *mats)


def _s5_lanes(x_ref, lo, hi):
    if len(x_ref.shape) == 2:
        return x_ref[:, lo:hi]
    return jnp.concatenate([x_ref[m] for m in range(lo // LANES, hi // LANES)], axis=1)


def _s5_in(ub, bre_ref, bim_ref, xr_ref, xi_ref):
    nj, kc, kp = bre_ref.shape
    for j in range(nj):
        uj = ub[:, j * kc : (j + 1) * kc]
        for x_ref, w_ref in ((xr_ref, bre_ref), (xi_ref, bim_ref)):
            bu = _dot(uj, w_ref[j])
            if len(x_ref.shape) == 2:
                x_ref[:, j * kp : (j + 1) * kp] = bu
            else:
                for m in range(kp // LANES):
                    x_ref[j * kp // LANES + m] = bu[:, m * LANES : (m + 1) * LANES]


def _s5_out(xr_ref, xi_ref, cre_ref, cim_ref):
    nj, kc, kp = cre_ref.shape
    tiles = []
    for j in range(nj):
        xr = _s5_lanes(xr_ref, j * kp, (j + 1) * kp).astype(BF16)
        xi = _s5_lanes(xi_ref, j * kp, (j + 1) * kp).astype(BF16)
        tiles.append(_dot_nt(xr, cre_ref[j]) - _dot_nt(xi, cim_ref[j]))
    return tiles


def _s5_step(lr, li, xr, xi, bur, bui):
    return lr * xr - li * xi + bur, lr * xi + li * xr + bui


def _s5_lambda(lre_ref, lim_ref, lo, hi, rows):
    cat = lambda ref: jnp.concatenate([ref[m] for m in range(lo // LANES, hi // LANES)], axis=1)
    return jnp.broadcast_to(cat(lre_ref), (rows, hi - lo)), jnp.broadcast_to(cat(lim_ref), (rows, hi - lo))


def _to_time_major(x):
    n, ts, w = x.shape
    return jnp.swapaxes(x, 0, 1).reshape(ts * n, w)


def _from_time_major(x, n):
    rows, w = x.shape
    return jnp.swapaxes(x.reshape(rows // n, n, w), 0, 1)


def _s5_prompt_kernel(*refs, nv, lane_block):
    u_refs = refs[:nv]
    (lre_ref, lim_ref, bre_ref, bim_ref, cre_ref, cim_ref, d_ref,
     y_ref, xr_out_ref, xi_out_ref, xr_ref, xi_ref, sr_ref, si_ref) = refs[nv:]
    ts = u_refs[0].shape[0]
    gp = xr_ref.shape[1]
    step = pl.program_id(0)

    @pl.when(step == 0)
    def _():
        sr_ref[...] = jnp.zeros_like(sr_ref)
        si_ref[...] = jnp.zeros_like(si_ref)

    u = _to_time_major(jnp.stack([r[...] for r in u_refs], axis=0))
    _s5_in(u.astype(BF16), bre_ref, bim_ref, xr_ref, xi_ref)

    for lo in range(0, gp, lane_block):
        ls = slice(lo, lo + lane_block)
        lr, li = _s5_lambda(lre_ref, lim_ref, lo, lo + lane_block, nv)

        xr, xi = sr_ref[:, ls], si_ref[:, ls]
        for r in range(ts):
            rows = slice(r * nv, (r + 1) * nv)
            xr, xi = _s5_step(lr, li, xr, xi, xr_ref[rows, ls], xi_ref[rows, ls])
            xr_ref[rows, ls] = xr
            xi_ref[rows, ls] = xi
        sr_ref[:, ls] = xr
        si_ref[:, ls] = xi

    kc = cre_ref.shape[1]
    d = d_ref[...]
    for j, y in enumerate(_s5_out(xr_ref, xi_ref, cre_ref, cim_ref)):
        cols = slice(j * kc, (j + 1) * kc)
        y_ref[:, :, cols] = _from_time_major(y + d[:, cols] * u[:, cols], nv)

    @pl.when(step == pl.num_programs(0) - 1)
    def _():
        xr_out_ref[...] = sr_ref[...]
        xi_out_ref[...] = si_ref[...]


def _s5_carry_kernel(y_ref, fr_ref, fi_ref, lre_ref, lim_ref, cre_ref, cim_ref,
                     o_ref, xr_out_ref, xi_out_ref, zr_ref, zi_ref, sr_ref, si_ref, *, half_len, lane_block):
    nrow, gp = fr_ref.shape
    tf = y_ref.shape[2]
    step = pl.program_id(0)
    odd = lax.broadcasted_iota(jnp.int32, (nrow, 1), 0) % 2 == 1

    @pl.when(step == 0)
    def _():
        for lo in range(0, gp, lane_block):
            ls = slice(lo, lo + lane_block)
            pr, pi = _s5_lambda(lre_ref, lim_ref, lo, lo + lane_block, nrow)
            n = 1
            while n < half_len:
                pr, pi = pr * pr - pi * pi, 2.0 * pr * pi
                n *= 2
            fr, fi = fr_ref[:, ls], fi_ref[:, ls]
            fr = jnp.where(odd, pltpu.roll(fr, 1, 0), fr)
            fi = jnp.where(odd, pltpu.roll(fi, 1, 0), fi)
            sr_ref[:, ls] = jnp.where(odd, pr * fr - pi * fi, fr)
            si_ref[:, ls] = jnp.where(odd, pr * fi + pi * fr, fi)

    for lo in range(0, gp, lane_block):
        ls = slice(lo, lo + lane_block)
        lr, li = _s5_lambda(lre_ref, lim_ref, lo, lo + lane_block, nrow)

        zr, zi = sr_ref[:, ls], si_ref[:, ls]
        for r in range(tf):
            rows = slice(r * nrow, (r + 1) * nrow)
            zr, zi = lr * zr - li * zi, lr * zi + li * zr
            zr_ref[rows, ls] = zr
            zi_ref[rows, ls] = zi
        sr_ref[:, ls] = zr
        si_ref[:, ls] = zi

    kc = cre_ref.shape[1]
    nseq = y_ref.shape[0]
    for j, y in enumerate(_s5_out(zr_ref, zi_ref, cre_ref, cim_ref)):
        cols = slice(j * kc, (j + 1) * kc)
        o_ref[:, :, :, cols] = y_ref[:, :, :, cols] + _from_time_major(y, nrow).reshape(nseq, nrow // nseq, tf, kc)

    @pl.when(step == pl.num_programs(0) - 1)
    def _():
        xr_out_ref[...] = fr_ref[...] + sr_ref[...]
        xi_out_ref[...] = fi_ref[...] + si_ref[...]


def _s5_weights_specs(ws):
    zero = lambda n: (lambda *_: (0,) * n)
    return [pl.BlockSpec(w.shape, zero(w.ndim)) for w in ws]


def _s5_prompt(proj, ws, *, nb, seq, ucol, w_b, ts=64, tf=64, lane_block=1024):
    lam_re, lam_im, bre, bim, cre, cim, d = ws
    gp = lam_re.shape[0] * LANES
    nv = SUBLANES
    assert nv == 2 * nb, "two pieces per sequence"
    plen, half_len = seq // 2, seq // 4
    assert plen % ts == 0 and half_len % tf == 0 and gp % lane_block == 0
    assert half_len & (half_len - 1) == 0, "repeated squaring needs a power of two"
    nblk = plen // ts
    u_spec = lambda n: pl.BlockSpec((ts, w_b), lambda s: (n * nblk + s, ucol))
    state = pl.BlockSpec((nv, gp), lambda s: (0, 0))
    state_shape = jax.ShapeDtypeStruct((nv, gp), F32)
    scratch = lambda steps: [pltpu.VMEM((nv * steps, gp), F32)] * 2 + [pltpu.VMEM((nv, gp), F32)] * 2
    y, fr, fi = pl.pallas_call(
        functools.partial(_s5_prompt_kernel, nv=nv, lane_block=lane_block),
        grid=(nblk,),
        in_specs=[u_spec(n) for n in range(nv)] + _s5_weights_specs(ws),
        out_specs=[pl.BlockSpec((nv, ts, w_b), lambda s: (0, s, 0)), state, state],
        out_shape=[jax.ShapeDtypeStruct((nv, plen, w_b), F32), state_shape, state_shape],
        scratch_shapes=scratch(ts),
        compiler_params=_params("arbitrary"),
        name="s5_prompt",
    )(*([proj] * nv), *ws)

    y5 = y.reshape(nb, 2, 2, half_len, w_b)
    y_spec = pl.BlockSpec((nb, None, 2, tf, w_b), lambda s: (0, 1, 0, s, 0))
    y5, xr, xi = pl.pallas_call(
        functools.partial(_s5_carry_kernel, half_len=half_len, lane_block=lane_block),
        grid=(half_len // tf,),
        in_specs=[y_spec, state, state] + _s5_weights_specs((lam_re, lam_im, cre, cim)),
        out_specs=[y_spec, state, state],
        out_shape=[jax.ShapeDtypeStruct(y5.shape, F32), state_shape, state_shape],
        scratch_shapes=scratch(tf),
        input_output_aliases={0: 0},
        compiler_params=_params("arbitrary"),
        name="s5_prompt_carry",
    )(y5, fr, fi, lam_re, lam_im, cre, cim)
    last = lambda x: x.reshape(nb, 2, gp)[:, 1]
    return y5.reshape(nb * seq, w_b), last(xr), last(xi)


def _s5_sample_kernel(u_ref, x0r_ref, x0i_ref, lre_ref, lim_ref, bre_ref, bim_ref, cre_ref, cim_ref, d_ref,
                      y_ref, xr_out_ref, xi_out_ref, xr_ref, xi_ref, *, seq):
    nb = x0r_ref.shape[0]
    u = u_ref[...]
    _s5_in(u.astype(BF16), bre_ref, bim_ref, xr_ref, xi_ref)
    for m in range(xr_ref.shape[0]):
        ls = slice(m * LANES, (m + 1) * LANES)
        lr, li = lre_ref[m], lim_ref[m]
        xr, xi = x0r_ref[:, ls], x0i_ref[:, ls]
        for r in range(seq):
            rows = pl.ds(r, nb, stride=seq)
            xr, xi = _s5_step(lr, li, xr, xi, xr_ref[m, rows, :], xi_ref[m, rows, :])
            xr_ref[m, rows, :] = xr
            xi_ref[m, rows, :] = xi
        xr_out_ref[:, ls] = xr
        xi_out_ref[:, ls] = xi
    kc = cre_ref.shape[1]
    d = d_ref[...]
    for j, y in enumerate(_s5_out(xr_ref, xi_ref, cre_ref, cim_ref)):
        cols = slice(j * kc, (j + 1) * kc)
        y_ref[:, cols] = y + d[:, cols] * u[:, cols]


def _s5_sample(proj, x0_re, x0_im, ws, *, row0, ucol, seq, w_b, nb=64):
    batch, gp = x0_re.shape
    n_slabs = gp // LANES
    assert batch % nb == 0 and row0 % (nb * seq) == 0
    t0 = row0 // (nb * seq)
    return pl.pallas_call(
        functools.partial(_s5_sample_kernel, seq=seq),
        grid=(batch // nb,),
        in_specs=[
            pl.BlockSpec((nb * seq, w_b), lambda p: (t0 + p, ucol)),
            pl.BlockSpec((nb, gp), lambda p: (p, 0)),
            pl.BlockSpec((nb, gp), lambda p: (p, 0)),
        ] + _s5_weights_specs(ws),
        out_specs=[
            pl.BlockSpec((nb * seq, w_b), lambda p: (p, 0)),
            pl.BlockSpec((nb, gp), lambda p: (p, 0)),
            pl.BlockSpec((nb, gp), lambda p: (p, 0)),
        ],
        out_shape=[
            jax.ShapeDtypeStruct((batch * seq, w_b), F32),
            jax.ShapeDtypeStruct((batch, gp), F32),
            jax.ShapeDtypeStruct((batch, gp), F32),
        ],
        scratch_shapes=[pltpu.VMEM((n_slabs, nb * seq, LANES), F32), pltpu.VMEM((n_slabs, nb * seq, LANES), F32)],
        compiler_params=_params("arbitrary"),
        name="s5_sample",
    )(proj, x0_re, x0_im, *ws)


def _mix_gates(o_ref, gate_ref, y_ref, ng_ref, gluw_ref, glub_ref, a_ref, src, dst):
    w_a = o_ref.shape[1]
    o = o_ref[src, :]
    var = jnp.mean(o * o, axis=-1, keepdims=True)
    gate = gate_ref[src, :]
    a_ref[dst, :w_a] = (o * lax.rsqrt(var + EPS) * ng_ref[...] * (gate * jax.nn.sigmoid(gate))).astype(BF16)
    y = jax.nn.gelu(y_ref[src, :])
    z = _dot(y.astype(BF16), gluw_ref[...]) + glub_ref[...]
    a_ref[dst, w_a:] = (y * jax.nn.sigmoid(z)).astype(BF16)


def _mix_kernel(o_ref, gate_ref, y_ref, x_ref, ng_ref, gluw_ref, glub_ref, wout_ref, out_ref, *rest,
                n_tiles, n_cols):
    *copies, a0_ref, a1_ref = rest
    glu16_ref, wout16_ref = copies or (None, None)
    i = pl.program_id(0)
    j = pl.program_id(1)
    rows_per_step = a0_ref.shape[0] // n_cols

    if glu16_ref is not None:
        @pl.when(jnp.logical_and(i == 0, j == 0))
        def _():
            _bf16_weights(gluw_ref, glu16_ref)
        gluw_ref = glu16_ref

    def step(a_gate, a_proj):
        pieces = out_ref.shape[1] // MXU_DIM
        sub = rows_per_step // pieces
        for p in range(pieces):
            cols = (slice(None), slice(p * MXU_DIM, (p + 1) * MXU_DIM))
            if a_gate is not None:
                dst = pl.ds(pl.multiple_of(j * rows_per_step + p * sub, sub), sub)
                _mix_gates(o_ref, gate_ref, y_ref, ng_ref, gluw_ref, glub_ref, a_gate,
                           slice(p * sub, (p + 1) * sub), dst)
            if a_proj is not None:
                w = _bf16_weights(wout_ref, wout16_ref, cols, first_row=1)
                out_ref[cols] = x_ref[cols] + _dot(a_proj[...], w)

    _by_parity(i, n_tiles, step, a0_ref, a1_ref)


def _mix(o, proj, y, x, norm_g, glu_w, glu_b, w_out, *, gate_col, name, tn=512):
    m, w_a = o.shape
    w_b = y.shape[1]
    d = w_out.shape[1]
    tm = _row_tile(m)
    nj = d // tn
    assert m % tm == 0 and d % tn == 0 and tm % (nj * SUBLANES) == 0
    n = m // tm
    nxt, cur, col, row_slice = _lead_in_maps(n, nj)
    const = lambda i, j: (0, 0)
    cur_tile = lambda i, j: (cur(i), col(i, j))
    glu_spec = ((w_b, w_b), const)
    wout_spec = ((w_a + w_b, tn), lambda i, j: (0, col(i, j)))
    assert glu_w.dtype == w_out.dtype
    copy_specs, copy_shapes = (a + b for a, b in zip(_weight_copy(glu_w, *glu_spec, (0, 0)),
                                                     _weight_copy(w_out, *wout_spec, (0, nj - 1), first_row=1)))
    return pl.pallas_call(
        functools.partial(_mix_kernel, n_tiles=n, n_cols=nj),
        grid=(n + 1, nj),
        in_specs=[
            pl.BlockSpec((tm // nj, w_a), lambda i, j: (row_slice(i, j), 0)),
            pl.BlockSpec((tm // nj, w_a), lambda i, j: (row_slice(i, j), gate_col)),
            pl.BlockSpec((tm // nj, w_b), lambda i, j: (row_slice(i, j), 0)),
            pl.BlockSpec((tm, tn), cur_tile),
            pl.BlockSpec((1, w_a), const),
            pl.BlockSpec(*glu_spec),
            pl.BlockSpec((1, w_b), const),
            pl.BlockSpec(*wout_spec),
        ],
        out_specs=[pl.BlockSpec((tm, tn), cur_tile)] + copy_specs,
        out_shape=[jax.ShapeDtypeStruct((m, d), F32)] + copy_shapes,
        scratch_shapes=[pltpu.VMEM((tm, w_a + w_b), BF16)] * 2,
        compiler_params=_params("arbitrary", "arbitrary"),
        name=name,
    )(o, proj, y, x, norm_g.reshape(1, w_a), glu_w, glu_b.reshape(1, w_b), w_out)


def _mlp_kernel(x_ref, g_ref, up_ref, down_ref, gf_ref, o_ref, *rest, rows, final_norm):
    *copies, h_ref = rest
    up16_ref, down16_ref = copies or (None, None)
    f = pl.program_id(1)

    @pl.when(f == 0)
    def _():
        _rmsnorm_rows(x_ref, g_ref[...], h_ref, rows)
        o_ref[...] = x_ref[...]

    a = jnp.square(jnp.maximum(_dot(h_ref[...], _bf16_weights(up_ref, up16_ref)), 0.0)).astype(BF16)
    o_ref[...] += _dot(a, _bf16_weights(down_ref, down16_ref))

    if final_norm:
        @pl.when(f == pl.num_programs(1) - 1)
        def _():
            _rmsnorm_rows(o_ref, gf_ref[...], o_ref, rows)


def _mlp(x, g, up, down, gf, *, final_norm, name, tf=512):
    m, d = x.shape
    dff = up.shape[1]
    tm = _row_tile(m)
    assert m % tm == 0 and dff % tf == 0 and up.dtype == down.dtype
    up_spec = ((d, tf), lambda i, f: (0, f))
    down_spec = ((tf, d), lambda i, f: (f, 0))
    nf = dff // tf
    copy_specs, copy_shapes = (a + b for a, b in zip(_weight_copy(up, *up_spec, (0, nf - 1)),
                                                     _weight_copy(down, *down_spec, (nf - 1, 0))))
    return pl.pallas_call(
        functools.partial(_mlp_kernel, rows=tm, final_norm=final_norm),
        grid=(m // tm, dff // tf),
        in_specs=[
            pl.BlockSpec((tm, d), lambda i, f: (i, 0)),
            pl.BlockSpec((1, d), lambda i, f: (0, 0)),
            pl.BlockSpec(*up_spec),
            pl.BlockSpec(*down_spec),
            pl.BlockSpec((1, d), lambda i, f: (0, 0)),
        ],
        out_specs=[pl.BlockSpec((tm, d), lambda i, f: (i, 0))] + copy_specs,
        out_shape=[jax.ShapeDtypeStruct((m, d), F32)] + copy_shapes,
        scratch_shapes=[pltpu.VMEM((tm, d), BF16)],
        compiler_params=_params("arbitrary", "arbitrary"),
        name=name,
    )(x, g.reshape(1, d), up, down, gf.reshape(1, d))


def kernel(x_prompt, x_sample, state_hgrn, state_s5_re, state_s5_im, w_in, w_out, norm1_g, norm2_g, hgrn_lb_logits, hgrn_norm_g, s5_a_re, s5_a_im, s5_b_re, s5_b_im, s5_c_re, s5_c_im, s5_d, s5_log_step, glu_w, glu_b, mlp_up, mlp_down, final_norm_g):
    bp, seq, d = x_prompt.shape
    bs, dseq, _ = x_sample.shape
    depth = w_in.shape[0]
    w_a = hgrn_norm_g.shape[1]
    w_b = s5_d.shape[1]
    g_b, p_state = s5_a_re.shape[1], s5_a_re.shape[2]
    mp, ms = bp * seq, bs * dseq
    assert w_in.shape[2] == 4 * w_a + w_b and w_a == w_b and g_b * S5_GROUP == w_b
    gate_col, ucol = 3, (4 * w_a) // w_b

    xp = x_prompt.reshape(mp, d)
    xs = x_sample.reshape(ms, d)
    lb_logits = hgrn_lb_logits.astype(F32)
    new_states = []
    for l in range(depth):
        proj_s, w_in16 = _inproj(xs, norm1_g[l], w_in[l].astype(F32), name="inproj_sample")
        proj_p, = _inproj(xp, norm1_g[l], w_in16, name="inproj_prompt")

        o_p, sh_p = _hgrn_prompt(proj_p, lb_logits, layer=l, batch=bp, seq=seq, w_a=w_a)
        o_s, sh_s = _hgrn_sample(proj_s, lb_logits, state_hgrn, layer=l, row0=0, batch=bs, seq=dseq, w_a=w_a)

        ws = (*_s5_weights(*(w[l].astype(F32) for w in (s5_a_re, s5_a_im, s5_log_step, s5_b_re, s5_b_im,
                                                        s5_c_re, s5_c_im))),
              s5_d[l].astype(F32).reshape(1, w_b))
        y_p, sr_p, si_p = _s5_prompt(proj_p, ws, nb=bp, seq=seq, ucol=ucol, w_b=w_b)
        y_s, sr_s, si_s = _s5_sample(
            proj_s, state_s5_re[l].reshape(bs, g_b * p_state).astype(F32), state_s5_im[l].reshape(bs, g_b * p_state).astype(F32),
            ws, row0=0, ucol=ucol, seq=dseq, w_b=w_b)

        last = l == depth - 1
        x1_s, glu16, w_out16 = _mix(o_s, proj_s, y_s, xs, hgrn_norm_g[l], glu_w[l].astype(F32), glu_b[l],
                                    w_out[l].astype(F32), gate_col=gate_col, name="mix_sample")
        xs, up16, down16 = _mlp(x1_s, norm2_g[l], mlp_up[l].astype(F32), mlp_down[l].astype(F32), final_norm_g,
                                final_norm=last, name="mlp_sample")
        x1_p, = _mix(o_p, proj_p, y_p, xp, hgrn_norm_g[l], glu16, glu_b[l], w_out16, gate_col=gate_col,
                     name="mix_prompt")
        xp, = _mlp(x1_p, norm2_g[l], up16, down16, final_norm_g, final_norm=last, name="mlp_prompt")
        new_states.append((sh_p, sr_p.reshape(bp, g_b, p_state), si_p.reshape(bp, g_b, p_state),
                           sh_s[0], sr_s.reshape(bs, g_b, p_state), si_s.reshape(bs, g_b, p_state)))

    y_prompt = xp.reshape(bp, seq, d).astype(x_prompt.dtype)
    y_sample = xs.reshape(bs, dseq, d).astype(x_sample.dtype)
    stacked = [jnp.stack([st[i] for st in new_states]) for i in range(6)]
    return (y_prompt, y_sample, *stacked)
```

```python
import functools

import jax
import jax.numpy as jnp
import numpy as np
from jax import lax
from jax.experimental import pallas as pl
from jax.experimental.pallas import tpu as pltpu

F32 = jnp.float32
BF16 = jnp.bfloat16
EPS = 1e-6

LANES = 128
SUBLANES = 8
MXU_DIM = 256
VMEM_LIMIT = 56 * 1024 * 1024

HEAD_DIM = 128
S5_GROUP = 16
HGRN_CHUNK = 64
ROW_TILE = 1024
NEG_BIG = -1e30


def _dot(a, b):
    return jnp.dot(a, b, preferred_element_type=F32)


def _dot_nt(a, b):
    return lax.dot_general(a, b, (((1,), (1,)), ((), ())), preferred_element_type=F32)


def _dot_tn(a, b):
    return lax.dot_general(a, b, (((0,), (0,)), ((), ())), preferred_element_type=F32)


def _split3(x):
    hi = x.astype(BF16)
    r1 = x - hi.astype(F32)
    mid = r1.astype(BF16)
    lo = (r1 - mid.astype(F32)).astype(BF16)
    return hi, mid, lo


def _params(*sem):
    return pltpu.CompilerParams(dimension_semantics=sem, vmem_limit_bytes=VMEM_LIMIT)


def _rmsnorm_rows(src_ref, g, dst_ref, rows, chunk=256):
    def body(c, carry):
        r0 = pl.multiple_of(c * chunk, chunk)
        x = src_ref[pl.ds(r0, chunk), :]
        var = jnp.mean(x * x, axis=-1, keepdims=True)
        dst_ref[pl.ds(r0, chunk), :] = (x * lax.rsqrt(var + EPS) * g).astype(dst_ref.dtype)
        return carry

    lax.fori_loop(0, rows // chunk, body, 0)


def _bf16_weights(w_ref, copy_ref, at=(Ellipsis,), first_row=0, tile=None):
    if len(w_ref.shape) == 3:
        return w_ref.at[tile][at]
    w = w_ref[at]
    if copy_ref is None:
        return w
    w = w.astype(BF16)

    @pl.when(pl.program_id(0) == first_row)
    def _():
        if len(copy_ref.shape) == 3:
            copy_ref.at[0][at] = w
        else:
            copy_ref[at] = w

    return w


def _weight_copy(w, block, index_map, last_index, first_row=0):
    if w.dtype == BF16:
        return [], []

    def once(i, j):
        during = index_map(jnp.full_like(i, first_row), j)
        before = index_map(jnp.full_like(i, first_row), jnp.zeros_like(j))
        pick = lambda b, d, a: jnp.where(i < first_row, b, jnp.where(i == first_row, d, a))
        return jax.tree.map(pick, before, during, last_index)

    return [pl.BlockSpec(block, once)], [jax.ShapeDtypeStruct(w.shape, BF16)]


def _column_tiles(w, tn, widen=1):
    if w.ndim == 3:
        assert w.dtype == BF16 and w.shape[2] == tn
        return pl.BlockSpec(w.shape, lambda i, j: (0, 0, 0), pipeline_mode=pl.Buffered(1)), ([], [])
    rows, n = w.shape
    nj, wide = n // tn, widen * tn
    col = lambda i, j: jnp.where(i < 1, 0, jnp.where(i == 1, j, nj - 1))
    copy_spec = pl.BlockSpec((1, rows, tn), lambda i, j: (col(i, j) // widen, 0, col(i, j) % widen))
    in_spec = pl.BlockSpec((rows, tn), lambda i, j: (0, jnp.where(i > 0, j, 0)))
    return in_spec, ([copy_spec], [jax.ShapeDtypeStruct((n // wide, rows, wide), BF16)])


def _lead_in_maps(n, nj):
    nxt = lambda i: jnp.minimum(i, n - 1)
    cur = lambda i: jnp.maximum(i - 1, 0)
    col = lambda i, j: jnp.where(i > 0, j, 0)
    row_slice = lambda i, j: nxt(i) * nj + jnp.where(i < n, j, nj - 1)
    return nxt, cur, col, row_slice


def _by_parity(i, n_tiles, step, buf0, buf1):
    even = i % 2 == 0
    inner = jnp.logical_and(i > 0, i < n_tiles)
    pl.when(i == 0)(lambda: step(buf0, None))
    pl.when(jnp.logical_and(inner, even))(lambda: step(buf0, buf1))
    pl.when(jnp.logical_and(inner, jnp.logical_not(even)))(lambda: step(buf1, buf0))
    pl.when(jnp.logical_and(i == n_tiles, even))(lambda: step(None, buf1))
    pl.when(jnp.logical_and(i == n_tiles, jnp.logical_not(even)))(lambda: step(None, buf0))


def _inproj_kernel(x_ref, g_ref, w_ref, o_ref, *rest, n_tiles, n_cols):
    *w16_ref, h0_ref, h1_ref = rest
    j = pl.program_id(1)
    rows_per_step = h0_ref.shape[0] // n_cols
    rows = pl.ds(pl.multiple_of(j * rows_per_step, rows_per_step), rows_per_step)

    def step(h_norm, h_proj):
        if h_norm is not None:
            x = x_ref[...]
            var = jnp.mean(x * x, axis=-1, keepdims=True)
            h_norm[rows, :] = (x * lax.rsqrt(var + EPS) * g_ref[...]).astype(BF16)
        if h_proj is not None:
            o_ref[...] = _dot(h_proj[...], _bf16_weights(w_ref, *(w16_ref or [None]), first_row=1, tile=j))

    _by_parity(pl.program_id(0), n_tiles, step, h0_ref, h1_ref)


def _row_tile(m):
    return ROW_TILE if m % ROW_TILE == 0 else ROW_TILE // 2


def _inproj(x, g, w, *, name, tn=1280):
    m, d = x.shape
    tm = _row_tile(m)
    if w.ndim == 3:
        n_out = w.shape[0] * w.shape[2]
    else:
        n_out = w.shape[1]
        tn //= 2
    nj = n_out // tn
    assert m % tm == 0 and n_out % tn == 0 and tm % (nj * SUBLANES) == 0
    n = m // tm
    nxt, cur, col, row_slice = _lead_in_maps(n, nj)
    w_in_spec, (copy_specs, copy_shapes) = _column_tiles(w, tn, widen=2)
    return pl.pallas_call(
        functools.partial(_inproj_kernel, n_tiles=n, n_cols=nj),
        grid=(n + 1, nj),
        in_specs=[
            pl.BlockSpec((tm // nj, d), lambda i, j: (row_slice(i, j), 0)),
            pl.BlockSpec((1, d), lambda i, j: (0, 0)),
            w_in_spec,
        ],
        out_specs=[pl.BlockSpec((tm, tn), lambda i, j: (cur(i), col(i, j)))] + copy_specs,
        out_shape=[jax.ShapeDtypeStruct((m, n_out), F32)] + copy_shapes,
        scratch_shapes=[pltpu.VMEM((tm, d), BF16)] * 2,
        compiler_params=_params("arbitrary", "arbitrary"),
        name=name,
    )(x, g.reshape(1, d), w)


def _lower_bound(logits, layer):
    m = jnp.max(logits, axis=0, keepdims=True)
    e = jnp.exp(logits - m)
    return jnp.sum(e[: layer + 1], axis=0, keepdims=True) / jnp.sum(e, axis=0, keepdims=True)


def _row_to_col(e_row):
    n = e_row.shape[1]
    hi, mid, lo = (p.astype(F32) for p in _split3(e_row))
    r = lax.broadcasted_iota(jnp.int32, (2 * SUBLANES, n), 0)
    pieces = jnp.where(r == 0, hi, jnp.where(r == 1, mid, jnp.where(r == 2, lo, 0.0)))
    return _dot_tn(pieces.astype(BF16), jnp.ones((2 * SUBLANES, LANES), BF16))


def _hgrn_gates(qp, fp, lb):
    q = qp * jax.nn.sigmoid(qp)
    f = lb + (1.0 - lb) * jax.nn.sigmoid(fp)
    return q, f, 1.0 - f, jnp.log(f)


def _hgrn_tables(c):
    t = np.arange(c)[:, None]
    j = np.arange(c)[None, :]
    sums, pairs = [j <= t], []
    bs = c
    while bs >= 2:
        hs = bs // 2
        ref = t - t % bs + hs - 1
        upper = t % bs >= hs
        sums.append(np.where(upper, (j > ref) & (j <= t), (j > t) & (j <= ref)))
        pairs.append((t // bs == j // bs) & upper & (j % bs < hs))
        bs = hs
    sums.append(j > t)
    pairs.append(t == j)
    sums = np.concatenate(sums, axis=0).astype(np.float32)
    return jnp.asarray(np.concatenate([sums] * 3, axis=1), BF16), jnp.asarray(np.stack(pairs), F32)


def _upper_q_lower_k(q, k, bs):
    c = q.shape[0]
    hs = bs // 2
    if hs % SUBLANES == 0:
        parts = []
        for r0 in range(0, c, bs):
            parts += [k[r0 : r0 + hs], q[r0 + hs : r0 + bs]]
        return jnp.concatenate(parts, axis=0)
    row = lax.broadcasted_iota(jnp.int32, (c, 1), 0)
    return jnp.where(row % bs >= hs, q, k)


def _hgrn_prep(qp, fp, v, lb, sums, levels):
    c = qp.shape[0]
    q = qp * jax.nn.sigmoid(qp)
    f = lb + (1.0 - lb) * jax.nn.sigmoid(fp)
    k = 1.0 - f
    e = jnp.exp2(_dot(sums, jnp.concatenate(_split3(jnp.log2(f)), axis=0)))
    e_b, e_rest = e[:c], e[(levels + 1) * c :]
    ys = [(_upper_q_lower_k(q, k, c >> l) * e[(l + 1) * c : (l + 2) * c]).astype(BF16) for l in range(levels)]
    return dict(ys=ys, q16=q.astype(BF16), k16=k.astype(BF16), v16=v.astype(BF16),
                qe=(q * e_b).astype(BF16), kd=(k * e_rest).astype(BF16), e_last=e_b[c - 1 : c])


def _hgrn_apply(prep, st_ref, pairs_ref, heads):
    levels = pairs_ref.shape[0] - 1
    pairs = [(n, h) for n in range(len(prep)) for h in range(heads)]
    head = lambda x, h: x[:, h * HEAD_DIM : (h + 1) * HEAD_DIM]
    scores = {}
    for n, h in pairs:
        p = prep[n]
        a = _dot_nt(head(p["q16"], h), head(p["k16"], h)) * pairs_ref[levels]
        for l in range(levels):
            y = head(p["ys"][l], h)
            a = a + _dot_nt(y, y) * pairs_ref[l]
        scores[n, h] = a.astype(BF16)
    outs = {}
    for n, h in pairs:
        p = prep[n]
        st = st_ref[n, h]
        outs[n, h] = _dot_nt(head(p["qe"], h), st.astype(BF16)) + _dot(scores[n, h], head(p["v16"], h))
        st_ref[n, h] = head(p["e_last"], h) * st + _dot_tn(head(p["v16"], h), head(p["kd"], h))
    return [jnp.concatenate([outs[n, h] for h in range(heads)], axis=1) for n in range(len(prep))]


def _hgrn_prompt_kernel(q_ref, f_ref, v_ref, lbl_ref, sums_ref, pairs_ref, o_ref, s_out_ref, st_ref, *,
                        layer, heads, n_chunks):
    t = pl.program_id(1)

    @pl.when(t == 0)
    def _():
        st_ref[...] = jnp.zeros_like(st_ref)

    lb = _lower_bound(lbl_ref[...], layer)
    n_seq = q_ref.shape[0]
    levels = pairs_ref.shape[0] - 1

    def prep(c):
        rows = slice(c * HGRN_CHUNK, (c + 1) * HGRN_CHUNK)
        return [_hgrn_prep(q_ref[n, rows, :], f_ref[n, rows, :], v_ref[n, rows, :], lb, sums_ref[...], levels)
                for n in range(n_seq)]

    nxt = prep(0)
    for c in range(n_chunks):
        cur, nxt = nxt, (prep(c + 1) if c + 1 < n_chunks else None)
        for n, o in enumerate(_hgrn_apply(cur, st_ref, pairs_ref, heads)):
            o_ref[n, c * HGRN_CHUNK : (c + 1) * HGRN_CHUNK, :] = o

    @pl.when(t == pl.num_programs(1) - 1)
    def _():
        for n in range(n_seq):
            for h in range(heads):
                s_out_ref[n, h] = st_ref[n, h].T


def _hgrn_prompt(proj, lb_logits, *, layer, batch, seq, w_a, tt=256, n_seq=2):
    heads = w_a // HEAD_DIM
    assert seq % tt == 0 and tt % HGRN_CHUNK == 0 and batch % n_seq == 0
    sums, pairs = _hgrn_tables(HGRN_CHUNK)
    proj3 = proj.reshape(batch, seq, proj.shape[1])
    col = lambda c: pl.BlockSpec((n_seq, tt, w_a), lambda b, t: (b, t, c))
    const = lambda x: pl.BlockSpec(x.shape, lambda b, t: (0,) * x.ndim)
    o, s = pl.pallas_call(
        functools.partial(_hgrn_prompt_kernel, layer=layer, heads=heads, n_chunks=tt // HGRN_CHUNK),
        grid=(batch // n_seq, seq // tt),
        in_specs=[col(0), col(1), col(2), const(lb_logits), const(sums), const(pairs)],
        out_specs=[
            pl.BlockSpec((n_seq, tt, w_a), lambda b, t: (b, t, 0)),
            pl.BlockSpec((n_seq, heads, HEAD_DIM, HEAD_DIM), lambda b, t: (b, 0, 0, 0)),
        ],
        out_shape=[
            jax.ShapeDtypeStruct((batch, seq, w_a), F32),
            jax.ShapeDtypeStruct((batch, heads, HEAD_DIM, HEAD_DIM), F32),
        ],
        scratch_shapes=[pltpu.VMEM((n_seq, heads, HEAD_DIM, HEAD_DIM), F32)],
        compiler_params=_params("arbitrary", "arbitrary"),
        name="hgrn_prompt",
    )(proj3, proj3, proj3, lb_logits, sums, pairs)
    return o.reshape(batch * seq, w_a), s


def _hgrn_sample_kernel(q_ref, f_ref, v_ref, lbl_ref, s_in_ref, o_ref, s_out_ref, *, layer, heads, seq):
    n_seq = SUBLANES // seq
    lb = _lower_bound(lbl_ref[...], layer)
    row = lax.broadcasted_iota(jnp.int32, (SUBLANES, 1), 0)
    pos = row % seq
    work = []
    for tile, h in ((tile, h) for tile in range(q_ref.shape[0] // SUBLANES) for h in range(heads)):
        cols = slice(h * HEAD_DIM, (h + 1) * HEAD_DIM)
        rows = slice(tile * SUBLANES, (tile + 1) * SUBLANES)
        q, _, k, logf = _hgrn_gates(q_ref[rows, cols], f_ref[rows, cols], lb[:, cols])
        v = v_ref[rows, cols]
        b = logf
        shift = 1
        while shift < seq:
            b = b + jnp.where(pos >= shift, pltpu.roll(b, shift, 0), 0.0)
            shift *= 2
        o = jnp.zeros((SUBLANES, HEAD_DIM), F32)
        for sl in range(SUBLANES):
            lo_t, hi_t = sl, (sl // seq + 1) * seq
            mask = jnp.logical_and(row >= lo_t, row < hi_t)
            z = q * jnp.exp(jnp.where(mask, b - b[sl : sl + 1], NEG_BIG)) * k[sl : sl + 1]
            o = o + jnp.sum(z, axis=-1, keepdims=True) * v[sl : sl + 1]
        per_seq = []
        for n in range(n_seq):
            own = jnp.logical_and(row >= n * seq, row < (n + 1) * seq)
            b_last = b[(n + 1) * seq - 1 : (n + 1) * seq, :]
            qe = jnp.where(own, q * jnp.exp(b), 0.0).astype(BF16)
            kd = jnp.where(own, k * jnp.exp(jnp.where(own, b_last - b, 0.0)), 0.0).astype(BF16)
            per_seq.append((tile * n_seq + n, qe, kd, jnp.exp(b_last)))
        work.append((rows, cols, h, o, v.astype(BF16), per_seq))
    for rows, cols, h, o, _, per_seq in work:
        for i, qe, _, _ in per_seq:
            o = o + _dot(qe, s_in_ref[0, i, h].astype(BF16))
        o_ref[rows, cols] = o
    for _, _, h, _, v16, per_seq in work:
        for i, _, kd, e_last in per_seq:
            s_out_ref[0, i, h] = _row_to_col(e_last) * s_in_ref[0, i, h] + _dot_tn(kd, v16)


def _hgrn_sample(proj, lb_logits, state, *, layer, row0, batch, seq, w_a, tiles=8):
    heads = w_a // HEAD_DIM
    rows = tiles * SUBLANES
    assert SUBLANES % seq == 0 and row0 % rows == 0 and (batch * seq) % rows == 0
    n_seq = rows // seq
    t0 = row0 // rows
    col = lambda c: pl.BlockSpec((rows, w_a), lambda p: (t0 + p, c))
    st = pl.BlockSpec((1, n_seq, heads, HEAD_DIM, HEAD_DIM), lambda p: (layer, p, 0, 0, 0))
    st_out = pl.BlockSpec((1, n_seq, heads, HEAD_DIM, HEAD_DIM), lambda p: (0, p, 0, 0, 0))
    return pl.pallas_call(
        functools.partial(_hgrn_sample_kernel, layer=layer, heads=heads, seq=seq),
        grid=(batch // n_seq,),
        in_specs=[col(0), col(1), col(2), pl.BlockSpec(lb_logits.shape, lambda p: (0, 0)), st],
        out_specs=[pl.BlockSpec((rows, w_a), lambda p: (p, 0)), st_out],
        out_shape=[
            jax.ShapeDtypeStruct((batch * seq, w_a), F32),
            jax.ShapeDtypeStruct((1, batch, heads, HEAD_DIM, HEAD_DIM), F32),
        ],
        compiler_params=_params("arbitrary"),
        name="hgrn_sample",
    )(proj, proj, proj, lb_logits, state)


def _s5_weights_kernel(are_ref, aim_ref, ls_ref, bre_ref, bim_ref, cre_ref, cim_ref, rep_ref, tile_ref, same_ref,
                       lre_ref, lim_ref, bbre_ref, bbim_ref, ccre_ref, ccim_ref):
    a_re = are_ref[...]
    a_im = aim_ref[...]
    dt = jnp.exp(ls_ref[...])
    mag = jnp.exp(a_re * dt)
    lam_re = mag * jnp.cos(a_im * dt)
    lam_im = mag * jnp.sin(a_im * dt)
    den = a_re * a_re + a_im * a_im
    nr, ni = lam_re - 1.0, lam_im
    r_re = (nr * a_re + ni * a_im) / den
    r_im = (ni * a_re - nr * a_im) / den
    lre_ref[...] = lam_re
    lim_ref[...] = lam_im
    b_re = bre_ref[...]
    b_im = bim_ref[...]
    same = same_ref[...]
    bbre_ref[0] = (_dot(rep_ref[...], (r_re * b_re - r_im * b_im).astype(BF16)) * same).astype(BF16)
    bbim_ref[0] = (_dot(rep_ref[...], (r_re * b_im + r_im * b_re).astype(BF16)) * same).astype(BF16)
    ccre_ref[0] = (_dot(cre_ref[...].astype(BF16), tile_ref[...]) * same).astype(BF16)
    ccim_ref[0] = (_dot(cim_ref[...].astype(BF16), tile_ref[...]) * same).astype(BF16)


def _s5_weights(a_re, a_im, log_step, b_re, b_im, c_re, c_im):
    g, p = a_re.shape
    c = b_re.shape[-1]
    gt = MXU_DIM // c
    nj, kc, kp = g // gt, gt * c, gt * p
    flat = lambda x: x.reshape(1, g * p)
    chan = lambda x: jnp.transpose(x, (2, 0, 1)).reshape(c, g * p)
    row_g, lane_g = np.arange(kc)[:, None] // c, np.arange(kp)[None, :] // p
    rep = jnp.asarray(np.arange(kc)[:, None] % c == np.arange(c)[None, :], BF16)
    tile = jnp.asarray(np.arange(p)[:, None] == np.arange(kp)[None, :] % p, BF16)
    same = jnp.asarray(row_g == lane_g, F32)
    lanes = lambda rows: pl.BlockSpec((rows, kp), lambda j: (0, j))
    const = lambda x: pl.BlockSpec(x.shape, lambda j: (0, 0))
    tiles = pl.BlockSpec((1, kc, kp), lambda j: (j, 0, 0))
    tiles_shape = jax.ShapeDtypeStruct((nj, kc, kp), BF16)
    lam_shape = jax.ShapeDtypeStruct((1, g * p), F32)
    lam_re, lam_im, *mats = pl.pallas_call(
        _s5_weights_kernel,
        grid=(nj,),
        in_specs=[lanes(1), lanes(1), lanes(1), lanes(c), lanes(c),
                  pl.BlockSpec((kc, p), lambda j: (j, 0)), pl.BlockSpec((kc, p), lambda j: (j, 0)),
                  const(rep), const(tile), const(same)],
        out_specs=[lanes(1), lanes(1), tiles, tiles, tiles, tiles],
        out_shape=[lam_shape, lam_shape] + [tiles_shape] * 4,
        compiler_params=_params("arbitrary"),
        name="s5_weights",
    )(flat(a_re), flat(a_im), flat(jnp.repeat(log_step, p)), chan(b_re), chan(b_im),
      c_re.reshape(g * c, p), c_im.reshape(g * c, p), rep, tile, same)
    return (lam_re.reshape(-1, 1, LANES), lam_im.reshape(-1, 1, LANES), *mats)


def _s5_lanes(x_ref, lo, hi):
    if len(x_ref.shape) == 2:
        return x_ref[:, lo:hi]
    return jnp.concatenate([x_ref[m] for m in range(lo // LANES, hi // LANES)], axis=1)


def _s5_in(ub, bre_ref, bim_ref, xr_ref, xi_ref):
    nj, kc, kp = bre_ref.shape
    for j in range(nj):
        uj = ub[:, j * kc : (j + 1) * kc]
        for x_ref, w_ref in ((xr_ref, bre_ref), (xi_ref, bim_ref)):
            bu = _dot(uj, w_ref[j])
            if len(x_ref.shape) == 2:
                x_ref[:, j * kp : (j + 1) * kp] = bu
            else:
                for m in range(kp // LANES):
                    x_ref[j * kp // LANES + m] = bu[:, m * LANES : (m + 1) * LANES]


def _s5_out(xr_ref, xi_ref, cre_ref, cim_ref):
    nj, kc, kp = cre_ref.shape
    tiles = []
    for j in range(nj):
        xr = _s5_lanes(xr_ref, j * kp, (j + 1) * kp).astype(BF16)
        xi = _s5_lanes(xi_ref, j * kp, (j + 1) * kp).astype(BF16)
        tiles.append(_dot_nt(xr, cre_ref[j]) - _dot_nt(xi, cim_ref[j]))
    return tiles


def _s5_step(lr, li, xr, xi, bur, bui):
    return lr * xr - li * xi + bur, lr * xi + li * xr + bui


def _s5_lambda(lre_ref, lim_ref, lo, hi, rows):
    cat = lambda ref: jnp.concatenate([ref[m] for m in range(lo // LANES, hi // LANES)], axis=1)
    return jnp.broadcast_to(cat(lre_ref), (rows, hi - lo)), jnp.broadcast_to(cat(lim_ref), (rows, hi - lo))


def _to_time_major(x):
    n, ts, w = x.shape
    return jnp.swapaxes(x, 0, 1).reshape(ts * n, w)


def _from_time_major(x, n):
    rows, w = x.shape
    return jnp.swapaxes(x.reshape(rows // n, n, w), 0, 1)


def _s5_prompt_kernel(*refs, nv, lane_block):
    u_refs = refs[:nv]
    (lre_ref, lim_ref, bre_ref, bim_ref, cre_ref, cim_ref, d_ref,
     y_ref, xr_out_ref, xi_out_ref, xr_ref, xi_ref, sr_ref, si_ref) = refs[nv:]
    ts = u_refs[0].shape[0]
    gp = xr_ref.shape[1]
    step = pl.program_id(0)

    @pl.when(step == 0)
    def _():
        sr_ref[...] = jnp.zeros_like(sr_ref)
        si_ref[...] = jnp.zeros_like(si_ref)

    u = _to_time_major(jnp.stack([r[...] for r in u_refs], axis=0))
    _s5_in(u.astype(BF16), bre_ref, bim_ref, xr_ref, xi_ref)

    for lo in range(0, gp, lane_block):
        ls = slice(lo, lo + lane_block)
        lr, li = _s5_lambda(lre_ref, lim_ref, lo, lo + lane_block, nv)

        xr, xi = sr_ref[:, ls], si_ref[:, ls]
        for r in range(ts):
            rows = slice(r * nv, (r + 1) * nv)
            xr, xi = _s5_step(lr, li, xr, xi, xr_ref[rows, ls], xi_ref[rows, ls])
            xr_ref[rows, ls] = xr
            xi_ref[rows, ls] = xi
        sr_ref[:, ls] = xr
        si_ref[:, ls] = xi

    kc = cre_ref.shape[1]
    d = d_ref[...]
    for j, y in enumerate(_s5_out(xr_ref, xi_ref, cre_ref, cim_ref)):
        cols = slice(j * kc, (j + 1) * kc)
        y_ref[:, :, cols] = _from_time_major(y + d[:, cols] * u[:, cols], nv)

    @pl.when(step == pl.num_programs(0) - 1)
    def _():
        xr_out_ref[...] = sr_ref[...]
        xi_out_ref[...] = si_ref[...]


def _s5_carry_kernel(y_ref, fr_ref, fi_ref, lre_ref, lim_ref, cre_ref, cim_ref,
                     o_ref, xr_out_ref, xi_out_ref, zr_ref, zi_ref, sr_ref, si_ref, *, half_len, lane_block):
    nrow, gp = fr_ref.shape
    tf = y_ref.shape[2]
    step = pl.program_id(0)
    odd = lax.broadcasted_iota(jnp.int32, (nrow, 1), 0) % 2 == 1

    @pl.when(step == 0)
    def _():
        for lo in range(0, gp, lane_block):
            ls = slice(lo, lo + lane_block)
            pr, pi = _s5_lambda(lre_ref, lim_ref, lo, lo + lane_block, nrow)
            n = 1
            while n < half_len:
                pr, pi = pr * pr - pi * pi, 2.0 * pr * pi
                n *= 2
            fr, fi = fr_ref[:, ls], fi_ref[:, ls]
            fr = jnp.where(odd, pltpu.roll(fr, 1, 0), fr)
            fi = jnp.where(odd, pltpu.roll(fi, 1, 0), fi)
            sr_ref[:, ls] = jnp.where(odd, pr * fr - pi * fi, fr)
            si_ref[:, ls] = jnp.where(odd, pr * fi + pi * fr, fi)

    for lo in range(0, gp, lane_block):
        ls = slice(lo, lo + lane_block)
        lr, li = _s5_lambda(lre_ref, lim_ref, lo, lo + lane_block, nrow)

        zr, zi = sr_ref[:, ls], si_ref[:, ls]
        for r in range(tf):
            rows = slice(r * nrow, (r + 1) * nrow)
            zr, zi = lr * zr - li * zi, lr * zi + li * zr
            zr_ref[rows, ls] = zr
            zi_ref[rows, ls] = zi
        sr_ref[:, ls] = zr
        si_ref[:, ls] = zi

    kc = cre_ref.shape[1]
    nseq = y_ref.shape[0]
    for j, y in enumerate(_s5_out(zr_ref, zi_ref, cre_ref, cim_ref)):
        cols = slice(j * kc, (j + 1) * kc)
        o_ref[:, :, :, cols] = y_ref[:, :, :, cols] + _from_time_major(y, nrow).reshape(nseq, nrow // nseq, tf, kc)

    @pl.when(step == pl.num_programs(0) - 1)
    def _():
        xr_out_ref[...] = fr_ref[...] + sr_ref[...]
        xi_out_ref[...] = fi_ref[...] + si_ref[...]


def _s5_weights_specs(ws):
    zero = lambda n: (lambda *_: (0,) * n)
    return [pl.BlockSpec(w.shape, zero(w.ndim)) for w in ws]


def _s5_prompt(proj, ws, *, nb, seq, ucol, w_b, ts=64, tf=64, lane_block=1024):
    lam_re, lam_im, bre, bim, cre, cim, d = ws
    gp = lam_re.shape[0] * LANES
    nv = SUBLANES
    assert nv == 2 * nb, "two pieces per sequence"
    plen, half_len = seq // 2, seq // 4
    assert plen % ts == 0 and half_len % tf == 0 and gp % lane_block == 0
    assert half_len & (half_len - 1) == 0, "repeated squaring needs a power of two"
    nblk = plen // ts
    u_spec = lambda n: pl.BlockSpec((ts, w_b), lambda s: (n * nblk + s, ucol))
    state = pl.BlockSpec((nv, gp), lambda s: (0, 0))
    state_shape = jax.ShapeDtypeStruct((nv, gp), F32)
    scratch = lambda steps: [pltpu.VMEM((nv * steps, gp), F32)] * 2 + [pltpu.VMEM((nv, gp), F32)] * 2
    y, fr, fi = pl.pallas_call(
        functools.partial(_s5_prompt_kernel, nv=nv, lane_block=lane_block),
        grid=(nblk,),
        in_specs=[u_spec(n) for n in range(nv)] + _s5_weights_specs(ws),
        out_specs=[pl.BlockSpec((nv, ts, w_b), lambda s: (0, s, 0)), state, state],
        out_shape=[jax.ShapeDtypeStruct((nv, plen, w_b), F32), state_shape, state_shape],
        scratch_shapes=scratch(ts),
        compiler_params=_params("arbitrary"),
        name="s5_prompt",
    )(*([proj] * nv), *ws)

    y5 = y.reshape(nb, 2, 2, half_len, w_b)
    y_spec = pl.BlockSpec((nb, None, 2, tf, w_b), lambda s: (0, 1, 0, s, 0))
    y5, xr, xi = pl.pallas_call(
        functools.partial(_s5_carry_kernel, half_len=half_len, lane_block=lane_block),
        grid=(half_len // tf,),
        in_specs=[y_spec, state, state] + _s5_weights_specs((lam_re, lam_im, cre, cim)),
        out_specs=[y_spec, state, state],
        out_shape=[jax.ShapeDtypeStruct(y5.shape, F32), state_shape, state_shape],
        scratch_shapes=scratch(tf),
        input_output_aliases={0: 0},
        compiler_params=_params("arbitrary"),
        name="s5_prompt_carry",
    )(y5, fr, fi, lam_re, lam_im, cre, cim)
    last = lambda x: x.reshape(nb, 2, gp)[:, 1]
    return y5.reshape(nb * seq, w_b), last(xr), last(xi)


def _s5_sample_kernel(u_ref, x0r_ref, x0i_ref, lre_ref, lim_ref, bre_ref, bim_ref, cre_ref, cim_ref, d_ref,
                      y_ref, xr_out_ref, xi_out_ref, xr_ref, xi_ref, *, seq):
    nb = x0r_ref.shape[0]
    u = u_ref[...]
    _s5_in(u.astype(BF16), bre_ref, bim_ref, xr_ref, xi_ref)
    for m in range(xr_ref.shape[0]):
        ls = slice(m * LANES, (m + 1) * LANES)
        lr, li = lre_ref[m], lim_ref[m]
        xr, xi = x0r_ref[:, ls], x0i_ref[:, ls]
        for r in range(seq):
            rows = pl.ds(r, nb, stride=seq)
            xr, xi = _s5_step(lr, li, xr, xi, xr_ref[m, rows, :], xi_ref[m, rows, :])
            xr_ref[m, rows, :] = xr
            xi_ref[m, rows, :] = xi
        xr_out_ref[:, ls] = xr
        xi_out_ref[:, ls] = xi
    kc = cre_ref.shape[1]
    d = d_ref[...]
    for j, y in enumerate(_s5_out(xr_ref, xi_ref, cre_ref, cim_ref)):
        cols = slice(j * kc, (j + 1) * kc)
        y_ref[:, cols] = y + d[:, cols] * u[:, cols]


def _s5_sample(proj, x0_re, x0_im, ws, *, row0, ucol, seq, w_b, nb=64):
    batch, gp = x0_re.shape
    n_slabs = gp // LANES
    assert batch % nb == 0 and row0 % (nb * seq) == 0
    t0 = row0 // (nb * seq)
    return pl.pallas_call(
        functools.partial(_s5_sample_kernel, seq=seq),
        grid=(batch // nb,),
        in_specs=[
            pl.BlockSpec((nb * seq, w_b), lambda p: (t0 + p, ucol)),
            pl.BlockSpec((nb, gp), lambda p: (p, 0)),
            pl.BlockSpec((nb, gp), lambda p: (p, 0)),
        ] + _s5_weights_specs(ws),
        out_specs=[
            pl.BlockSpec((nb * seq, w_b), lambda p: (p, 0)),
            pl.BlockSpec((nb, gp), lambda p: (p, 0)),
            pl.BlockSpec((nb, gp), lambda p: (p, 0)),
        ],
        out_shape=[
            jax.ShapeDtypeStruct((batch * seq, w_b), F32),
            jax.ShapeDtypeStruct((batch, gp), F32),
            jax.ShapeDtypeStruct((batch, gp), F32),
        ],
        scratch_shapes=[pltpu.VMEM((n_slabs, nb * seq, LANES), F32), pltpu.VMEM((n_slabs, nb * seq, LANES), F32)],
        compiler_params=_params("arbitrary"),
        name="s5_sample",
    )(proj, x0_re, x0_im, *ws)


def _mix_gates(o_ref, gate_ref, y_ref, ng_ref, gluw_ref, glub_ref, a_ref, src, dst):
    w_a = o_ref.shape[1]
    o = o_ref[src, :]
    var = jnp.mean(o * o, axis=-1, keepdims=True)
    gate = gate_ref[src, :]
    a_ref[dst, :w_a] = (o * lax.rsqrt(var + EPS) * ng_ref[...] * (gate * jax.nn.sigmoid(gate))).astype(BF16)
    y = jax.nn.gelu(y_ref[src, :])
    z = _dot(y.astype(BF16), gluw_ref[...]) + glub_ref[...]
    a_ref[dst, w_a:] = (y * jax.nn.sigmoid(z)).astype(BF16)


def _mix_kernel(o_ref, gate_ref, y_ref, x_ref, ng_ref, gluw_ref, glub_ref, wout_ref, out_ref, *rest,
                n_tiles, n_cols):
    *copies, a0_ref, a1_ref = rest
    glu16_ref, wout16_ref = copies or (None, None)
    i = pl.program_id(0)
    j = pl.program_id(1)
    rows_per_step = a0_ref.shape[0] // n_cols

    if glu16_ref is not None:
        @pl.when(jnp.logical_and(i == 0, j == 0))
        def _():
            _bf16_weights(gluw_ref, glu16_ref)
        gluw_ref = glu16_ref

    def step(a_gate, a_proj):
        pieces = out_ref.shape[1] // MXU_DIM
        sub = rows_per_step // pieces
        for p in range(pieces):
            cols = (slice(None), slice(p * MXU_DIM, (p + 1) * MXU_DIM))
            if a_gate is not None:
                dst = pl.ds(pl.multiple_of(j * rows_per_step + p * sub, sub), sub)
                _mix_gates(o_ref, gate_ref, y_ref, ng_ref, gluw_ref, glub_ref, a_gate,
                           slice(p * sub, (p + 1) * sub), dst)
            if a_proj is not None:
                w = _bf16_weights(wout_ref, wout16_ref, cols, first_row=1, tile=j)
                out_ref[cols] = x_ref[cols] + _dot(a_proj[...], w)

    _by_parity(i, n_tiles, step, a0_ref, a1_ref)


def _mix(o, proj, y, x, norm_g, glu_w, glu_b, w_out, *, gate_col, name, tn=512):
    m, w_a = o.shape
    w_b = y.shape[1]
    d = w_out.shape[0] * w_out.shape[2] if w_out.ndim == 3 else w_out.shape[1]
    tm = _row_tile(m)
    nj = d // tn
    assert m % tm == 0 and d % tn == 0 and tm % (nj * SUBLANES) == 0
    n = m // tm
    nxt, cur, col, row_slice = _lead_in_maps(n, nj)
    const = lambda i, j: (0, 0)
    cur_tile = lambda i, j: (cur(i), col(i, j))
    glu_spec = ((w_b, w_b), const)
    assert glu_w.dtype == w_out.dtype
    wout_spec, wout_copy = _column_tiles(w_out, tn)
    copy_specs, copy_shapes = (a + b for a, b in zip(_weight_copy(glu_w, *glu_spec, (0, 0)), wout_copy))
    return pl.pallas_call(
        functools.partial(_mix_kernel, n_tiles=n, n_cols=nj),
        grid=(n + 1, nj),
        in_specs=[
            pl.BlockSpec((tm // nj, w_a), lambda i, j: (row_slice(i, j), 0)),
            pl.BlockSpec((tm // nj, w_a), lambda i, j: (row_slice(i, j), gate_col)),
            pl.BlockSpec((tm // nj, w_b), lambda i, j: (row_slice(i, j), 0)),
            pl.BlockSpec((tm, tn), cur_tile),
            pl.BlockSpec((1, w_a), const),
            pl.BlockSpec(*glu_spec),
            pl.BlockSpec((1, w_b), const),
            wout_spec,
        ],
        out_specs=[pl.BlockSpec((tm, tn), cur_tile)] + copy_specs,
        out_shape=[jax.ShapeDtypeStruct((m, d), F32)] + copy_shapes,
        scratch_shapes=[pltpu.VMEM((tm, w_a + w_b), BF16)] * 2,
        compiler_params=_params("arbitrary", "arbitrary"),
        name=name,
    )(o, proj, y, x, norm_g.reshape(1, w_a), glu_w, glu_b.reshape(1, w_b), w_out)


def _mlp_kernel(x_ref, g_ref, up_ref, down_ref, gf_ref, o_ref, *rest, rows, final_norm):
    *copies, h_ref = rest
    up16_ref, down16_ref = copies or (None, None)
    f = pl.program_id(1)

    @pl.when(f == 0)
    def _():
        _rmsnorm_rows(x_ref, g_ref[...], h_ref, rows)
        o_ref[...] = x_ref[...]

    a = jnp.square(jnp.maximum(_dot(h_ref[...], _bf16_weights(up_ref, up16_ref)), 0.0)).astype(BF16)
    o_ref[...] += _dot(a, _bf16_weights(down_ref, down16_ref))

    if final_norm:
        @pl.when(f == pl.num_programs(1) - 1)
        def _():
            _rmsnorm_rows(o_ref, gf_ref[...], o_ref, rows)


def _mlp(x, g, up, down, gf, *, final_norm, name, tf=512):
    m, d = x.shape
    dff = up.shape[1]
    tm = _row_tile(m)
    assert m % tm == 0 and dff % tf == 0 and up.dtype == down.dtype
    up_spec = ((d, tf), lambda i, f: (0, f))
    down_spec = ((tf, d), lambda i, f: (f, 0))
    nf = dff // tf
    copy_specs, copy_shapes = (a + b for a, b in zip(_weight_copy(up, *up_spec, (0, nf - 1)),
                                                     _weight_copy(down, *down_spec, (nf - 1, 0))))
    return pl.pallas_call(
        functools.partial(_mlp_kernel, rows=tm, final_norm=final_norm),
        grid=(m // tm, dff // tf),
        in_specs=[
            pl.BlockSpec((tm, d), lambda i, f: (i, 0)),
            pl.BlockSpec((1, d), lambda i, f: (0, 0)),
            pl.BlockSpec(*up_spec),
            pl.BlockSpec(*down_spec),
            pl.BlockSpec((1, d), lambda i, f: (0, 0)),
        ],
        out_specs=[pl.BlockSpec((tm, d), lambda i, f: (i, 0))] + copy_specs,
        out_shape=[jax.ShapeDtypeStruct((m, d), F32)] + copy_shapes,
        scratch_shapes=[pltpu.VMEM((tm, d), BF16)],
        compiler_params=_params("arbitrary", "arbitrary"),
        name=name,
    )(x, g.reshape(1, d), up, down, gf.reshape(1, d))


def kernel(x_prompt, x_sample, state_hgrn, state_s5_re, state_s5_im, w_in, w_out, norm1_g, norm2_g, hgrn_lb_logits, hgrn_norm_g, s5_a_re, s5_a_im, s5_b_re, s5_b_im, s5_c_re, s5_c_im, s5_d, s5_log_step, glu_w, glu_b, mlp_up, mlp_down, final_norm_g):
    bp, seq, d = x_prompt.shape
    bs, dseq, _ = x_sample.shape
    depth = w_in.shape[0]
    w_a = hgrn_norm_g.shape[1]
    w_b = s5_d.shape[1]
    g_b, p_state = s5_a_re.shape[1], s5_a_re.shape[2]
    mp, ms = bp * seq, bs * dseq
    assert w_in.shape[2] == 4 * w_a + w_b and w_a == w_b and g_b * S5_GROUP == w_b
    gate_col, ucol = 3, (4 * w_a) // w_b

    xp = x_prompt.reshape(mp, d)
    xs = x_sample.reshape(ms, d)
    lb_logits = hgrn_lb_logits.astype(F32)
    new_states = []
    for l in range(depth):
        proj_s, w_in16 = _inproj(xs, norm1_g[l], w_in[l].astype(F32), name="inproj_sample")
        proj_p, = _inproj(xp, norm1_g[l], w_in16, name="inproj_prompt")

        o_p, sh_p = _hgrn_prompt(proj_p, lb_logits, layer=l, batch=bp, seq=seq, w_a=w_a)
        o_s, sh_s = _hgrn_sample(proj_s, lb_logits, state_hgrn, layer=l, row0=0, batch=bs, seq=dseq, w_a=w_a)

        ws = (*_s5_weights(*(w[l].astype(F32) for w in (s5_a_re, s5_a_im, s5_log_step, s5_b_re, s5_b_im,
                                                        s5_c_re, s5_c_im))),
              s5_d[l].astype(F32).reshape(1, w_b))
        y_p, sr_p, si_p = _s5_prompt(proj_p, ws, nb=bp, seq=seq, ucol=ucol, w_b=w_b)
        y_s, sr_s, si_s = _s5_sample(
            proj_s, state_s5_re[l].reshape(bs, g_b * p_state).astype(F32), state_s5_im[l].reshape(bs, g_b * p_state).astype(F32),
            ws, row0=0, ucol=ucol, seq=dseq, w_b=w_b)

        last = l == depth - 1
        x1_s, glu16, w_out16 = _mix(o_s, proj_s, y_s, xs, hgrn_norm_g[l], glu_w[l].astype(F32), glu_b[l],
                                    w_out[l].astype(F32), gate_col=gate_col, name="mix_sample")
        xs, up16, down16 = _mlp(x1_s, norm2_g[l], mlp_up[l].astype(F32), mlp_down[l].astype(F32), final_norm_g,
                                final_norm=last, name="mlp_sample")
        x1_p, = _mix(o_p, proj_p, y_p, xp, hgrn_norm_g[l], glu16, glu_b[l], w_out16, gate_col=gate_col,
                     name="mix_prompt")
        xp, = _mlp(x1_p, norm2_g[l], up16, down16, final_norm_g, final_norm=last, name="mlp_prompt")
        new_states.append((sh_p, sr_p.reshape(bp, g_b, p_state), si_p.reshape(bp, g_b, p_state),
                           sh_s[0], sr_s.reshape(bs, g_b, p_state), si_s.reshape(bs, g_b, p_state)))

    y_prompt = xp.reshape(bp, seq, d).astype(x_prompt.dtype)
    y_sample = xs.reshape(bs, dseq, d).astype(x_sample.dtype)
    stacked = [jnp.stack([st[i] for st in new_states]) for i in range(6)]
    return (y_prompt, y_sample, *stacked)
```

```python
import functools

import jax
import jax.numpy as jnp
import numpy as np
from jax import lax
from jax.experimental import pallas as pl
from jax.experimental.pallas import tpu as pltpu

F32 = jnp.float32
BF16 = jnp.bfloat16
EPS = 1e-6

LANES = 128
SUBLANES = 8
MXU_DIM = 256
VMEM_LIMIT = 56 * 1024 * 1024

HEAD_DIM = 128
S5_GROUP = 16
HGRN_CHUNK = 64
HGRN_HEAD_GROUP = 2
ROW_TILE = 1024
NEG_BIG = -1e30


def _dot(a, b):
    return jnp.dot(a, b, preferred_element_type=F32)


def _dot_nt(a, b):
    return lax.dot_general(a, b, (((1,), (1,)), ((), ())), preferred_element_type=F32)


def _dot_tn(a, b):
    return lax.dot_general(a, b, (((0,), (0,)), ((), ())), preferred_element_type=F32)


def _split3(x):
    hi = x.astype(BF16)
    r1 = x - hi.astype(F32)
    mid = r1.astype(BF16)
    lo = (r1 - mid.astype(F32)).astype(BF16)
    return hi, mid, lo


def _params(*sem):
    return pltpu.CompilerParams(dimension_semantics=sem, vmem_limit_bytes=VMEM_LIMIT)


def _rmsnorm_rows(src_ref, g, dst_ref, rows, chunk=256):
    def body(c, carry):
        r0 = pl.multiple_of(c * chunk, chunk)
        x = src_ref[pl.ds(r0, chunk), :]
        var = jnp.mean(x * x, axis=-1, keepdims=True)
        dst_ref[pl.ds(r0, chunk), :] = (x * lax.rsqrt(var + EPS) * g).astype(dst_ref.dtype)
        return carry

    lax.fori_loop(0, rows // chunk, body, 0)


def _bf16_weights(w_ref, copy_ref, at=(Ellipsis,), first_row=0):
    w = w_ref[at]
    if copy_ref is None:
        return w
    w = w.astype(BF16)

    @pl.when(pl.program_id(0) == first_row)
    def _():
        copy_ref[at] = w

    return w


def _weight_copy(w, block, index_map, last_index, first_row=0):
    if w.dtype == BF16:
        return [], []

    def once(i, j):
        during = index_map(jnp.full_like(i, first_row), j)
        before = index_map(jnp.full_like(i, first_row), jnp.zeros_like(j))
        pick = lambda b, d, a: jnp.where(i < first_row, b, jnp.where(i == first_row, d, a))
        return jax.tree.map(pick, before, during, last_index)

    return [pl.BlockSpec(block, once)], [jax.ShapeDtypeStruct(w.shape, BF16)]


def _lead_in_maps(n, nj):
    nxt = lambda i: jnp.minimum(i, n - 1)
    cur = lambda i: jnp.maximum(i - 1, 0)
    col = lambda i, j: jnp.where(i > 0, j, 0)
    row_slice = lambda i, j: nxt(i) * nj + jnp.where(i < n, j, nj - 1)
    return nxt, cur, col, row_slice


def _by_parity(i, n_tiles, step, buf0, buf1):
    even = i % 2 == 0
    inner = jnp.logical_and(i > 0, i < n_tiles)
    pl.when(i == 0)(lambda: step(buf0, None))
    pl.when(jnp.logical_and(inner, even))(lambda: step(buf0, buf1))
    pl.when(jnp.logical_and(inner, jnp.logical_not(even)))(lambda: step(buf1, buf0))
    pl.when(jnp.logical_and(i == n_tiles, even))(lambda: step(None, buf1))
    pl.when(jnp.logical_and(i == n_tiles, jnp.logical_not(even)))(lambda: step(None, buf0))


def _inproj_kernel(x_ref, g_ref, w_ref, o_ref, *rest, n_tiles, n_cols):
    *w16_ref, h0_ref, h1_ref = rest
    j = pl.program_id(1)
    rows_per_step = h0_ref.shape[0] // n_cols
    rows = pl.ds(pl.multiple_of(j * rows_per_step, rows_per_step), rows_per_step)

    def step(h_norm, h_proj):
        if h_norm is not None:
            x = x_ref[...]
            var = jnp.mean(x * x, axis=-1, keepdims=True)
            h_norm[rows, :] = (x * lax.rsqrt(var + EPS) * g_ref[...]).astype(BF16)
        if h_proj is not None:
            o_ref[...] = _dot(h_proj[...], _bf16_weights(w_ref, *(w16_ref or [None]), first_row=1))

    _by_parity(pl.program_id(0), n_tiles, step, h0_ref, h1_ref)


def _row_tile(m):
    return ROW_TILE if m % ROW_TILE == 0 else ROW_TILE // 2


def _inproj(x, g, w, *, name, tn=1280):
    m, d = x.shape
    n_out = w.shape[1]
    tm = _row_tile(m)
    if w.dtype != BF16:
        tn //= 2
    nj = n_out // tn
    assert m % tm == 0 and n_out % tn == 0 and tm % (nj * SUBLANES) == 0
    n = m // tm
    nxt, cur, col, row_slice = _lead_in_maps(n, nj)
    w_spec = ((d, tn), lambda i, j: (0, col(i, j)))
    copy_specs, copy_shapes = _weight_copy(w, *w_spec, (0, nj - 1), first_row=1)
    return pl.pallas_call(
        functools.partial(_inproj_kernel, n_tiles=n, n_cols=nj),
        grid=(n + 1, nj),
        in_specs=[
            pl.BlockSpec((tm // nj, d), lambda i, j: (row_slice(i, j), 0)),
            pl.BlockSpec((1, d), lambda i, j: (0, 0)),
            pl.BlockSpec(*w_spec),
        ],
        out_specs=[pl.BlockSpec((tm, tn), lambda i, j: (cur(i), col(i, j)))] + copy_specs,
        out_shape=[jax.ShapeDtypeStruct((m, n_out), F32)] + copy_shapes,
        scratch_shapes=[pltpu.VMEM((tm, d), BF16)] * 2,
        compiler_params=_params("arbitrary", "arbitrary"),
        name=name,
    )(x, g.reshape(1, d), w)


def _lower_bound(logits, layer):
    m = jnp.max(logits, axis=0, keepdims=True)
    e = jnp.exp(logits - m)
    return jnp.sum(e[: layer + 1], axis=0, keepdims=True) / jnp.sum(e, axis=0, keepdims=True)


def _row_to_col(e_row):
    n = e_row.shape[1]
    hi, mid, lo = (p.astype(F32) for p in _split3(e_row))
    r = lax.broadcasted_iota(jnp.int32, (2 * SUBLANES, n), 0)
    pieces = jnp.where(r == 0, hi, jnp.where(r == 1, mid, jnp.where(r == 2, lo, 0.0)))
    return _dot_tn(pieces.astype(BF16), jnp.ones((2 * SUBLANES, LANES), BF16))


def _hgrn_gates(qp, fp, lb):
    q = qp * jax.nn.sigmoid(qp)
    f = lb + (1.0 - lb) * jax.nn.sigmoid(fp)
    return q, f, 1.0 - f, jnp.log(f)


def _hgrn_tables(c):
    t = np.arange(c)[:, None]
    j = np.arange(c)[None, :]
    sums, pairs = [j <= t], []
    bs = c
    while bs >= 2:
        hs = bs // 2
        ref = t - t % bs + hs - 1
        upper = t % bs >= hs
        sums.append(np.where(upper, (j > ref) & (j <= t), (j > t) & (j <= ref)))
        pairs.append((t // bs == j // bs) & upper & (j % bs < hs))
        bs = hs
    sums.append(j > t)
    pairs.append(t == j)
    sums = np.concatenate(sums, axis=0).astype(np.float32)
    return jnp.asarray(np.concatenate([sums] * 3, axis=1), BF16), jnp.asarray(np.stack(pairs), F32)


def _upper_q_lower_k(q, k, bs):
    c = q.shape[0]
    hs = bs // 2
    if hs % SUBLANES == 0:
        parts = []
        for r0 in range(0, c, bs):
            parts += [k[r0 : r0 + hs], q[r0 + hs : r0 + bs]]
        return jnp.concatenate(parts, axis=0)
    row = lax.broadcasted_iota(jnp.int32, (c, 1), 0)
    return jnp.where(row % bs >= hs, q, k)


def _hgrn_prep(qp, fp, v, lb, sums, levels):
    c = qp.shape[0]
    q = qp * jax.nn.sigmoid(qp)
    f = lb + (1.0 - lb) * jax.nn.sigmoid(fp)
    k = 1.0 - f
    e = jnp.exp2(_dot(sums, jnp.concatenate(_split3(jnp.log2(f)), axis=0)))
    e_b, e_rest = e[:c], e[(levels + 1) * c :]
    ys = [(_upper_q_lower_k(q, k, c >> l) * e[(l + 1) * c : (l + 2) * c]).astype(BF16) for l in range(levels)]
    return dict(ys=ys, q16=q.astype(BF16), k16=k.astype(BF16), v16=v.astype(BF16),
                qe=(q * e_b).astype(BF16), kd=(k * e_rest).astype(BF16), e_last=e_b[c - 1 : c])


def _hgrn_apply(prep, st_ref, pairs_ref, n_seq, heads, emit_between):
    levels = pairs_ref.shape[0] - 1
    pairs = [(n, h) for n in range(n_seq) for h in range(heads)]

    def head(n, h, name, l=None):
        x = prep[n, h // HGRN_HEAD_GROUP][name]
        x = x if l is None else x[l]
        lo = h % HGRN_HEAD_GROUP * HEAD_DIM
        return x[:, lo : lo + HEAD_DIM]

    scores = {}
    for i, (n, h) in enumerate(pairs):
        a = _dot_nt(head(n, h, "q16"), head(n, h, "k16")) * pairs_ref[levels]
        for l in range(levels):
            y = head(n, h, "ys", l)
            a = a + _dot_nt(y, y) * pairs_ref[l]
        scores[n, h] = a.astype(BF16)
        emit_between(i)
    outs = {}
    for n, h in pairs:
        st = st_ref[n, h]
        outs[n, h] = _dot_nt(head(n, h, "qe"), st.astype(BF16)) + _dot(scores[n, h], head(n, h, "v16"))
        st_ref[n, h] = head(n, h, "e_last") * st + _dot_tn(head(n, h, "v16"), head(n, h, "kd"))
    return [jnp.concatenate([outs[n, h] for h in range(heads)], axis=1) for n in range(n_seq)]


def _hgrn_prompt_kernel(q_ref, f_ref, v_ref, lbl_ref, sums_ref, pairs_ref, o_ref, s_out_ref, st_ref, *,
                        layer, heads, n_chunks):
    t = pl.program_id(1)

    @pl.when(t == 0)
    def _():
        st_ref[...] = jnp.zeros_like(st_ref)

    lb = _lower_bound(lbl_ref[...], layer)
    n_seq = q_ref.shape[0]
    levels = pairs_ref.shape[0] - 1
    pieces = [(n, g) for n in range(n_seq) for g in range(heads // HGRN_HEAD_GROUP)]

    def prep(c, n, g):
        rows = slice(c * HGRN_CHUNK, (c + 1) * HGRN_CHUNK)
        cols = slice(g * HGRN_HEAD_GROUP * HEAD_DIM, (g + 1) * HGRN_HEAD_GROUP * HEAD_DIM)
        return _hgrn_prep(q_ref[n, rows, cols], f_ref[n, rows, cols], v_ref[n, rows, cols], lb[:, cols],
                          sums_ref[...], levels)

    nxt = {p: prep(0, *p) for p in pieces}
    for c in range(n_chunks):
        cur, nxt = nxt, {}

        def emit_between(i, c=c, nxt=nxt):
            if c + 1 < n_chunks and (i + 1) % HGRN_HEAD_GROUP == 0:
                p = pieces[i // HGRN_HEAD_GROUP]
                nxt[p] = prep(c + 1, *p)

        for n, o in enumerate(_hgrn_apply(cur, st_ref, pairs_ref, n_seq, heads, emit_between)):
            o_ref[n, c * HGRN_CHUNK : (c + 1) * HGRN_CHUNK, :] = o

    @pl.when(t == pl.num_programs(1) - 1)
    def _():
        for n in range(n_seq):
            for h in range(heads):
                s_out_ref[n, h] = st_ref[n, h].T


def _hgrn_prompt(proj, lb_logits, *, layer, batch, seq, w_a, tt=256, n_seq=2):
    heads = w_a // HEAD_DIM
    assert seq % tt == 0 and tt % HGRN_CHUNK == 0 and batch % n_seq == 0
    sums, pairs = _hgrn_tables(HGRN_CHUNK)
    proj3 = proj.reshape(batch, seq, proj.shape[1])
    col = lambda c: pl.BlockSpec((n_seq, tt, w_a), lambda b, t: (b, t, c))
    const = lambda x: pl.BlockSpec(x.shape, lambda b, t: (0,) * x.ndim)
    o, s = pl.pallas_call(
        functools.partial(_hgrn_prompt_kernel, layer=layer, heads=heads, n_chunks=tt // HGRN_CHUNK),
        grid=(batch // n_seq, seq // tt),
        in_specs=[col(0), col(1), col(2), const(lb_logits), const(sums), const(pairs)],
        out_specs=[
            pl.BlockSpec((n_seq, tt, w_a), lambda b, t: (b, t, 0)),
            pl.BlockSpec((n_seq, heads, HEAD_DIM, HEAD_DIM), lambda b, t: (b, 0, 0, 0)),
        ],
        out_shape=[
            jax.ShapeDtypeStruct((batch, seq, w_a), F32),
            jax.ShapeDtypeStruct((batch, heads, HEAD_DIM, HEAD_DIM), F32),
        ],
        scratch_shapes=[pltpu.VMEM((n_seq, heads, HEAD_DIM, HEAD_DIM), F32)],
        compiler_params=_params("arbitrary", "arbitrary"),
        name="hgrn_prompt",
    )(proj3, proj3, proj3, lb_logits, sums, pairs)
    return o.reshape(batch * seq, w_a), s


def _hgrn_sample_kernel(q_ref, f_ref, v_ref, lbl_ref, s_in_ref, o_ref, s_out_ref, *, layer, heads, seq):
    n_seq = SUBLANES // seq
    lb = _lower_bound(lbl_ref[...], layer)
    row = lax.broadcasted_iota(jnp.int32, (SUBLANES, 1), 0)
    pos = row % seq
    work = []
    for tile, h in ((tile, h) for tile in range(q_ref.shape[0] // SUBLANES) for h in range(heads)):
        cols = slice(h * HEAD_DIM, (h + 1) * HEAD_DIM)
        rows = slice(tile * SUBLANES, (tile + 1) * SUBLANES)
        q, _, k, logf = _hgrn_gates(q_ref[rows, cols], f_ref[rows, cols], lb[:, cols])
        v = v_ref[rows, cols]
        b = logf
        shift = 1
        while shift < seq:
            b = b + jnp.where(pos >= shift, pltpu.roll(b, shift, 0), 0.0)
            shift *= 2
        o = jnp.zeros((SUBLANES, HEAD_DIM), F32)
        for sl in range(SUBLANES):
            lo_t, hi_t = sl, (sl // seq + 1) * seq
            mask = jnp.logical_and(row >= lo_t, row < hi_t)
            z = q * jnp.exp(jnp.where(mask, b - b[sl : sl + 1], NEG_BIG)) * k[sl : sl + 1]
            o = o + jnp.sum(z, axis=-1, keepdims=True) * v[sl : sl + 1]
        per_seq = []
        for n in range(n_seq):
            own = jnp.logical_and(row >= n * seq, row < (n + 1) * seq)
            b_last = b[(n + 1) * seq - 1 : (n + 1) * seq, :]
            qe = jnp.where(own, q * jnp.exp(b), 0.0).astype(BF16)
            kd = jnp.where(own, k * jnp.exp(jnp.where(own, b_last - b, 0.0)), 0.0).astype(BF16)
            per_seq.append((tile * n_seq + n, qe, kd, jnp.exp(b_last)))
        work.append((rows, cols, h, o, v.astype(BF16), per_seq))
    for rows, cols, h, o, _, per_seq in work:
        for i, qe, _, _ in per_seq:
            o = o + _dot(qe, s_in_ref[0, i, h].astype(BF16))
        o_ref[rows, cols] = o
    for _, _, h, _, v16, per_seq in work:
        for i, _, kd, e_last in per_seq:
            s_out_ref[0, i, h] = _row_to_col(e_last) * s_in_ref[0, i, h] + _dot_tn(kd, v16)


def _hgrn_sample(proj, lb_logits, state, *, layer, row0, batch, seq, w_a, tiles=8):
    heads = w_a // HEAD_DIM
    rows = tiles * SUBLANES
    assert SUBLANES % seq == 0 and row0 % rows == 0 and (batch * seq) % rows == 0
    n_seq = rows // seq
    t0 = row0 // rows
    col = lambda c: pl.BlockSpec((rows, w_a), lambda p: (t0 + p, c))
    st = pl.BlockSpec((1, n_seq, heads, HEAD_DIM, HEAD_DIM), lambda p: (layer, p, 0, 0, 0))
    st_out = pl.BlockSpec((1, n_seq, heads, HEAD_DIM, HEAD_DIM), lambda p: (0, p, 0, 0, 0))
    return pl.pallas_call(
        functools.partial(_hgrn_sample_kernel, layer=layer, heads=heads, seq=seq),
        grid=(batch // n_seq,),
        in_specs=[col(0), col(1), col(2), pl.BlockSpec(lb_logits.shape, lambda p: (0, 0)), st],
        out_specs=[pl.BlockSpec((rows, w_a), lambda p: (p, 0)), st_out],
        out_shape=[
            jax.ShapeDtypeStruct((batch * seq, w_a), F32),
            jax.ShapeDtypeStruct((1, batch, heads, HEAD_DIM, HEAD_DIM), F32),
        ],
        compiler_params=_params("arbitrary"),
        name="hgrn_sample",
    )(proj, proj, proj, lb_logits, state)


def _s5_weights_kernel(are_ref, aim_ref, ls_ref, bre_ref, bim_ref, cre_ref, cim_ref, rep_ref, tile_ref, same_ref,
                       lre_ref, lim_ref, bbre_ref, bbim_ref, ccre_ref, ccim_ref):
    a_re = are_ref[...]
    a_im = aim_ref[...]
    dt = jnp.exp(ls_ref[...])
    mag = jnp.exp(a_re * dt)
    lam_re = mag * jnp.cos(a_im * dt)
    lam_im = mag * jnp.sin(a_im * dt)
    den = a_re * a_re + a_im * a_im
    nr, ni = lam_re - 1.0, lam_im
    r_re = (nr * a_re + ni * a_im) / den
    r_im = (ni * a_re - nr * a_im) / den
    lre_ref[...] = lam_re
    lim_ref[...] = lam_im
    b_re = bre_ref[...]
    b_im = bim_ref[...]
    same = same_ref[...]
    bbre_ref[0] = (_dot(rep_ref[...], (r_re * b_re - r_im * b_im).astype(BF16)) * same).astype(BF16)
    bbim_ref[0] = (_dot(rep_ref[...], (r_re * b_im + r_im * b_re).astype(BF16)) * same).astype(BF16)
    ccre_ref[0] = (_dot(cre_ref[...].astype(BF16), tile_ref[...]) * same).astype(BF16)
    ccim_ref[0] = (_dot(cim_ref[...].astype(BF16), tile_ref[...]) * same).astype(BF16)


def _s5_weights(a_re, a_im, log_step, b_re, b_im, c_re, c_im):
    g, p = a_re.shape
    c = b_re.shape[-1]
    gt = MXU_DIM // c
    nj, kc, kp = g // gt, gt * c, gt * p
    flat = lambda x: x.reshape(1, g * p)
    chan = lambda x: jnp.transpose(x, (2, 0, 1)).reshape(c, g * p)
    row_g, lane_g = np.arange(kc)[:, None] // c, np.arange(kp)[None, :] // p
    rep = jnp.asarray(np.arange(kc)[:, None] % c == np.arange(c)[None, :], BF16)
    tile = jnp.asarray(np.arange(p)[:, None] == np.arange(kp)[None, :] % p, BF16)
    same = jnp.asarray(row_g == lane_g, F32)
    lanes = lambda rows: pl.BlockSpec((rows, kp), lambda j: (0, j))
    const = lambda x: pl.BlockSpec(x.shape, lambda j: (0, 0))
    tiles = pl.BlockSpec((1, kc, kp), lambda j: (j, 0, 0))
    tiles_shape = jax.ShapeDtypeStruct((nj, kc, kp), BF16)
    lam_shape = jax.ShapeDtypeStruct((1, g * p), F32)
    lam_re, lam_im, *mats = pl.pallas_call(
        _s5_weights_kernel,
        grid=(nj,),
        in_specs=[lanes(1), lanes(1), lanes(1), lanes(c), lanes(c),
                  pl.BlockSpec((kc, p), lambda j: (j, 0)), pl.BlockSpec((kc, p), lambda j: (j, 0)),
                  const(rep), const(tile), const(same)],
        out_specs=[lanes(1), lanes(1), tiles, tiles, tiles, tiles],
        out_shape=[lam_shape, lam_shape] + [tiles_shape] * 4,
        compiler_params=_params("arbitrary"),
        name="s5_weights",
    )(flat(a_re), flat(a_im), flat(jnp.repeat(log_step, p)), chan(b_re), chan(b_im),
      c_re.reshape(g * c, p), c_im.reshape(g * c, p), rep, tile, same)
    return (lam_re.reshape(-1, 1, LANES), lam_im.reshape(-1, 1, LANES), *mats)


def _s5_lanes(x_ref, lo, hi):
    if len(x_ref.shape) == 2:
        return x_ref[:, lo:hi]
    return jnp.concatenate([x_ref[m] for m in range(lo // LANES, hi // LANES)], axis=1)


def _s5_in_tile(ub, j, bre_ref, bim_ref, xr_ref, xi_ref):
    _, kc, kp = bre_ref.shape
    uj = ub[:, j * kc : (j + 1) * kc]
    for x_ref, w_ref in ((xr_ref, bre_ref), (xi_ref, bim_ref)):
        bu = _dot(uj, w_ref[j])
        if len(x_ref.shape) == 2:
            x_ref[:, j * kp : (j + 1) * kp] = bu
        else:
            for m in range(kp // LANES):
                x_ref[j * kp // LANES + m] = bu[:, m * LANES : (m + 1) * LANES]


def _s5_in(ub, bre_ref, bim_ref, xr_ref, xi_ref):
    for j in range(bre_ref.shape[0]):
        _s5_in_tile(ub, j, bre_ref, bim_ref, xr_ref, xi_ref)


def _s5_out_tile(j, xr_ref, xi_ref, cre_ref, cim_ref):
    kp = cre_ref.shape[2]
    xr = _s5_lanes(xr_ref, j * kp, (j + 1) * kp).astype(BF16)
    xi = _s5_lanes(xi_ref, j * kp, (j + 1) * kp).astype(BF16)
    return _dot_nt(xr, cre_ref[j]) - _dot_nt(xi, cim_ref[j])


def _s5_out(xr_ref, xi_ref, cre_ref, cim_ref):
    return [_s5_out_tile(j, xr_ref, xi_ref, cre_ref, cim_ref) for j in range(cre_ref.shape[0])]


def _s5_step(lr, li, xr, xi, bur, bui):
    return lr * xr - li * xi + bur, lr * xi + li * xr + bui


def _s5_lambda(lre_ref, lim_ref, lo, hi, rows):
    cat = lambda ref: jnp.concatenate([ref[m] for m in range(lo // LANES, hi // LANES)], axis=1)
    return jnp.broadcast_to(cat(lre_ref), (rows, hi - lo)), jnp.broadcast_to(cat(lim_ref), (rows, hi - lo))


def _to_time_major(x):
    n, ts, w = x.shape
    return jnp.swapaxes(x, 0, 1).reshape(ts * n, w)


def _from_time_major(x, n):
    rows, w = x.shape
    return jnp.swapaxes(x.reshape(rows // n, n, w), 0, 1)


def _s5_prompt_kernel(*refs, nv):
    u_refs = refs[:nv]
    (lre_ref, lim_ref, bre_ref, bim_ref, cre_ref, cim_ref, d_ref,
     y_ref, xr_out_ref, xi_out_ref, xr_ref, xi_ref, sr_ref, si_ref) = refs[nv:]
    ts = u_refs[0].shape[0]
    step = pl.program_id(0)

    @pl.when(step == 0)
    def _():
        sr_ref[...] = jnp.zeros_like(sr_ref)
        si_ref[...] = jnp.zeros_like(si_ref)

    u = _to_time_major(jnp.stack([r[...] for r in u_refs], axis=0))
    ub = u.astype(BF16)
    nj, kc, kp = bre_ref.shape
    d = d_ref[...]

    def project(j):
        cols = slice(j * kc, (j + 1) * kc)
        y = _s5_out_tile(j, xr_ref, xi_ref, cre_ref, cim_ref)
        y_ref[:, :, cols] = _from_time_major(y + d[:, cols] * u[:, cols], nv)

    def b_piece(j, n, x_ref, w_ref):
        cols = slice(n * MXU_DIM, (n + 1) * MXU_DIM)
        x_ref[:, j * kp + n * MXU_DIM : j * kp + (n + 1) * MXU_DIM] = _dot(ub[:, j * kc : (j + 1) * kc], w_ref[j][:, cols])

    def scan_piece(j, state, r0, r1):
        ls = slice(j * kp, (j + 1) * kp)
        lr, li, xr, xi = state
        for r in range(r0, r1):
            rows = slice(r * nv, (r + 1) * nv)
            xr, xi = _s5_step(lr, li, xr, xi, xr_ref[rows, ls], xi_ref[rows, ls])
            xr_ref[rows, ls] = xr
            xi_ref[rows, ls] = xi
        return lr, li, xr, xi

    pieces = [(n, x_ref, w_ref) for n in range(kp // MXU_DIM) for x_ref, w_ref in ((xr_ref, bre_ref), (xi_ref, bim_ref))]
    per = ts // len(pieces)
    assert per * len(pieces) == ts
    for s in range(nj + 2):
        state = None
        if 0 <= s - 1 < nj:
            ls = slice((s - 1) * kp, s * kp)
            state = (*_s5_lambda(lre_ref, lim_ref, (s - 1) * kp, s * kp, nv), sr_ref[:, ls], si_ref[:, ls])
        for p, (n, x_ref, w_ref) in enumerate(pieces):
            if s < nj:
                b_piece(s, n, x_ref, w_ref)
            if state is not None:
                state = scan_piece(s - 1, state, p * per, (p + 1) * per)
        if state is not None:
            sr_ref[:, ls] = state[2]
            si_ref[:, ls] = state[3]
        if 0 <= s - 2 < nj:
            project(s - 2)

    @pl.when(step == pl.num_programs(0) - 1)
    def _():
        xr_out_ref[...] = sr_ref[...]
        xi_out_ref[...] = si_ref[...]


def _s5_carry_kernel(y_ref, fr_ref, fi_ref, lre_ref, lim_ref, cre_ref, cim_ref,
                     o_ref, xr_out_ref, xi_out_ref, zr_ref, zi_ref, sr_ref, si_ref, *, half_len, lane_block):
    nrow, gp = fr_ref.shape
    tf = y_ref.shape[2]
    step = pl.program_id(0)
    odd = lax.broadcasted_iota(jnp.int32, (nrow, 1), 0) % 2 == 1

    @pl.when(step == 0)
    def _():
        for lo in range(0, gp, lane_block):
            ls = slice(lo, lo + lane_block)
            pr, pi = _s5_lambda(lre_ref, lim_ref, lo, lo + lane_block, nrow)
            n = 1
            while n < half_len:
                pr, pi = pr * pr - pi * pi, 2.0 * pr * pi
                n *= 2
            fr, fi = fr_ref[:, ls], fi_ref[:, ls]
            fr = jnp.where(odd, pltpu.roll(fr, 1, 0), fr)
            fi = jnp.where(odd, pltpu.roll(fi, 1, 0), fi)
            sr_ref[:, ls] = jnp.where(odd, pr * fr - pi * fi, fr)
            si_ref[:, ls] = jnp.where(odd, pr * fi + pi * fr, fi)

    nj, kc, kp = cre_ref.shape
    nseq = y_ref.shape[0]
    for s in range(nj + 1):
        if s < nj:
            ls = slice(s * kp, (s + 1) * kp)
            lr, li = _s5_lambda(lre_ref, lim_ref, s * kp, (s + 1) * kp, nrow)
            zr, zi = sr_ref[:, ls], si_ref[:, ls]
            for r in range(tf):
                rows = slice(r * nrow, (r + 1) * nrow)
                zr, zi = lr * zr - li * zi, lr * zi + li * zr
                zr_ref[rows, ls] = zr
                zi_ref[rows, ls] = zi
            sr_ref[:, ls] = zr
            si_ref[:, ls] = zi
        if s > 0:
            cols = slice((s - 1) * kc, s * kc)
            y = _from_time_major(_s5_out_tile(s - 1, zr_ref, zi_ref, cre_ref, cim_ref), nrow)
            o_ref[:, :, :, cols] = y_ref[:, :, :, cols] + y.reshape(nseq, nrow // nseq, tf, kc)

    @pl.when(step == pl.num_programs(0) - 1)
    def _():
        xr_out_ref[...] = fr_ref[...] + sr_ref[...]
        xi_out_ref[...] = fi_ref[...] + si_ref[...]


def _s5_weights_specs(ws):
    zero = lambda n: (lambda *_: (0,) * n)
    return [pl.BlockSpec(w.shape, zero(w.ndim)) for w in ws]


def _s5_prompt(proj, ws, *, nb, seq, ucol, w_b, ts=64, tf=64, lane_block=1024):
    lam_re, lam_im, bre, bim, cre, cim, d = ws
    gp = lam_re.shape[0] * LANES
    nv = SUBLANES
    assert nv == 2 * nb, "two pieces per sequence"
    plen, half_len = seq // 2, seq // 4
    assert plen % ts == 0 and half_len % tf == 0 and gp % lane_block == 0
    assert half_len & (half_len - 1) == 0, "repeated squaring needs a power of two"
    nblk = plen // ts
    u_spec = lambda n: pl.BlockSpec((ts, w_b), lambda s: (n * nblk + s, ucol))
    state = pl.BlockSpec((nv, gp), lambda s: (0, 0))
    state_shape = jax.ShapeDtypeStruct((nv, gp), F32)
    scratch = lambda steps: [pltpu.VMEM((nv * steps, gp), F32)] * 2 + [pltpu.VMEM((nv, gp), F32)] * 2
    y, fr, fi = pl.pallas_call(
        functools.partial(_s5_prompt_kernel, nv=nv),
        grid=(nblk,),
        in_specs=[u_spec(n) for n in range(nv)] + _s5_weights_specs(ws),
        out_specs=[pl.BlockSpec((nv, ts, w_b), lambda s: (0, s, 0)), state, state],
        out_shape=[jax.ShapeDtypeStruct((nv, plen, w_b), F32), state_shape, state_shape],
        scratch_shapes=scratch(ts),
        compiler_params=_params("arbitrary"),
        name="s5_prompt",
    )(*([proj] * nv), *ws)

    y5 = y.reshape(nb, 2, 2, half_len, w_b)
    y_spec = pl.BlockSpec((nb, None, 2, tf, w_b), lambda s: (0, 1, 0, s, 0))
    y5, xr, xi = pl.pallas_call(
        functools.partial(_s5_carry_kernel, half_len=half_len, lane_block=lane_block),
        grid=(half_len // tf,),
        in_specs=[y_spec, state, state] + _s5_weights_specs((lam_re, lam_im, cre, cim)),
        out_specs=[y_spec, state, state],
        out_shape=[jax.ShapeDtypeStruct(y5.shape, F32), state_shape, state_shape],
        scratch_shapes=scratch(tf),
        input_output_aliases={0: 0},
        compiler_params=_params("arbitrary"),
        name="s5_prompt_carry",
    )(y5, fr, fi, lam_re, lam_im, cre, cim)
    last = lambda x: x.reshape(nb, 2, gp)[:, 1]
    return y5.reshape(nb * seq, w_b), last(xr), last(xi)


def _s5_sample_kernel(u_ref, x0r_ref, x0i_ref, lre_ref, lim_ref, bre_ref, bim_ref, cre_ref, cim_ref, d_ref,
                      y_ref, xr_out_ref, xi_out_ref, xr_ref, xi_ref, *, seq):
    nb = x0r_ref.shape[0]
    u = u_ref[...]
    _s5_in(u.astype(BF16), bre_ref, bim_ref, xr_ref, xi_ref)
    for m in range(xr_ref.shape[0]):
        ls = slice(m * LANES, (m + 1) * LANES)
        lr, li = lre_ref[m], lim_ref[m]
        xr, xi = x0r_ref[:, ls], x0i_ref[:, ls]
        for r in range(seq):
            rows = pl.ds(r, nb, stride=seq)
            xr, xi = _s5_step(lr, li, xr, xi, xr_ref[m, rows, :], xi_ref[m, rows, :])
            xr_ref[m, rows, :] = xr
            xi_ref[m, rows, :] = xi
        xr_out_ref[:, ls] = xr
        xi_out_ref[:, ls] = xi
    kc = cre_ref.shape[1]
    d = d_ref[...]
    for j, y in enumerate(_s5_out(xr_ref, xi_ref, cre_ref, cim_ref)):
        cols = slice(j * kc, (j + 1) * kc)
        y_ref[:, cols] = y + d[:, cols] * u[:, cols]


def _s5_sample(proj, x0_re, x0_im, ws, *, row0, ucol, seq, w_b, nb=64):
    batch, gp = x0_re.shape
    n_slabs = gp // LANES
    assert batch % nb == 0 and row0 % (nb * seq) == 0
    t0 = row0 // (nb * seq)
    return pl.pallas_call(
        functools.partial(_s5_sample_kernel, seq=seq),
        grid=(batch // nb,),
        in_specs=[
            pl.BlockSpec((nb * seq, w_b), lambda p: (t0 + p, ucol)),
            pl.BlockSpec((nb, gp), lambda p: (p, 0)),
            pl.BlockSpec((nb, gp), lambda p: (p, 0)),
        ] + _s5_weights_specs(ws),
        out_specs=[
            pl.BlockSpec((nb * seq, w_b), lambda p: (p, 0)),
            pl.BlockSpec((nb, gp), lambda p: (p, 0)),
            pl.BlockSpec((nb, gp), lambda p: (p, 0)),
        ],
        out_shape=[
            jax.ShapeDtypeStruct((batch * seq, w_b), F32),
            jax.ShapeDtypeStruct((batch, gp), F32),
            jax.ShapeDtypeStruct((batch, gp), F32),
        ],
        scratch_shapes=[pltpu.VMEM((n_slabs, nb * seq, LANES), F32), pltpu.VMEM((n_slabs, nb * seq, LANES), F32)],
        compiler_params=_params("arbitrary"),
        name="s5_sample",
    )(proj, x0_re, x0_im, *ws)


def _mix_gates(o_ref, gate_ref, y_ref, ng_ref, gluw_ref, glub_ref, a_ref, src, dst):
    w_a = o_ref.shape[1]
    o = o_ref[src, :]
    var = jnp.mean(o * o, axis=-1, keepdims=True)
    gate = gate_ref[src, :]
    a_ref[dst, :w_a] = (o * lax.rsqrt(var + EPS) * ng_ref[...] * (gate * jax.nn.sigmoid(gate))).astype(BF16)
    y = jax.nn.gelu(y_ref[src, :])
    z = _dot(y.astype(BF16), gluw_ref[...]) + glub_ref[...]
    a_ref[dst, w_a:] = (y * jax.nn.sigmoid(z)).astype(BF16)


def _mix_kernel(o_ref, gate_ref, y_ref, x_ref, ng_ref, gluw_ref, glub_ref, wout_ref, out_ref, *rest,
                n_tiles, n_cols):
    *copies, a0_ref, a1_ref = rest
    glu16_ref, wout16_ref = copies or (None, None)
    i = pl.program_id(0)
    j = pl.program_id(1)
    rows_per_step = a0_ref.shape[0] // n_cols

    if glu16_ref is not None:
        @pl.when(jnp.logical_and(i == 0, j == 0))
        def _():
            _bf16_weights(gluw_ref, glu16_ref)
        gluw_ref = glu16_ref

    def step(a_gate, a_proj):
        pieces = out_ref.shape[1] // MXU_DIM
        sub = rows_per_step // pieces
        for p in range(pieces):
            cols = (slice(None), slice(p * MXU_DIM, (p + 1) * MXU_DIM))
            if a_gate is not None:
                dst = pl.ds(pl.multiple_of(j * rows_per_step + p * sub, sub), sub)
                _mix_gates(o_ref, gate_ref, y_ref, ng_ref, gluw_ref, glub_ref, a_gate,
                           slice(p * sub, (p + 1) * sub), dst)
            if a_proj is not None:
                w = _bf16_weights(wout_ref, wout16_ref, cols, first_row=1)
                out_ref[cols] = x_ref[cols] + _dot(a_proj[...], w)

    _by_parity(i, n_tiles, step, a0_ref, a1_ref)


def _mix(o, proj, y, x, norm_g, glu_w, glu_b, w_out, *, gate_col, name, tn=512):
    m, w_a = o.shape
    w_b = y.shape[1]
    d = w_out.shape[1]
    tm = _row_tile(m)
    nj = d // tn
    assert m % tm == 0 and d % tn == 0 and tm % (nj * SUBLANES) == 0
    n = m // tm
    nxt, cur, col, row_slice = _lead_in_maps(n, nj)
    const = lambda i, j: (0, 0)
    cur_tile = lambda i, j: (cur(i), col(i, j))
    glu_spec = ((w_b, w_b), const)
    wout_spec = ((w_a + w_b, tn), lambda i, j: (0, col(i, j)))
    assert glu_w.dtype == w_out.dtype
    copy_specs, copy_shapes = (a + b for a, b in zip(_weight_copy(glu_w, *glu_spec, (0, 0)),
                                                     _weight_copy(w_out, *wout_spec, (0, nj - 1), first_row=1)))
    return pl.pallas_call(
        functools.partial(_mix_kernel, n_tiles=n, n_cols=nj),
        grid=(n + 1, nj),
        in_specs=[
            pl.BlockSpec((tm // nj, w_a), lambda i, j: (row_slice(i, j), 0)),
            pl.BlockSpec((tm // nj, w_a), lambda i, j: (row_slice(i, j), gate_col)),
            pl.BlockSpec((tm // nj, w_b), lambda i, j: (row_slice(i, j), 0)),
            pl.BlockSpec((tm, tn), cur_tile),
            pl.BlockSpec((1, w_a), const),
            pl.BlockSpec(*glu_spec),
            pl.BlockSpec((1, w_b), const),
            pl.BlockSpec(*wout_spec),
        ],
        out_specs=[pl.BlockSpec((tm, tn), cur_tile)] + copy_specs,
        out_shape=[jax.ShapeDtypeStruct((m, d), F32)] + copy_shapes,
        scratch_shapes=[pltpu.VMEM((tm, w_a + w_b), BF16)] * 2,
        compiler_params=_params("arbitrary", "arbitrary"),
        name=name,
    )(o, proj, y, x, norm_g.reshape(1, w_a), glu_w, glu_b.reshape(1, w_b), w_out)


def _mlp_kernel(x_ref, g_ref, up_ref, down_ref, gf_ref, o_ref, *rest, rows, final_norm):
    *copies, h_ref = rest
    up16_ref, down16_ref = copies or (None, None)
    f = pl.program_id(1)

    @pl.when(f == 0)
    def _():
        _rmsnorm_rows(x_ref, g_ref[...], h_ref, rows)
        o_ref[...] = x_ref[...]

    a = jnp.square(jnp.maximum(_dot(h_ref[...], _bf16_weights(up_ref, up16_ref)), 0.0)).astype(BF16)
    o_ref[...] += _dot(a, _bf16_weights(down_ref, down16_ref))

    if final_norm:
        @pl.when(f == pl.num_programs(1) - 1)
        def _():
            _rmsnorm_rows(o_ref, gf_ref[...], o_ref, rows)


def _mlp(x, g, up, down, gf, *, final_norm, name, tf=512):
    m, d = x.shape
    dff = up.shape[1]
    tm = _row_tile(m)
    assert m % tm == 0 and dff % tf == 0 and up.dtype == down.dtype
    up_spec = ((d, tf), lambda i, f: (0, f))
    down_spec = ((tf, d), lambda i, f: (f, 0))
    nf = dff // tf
    copy_specs, copy_shapes = (a + b for a, b in zip(_weight_copy(up, *up_spec, (0, nf - 1)),
                                                     _weight_copy(down, *down_spec, (nf - 1, 0))))
    return pl.pallas_call(
        functools.partial(_mlp_kernel, rows=tm, final_norm=final_norm),
        grid=(m // tm, dff // tf),
        in_specs=[
            pl.BlockSpec((tm, d), lambda i, f: (i, 0)),
            pl.BlockSpec((1, d), lambda i, f: (0, 0)),
            pl.BlockSpec(*up_spec),
            pl.BlockSpec(*down_spec),
            pl.BlockSpec((1, d), lambda i, f: (0, 0)),
        ],
        out_specs=[pl.BlockSpec((tm, d), lambda i, f: (i, 0))] + copy_specs,
        out_shape=[jax.ShapeDtypeStruct((m, d), F32)] + copy_shapes,
        scratch_shapes=[pltpu.VMEM((tm, d), BF16)],
        compiler_params=_params("arbitrary", "arbitrary"),
        name=name,
    )(x, g.reshape(1, d), up, down, gf.reshape(1, d))


def kernel(x_prompt, x_sample, state_hgrn, state_s5_re, state_s5_im, w_in, w_out, norm1_g, norm2_g, hgrn_lb_logits, hgrn_norm_g, s5_a_re, s5_a_im, s5_b_re, s5_b_im, s5_c_re, s5_c_im, s5_d, s5_log_step, glu_w, glu_b, mlp_up, mlp_down, final_norm_g):
    bp, seq, d = x_prompt.shape
    bs, dseq, _ = x_sample.shape
    depth = w_in.shape[0]
    w_a = hgrn_norm_g.shape[1]
    w_b = s5_d.shape[1]
    g_b, p_state = s5_a_re.shape[1], s5_a_re.shape[2]
    mp, ms = bp * seq, bs * dseq
    assert w_in.shape[2] == 4 * w_a + w_b and w_a == w_b and g_b * S5_GROUP == w_b
    gate_col, ucol = 3, (4 * w_a) // w_b

    xp = x_prompt.reshape(mp, d)
    xs = x_sample.reshape(ms, d)
    lb_logits = hgrn_lb_logits.astype(F32)
    new_states = []
    for l in range(depth):
        proj_s, w_in16 = _inproj(xs, norm1_g[l], w_in[l].astype(F32), name="inproj_sample")
        proj_p, = _inproj(xp, norm1_g[l], w_in16, name="inproj_prompt")

        o_p, sh_p = _hgrn_prompt(proj_p, lb_logits, layer=l, batch=bp, seq=seq, w_a=w_a)
        o_s, sh_s = _hgrn_sample(proj_s, lb_logits, state_hgrn, layer=l, row0=0, batch=bs, seq=dseq, w_a=w_a)

        ws = (*_s5_weights(*(w[l].astype(F32) for w in (s5_a_re, s5_a_im, s5_log_step, s5_b_re, s5_b_im,
                                                        s5_c_re, s5_c_im))),
              s5_d[l].astype(F32).reshape(1, w_b))
        y_p, sr_p, si_p = _s5_prompt(proj_p, ws, nb=bp, seq=seq, ucol=ucol, w_b=w_b)
        y_s, sr_s, si_s = _s5_sample(
            proj_s, state_s5_re[l].reshape(bs, g_b * p_state).astype(F32), state_s5_im[l].reshape(bs, g_b * p_state).astype(F32),
            ws, row0=0, ucol=ucol, seq=dseq, w_b=w_b)

        last = l == depth - 1
        x1_s, glu16, w_out16 = _mix(o_s, proj_s, y_s, xs, hgrn_norm_g[l], glu_w[l].astype(F32), glu_b[l],
                                    w_out[l].astype(F32), gate_col=gate_col, name="mix_sample")
        xs, up16, down16 = _mlp(x1_s, norm2_g[l], mlp_up[l].astype(F32), mlp_down[l].astype(F32), final_norm_g,
                                final_norm=last, name="mlp_sample")
        x1_p, = _mix(o_p, proj_p, y_p, xp, hgrn_norm_g[l], glu16, glu_b[l], w_out16, gate_col=gate_col,
                     name="mix_prompt")
        xp, = _mlp(x1_p, norm2_g[l], up16, down16, final_norm_g, final_norm=last, name="mlp_prompt")
        new_states.append((sh_p, sr_p.reshape(bp, g_b, p_state), si_p.reshape(bp, g_b, p_state),
                           sh_s[0], sr_s.reshape(bs, g_b, p_state), si_s.reshape(bs, g_b, p_state)))

    y_prompt = xp.reshape(bp, seq, d).astype(x_prompt.dtype)
    y_sample = xs.reshape(bs, dseq, d).astype(x_sample.dtype)
    stacked = [jnp.stack([st[i] for st in new_states]) for i in range(6)]
    return (y_prompt, y_sample, *stacked)
```

```python
import functools

import jax
import jax.numpy as jnp
import numpy as np
from jax import lax
from jax.experimental import pallas as pl
from jax.experimental.pallas import tpu as pltpu

F32 = jnp.float32
BF16 = jnp.bfloat16
EPS = 1e-6

LANES = 128
SUBLANES = 8
MXU_DIM = 256
VMEM_LIMIT = 56 * 1024 * 1024

HEAD_DIM = 128
S5_GROUP = 16
HGRN_CHUNK = 64
HGRN_HEAD_GROUP = 2
HGRN_ROUND_EVERY = 4
ROW_TILE = 1024
NEG_BIG = -1e30


def _dot(a, b):
    return jnp.dot(a, b, preferred_element_type=F32)


def _dot_nt(a, b):
    return lax.dot_general(a, b, (((1,), (1,)), ((), ())), preferred_element_type=F32)


def _dot_tn(a, b):
    return lax.dot_general(a, b, (((0,), (0,)), ((), ())), preferred_element_type=F32)


def _split3(x):
    hi = x.astype(BF16)
    r1 = x - hi.astype(F32)
    mid = r1.astype(BF16)
    lo = (r1 - mid.astype(F32)).astype(BF16)
    return hi, mid, lo


def _params(*sem):
    return pltpu.CompilerParams(dimension_semantics=sem, vmem_limit_bytes=VMEM_LIMIT)


def _rmsnorm_rows(src_ref, g, dst_ref, rows, chunk=256, copy_ref=None):
    def body(c, carry):
        r0 = pl.multiple_of(c * chunk, chunk)
        x = src_ref[pl.ds(r0, chunk), :]
        if copy_ref is not None:
            copy_ref[pl.ds(r0, chunk), :] = x
        var = jnp.mean(x * x, axis=-1, keepdims=True)
        dst_ref[pl.ds(r0, chunk), :] = (x * lax.rsqrt(var + EPS) * g).astype(dst_ref.dtype)
        return carry

    lax.fori_loop(0, rows // chunk, body, 0)


def _bf16_weights(w_ref, copy_ref, at=(Ellipsis,), first_row=0):
    w = w_ref[at]
    if copy_ref is None:
        return w
    w = w.astype(BF16)

    @pl.when(pl.program_id(0) == first_row)
    def _():
        copy_ref[at] = w

    return w


def _weight_copy(w, block, index_map, last_index, first_row=0):
    if w.dtype == BF16:
        return [], []

    def once(i, j):
        during = index_map(jnp.full_like(i, first_row), j)
        before = index_map(jnp.full_like(i, first_row), jnp.zeros_like(j))
        pick = lambda b, d, a: jnp.where(i < first_row, b, jnp.where(i == first_row, d, a))
        return jax.tree.map(pick, before, during, last_index)

    return [pl.BlockSpec(block, once)], [jax.ShapeDtypeStruct(w.shape, BF16)]


def _rider_specs(round_weights, steps, slots, step_of):
    def spec(w, axis):
        block = tuple(n // steps if a == axis else n for a, n in enumerate(w.shape))
        assert w.ndim == 2 and w.dtype == F32 and w.shape[axis] % steps == 0
        assert block[0] % (2 * SUBLANES * slots) == 0 and block[1] % LANES == 0
        return pl.BlockSpec(block, lambda *ids: tuple(step_of(*ids) if a == axis else 0 for a in range(2)))

    return [spec(w, axis) for w, axis in round_weights]


def _round_piece(w_refs, w16_refs, k, slots):
    for w_ref, w16_ref in zip(w_refs, w16_refs):
        n = w_ref.shape[0] // slots
        w16_ref[k * n : (k + 1) * n, :] = w_ref[k * n : (k + 1) * n, :].astype(BF16)


def _lead_in_maps(n, nj):
    nxt = lambda i: jnp.minimum(i, n - 1)
    cur = lambda i: jnp.maximum(i - 1, 0)
    col = lambda i, j: jnp.where(i > 0, j, 0)
    row_slice = lambda i, j: nxt(i) * nj + jnp.where(i < n, j, nj - 1)
    return nxt, cur, col, row_slice


def _by_parity(i, n_tiles, step, buf0, buf1):
    even = i % 2 == 0
    inner = jnp.logical_and(i > 0, i < n_tiles)
    pl.when(i == 0)(lambda: step(buf0, None))
    pl.when(jnp.logical_and(inner, even))(lambda: step(buf0, buf1))
    pl.when(jnp.logical_and(inner, jnp.logical_not(even)))(lambda: step(buf1, buf0))
    pl.when(jnp.logical_and(i == n_tiles, even))(lambda: step(None, buf1))
    pl.when(jnp.logical_and(i == n_tiles, jnp.logical_not(even)))(lambda: step(None, buf0))


def _inproj_kernel(x_ref, g_ref, w_ref, o_ref, *rest, n_tiles, n_cols):
    *w16_ref, h0_ref, h1_ref = rest
    j = pl.program_id(1)
    rows_per_step = h0_ref.shape[0] // n_cols
    rows = pl.ds(pl.multiple_of(j * rows_per_step, rows_per_step), rows_per_step)

    def step(h_norm, h_proj):
        if h_norm is not None:
            x = x_ref[...]
            var = jnp.mean(x * x, axis=-1, keepdims=True)
            h_norm[rows, :] = (x * lax.rsqrt(var + EPS) * g_ref[...]).astype(BF16)
        if h_proj is not None:
            o_ref[...] = _dot(h_proj[...], _bf16_weights(w_ref, *(w16_ref or [None]), first_row=1))

    _by_parity(pl.program_id(0), n_tiles, step, h0_ref, h1_ref)


def _row_tile(m):
    return ROW_TILE if m % ROW_TILE == 0 else ROW_TILE // 2


def _inproj(x, g, w, *, name, tn=1280):
    m, d = x.shape
    n_out = w.shape[1]
    tm = _row_tile(m)
    if w.dtype != BF16:
        tn //= 2
    nj = n_out // tn
    assert m % tm == 0 and n_out % tn == 0 and tm % (nj * SUBLANES) == 0
    n = m // tm
    nxt, cur, col, row_slice = _lead_in_maps(n, nj)
    w_spec = ((d, tn), lambda i, j: (0, col(i, j)))
    copy_specs, copy_shapes = _weight_copy(w, *w_spec, (0, nj - 1), first_row=1)
    return pl.pallas_call(
        functools.partial(_inproj_kernel, n_tiles=n, n_cols=nj),
        grid=(n + 1, nj),
        in_specs=[
            pl.BlockSpec((tm // nj, d), lambda i, j: (row_slice(i, j), 0)),
            pl.BlockSpec((1, d), lambda i, j: (0, 0)),
            pl.BlockSpec(*w_spec),
        ],
        out_specs=[pl.BlockSpec((tm, tn), lambda i, j: (cur(i), col(i, j)))] + copy_specs,
        out_shape=[jax.ShapeDtypeStruct((m, n_out), F32)] + copy_shapes,
        scratch_shapes=[pltpu.VMEM((tm, d), BF16)] * 2,
        compiler_params=_params("arbitrary", "arbitrary"),
        name=name,
    )(x, g.reshape(1, d), w)


def _lower_bound(logits, layer):
    m = jnp.max(logits, axis=0, keepdims=True)
    e = jnp.exp(logits - m)
    return jnp.sum(e[: layer + 1], axis=0, keepdims=True) / jnp.sum(e, axis=0, keepdims=True)


def _row_to_col(e_row):
    n = e_row.shape[1]
    hi, mid, lo = (p.astype(F32) for p in _split3(e_row))
    r = lax.broadcasted_iota(jnp.int32, (2 * SUBLANES, n), 0)
    pieces = jnp.where(r == 0, hi, jnp.where(r == 1, mid, jnp.where(r == 2, lo, 0.0)))
    return _dot_tn(pieces.astype(BF16), jnp.ones((2 * SUBLANES, LANES), BF16))


def _hgrn_gates(qp, fp, lb):
    q = qp * jax.nn.sigmoid(qp)
    f = lb + (1.0 - lb) * jax.nn.sigmoid(fp)
    return q, f, 1.0 - f, jnp.log(f)


def _hgrn_tables(c):
    t = np.arange(c)[:, None]
    j = np.arange(c)[None, :]
    sums, pairs = [j <= t], []
    bs = c
    while bs >= 2:
        hs = bs // 2
        ref = t - t % bs + hs - 1
        upper = t % bs >= hs
        sums.append(np.where(upper, (j > ref) & (j <= t), (j > t) & (j <= ref)))
        pairs.append((t // bs == j // bs) & upper & (j % bs < hs))
        bs = hs
    sums.append(j > t)
    pairs.append(t == j)
    sums = np.concatenate(sums, axis=0).astype(np.float32)
    return jnp.asarray(np.concatenate([sums] * 3, axis=1), BF16), jnp.asarray(np.stack(pairs), F32)


def _upper_q_lower_k(q, k, bs):
    c = q.shape[0]
    hs = bs // 2
    if hs % SUBLANES == 0:
        parts = []
        for r0 in range(0, c, bs):
            parts += [k[r0 : r0 + hs], q[r0 + hs : r0 + bs]]
        return jnp.concatenate(parts, axis=0)
    row = lax.broadcasted_iota(jnp.int32, (c, 1), 0)
    return jnp.where(row % bs >= hs, q, k)


def _hgrn_prep(qp, fp, v, lb, sums, levels):
    c = qp.shape[0]
    q = qp * jax.nn.sigmoid(qp)
    f = lb + (1.0 - lb) * jax.nn.sigmoid(fp)
    k = 1.0 - f
    e = jnp.exp2(_dot(sums, jnp.concatenate(_split3(jnp.log2(f)), axis=0)))
    e_b, e_rest = e[:c], e[(levels + 1) * c :]
    ys = [(_upper_q_lower_k(q, k, c >> l) * e[(l + 1) * c : (l + 2) * c]).astype(BF16) for l in range(levels)]
    return dict(ys=ys, q16=q.astype(BF16), k16=k.astype(BF16), v16=v.astype(BF16),
                qe=(q * e_b).astype(BF16), kd=(k * e_rest).astype(BF16), e_last=e_b[c - 1 : c])


def _hgrn_apply(prep, st_ref, pairs_ref, n_seq, heads, emit_between):
    levels = pairs_ref.shape[0] - 1
    pairs = [(n, h) for n in range(n_seq) for h in range(heads)]

    def head(n, h, name, l=None):
        x = prep[n, h // HGRN_HEAD_GROUP][name]
        x = x if l is None else x[l]
        lo = h % HGRN_HEAD_GROUP * HEAD_DIM
        return x[:, lo : lo + HEAD_DIM]

    scores = {}
    for i, (n, h) in enumerate(pairs):
        a = _dot_nt(head(n, h, "q16"), head(n, h, "k16")) * pairs_ref[levels]
        for l in range(levels):
            y = head(n, h, "ys", l)
            a = a + _dot_nt(y, y) * pairs_ref[l]
        scores[n, h] = a.astype(BF16)
        emit_between(i)
    outs = {}
    for n, h in pairs:
        st = st_ref[n, h]
        outs[n, h] = _dot_nt(head(n, h, "qe"), st.astype(BF16)) + _dot(scores[n, h], head(n, h, "v16"))
        st_ref[n, h] = head(n, h, "e_last") * st + _dot_tn(head(n, h, "v16"), head(n, h, "kd"))
    return [jnp.concatenate([outs[n, h] for h in range(heads)], axis=1) for n in range(n_seq)]


def _hgrn_prompt_kernel(q_ref, f_ref, v_ref, lbl_ref, sums_ref, pairs_ref, *rest, layer, heads, n_chunks):
    n_round = (len(rest) - 3) // 2
    w_refs, (o_ref, s_out_ref), w16_refs, st_ref = rest[:n_round], rest[n_round:n_round + 2], rest[n_round + 2:-1], rest[-1]
    t = pl.program_id(1)

    @pl.when(t == 0)
    def _():
        st_ref[...] = jnp.zeros_like(st_ref)

    lb = _lower_bound(lbl_ref[...], layer)
    n_seq = q_ref.shape[0]
    levels = pairs_ref.shape[0] - 1
    pieces = [(n, g) for n in range(n_seq) for g in range(heads // HGRN_HEAD_GROUP)]

    def prep(c, n, g):
        rows = slice(c * HGRN_CHUNK, (c + 1) * HGRN_CHUNK)
        cols = slice(g * HGRN_HEAD_GROUP * HEAD_DIM, (g + 1) * HGRN_HEAD_GROUP * HEAD_DIM)
        return _hgrn_prep(q_ref[n, rows, cols], f_ref[n, rows, cols], v_ref[n, rows, cols], lb[:, cols],
                          sums_ref[...], levels)

    nxt = {p: prep(0, *p) for p in pieces}
    for c in range(n_chunks):
        cur, nxt = nxt, {}

        def emit_between(i, c=c, nxt=nxt):
            if c + 1 < n_chunks and (i + 1) % HGRN_HEAD_GROUP == 0:
                p = pieces[i // HGRN_HEAD_GROUP]
                nxt[p] = prep(c + 1, *p)
            if (i + 1) % HGRN_ROUND_EVERY == 0:
                per_chunk = n_seq * heads // HGRN_ROUND_EVERY
                _round_piece(w_refs, w16_refs, c * per_chunk + i // HGRN_ROUND_EVERY, n_chunks * per_chunk)

        for n, o in enumerate(_hgrn_apply(cur, st_ref, pairs_ref, n_seq, heads, emit_between)):
            o_ref[n, c * HGRN_CHUNK : (c + 1) * HGRN_CHUNK, :] = o

    @pl.when(t == pl.num_programs(1) - 1)
    def _():
        for n in range(n_seq):
            for h in range(heads):
                s_out_ref[n, h] = st_ref[n, h].T


def _hgrn_prompt(proj, lb_logits, *, layer, batch, seq, w_a, round_weights=(), tt=256, n_seq=2):
    heads = w_a // HEAD_DIM
    assert seq % tt == 0 and tt % HGRN_CHUNK == 0 and batch % n_seq == 0
    sums, pairs = _hgrn_tables(HGRN_CHUNK)
    proj3 = proj.reshape(batch, seq, proj.shape[1])
    col = lambda c: pl.BlockSpec((n_seq, tt, w_a), lambda b, t: (b, t, c))
    const = lambda x: pl.BlockSpec(x.shape, lambda b, t: (0,) * x.ndim)
    nt = seq // tt
    slabs = _rider_specs(round_weights, batch // n_seq * nt, tt // HGRN_CHUNK * n_seq * heads // HGRN_ROUND_EVERY,
                         lambda b, t: b * nt + t)
    weights = [w for w, _ in round_weights]
    o, s, *rounded = pl.pallas_call(
        functools.partial(_hgrn_prompt_kernel, layer=layer, heads=heads, n_chunks=tt // HGRN_CHUNK),
        grid=(batch // n_seq, nt),
        in_specs=[col(0), col(1), col(2), const(lb_logits), const(sums), const(pairs)] + slabs,
        out_specs=[
            pl.BlockSpec((n_seq, tt, w_a), lambda b, t: (b, t, 0)),
            pl.BlockSpec((n_seq, heads, HEAD_DIM, HEAD_DIM), lambda b, t: (b, 0, 0, 0)),
        ] + slabs,
        out_shape=[
            jax.ShapeDtypeStruct((batch, seq, w_a), F32),
            jax.ShapeDtypeStruct((batch, heads, HEAD_DIM, HEAD_DIM), F32),
        ] + [jax.ShapeDtypeStruct(w.shape, BF16) for w in weights],
        scratch_shapes=[pltpu.VMEM((n_seq, heads, HEAD_DIM, HEAD_DIM), F32)],
        compiler_params=_params("arbitrary", "arbitrary"),
        name="hgrn_prompt",
    )(proj3, proj3, proj3, lb_logits, sums, pairs, *weights)
    return o.reshape(batch * seq, w_a), s, rounded


def _hgrn_sample_kernel(q_ref, f_ref, v_ref, lbl_ref, s_in_ref, o_ref, s_out_ref, *, layer, heads, seq):
    n_seq = SUBLANES // seq
    lb = _lower_bound(lbl_ref[...], layer)
    row = lax.broadcasted_iota(jnp.int32, (SUBLANES, 1), 0)
    pos = row % seq
    work = []
    for tile, h in ((tile, h) for tile in range(q_ref.shape[0] // SUBLANES) for h in range(heads)):
        cols = slice(h * HEAD_DIM, (h + 1) * HEAD_DIM)
        rows = slice(tile * SUBLANES, (tile + 1) * SUBLANES)
        q, _, k, logf = _hgrn_gates(q_ref[rows, cols], f_ref[rows, cols], lb[:, cols])
        v = v_ref[rows, cols]
        b = logf
        shift = 1
        while shift < seq:
            b = b + jnp.where(pos >= shift, pltpu.roll(b, shift, 0), 0.0)
            shift *= 2
        o = jnp.zeros((SUBLANES, HEAD_DIM), F32)
        for sl in range(SUBLANES):
            lo_t, hi_t = sl, (sl // seq + 1) * seq
            mask = jnp.logical_and(row >= lo_t, row < hi_t)
            z = q * jnp.exp(jnp.where(mask, b - b[sl : sl + 1], NEG_BIG)) * k[sl : sl + 1]
            o = o + jnp.sum(z, axis=-1, keepdims=True) * v[sl : sl + 1]
        per_seq = []
        for n in range(n_seq):
            own = jnp.logical_and(row >= n * seq, row < (n + 1) * seq)
            b_last = b[(n + 1) * seq - 1 : (n + 1) * seq, :]
            qe = jnp.where(own, q * jnp.exp(b), 0.0).astype(BF16)
            kd = jnp.where(own, k * jnp.exp(jnp.where(own, b_last - b, 0.0)), 0.0).astype(BF16)
            per_seq.append((tile * n_seq + n, qe, kd, jnp.exp(b_last)))
        work.append((rows, cols, h, o, v.astype(BF16), per_seq))
    for rows, cols, h, o, _, per_seq in work:
        for i, qe, _, _ in per_seq:
            o = o + _dot(qe, s_in_ref[0, i, h].astype(BF16))
        o_ref[rows, cols] = o
    for _, _, h, _, v16, per_seq in work:
        for i, _, kd, e_last in per_seq:
            s_out_ref[0, i, h] = _row_to_col(e_last) * s_in_ref[0, i, h] + _dot_tn(kd, v16)


def _hgrn_sample(proj, lb_logits, state, *, layer, row0, batch, seq, w_a, tiles=8):
    heads = w_a // HEAD_DIM
    rows = tiles * SUBLANES
    assert SUBLANES % seq == 0 and row0 % rows == 0 and (batch * seq) % rows == 0
    n_seq = rows // seq
    t0 = row0 // rows
    col = lambda c: pl.BlockSpec((rows, w_a), lambda p: (t0 + p, c))
    st = pl.BlockSpec((1, n_seq, heads, HEAD_DIM, HEAD_DIM), lambda p: (layer, p, 0, 0, 0))
    st_out = pl.BlockSpec((1, n_seq, heads, HEAD_DIM, HEAD_DIM), lambda p: (0, p, 0, 0, 0))
    return pl.pallas_call(
        functools.partial(_hgrn_sample_kernel, layer=layer, heads=heads, seq=seq),
        grid=(batch // n_seq,),
        in_specs=[col(0), col(1), col(2), pl.BlockSpec(lb_logits.shape, lambda p: (0, 0)), st],
        out_specs=[pl.BlockSpec((rows, w_a), lambda p: (p, 0)), st_out],
        out_shape=[
            jax.ShapeDtypeStruct((batch * seq, w_a), F32),
            jax.ShapeDtypeStruct((1, batch, heads, HEAD_DIM, HEAD_DIM), F32),
        ],
        compiler_params=_params("arbitrary"),
        name="hgrn_sample",
    )(proj, proj, proj, lb_logits, state)


def _s5_weights_kernel(are_ref, aim_ref, ls_ref, bre_ref, bim_ref, cre_ref, cim_ref, rep_ref, tile_ref, same_ref,
                       lre_ref, lim_ref, bbre_ref, bbim_ref, ccre_ref, ccim_ref):
    a_re = are_ref[...]
    a_im = aim_ref[...]
    dt = jnp.exp(ls_ref[...])
    mag = jnp.exp(a_re * dt)
    lam_re = mag * jnp.cos(a_im * dt)
    lam_im = mag * jnp.sin(a_im * dt)
    den = a_re * a_re + a_im * a_im
    nr, ni = lam_re - 1.0, lam_im
    r_re = (nr * a_re + ni * a_im) / den
    r_im = (ni * a_re - nr * a_im) / den
    lre_ref[...] = lam_re
    lim_ref[...] = lam_im
    b_re = bre_ref[...]
    b_im = bim_ref[...]
    same = same_ref[...]
    bbre_ref[0] = (_dot(rep_ref[...], (r_re * b_re - r_im * b_im).astype(BF16)) * same).astype(BF16)
    bbim_ref[0] = (_dot(rep_ref[...], (r_re * b_im + r_im * b_re).astype(BF16)) * same).astype(BF16)
    ccre_ref[0] = (_dot(cre_ref[...].astype(BF16), tile_ref[...]) * same).astype(BF16)
    ccim_ref[0] = (_dot(cim_ref[...].astype(BF16), tile_ref[...]) * same).astype(BF16)


def _s5_weights(a_re, a_im, log_step, b_re, b_im, c_re, c_im):
    g, p = a_re.shape
    c = b_re.shape[-1]
    gt = MXU_DIM // c
    nj, kc, kp = g // gt, gt * c, gt * p
    flat = lambda x: x.reshape(1, g * p)
    chan = lambda x: jnp.transpose(x, (2, 0, 1)).reshape(c, g * p)
    row_g, lane_g = np.arange(kc)[:, None] // c, np.arange(kp)[None, :] // p
    rep = jnp.asarray(np.arange(kc)[:, None] % c == np.arange(c)[None, :], BF16)
    tile = jnp.asarray(np.arange(p)[:, None] == np.arange(kp)[None, :] % p, BF16)
    same = jnp.asarray(row_g == lane_g, F32)
    lanes = lambda rows: pl.BlockSpec((rows, kp), lambda j: (0, j))
    const = lambda x: pl.BlockSpec(x.shape, lambda j: (0, 0))
    tiles = pl.BlockSpec((1, kc, kp), lambda j: (j, 0, 0))
    tiles_shape = jax.ShapeDtypeStruct((nj, kc, kp), BF16)
    lam_shape = jax.ShapeDtypeStruct((1, g * p), F32)
    lam_re, lam_im, *mats = pl.pallas_call(
        _s5_weights_kernel,
        grid=(nj,),
        in_specs=[lanes(1), lanes(1), lanes(1), lanes(c), lanes(c),
                  pl.BlockSpec((kc, p), lambda j: (j, 0)), pl.BlockSpec((kc, p), lambda j: (j, 0)),
                  const(rep), const(tile), const(same)],
        out_specs=[lanes(1), lanes(1), tiles, tiles, tiles, tiles],
        out_shape=[lam_shape, lam_shape] + [tiles_shape] * 4,
        compiler_params=_params("arbitrary"),
        name="s5_weights",
    )(flat(a_re), flat(a_im), flat(jnp.repeat(log_step, p)), chan(b_re), chan(b_im),
      c_re.reshape(g * c, p), c_im.reshape(g * c, p), rep, tile, same)
    return (lam_re.reshape(-1, 1, LANES), lam_im.reshape(-1, 1, LANES), *mats)


def _s5_lanes(x_ref, lo, hi):
    if len(x_ref.shape) == 2:
        return x_ref[:, lo:hi]
    return jnp.concatenate([x_ref[m] for m in range(lo // LANES, hi // LANES)], axis=1)


def _s5_in_tile(ub, j, bre_ref, bim_ref, xr_ref, xi_ref):
    _, kc, kp = bre_ref.shape
    uj = ub[:, j * kc : (j + 1) * kc]
    for x_ref, w_ref in ((xr_ref, bre_ref), (xi_ref, bim_ref)):
        bu = _dot(uj, w_ref[j])
        if len(x_ref.shape) == 2:
            x_ref[:, j * kp : (j + 1) * kp] = bu
        else:
            for m in range(kp // LANES):
                x_ref[j * kp // LANES + m] = bu[:, m * LANES : (m + 1) * LANES]


def _s5_in(ub, bre_ref, bim_ref, xr_ref, xi_ref):
    for j in range(bre_ref.shape[0]):
        _s5_in_tile(ub, j, bre_ref, bim_ref, xr_ref, xi_ref)


def _s5_out_tile(j, xr_ref, xi_ref, cre_ref, cim_ref):
    kp = cre_ref.shape[2]
    xr = _s5_lanes(xr_ref, j * kp, (j + 1) * kp).astype(BF16)
    xi = _s5_lanes(xi_ref, j * kp, (j + 1) * kp).astype(BF16)
    return _dot_nt(xr, cre_ref[j]) - _dot_nt(xi, cim_ref[j])


def _s5_out(xr_ref, xi_ref, cre_ref, cim_ref):
    return [_s5_out_tile(j, xr_ref, xi_ref, cre_ref, cim_ref) for j in range(cre_ref.shape[0])]


def _s5_step(lr, li, xr, xi, bur, bui):
    return lr * xr - li * xi + bur, lr * xi + li * xr + bui


def _s5_lambda(lre_ref, lim_ref, lo, hi, rows):
    cat = lambda ref: jnp.concatenate([ref[m] for m in range(lo // LANES, hi // LANES)], axis=1)
    return jnp.broadcast_to(cat(lre_ref), (rows, hi - lo)), jnp.broadcast_to(cat(lim_ref), (rows, hi - lo))


def _to_time_major(x):
    n, ts, w = x.shape
    return jnp.swapaxes(x, 0, 1).reshape(ts * n, w)


def _from_time_major(x, n):
    rows, w = x.shape
    return jnp.swapaxes(x.reshape(rows // n, n, w), 0, 1)


def _s5_prompt_kernel(*refs, nv):
    u_refs = refs[:nv]
    (lre_ref, lim_ref, bre_ref, bim_ref, cre_ref, cim_ref, d_ref,
     y_ref, xr_out_ref, xi_out_ref, xr_ref, xi_ref, sr_ref, si_ref) = refs[nv:]
    ts = u_refs[0].shape[0]
    step = pl.program_id(0)

    @pl.when(step == 0)
    def _():
        sr_ref[...] = jnp.zeros_like(sr_ref)
        si_ref[...] = jnp.zeros_like(si_ref)

    u = _to_time_major(jnp.stack([r[...] for r in u_refs], axis=0))
    ub = u.astype(BF16)
    nj, kc, kp = bre_ref.shape
    d = d_ref[...]

    def project(j):
        cols = slice(j * kc, (j + 1) * kc)
        y = _s5_out_tile(j, xr_ref, xi_ref, cre_ref, cim_ref)
        y_ref[:, :, cols] = _from_time_major(y + d[:, cols] * u[:, cols], nv)

    def b_piece(j, n, x_ref, w_ref):
        cols = slice(n * MXU_DIM, (n + 1) * MXU_DIM)
        x_ref[:, j * kp + n * MXU_DIM : j * kp + (n + 1) * MXU_DIM] = _dot(ub[:, j * kc : (j + 1) * kc], w_ref[j][:, cols])

    def scan_piece(j, state, r0, r1):
        ls = slice(j * kp, (j + 1) * kp)
        lr, li, xr, xi = state
        for r in range(r0, r1):
            rows = slice(r * nv, (r + 1) * nv)
            xr, xi = _s5_step(lr, li, xr, xi, xr_ref[rows, ls], xi_ref[rows, ls])
            xr_ref[rows, ls] = xr
            xi_ref[rows, ls] = xi
        return lr, li, xr, xi

    pieces = [(n, x_ref, w_ref) for n in range(kp // MXU_DIM) for x_ref, w_ref in ((xr_ref, bre_ref), (xi_ref, bim_ref))]
    per = ts // len(pieces)
    assert per * len(pieces) == ts
    for s in range(nj + 2):
        state = None
        if 0 <= s - 1 < nj:
            ls = slice((s - 1) * kp, s * kp)
            state = (*_s5_lambda(lre_ref, lim_ref, (s - 1) * kp, s * kp, nv), sr_ref[:, ls], si_ref[:, ls])
        for p, (n, x_ref, w_ref) in enumerate(pieces):
            if s < nj:
                b_piece(s, n, x_ref, w_ref)
            if state is not None:
                state = scan_piece(s - 1, state, p * per, (p + 1) * per)
        if state is not None:
            sr_ref[:, ls] = state[2]
            si_ref[:, ls] = state[3]
        if 0 <= s - 2 < nj:
            project(s - 2)

    @pl.when(step == pl.num_programs(0) - 1)
    def _():
        xr_out_ref[...] = sr_ref[...]
        xi_out_ref[...] = si_ref[...]


def _s5_carry_kernel(y_ref, fr_ref, fi_ref, lre_ref, lim_ref, cre_ref, cim_ref,
                     o_ref, xr_out_ref, xi_out_ref, zr_ref, zi_ref, sr_ref, si_ref, *, half_len, lane_block):
    nrow, gp = fr_ref.shape
    tf = y_ref.shape[2]
    step = pl.program_id(0)
    odd = lax.broadcasted_iota(jnp.int32, (nrow, 1), 0) % 2 == 1

    @pl.when(step == 0)
    def _():
        for lo in range(0, gp, lane_block):
            ls = slice(lo, lo + lane_block)
            pr, pi = _s5_lambda(lre_ref, lim_ref, lo, lo + lane_block, nrow)
            n = 1
            while n < half_len:
                pr, pi = pr * pr - pi * pi, 2.0 * pr * pi
                n *= 2
            fr, fi = fr_ref[:, ls], fi_ref[:, ls]
            fr = jnp.where(odd, pltpu.roll(fr, 1, 0), fr)
            fi = jnp.where(odd, pltpu.roll(fi, 1, 0), fi)
            sr_ref[:, ls] = jnp.where(odd, pr * fr - pi * fi, fr)
            si_ref[:, ls] = jnp.where(odd, pr * fi + pi * fr, fi)

    nj, kc, kp = cre_ref.shape
    nseq = y_ref.shape[0]
    for s in range(nj + 1):
        if s < nj:
            ls = slice(s * kp, (s + 1) * kp)
            lr, li = _s5_lambda(lre_ref, lim_ref, s * kp, (s + 1) * kp, nrow)
            zr, zi = sr_ref[:, ls], si_ref[:, ls]
            for r in range(tf):
                rows = slice(r * nrow, (r + 1) * nrow)
                zr, zi = lr * zr - li * zi, lr * zi + li * zr
                zr_ref[rows, ls] = zr
                zi_ref[rows, ls] = zi
            sr_ref[:, ls] = zr
            si_ref[:, ls] = zi
        if s > 0:
            cols = slice((s - 1) * kc, s * kc)
            y = _from_time_major(_s5_out_tile(s - 1, zr_ref, zi_ref, cre_ref, cim_ref), nrow)
            o_ref[:, :, :, cols] = y_ref[:, :, :, cols] + y.reshape(nseq, nrow // nseq, tf, kc)

    @pl.when(step == pl.num_programs(0) - 1)
    def _():
        xr_out_ref[...] = fr_ref[...] + sr_ref[...]
        xi_out_ref[...] = fi_ref[...] + si_ref[...]


def _s5_weights_specs(ws):
    zero = lambda n: (lambda *_: (0,) * n)
    return [pl.BlockSpec(w.shape, zero(w.ndim)) for w in ws]


def _s5_prompt(proj, ws, *, nb, seq, ucol, w_b, ts=64, tf=64, lane_block=1024):
    lam_re, lam_im, bre, bim, cre, cim, d = ws
    gp = lam_re.shape[0] * LANES
    nv = SUBLANES
    assert nv == 2 * nb, "two pieces per sequence"
    plen, half_len = seq // 2, seq // 4
    assert plen % ts == 0 and half_len % tf == 0 and gp % lane_block == 0
    assert half_len & (half_len - 1) == 0, "repeated squaring needs a power of two"
    nblk = plen // ts
    u_spec = lambda n: pl.BlockSpec((ts, w_b), lambda s: (n * nblk + s, ucol))
    state = pl.BlockSpec((nv, gp), lambda s: (0, 0))
    state_shape = jax.ShapeDtypeStruct((nv, gp), F32)
    scratch = lambda steps: [pltpu.VMEM((nv * steps, gp), F32)] * 2 + [pltpu.VMEM((nv, gp), F32)] * 2
    y, fr, fi = pl.pallas_call(
        functools.partial(_s5_prompt_kernel, nv=nv),
        grid=(nblk,),
        in_specs=[u_spec(n) for n in range(nv)] + _s5_weights_specs(ws),
        out_specs=[pl.BlockSpec((nv, ts, w_b), lambda s: (0, s, 0)), state, state],
        out_shape=[jax.ShapeDtypeStruct((nv, plen, w_b), F32), state_shape, state_shape],
        scratch_shapes=scratch(ts),
        compiler_params=_params("arbitrary"),
        name="s5_prompt",
    )(*([proj] * nv), *ws)

    y5 = y.reshape(nb, 2, 2, half_len, w_b)
    y_spec = pl.BlockSpec((nb, None, 2, tf, w_b), lambda s: (0, 1, 0, s, 0))
    y5, xr, xi = pl.pallas_call(
        functools.partial(_s5_carry_kernel, half_len=half_len, lane_block=lane_block),
        grid=(half_len // tf,),
        in_specs=[y_spec, state, state] + _s5_weights_specs((lam_re, lam_im, cre, cim)),
        out_specs=[y_spec, state, state],
        out_shape=[jax.ShapeDtypeStruct(y5.shape, F32), state_shape, state_shape],
        scratch_shapes=scratch(tf),
        input_output_aliases={0: 0},
        compiler_params=_params("arbitrary"),
        name="s5_prompt_carry",
    )(y5, fr, fi, lam_re, lam_im, cre, cim)
    last = lambda x: x.reshape(nb, 2, gp)[:, 1]
    return y5.reshape(nb * seq, w_b), last(xr), last(xi)


def _s5_sample_kernel(u_ref, x0r_ref, x0i_ref, lre_ref, lim_ref, bre_ref, bim_ref, cre_ref, cim_ref, d_ref,
                      y_ref, xr_out_ref, xi_out_ref, xr_ref, xi_ref, *, seq):
    nb = x0r_ref.shape[0]
    u = u_ref[...]
    _s5_in(u.astype(BF16), bre_ref, bim_ref, xr_ref, xi_ref)
    for m in range(xr_ref.shape[0]):
        ls = slice(m * LANES, (m + 1) * LANES)
        lr, li = lre_ref[m], lim_ref[m]
        xr, xi = x0r_ref[:, ls], x0i_ref[:, ls]
        for r in range(seq):
            rows = pl.ds(r, nb, stride=seq)
            xr, xi = _s5_step(lr, li, xr, xi, xr_ref[m, rows, :], xi_ref[m, rows, :])
            xr_ref[m, rows, :] = xr
            xi_ref[m, rows, :] = xi
        xr_out_ref[:, ls] = xr
        xi_out_ref[:, ls] = xi
    kc = cre_ref.shape[1]
    d = d_ref[...]
    for j, y in enumerate(_s5_out(xr_ref, xi_ref, cre_ref, cim_ref)):
        cols = slice(j * kc, (j + 1) * kc)
        y_ref[:, cols] = y + d[:, cols] * u[:, cols]


def _s5_sample(proj, x0_re, x0_im, ws, *, row0, ucol, seq, w_b, nb=64):
    batch, gp = x0_re.shape
    n_slabs = gp // LANES
    assert batch % nb == 0 and row0 % (nb * seq) == 0
    t0 = row0 // (nb * seq)
    return pl.pallas_call(
        functools.partial(_s5_sample_kernel, seq=seq),
        grid=(batch // nb,),
        in_specs=[
            pl.BlockSpec((nb * seq, w_b), lambda p: (t0 + p, ucol)),
            pl.BlockSpec((nb, gp), lambda p: (p, 0)),
            pl.BlockSpec((nb, gp), lambda p: (p, 0)),
        ] + _s5_weights_specs(ws),
        out_specs=[
            pl.BlockSpec((nb * seq, w_b), lambda p: (p, 0)),
            pl.BlockSpec((nb, gp), lambda p: (p, 0)),
            pl.BlockSpec((nb, gp), lambda p: (p, 0)),
        ],
        out_shape=[
            jax.ShapeDtypeStruct((batch * seq, w_b), F32),
            jax.ShapeDtypeStruct((batch, gp), F32),
            jax.ShapeDtypeStruct((batch, gp), F32),
        ],
        scratch_shapes=[pltpu.VMEM((n_slabs, nb * seq, LANES), F32), pltpu.VMEM((n_slabs, nb * seq, LANES), F32)],
        compiler_params=_params("arbitrary"),
        name="s5_sample",
    )(proj, x0_re, x0_im, *ws)


def _mix_gates(o_ref, gate_ref, y_ref, ng_ref, gluw_ref, glub_ref, a_ref, src, dst):
    w_a = o_ref.shape[1]
    o = o_ref[src, :]
    var = jnp.mean(o * o, axis=-1, keepdims=True)
    gate = gate_ref[src, :]
    a_ref[dst, :w_a] = (o * lax.rsqrt(var + EPS) * ng_ref[...] * (gate * jax.nn.sigmoid(gate))).astype(BF16)
    y = jax.nn.gelu(y_ref[src, :])
    z = _dot(y.astype(BF16), gluw_ref[...]) + glub_ref[...]
    a_ref[dst, w_a:] = (y * jax.nn.sigmoid(z)).astype(BF16)


def _mix_kernel(o_ref, gate_ref, y_ref, x_ref, ng_ref, gluw_ref, glub_ref, wout_ref, out_ref, *rest,
                n_tiles, n_cols):
    *copies, a0_ref, a1_ref = rest
    glu16_ref, wout16_ref = copies or (None, None)
    i = pl.program_id(0)
    j = pl.program_id(1)
    rows_per_step = a0_ref.shape[0] // n_cols

    if glu16_ref is not None:
        @pl.when(jnp.logical_and(i == 0, j == 0))
        def _():
            _bf16_weights(gluw_ref, glu16_ref)
        gluw_ref = glu16_ref

    def step(a_gate, a_proj):
        pieces = out_ref.shape[1] // MXU_DIM
        sub = rows_per_step // pieces
        for p in range(pieces):
            cols = (slice(None), slice(p * MXU_DIM, (p + 1) * MXU_DIM))
            if a_gate is not None:
                dst = pl.ds(pl.multiple_of(j * rows_per_step + p * sub, sub), sub)
                _mix_gates(o_ref, gate_ref, y_ref, ng_ref, gluw_ref, glub_ref, a_gate,
                           slice(p * sub, (p + 1) * sub), dst)
            if a_proj is not None:
                w = _bf16_weights(wout_ref, wout16_ref, cols, first_row=1)
                out_ref[cols] = x_ref[cols] + _dot(a_proj[...], w)

    _by_parity(i, n_tiles, step, a0_ref, a1_ref)


def _mix(o, proj, y, x, norm_g, glu_w, glu_b, w_out, *, gate_col, name, tn=512):
    m, w_a = o.shape
    w_b = y.shape[1]
    d = w_out.shape[1]
    tm = _row_tile(m)
    nj = d // tn
    assert m % tm == 0 and d % tn == 0 and tm % (nj * SUBLANES) == 0
    n = m // tm
    nxt, cur, col, row_slice = _lead_in_maps(n, nj)
    const = lambda i, j: (0, 0)
    cur_tile = lambda i, j: (cur(i), col(i, j))
    glu_spec = ((w_b, w_b), const)
    wout_spec = ((w_a + w_b, tn), lambda i, j: (0, col(i, j)))
    assert glu_w.dtype == w_out.dtype
    copy_specs, copy_shapes = (a + b for a, b in zip(_weight_copy(glu_w, *glu_spec, (0, 0)),
                                                     _weight_copy(w_out, *wout_spec, (0, nj - 1), first_row=1)))
    return pl.pallas_call(
        functools.partial(_mix_kernel, n_tiles=n, n_cols=nj),
        grid=(n + 1, nj),
        in_specs=[
            pl.BlockSpec((tm // nj, w_a), lambda i, j: (row_slice(i, j), 0)),
            pl.BlockSpec((tm // nj, w_a), lambda i, j: (row_slice(i, j), gate_col)),
            pl.BlockSpec((tm // nj, w_b), lambda i, j: (row_slice(i, j), 0)),
            pl.BlockSpec((tm, tn), cur_tile),
            pl.BlockSpec((1, w_a), const),
            pl.BlockSpec(*glu_spec),
            pl.BlockSpec((1, w_b), const),
            pl.BlockSpec(*wout_spec),
        ],
        out_specs=[pl.BlockSpec((tm, tn), cur_tile)] + copy_specs,
        out_shape=[jax.ShapeDtypeStruct((m, d), F32)] + copy_shapes,
        scratch_shapes=[pltpu.VMEM((tm, w_a + w_b), BF16)] * 2,
        compiler_params=_params("arbitrary", "arbitrary"),
        name=name,
    )(o, proj, y, x, norm_g.reshape(1, w_a), glu_w, glu_b.reshape(1, w_b), w_out)


def _mlp_kernel(x_ref, g_ref, up_ref, down_ref, gf_ref, o_ref, *rest, rows, final_norm):
    *copies, h_ref = rest
    up16_ref, down16_ref = copies or (None, None)
    f = pl.program_id(1)

    @pl.when(f == 0)
    def _():
        _rmsnorm_rows(x_ref, g_ref[...], h_ref, rows, copy_ref=o_ref)

    a = jnp.square(jnp.maximum(_dot(h_ref[...], _bf16_weights(up_ref, up16_ref)), 0.0)).astype(BF16)
    o_ref[...] += _dot(a, _bf16_weights(down_ref, down16_ref))

    if final_norm:
        @pl.when(f == pl.num_programs(1) - 1)
        def _():
            _rmsnorm_rows(o_ref, gf_ref[...], o_ref, rows)


def _mlp(x, g, up, down, gf, *, final_norm, name, tf=512):
    m, d = x.shape
    dff = up.shape[1]
    tm = _row_tile(m)
    assert m % tm == 0 and dff % tf == 0 and up.dtype == down.dtype
    up_spec = ((d, tf), lambda i, f: (0, f))
    down_spec = ((tf, d), lambda i, f: (f, 0))
    nf = dff // tf
    copy_specs, copy_shapes = (a + b for a, b in zip(_weight_copy(up, *up_spec, (0, nf - 1)),
                                                     _weight_copy(down, *down_spec, (nf - 1, 0))))
    return pl.pallas_call(
        functools.partial(_mlp_kernel, rows=tm, final_norm=final_norm),
        grid=(m // tm, dff // tf),
        in_specs=[
            pl.BlockSpec((tm, d), lambda i, f: (i, 0)),
            pl.BlockSpec((1, d), lambda i, f: (0, 0)),
            pl.BlockSpec(*up_spec),
            pl.BlockSpec(*down_spec),
            pl.BlockSpec((1, d), lambda i, f: (0, 0)),
        ],
        out_specs=[pl.BlockSpec((tm, d), lambda i, f: (i, 0))] + copy_specs,
        out_shape=[jax.ShapeDtypeStruct((m, d), F32)] + copy_shapes,
        scratch_shapes=[pltpu.VMEM((tm, d), BF16)],
        compiler_params=_params("arbitrary", "arbitrary"),
        name=name,
    )(x, g.reshape(1, d), up, down, gf.reshape(1, d))


def kernel(x_prompt, x_sample, state_hgrn, state_s5_re, state_s5_im, w_in, w_out, norm1_g, norm2_g, hgrn_lb_logits, hgrn_norm_g, s5_a_re, s5_a_im, s5_b_re, s5_b_im, s5_c_re, s5_c_im, s5_d, s5_log_step, glu_w, glu_b, mlp_up, mlp_down, final_norm_g):
    bp, seq, d = x_prompt.shape
    bs, dseq, _ = x_sample.shape
    depth = w_in.shape[0]
    w_a = hgrn_norm_g.shape[1]
    w_b = s5_d.shape[1]
    g_b, p_state = s5_a_re.shape[1], s5_a_re.shape[2]
    mp, ms = bp * seq, bs * dseq
    assert w_in.shape[2] == 4 * w_a + w_b and w_a == w_b and g_b * S5_GROUP == w_b
    gate_col, ucol = 3, (4 * w_a) // w_b

    xp = x_prompt.reshape(mp, d)
    xs = x_sample.reshape(ms, d)
    lb_logits = hgrn_lb_logits.astype(F32)
    new_states = []
    for l in range(depth):
        proj_s, w_in16 = _inproj(xs, norm1_g[l], w_in[l].astype(F32), name="inproj_sample")
        proj_p, = _inproj(xp, norm1_g[l], w_in16, name="inproj_prompt")

        o_p, sh_p, (up16, down16) = _hgrn_prompt(proj_p, lb_logits, layer=l, batch=bp, seq=seq, w_a=w_a,
                                                 round_weights=[(mlp_up[l].astype(F32), 1), (mlp_down[l].astype(F32), 0)])
        o_s, sh_s = _hgrn_sample(proj_s, lb_logits, state_hgrn, layer=l, row0=0, batch=bs, seq=dseq, w_a=w_a)

        ws = (*_s5_weights(*(w[l].astype(F32) for w in (s5_a_re, s5_a_im, s5_log_step, s5_b_re, s5_b_im,
                                                        s5_c_re, s5_c_im))),
              s5_d[l].astype(F32).reshape(1, w_b))
        y_p, sr_p, si_p = _s5_prompt(proj_p, ws, nb=bp, seq=seq, ucol=ucol, w_b=w_b)
        y_s, sr_s, si_s = _s5_sample(
            proj_s, state_s5_re[l].reshape(bs, g_b * p_state).astype(F32), state_s5_im[l].reshape(bs, g_b * p_state).astype(F32),
            ws, row0=0, ucol=ucol, seq=dseq, w_b=w_b)

        last = l == depth - 1
        x1_s, glu16, w_out16 = _mix(o_s, proj_s, y_s, xs, hgrn_norm_g[l], glu_w[l].astype(F32), glu_b[l],
                                    w_out[l].astype(F32), gate_col=gate_col, name="mix_sample")
        xs, = _mlp(x1_s, norm2_g[l], up16, down16, final_norm_g, final_norm=last, name="mlp_sample")
        x1_p, = _mix(o_p, proj_p, y_p, xp, hgrn_norm_g[l], glu16, glu_b[l], w_out16, gate_col=gate_col,
                     name="mix_prompt")
        xp, = _mlp(x1_p, norm2_g[l], up16, down16, final_norm_g, final_norm=last, name="mlp_prompt")
        new_states.append((sh_p, sr_p.reshape(bp, g_b, p_state), si_p.reshape(bp, g_b, p_state),
                           sh_s[0], sr_s.reshape(bs, g_b, p_state), si_s.reshape(bs, g_b, p_state)))

    y_prompt = xp.reshape(bp, seq, d).astype(x_prompt.dtype)
    y_sample = xs.reshape(bs, dseq, d).astype(x_sample.dtype)
    stacked = [jnp.stack([st[i] for st in new_states]) for i in range(6)]
    return (y_prompt, y_sample, *stacked)
```

```python
import functools

import jax
import jax.numpy as jnp
import numpy as np
from jax import lax
from jax.experimental import pallas as pl
from jax.experimental.pallas import tpu as pltpu

F32 = jnp.float32
BF16 = jnp.bfloat16
EPS = 1e-6

LANES = 128
SUBLANES = 8
MXU_DIM = 256
VMEM_LIMIT = 56 * 1024 * 1024

HEAD_DIM = 128
S5_GROUP = 16
HGRN_CHUNK = 64
HGRN_HEAD_GROUP = 2
HGRN_ROUND_EVERY = 4
MIX_SPARE_ROWS = 16
ROW_TILE = 1024
NEG_BIG = -1e30


def _dot(a, b):
    return jnp.dot(a, b, preferred_element_type=F32)


def _dot_nt(a, b):
    return lax.dot_general(a, b, (((1,), (1,)), ((), ())), preferred_element_type=F32)


def _dot_tn(a, b):
    return lax.dot_general(a, b, (((0,), (0,)), ((), ())), preferred_element_type=F32)


def _split3(x):
    hi = x.astype(BF16)
    r1 = x - hi.astype(F32)
    mid = r1.astype(BF16)
    lo = (r1 - mid.astype(F32)).astype(BF16)
    return hi, mid, lo


def _params(*sem):
    return pltpu.CompilerParams(dimension_semantics=sem, vmem_limit_bytes=VMEM_LIMIT)


def _rmsnorm_rows(src_ref, g, dst_ref, rows, chunk=256, copy_ref=None):
    def body(c, carry):
        r0 = pl.multiple_of(c * chunk, chunk)
        x = src_ref[pl.ds(r0, chunk), :]
        if copy_ref is not None:
            copy_ref[pl.ds(r0, chunk), :] = x
        var = jnp.mean(x * x, axis=-1, keepdims=True)
        dst_ref[pl.ds(r0, chunk), :] = (x * lax.rsqrt(var + EPS) * g).astype(dst_ref.dtype)
        return carry

    lax.fori_loop(0, rows // chunk, body, 0)


def _bf16_weights(w_ref, copy_ref, at=(Ellipsis,), first_row=0):
    w = w_ref[at]
    if copy_ref is None:
        return w
    w = w.astype(BF16)

    @pl.when(pl.program_id(0) == first_row)
    def _():
        copy_ref[at] = w

    return w


def _weight_copy(w, block, index_map, last_index, first_row=0):
    if w.dtype == BF16:
        return [], []

    def once(i, j):
        during = index_map(jnp.full_like(i, first_row), j)
        before = index_map(jnp.full_like(i, first_row), jnp.zeros_like(j))
        pick = lambda b, d, a: jnp.where(i < first_row, b, jnp.where(i == first_row, d, a))
        return jax.tree.map(pick, before, during, last_index)

    return [pl.BlockSpec(block, once)], [jax.ShapeDtypeStruct(w.shape, BF16)]


def _rider_specs(round_weights, steps, slots, step_of):
    def spec(w, axis):
        block = tuple(n // steps if a == axis else n for a, n in enumerate(w.shape))
        assert w.ndim == 2 and w.dtype == F32 and w.shape[axis] % steps == 0
        assert block[0] % (2 * SUBLANES * slots) == 0 and block[1] % LANES == 0
        return pl.BlockSpec(block, lambda *ids: tuple(step_of(*ids) if a == axis else 0 for a in range(2)))

    return [spec(w, axis) for w, axis in round_weights]


def _round_piece(w_refs, w16_refs, k, slots):
    for w_ref, w16_ref in zip(w_refs, w16_refs):
        n = w_ref.shape[0] // slots
        w16_ref[k * n : (k + 1) * n, :] = w_ref[k * n : (k + 1) * n, :].astype(BF16)


def _lead_in_maps(n, nj):
    nxt = lambda i: jnp.minimum(i, n - 1)
    cur = lambda i: jnp.maximum(i - 1, 0)
    col = lambda i, j: jnp.where(i > 0, j, 0)
    row_slice = lambda i, j: nxt(i) * nj + jnp.where(i < n, j, nj - 1)
    return nxt, cur, col, row_slice


def _by_parity(i, n_tiles, step, buf0, buf1):
    even = i % 2 == 0
    inner = jnp.logical_and(i > 0, i < n_tiles)
    pl.when(i == 0)(lambda: step(buf0, None))
    pl.when(jnp.logical_and(inner, even))(lambda: step(buf0, buf1))
    pl.when(jnp.logical_and(inner, jnp.logical_not(even)))(lambda: step(buf1, buf0))
    pl.when(jnp.logical_and(i == n_tiles, even))(lambda: step(None, buf1))
    pl.when(jnp.logical_and(i == n_tiles, jnp.logical_not(even)))(lambda: step(None, buf0))


def _inproj_kernel(x_ref, g_ref, w_ref, o_ref, *rest, n_tiles, n_cols):
    *w16_ref, h0_ref, h1_ref = rest
    j = pl.program_id(1)
    rows_per_step = h0_ref.shape[0] // n_cols
    rows = pl.ds(pl.multiple_of(j * rows_per_step, rows_per_step), rows_per_step)

    def step(h_norm, h_proj):
        if h_norm is not None:
            x = x_ref[...]
            var = jnp.mean(x * x, axis=-1, keepdims=True)
            h_norm[rows, :] = (x * lax.rsqrt(var + EPS) * g_ref[...]).astype(BF16)
        if h_proj is not None:
            o_ref[...] = _dot(h_proj[...], _bf16_weights(w_ref, *(w16_ref or [None]), first_row=1))

    _by_parity(pl.program_id(0), n_tiles, step, h0_ref, h1_ref)


def _row_tile(m):
    return ROW_TILE if m % ROW_TILE == 0 else ROW_TILE // 2


def _inproj(x, g, w, *, name, tn=1280):
    m, d = x.shape
    n_out = w.shape[1]
    tm = _row_tile(m)
    if w.dtype != BF16:
        tn //= 2
    nj = n_out // tn
    assert m % tm == 0 and n_out % tn == 0 and tm % (nj * SUBLANES) == 0
    n = m // tm
    nxt, cur, col, row_slice = _lead_in_maps(n, nj)
    w_spec = ((d, tn), lambda i, j: (0, col(i, j)))
    copy_specs, copy_shapes = _weight_copy(w, *w_spec, (0, nj - 1), first_row=1)
    return pl.pallas_call(
        functools.partial(_inproj_kernel, n_tiles=n, n_cols=nj),
        grid=(n + 1, nj),
        in_specs=[
            pl.BlockSpec((tm // nj, d), lambda i, j: (row_slice(i, j), 0)),
            pl.BlockSpec((1, d), lambda i, j: (0, 0)),
            pl.BlockSpec(*w_spec),
        ],
        out_specs=[pl.BlockSpec((tm, tn), lambda i, j: (cur(i), col(i, j)))] + copy_specs,
        out_shape=[jax.ShapeDtypeStruct((m, n_out), F32)] + copy_shapes,
        scratch_shapes=[pltpu.VMEM((tm, d), BF16)] * 2,
        compiler_params=_params("arbitrary", "arbitrary"),
        name=name,
    )(x, g.reshape(1, d), w)


def _lower_bound(logits, layer):
    m = jnp.max(logits, axis=0, keepdims=True)
    e = jnp.exp(logits - m)
    return jnp.sum(e[: layer + 1], axis=0, keepdims=True) / jnp.sum(e, axis=0, keepdims=True)


def _row_to_col(e_row):
    n = e_row.shape[1]
    hi, mid, lo = (p.astype(F32) for p in _split3(e_row))
    r = lax.broadcasted_iota(jnp.int32, (2 * SUBLANES, n), 0)
    pieces = jnp.where(r == 0, hi, jnp.where(r == 1, mid, jnp.where(r == 2, lo, 0.0)))
    return _dot_tn(pieces.astype(BF16), jnp.ones((2 * SUBLANES, LANES), BF16))


def _hgrn_gates(qp, fp, lb):
    q = qp * jax.nn.sigmoid(qp)
    f = lb + (1.0 - lb) * jax.nn.sigmoid(fp)
    return q, f, 1.0 - f, jnp.log(f)


def _hgrn_tables(c):
    t = np.arange(c)[:, None]
    j = np.arange(c)[None, :]
    sums, pairs = [j <= t], []
    bs = c
    while bs >= 2:
        hs = bs // 2
        ref = t - t % bs + hs - 1
        upper = t % bs >= hs
        sums.append(np.where(upper, (j > ref) & (j <= t), (j > t) & (j <= ref)))
        pairs.append((t // bs == j // bs) & upper & (j % bs < hs))
        bs = hs
    sums.append(j > t)
    pairs.append(t == j)
    sums = np.concatenate(sums, axis=0).astype(np.float32)
    return jnp.asarray(np.concatenate([sums] * 3, axis=1), BF16), jnp.asarray(np.stack(pairs), F32)


def _upper_q_lower_k(q, k, bs):
    c = q.shape[0]
    hs = bs // 2
    if hs % SUBLANES == 0:
        parts = []
        for r0 in range(0, c, bs):
            parts += [k[r0 : r0 + hs], q[r0 + hs : r0 + bs]]
        return jnp.concatenate(parts, axis=0)
    row = lax.broadcasted_iota(jnp.int32, (c, 1), 0)
    return jnp.where(row % bs >= hs, q, k)


def _hgrn_prep(qp, fp, v, lb, sums, levels):
    c = qp.shape[0]
    q = qp * jax.nn.sigmoid(qp)
    f = lb + (1.0 - lb) * jax.nn.sigmoid(fp)
    k = 1.0 - f
    e = jnp.exp2(_dot(sums, jnp.concatenate(_split3(jnp.log2(f)), axis=0)))
    e_b, e_rest = e[:c], e[(levels + 1) * c :]
    ys = [(_upper_q_lower_k(q, k, c >> l) * e[(l + 1) * c : (l + 2) * c]).astype(BF16) for l in range(levels)]
    return dict(ys=ys, q16=q.astype(BF16), k16=k.astype(BF16), v16=v.astype(BF16),
                qe=(q * e_b).astype(BF16), kd=(k * e_rest).astype(BF16), e_last=e_b[c - 1 : c])


def _hgrn_apply(prep, st_ref, pairs_ref, n_seq, heads, emit_between):
    levels = pairs_ref.shape[0] - 1
    pairs = [(n, h) for n in range(n_seq) for h in range(heads)]

    def head(n, h, name, l=None):
        x = prep[n, h // HGRN_HEAD_GROUP][name]
        x = x if l is None else x[l]
        lo = h % HGRN_HEAD_GROUP * HEAD_DIM
        return x[:, lo : lo + HEAD_DIM]

    scores = {}
    for i, (n, h) in enumerate(pairs):
        a = _dot_nt(head(n, h, "q16"), head(n, h, "k16")) * pairs_ref[levels]
        for l in range(levels):
            y = head(n, h, "ys", l)
            a = a + _dot_nt(y, y) * pairs_ref[l]
        scores[n, h] = a.astype(BF16)
        emit_between(i)
    outs = {}
    for n, h in pairs:
        st = st_ref[n, h]
        outs[n, h] = _dot_nt(head(n, h, "qe"), st.astype(BF16)) + _dot(scores[n, h], head(n, h, "v16"))
        st_ref[n, h] = head(n, h, "e_last") * st + _dot_tn(head(n, h, "v16"), head(n, h, "kd"))
    return [jnp.concatenate([outs[n, h] for h in range(heads)], axis=1) for n in range(n_seq)]


def _hgrn_prompt_kernel(q_ref, f_ref, v_ref, lbl_ref, sums_ref, pairs_ref, *rest, layer, heads, n_chunks):
    n_round = (len(rest) - 3) // 2
    w_refs, (o_ref, s_out_ref), w16_refs, st_ref = rest[:n_round], rest[n_round:n_round + 2], rest[n_round + 2:-1], rest[-1]
    t = pl.program_id(1)

    @pl.when(t == 0)
    def _():
        st_ref[...] = jnp.zeros_like(st_ref)

    lb = _lower_bound(lbl_ref[...], layer)
    n_seq = q_ref.shape[0]
    levels = pairs_ref.shape[0] - 1
    pieces = [(n, g) for n in range(n_seq) for g in range(heads // HGRN_HEAD_GROUP)]

    def prep(c, n, g):
        rows = slice(c * HGRN_CHUNK, (c + 1) * HGRN_CHUNK)
        cols = slice(g * HGRN_HEAD_GROUP * HEAD_DIM, (g + 1) * HGRN_HEAD_GROUP * HEAD_DIM)
        return _hgrn_prep(q_ref[n, rows, cols], f_ref[n, rows, cols], v_ref[n, rows, cols], lb[:, cols],
                          sums_ref[...], levels)

    nxt = {p: prep(0, *p) for p in pieces}
    for c in range(n_chunks):
        cur, nxt = nxt, {}

        def emit_between(i, c=c, nxt=nxt):
            if c + 1 < n_chunks and (i + 1) % HGRN_HEAD_GROUP == 0:
                p = pieces[i // HGRN_HEAD_GROUP]
                nxt[p] = prep(c + 1, *p)
            if (i + 1) % HGRN_ROUND_EVERY == 0:
                per_chunk = n_seq * heads // HGRN_ROUND_EVERY
                _round_piece(w_refs, w16_refs, c * per_chunk + i // HGRN_ROUND_EVERY, n_chunks * per_chunk)

        for n, o in enumerate(_hgrn_apply(cur, st_ref, pairs_ref, n_seq, heads, emit_between)):
            o_ref[n, c * HGRN_CHUNK : (c + 1) * HGRN_CHUNK, :] = o

    @pl.when(t == pl.num_programs(1) - 1)
    def _():
        for n in range(n_seq):
            for h in range(heads):
                s_out_ref[n, h] = st_ref[n, h].T


def _hgrn_prompt(proj, lb_logits, *, layer, batch, seq, w_a, round_weights=(), tt=256, n_seq=2):
    heads = w_a // HEAD_DIM
    assert seq % tt == 0 and tt % HGRN_CHUNK == 0 and batch % n_seq == 0
    sums, pairs = _hgrn_tables(HGRN_CHUNK)
    proj3 = proj.reshape(batch, seq, proj.shape[1])
    col = lambda c: pl.BlockSpec((n_seq, tt, w_a), lambda b, t: (b, t, c))
    const = lambda x: pl.BlockSpec(x.shape, lambda b, t: (0,) * x.ndim)
    nt = seq // tt
    slabs = _rider_specs(round_weights, batch // n_seq * nt, tt // HGRN_CHUNK * n_seq * heads // HGRN_ROUND_EVERY,
                         lambda b, t: b * nt + t)
    weights = [w for w, _ in round_weights]
    o, s, *rounded = pl.pallas_call(
        functools.partial(_hgrn_prompt_kernel, layer=layer, heads=heads, n_chunks=tt // HGRN_CHUNK),
        grid=(batch // n_seq, nt),
        in_specs=[col(0), col(1), col(2), const(lb_logits), const(sums), const(pairs)] + slabs,
        out_specs=[
            pl.BlockSpec((n_seq, tt, w_a), lambda b, t: (b, t, 0)),
            pl.BlockSpec((n_seq, heads, HEAD_DIM, HEAD_DIM), lambda b, t: (b, 0, 0, 0)),
        ] + slabs,
        out_shape=[
            jax.ShapeDtypeStruct((batch, seq, w_a), F32),
            jax.ShapeDtypeStruct((batch, heads, HEAD_DIM, HEAD_DIM), F32),
        ] + [jax.ShapeDtypeStruct(w.shape, BF16) for w in weights],
        scratch_shapes=[pltpu.VMEM((n_seq, heads, HEAD_DIM, HEAD_DIM), F32)],
        compiler_params=_params("arbitrary", "arbitrary"),
        name="hgrn_prompt",
    )(proj3, proj3, proj3, lb_logits, sums, pairs, *weights)
    return o.reshape(batch * seq, w_a), s, rounded


def _hgrn_sample_kernel(q_ref, f_ref, v_ref, lbl_ref, s_in_ref, o_ref, s_out_ref, *, layer, heads, seq):
    n_seq = SUBLANES // seq
    lb = _lower_bound(lbl_ref[...], layer)
    row = lax.broadcasted_iota(jnp.int32, (SUBLANES, 1), 0)
    pos = row % seq
    work = []
    for tile, h in ((tile, h) for tile in range(q_ref.shape[0] // SUBLANES) for h in range(heads)):
        cols = slice(h * HEAD_DIM, (h + 1) * HEAD_DIM)
        rows = slice(tile * SUBLANES, (tile + 1) * SUBLANES)
        q, _, k, logf = _hgrn_gates(q_ref[rows, cols], f_ref[rows, cols], lb[:, cols])
        v = v_ref[rows, cols]
        b = logf
        shift = 1
        while shift < seq:
            b = b + jnp.where(pos >= shift, pltpu.roll(b, shift, 0), 0.0)
            shift *= 2
        o = jnp.zeros((SUBLANES, HEAD_DIM), F32)
        for sl in range(SUBLANES):
            lo_t, hi_t = sl, (sl // seq + 1) * seq
            mask = jnp.logical_and(row >= lo_t, row < hi_t)
            z = q * jnp.exp(jnp.where(mask, b - b[sl : sl + 1], NEG_BIG)) * k[sl : sl + 1]
            o = o + jnp.sum(z, axis=-1, keepdims=True) * v[sl : sl + 1]
        per_seq = []
        for n in range(n_seq):
            own = jnp.logical_and(row >= n * seq, row < (n + 1) * seq)
            b_last = b[(n + 1) * seq - 1 : (n + 1) * seq, :]
            qe = jnp.where(own, q * jnp.exp(b), 0.0).astype(BF16)
            kd = jnp.where(own, k * jnp.exp(jnp.where(own, b_last - b, 0.0)), 0.0).astype(BF16)
            per_seq.append((tile * n_seq + n, qe, kd, jnp.exp(b_last)))
        work.append((rows, cols, h, o, v.astype(BF16), per_seq))
    for rows, cols, h, o, _, per_seq in work:
        for i, qe, _, _ in per_seq:
            o = o + _dot(qe, s_in_ref[0, i, h].astype(BF16))
        o_ref[rows, cols] = o
    for _, _, h, _, v16, per_seq in work:
        for i, _, kd, e_last in per_seq:
            s_out_ref[0, i, h] = _row_to_col(e_last) * s_in_ref[0, i, h] + _dot_tn(kd, v16)


def _hgrn_sample(proj, lb_logits, state, *, layer, row0, batch, seq, w_a, tiles=8):
    heads = w_a // HEAD_DIM
    rows = tiles * SUBLANES
    assert SUBLANES % seq == 0 and row0 % rows == 0 and (batch * seq) % rows == 0
    n_seq = rows // seq
    t0 = row0 // rows
    col = lambda c: pl.BlockSpec((rows, w_a), lambda p: (t0 + p, c))
    st = pl.BlockSpec((1, n_seq, heads, HEAD_DIM, HEAD_DIM), lambda p: (layer, p, 0, 0, 0))
    st_out = pl.BlockSpec((1, n_seq, heads, HEAD_DIM, HEAD_DIM), lambda p: (0, p, 0, 0, 0))
    return pl.pallas_call(
        functools.partial(_hgrn_sample_kernel, layer=layer, heads=heads, seq=seq),
        grid=(batch // n_seq,),
        in_specs=[col(0), col(1), col(2), pl.BlockSpec(lb_logits.shape, lambda p: (0, 0)), st],
        out_specs=[pl.BlockSpec((rows, w_a), lambda p: (p, 0)), st_out],
        out_shape=[
            jax.ShapeDtypeStruct((batch * seq, w_a), F32),
            jax.ShapeDtypeStruct((1, batch, heads, HEAD_DIM, HEAD_DIM), F32),
        ],
        compiler_params=_params("arbitrary"),
        name="hgrn_sample",
    )(proj, proj, proj, lb_logits, state)


def _s5_weights_kernel(are_ref, aim_ref, ls_ref, bre_ref, bim_ref, cre_ref, cim_ref, rep_ref, tile_ref, same_ref,
                       lre_ref, lim_ref, bbre_ref, bbim_ref, ccre_ref, ccim_ref):
    a_re = are_ref[...]
    a_im = aim_ref[...]
    dt = jnp.exp(ls_ref[...])
    mag = jnp.exp(a_re * dt)
    lam_re = mag * jnp.cos(a_im * dt)
    lam_im = mag * jnp.sin(a_im * dt)
    den = a_re * a_re + a_im * a_im
    nr, ni = lam_re - 1.0, lam_im
    r_re = (nr * a_re + ni * a_im) / den
    r_im = (ni * a_re - nr * a_im) / den
    lre_ref[...] = lam_re
    lim_ref[...] = lam_im
    b_re = bre_ref[...]
    b_im = bim_ref[...]
    same = same_ref[...]
    bbre_ref[0] = (_dot(rep_ref[...], (r_re * b_re - r_im * b_im).astype(BF16)) * same).astype(BF16)
    bbim_ref[0] = (_dot(rep_ref[...], (r_re * b_im + r_im * b_re).astype(BF16)) * same).astype(BF16)
    ccre_ref[0] = (_dot(cre_ref[...].astype(BF16), tile_ref[...]) * same).astype(BF16)
    ccim_ref[0] = (_dot(cim_ref[...].astype(BF16), tile_ref[...]) * same).astype(BF16)


def _s5_weights(a_re, a_im, log_step, b_re, b_im, c_re, c_im):
    g, p = a_re.shape
    c = b_re.shape[-1]
    gt = MXU_DIM // c
    nj, kc, kp = g // gt, gt * c, gt * p
    flat = lambda x: x.reshape(1, g * p)
    chan = lambda x: jnp.transpose(x, (2, 0, 1)).reshape(c, g * p)
    row_g, lane_g = np.arange(kc)[:, None] // c, np.arange(kp)[None, :] // p
    rep = jnp.asarray(np.arange(kc)[:, None] % c == np.arange(c)[None, :], BF16)
    tile = jnp.asarray(np.arange(p)[:, None] == np.arange(kp)[None, :] % p, BF16)
    same = jnp.asarray(row_g == lane_g, F32)
    lanes = lambda rows: pl.BlockSpec((rows, kp), lambda j: (0, j))
    const = lambda x: pl.BlockSpec(x.shape, lambda j: (0, 0))
    tiles = pl.BlockSpec((1, kc, kp), lambda j: (j, 0, 0))
    tiles_shape = jax.ShapeDtypeStruct((nj, kc, kp), BF16)
    lam_shape = jax.ShapeDtypeStruct((1, g * p), F32)
    lam_re, lam_im, *mats = pl.pallas_call(
        _s5_weights_kernel,
        grid=(nj,),
        in_specs=[lanes(1), lanes(1), lanes(1), lanes(c), lanes(c),
                  pl.BlockSpec((kc, p), lambda j: (j, 0)), pl.BlockSpec((kc, p), lambda j: (j, 0)),
                  const(rep), const(tile), const(same)],
        out_specs=[lanes(1), lanes(1), tiles, tiles, tiles, tiles],
        out_shape=[lam_shape, lam_shape] + [tiles_shape] * 4,
        compiler_params=_params("arbitrary"),
        name="s5_weights",
    )(flat(a_re), flat(a_im), flat(jnp.repeat(log_step, p)), chan(b_re), chan(b_im),
      c_re.reshape(g * c, p), c_im.reshape(g * c, p), rep, tile, same)
    return (lam_re.reshape(-1, 1, LANES), lam_im.reshape(-1, 1, LANES), *mats)


def _s5_lanes(x_ref, lo, hi):
    if len(x_ref.shape) == 2:
        return x_ref[:, lo:hi]
    return jnp.concatenate([x_ref[m] for m in range(lo // LANES, hi // LANES)], axis=1)


def _s5_in_tile(ub, j, bre_ref, bim_ref, xr_ref, xi_ref):
    _, kc, kp = bre_ref.shape
    uj = ub[:, j * kc : (j + 1) * kc]
    for x_ref, w_ref in ((xr_ref, bre_ref), (xi_ref, bim_ref)):
        bu = _dot(uj, w_ref[j])
        if len(x_ref.shape) == 2:
            x_ref[:, j * kp : (j + 1) * kp] = bu
        else:
            for m in range(kp // LANES):
                x_ref[j * kp // LANES + m] = bu[:, m * LANES : (m + 1) * LANES]


def _s5_in(ub, bre_ref, bim_ref, xr_ref, xi_ref):
    for j in range(bre_ref.shape[0]):
        _s5_in_tile(ub, j, bre_ref, bim_ref, xr_ref, xi_ref)


def _s5_out_tile(j, xr_ref, xi_ref, cre_ref, cim_ref):
    kp = cre_ref.shape[2]
    xr = _s5_lanes(xr_ref, j * kp, (j + 1) * kp).astype(BF16)
    xi = _s5_lanes(xi_ref, j * kp, (j + 1) * kp).astype(BF16)
    return _dot_nt(xr, cre_ref[j]) - _dot_nt(xi, cim_ref[j])


def _s5_out(xr_ref, xi_ref, cre_ref, cim_ref):
    return [_s5_out_tile(j, xr_ref, xi_ref, cre_ref, cim_ref) for j in range(cre_ref.shape[0])]


def _s5_step(lr, li, xr, xi, bur, bui):
    return lr * xr - li * xi + bur, lr * xi + li * xr + bui


def _s5_lambda(lre_ref, lim_ref, lo, hi, rows):
    cat = lambda ref: jnp.concatenate([ref[m] for m in range(lo // LANES, hi // LANES)], axis=1)
    return jnp.broadcast_to(cat(lre_ref), (rows, hi - lo)), jnp.broadcast_to(cat(lim_ref), (rows, hi - lo))


def _to_time_major(x):
    n, ts, w = x.shape
    return jnp.swapaxes(x, 0, 1).reshape(ts * n, w)


def _from_time_major(x, n):
    rows, w = x.shape
    return jnp.swapaxes(x.reshape(rows // n, n, w), 0, 1)


def _s5_prompt_kernel(*refs, nv):
    u_refs = refs[:nv]
    (lre_ref, lim_ref, bre_ref, bim_ref, cre_ref, cim_ref, d_ref,
     y_ref, xr_out_ref, xi_out_ref, xr_ref, xi_ref, sr_ref, si_ref) = refs[nv:]
    ts = u_refs[0].shape[0]
    step = pl.program_id(0)

    @pl.when(step == 0)
    def _():
        sr_ref[...] = jnp.zeros_like(sr_ref)
        si_ref[...] = jnp.zeros_like(si_ref)

    u = _to_time_major(jnp.stack([r[...] for r in u_refs], axis=0))
    ub = u.astype(BF16)
    nj, kc, kp = bre_ref.shape
    d = d_ref[...]

    def project(j):
        cols = slice(j * kc, (j + 1) * kc)
        y = _s5_out_tile(j, xr_ref, xi_ref, cre_ref, cim_ref)
        y_ref[:, :, cols] = _from_time_major(y + d[:, cols] * u[:, cols], nv)

    def b_piece(j, n, x_ref, w_ref):
        cols = slice(n * MXU_DIM, (n + 1) * MXU_DIM)
        x_ref[:, j * kp + n * MXU_DIM : j * kp + (n + 1) * MXU_DIM] = _dot(ub[:, j * kc : (j + 1) * kc], w_ref[j][:, cols])

    def scan_piece(j, state, r0, r1):
        ls = slice(j * kp, (j + 1) * kp)
        lr, li, xr, xi = state
        for r in range(r0, r1):
            rows = slice(r * nv, (r + 1) * nv)
            xr, xi = _s5_step(lr, li, xr, xi, xr_ref[rows, ls], xi_ref[rows, ls])
            xr_ref[rows, ls] = xr
            xi_ref[rows, ls] = xi
        return lr, li, xr, xi

    pieces = [(n, x_ref, w_ref) for n in range(kp // MXU_DIM) for x_ref, w_ref in ((xr_ref, bre_ref), (xi_ref, bim_ref))]
    per = ts // len(pieces)
    assert per * len(pieces) == ts
    for s in range(nj + 2):
        state = None
        if 0 <= s - 1 < nj:
            ls = slice((s - 1) * kp, s * kp)
            state = (*_s5_lambda(lre_ref, lim_ref, (s - 1) * kp, s * kp, nv), sr_ref[:, ls], si_ref[:, ls])
        for p, (n, x_ref, w_ref) in enumerate(pieces):
            if s < nj:
                b_piece(s, n, x_ref, w_ref)
            if state is not None:
                state = scan_piece(s - 1, state, p * per, (p + 1) * per)
        if state is not None:
            sr_ref[:, ls] = state[2]
            si_ref[:, ls] = state[3]
        if 0 <= s - 2 < nj:
            project(s - 2)

    @pl.when(step == pl.num_programs(0) - 1)
    def _():
        xr_out_ref[...] = sr_ref[...]
        xi_out_ref[...] = si_ref[...]


def _s5_carry_kernel(y_ref, fr_ref, fi_ref, lre_ref, lim_ref, cre_ref, cim_ref,
                     o_ref, xr_out_ref, xi_out_ref, zr_ref, zi_ref, sr_ref, si_ref, *, half_len, lane_block):
    nrow, gp = fr_ref.shape
    tf = y_ref.shape[2]
    step = pl.program_id(0)
    odd = lax.broadcasted_iota(jnp.int32, (nrow, 1), 0) % 2 == 1

    @pl.when(step == 0)
    def _():
        for lo in range(0, gp, lane_block):
            ls = slice(lo, lo + lane_block)
            pr, pi = _s5_lambda(lre_ref, lim_ref, lo, lo + lane_block, nrow)
            n = 1
            while n < half_len:
                pr, pi = pr * pr - pi * pi, 2.0 * pr * pi
                n *= 2
            fr, fi = fr_ref[:, ls], fi_ref[:, ls]
            fr = jnp.where(odd, pltpu.roll(fr, 1, 0), fr)
            fi = jnp.where(odd, pltpu.roll(fi, 1, 0), fi)
            sr_ref[:, ls] = jnp.where(odd, pr * fr - pi * fi, fr)
            si_ref[:, ls] = jnp.where(odd, pr * fi + pi * fr, fi)

    nj, kc, kp = cre_ref.shape
    nseq = y_ref.shape[0]
    for s in range(nj + 1):
        if s < nj:
            ls = slice(s * kp, (s + 1) * kp)
            lr, li = _s5_lambda(lre_ref, lim_ref, s * kp, (s + 1) * kp, nrow)
            zr, zi = sr_ref[:, ls], si_ref[:, ls]
            for r in range(tf):
                rows = slice(r * nrow, (r + 1) * nrow)
                zr, zi = lr * zr - li * zi, lr * zi + li * zr
                zr_ref[rows, ls] = zr
                zi_ref[rows, ls] = zi
            sr_ref[:, ls] = zr
            si_ref[:, ls] = zi
        if s > 0:
            cols = slice((s - 1) * kc, s * kc)
            y = _from_time_major(_s5_out_tile(s - 1, zr_ref, zi_ref, cre_ref, cim_ref), nrow)
            o_ref[:, :, :, cols] = y_ref[:, :, :, cols] + y.reshape(nseq, nrow // nseq, tf, kc)

    @pl.when(step == pl.num_programs(0) - 1)
    def _():
        xr_out_ref[...] = fr_ref[...] + sr_ref[...]
        xi_out_ref[...] = fi_ref[...] + si_ref[...]


def _s5_weights_specs(ws):
    zero = lambda n: (lambda *_: (0,) * n)
    return [pl.BlockSpec(w.shape, zero(w.ndim)) for w in ws]


def _s5_prompt(proj, ws, *, nb, seq, ucol, w_b, ts=64, tf=64, lane_block=1024):
    lam_re, lam_im, bre, bim, cre, cim, d = ws
    gp = lam_re.shape[0] * LANES
    nv = SUBLANES
    assert nv == 2 * nb, "two pieces per sequence"
    plen, half_len = seq // 2, seq // 4
    assert plen % ts == 0 and half_len % tf == 0 and gp % lane_block == 0
    assert half_len & (half_len - 1) == 0, "repeated squaring needs a power of two"
    nblk = plen // ts
    u_spec = lambda n: pl.BlockSpec((ts, w_b), lambda s: (n * nblk + s, ucol))
    state = pl.BlockSpec((nv, gp), lambda s: (0, 0))
    state_shape = jax.ShapeDtypeStruct((nv, gp), F32)
    scratch = lambda steps: [pltpu.VMEM((nv * steps, gp), F32)] * 2 + [pltpu.VMEM((nv, gp), F32)] * 2
    y, fr, fi = pl.pallas_call(
        functools.partial(_s5_prompt_kernel, nv=nv),
        grid=(nblk,),
        in_specs=[u_spec(n) for n in range(nv)] + _s5_weights_specs(ws),
        out_specs=[pl.BlockSpec((nv, ts, w_b), lambda s: (0, s, 0)), state, state],
        out_shape=[jax.ShapeDtypeStruct((nv, plen, w_b), F32), state_shape, state_shape],
        scratch_shapes=scratch(ts),
        compiler_params=_params("arbitrary"),
        name="s5_prompt",
    )(*([proj] * nv), *ws)

    y5 = y.reshape(nb, 2, 2, half_len, w_b)
    y_spec = pl.BlockSpec((nb, None, 2, tf, w_b), lambda s: (0, 1, 0, s, 0))
    y5, xr, xi = pl.pallas_call(
        functools.partial(_s5_carry_kernel, half_len=half_len, lane_block=lane_block),
        grid=(half_len // tf,),
        in_specs=[y_spec, state, state] + _s5_weights_specs((lam_re, lam_im, cre, cim)),
        out_specs=[y_spec, state, state],
        out_shape=[jax.ShapeDtypeStruct(y5.shape, F32), state_shape, state_shape],
        scratch_shapes=scratch(tf),
        input_output_aliases={0: 0},
        compiler_params=_params("arbitrary"),
        name="s5_prompt_carry",
    )(y5, fr, fi, lam_re, lam_im, cre, cim)
    last = lambda x: x.reshape(nb, 2, gp)[:, 1]
    return y5.reshape(nb * seq, w_b), last(xr), last(xi)


def _s5_sample_kernel(u_ref, x0r_ref, x0i_ref, lre_ref, lim_ref, bre_ref, bim_ref, cre_ref, cim_ref, d_ref,
                      y_ref, xr_out_ref, xi_out_ref, xr_ref, xi_ref, *, seq):
    nb = x0r_ref.shape[0]
    u = u_ref[...]
    _s5_in(u.astype(BF16), bre_ref, bim_ref, xr_ref, xi_ref)
    for m in range(xr_ref.shape[0]):
        ls = slice(m * LANES, (m + 1) * LANES)
        lr, li = lre_ref[m], lim_ref[m]
        xr, xi = x0r_ref[:, ls], x0i_ref[:, ls]
        for r in range(seq):
            rows = pl.ds(r, nb, stride=seq)
            xr, xi = _s5_step(lr, li, xr, xi, xr_ref[m, rows, :], xi_ref[m, rows, :])
            xr_ref[m, rows, :] = xr
            xi_ref[m, rows, :] = xi
        xr_out_ref[:, ls] = xr
        xi_out_ref[:, ls] = xi
    kc = cre_ref.shape[1]
    d = d_ref[...]
    for j, y in enumerate(_s5_out(xr_ref, xi_ref, cre_ref, cim_ref)):
        cols = slice(j * kc, (j + 1) * kc)
        y_ref[:, cols] = y + d[:, cols] * u[:, cols]


def _s5_sample(proj, x0_re, x0_im, ws, *, row0, ucol, seq, w_b, nb=64):
    batch, gp = x0_re.shape
    n_slabs = gp // LANES
    assert batch % nb == 0 and row0 % (nb * seq) == 0
    t0 = row0 // (nb * seq)
    return pl.pallas_call(
        functools.partial(_s5_sample_kernel, seq=seq),
        grid=(batch // nb,),
        in_specs=[
            pl.BlockSpec((nb * seq, w_b), lambda p: (t0 + p, ucol)),
            pl.BlockSpec((nb, gp), lambda p: (p, 0)),
            pl.BlockSpec((nb, gp), lambda p: (p, 0)),
        ] + _s5_weights_specs(ws),
        out_specs=[
            pl.BlockSpec((nb * seq, w_b), lambda p: (p, 0)),
            pl.BlockSpec((nb, gp), lambda p: (p, 0)),
            pl.BlockSpec((nb, gp), lambda p: (p, 0)),
        ],
        out_shape=[
            jax.ShapeDtypeStruct((batch * seq, w_b), F32),
            jax.ShapeDtypeStruct((batch, gp), F32),
            jax.ShapeDtypeStruct((batch, gp), F32),
        ],
        scratch_shapes=[pltpu.VMEM((n_slabs, nb * seq, LANES), F32), pltpu.VMEM((n_slabs, nb * seq, LANES), F32)],
        compiler_params=_params("arbitrary"),
        name="s5_sample",
    )(proj, x0_re, x0_im, *ws)


def _mix_gates(o_ref, gate_ref, y_ref, ng_ref, gluw_ref, glub_ref, a_ref, src, dst):
    w_a = o_ref.shape[1]
    o = o_ref[src, :]
    var = jnp.mean(o * o, axis=-1, keepdims=True)
    gate = gate_ref[src, :]
    a_ref[dst, :w_a] = (o * lax.rsqrt(var + EPS) * ng_ref[...] * (gate * jax.nn.sigmoid(gate))).astype(BF16)
    y = jax.nn.gelu(y_ref[src, :])
    z = _dot(y.astype(BF16), gluw_ref[...]) + glub_ref[...]
    a_ref[dst, w_a:] = (y * jax.nn.sigmoid(z)).astype(BF16)


def _mix_kernel(o_ref, gate_ref, y_ref, x_ref, ng_ref, gluw_ref, glub_ref, wout_ref, out_ref, *rest,
                n_tiles, n_cols):
    *copies, a0_ref, a1_ref = rest
    glu16_ref, wout16_ref = copies or (None, None)
    i = pl.program_id(0)
    j = pl.program_id(1)
    rows_per_step = (a0_ref.shape[0] - MIX_SPARE_ROWS) // n_cols

    if glu16_ref is not None:
        @pl.when(jnp.logical_and(i == 0, j == 0))
        def _():
            _bf16_weights(gluw_ref, glu16_ref)
        gluw_ref = glu16_ref

    def step(a_gate, a_proj):
        w_a = o_ref.shape[1]
        tm = rows_per_step * n_cols
        start = pl.multiple_of(j * rows_per_step, rows_per_step)
        dst = pl.ds(start, rows_per_step)
        gated = {}

        def gelu_y():
            gated["y"] = jax.nn.gelu(y_ref[...])

        def glu_dot():
            gated["z"] = _dot(gated["y"].astype(BF16), gluw_ref[...]) + glub_ref[...]

        def glu_out():
            a_gate[dst, w_a:] = (gated["y"] * jax.nn.sigmoid(gated["z"])).astype(BF16)

        def head_out():
            o = o_ref[...]
            var = jnp.mean(o * o, axis=-1, keepdims=True)
            gate = gate_ref[...]
            a_gate[dst, :w_a] = (o * lax.rsqrt(var + EPS) * ng_ref[...] * (gate * jax.nn.sigmoid(gate))).astype(BF16)

        chunks = []
        if a_proj is not None:
            half = tm // 2
            weights = {}

            def chunk(p, h, last):
                cols = slice(p * MXU_DIM, (p + 1) * MXU_DIM)
                rows = slice(h * half, (h + 1) * half)
                if p not in weights:
                    weights[p] = _bf16_weights(wout_ref, wout16_ref, (slice(None), cols), first_row=1)
                if last and a_gate is not None:
                    anywhere = pl.multiple_of(start + jnp.minimum(i, 0) * MIX_SPARE_ROWS, MIX_SPARE_ROWS)
                    a_proj[tm : tm + MIX_SPARE_ROWS, :] = a_gate[pl.ds(anywhere, MIX_SPARE_ROWS), :]
                    lhs = a_proj[pl.ds(pl.multiple_of(h * half + jnp.minimum(i, 0), half), half), :]
                else:
                    lhs = a_proj[rows, :]
                out_ref[rows, cols] = x_ref[rows, cols] + _dot(lhs, weights[p])

            n = out_ref.shape[1] // MXU_DIM
            chunks = [functools.partial(chunk, p, h, (p, h) == (n - 1, 1)) for p in range(n) for h in range(2)]
        phases = [[gelu_y], [glu_dot, head_out], [glu_out]] if a_gate is not None else []
        for k in range(max(len(phases), len(chunks))):
            for phase in phases[k] if k < len(phases) else []:
                phase()
            if k < len(chunks):
                chunks[k]()

    _by_parity(i, n_tiles, step, a0_ref, a1_ref)


def _mix(o, proj, y, x, norm_g, glu_w, glu_b, w_out, *, gate_col, name, tn=512):
    m, w_a = o.shape
    w_b = y.shape[1]
    d = w_out.shape[1]
    tm = _row_tile(m)
    nj = d // tn
    assert m % tm == 0 and d % tn == 0 and tm % (nj * SUBLANES) == 0
    n = m // tm
    nxt, cur, col, row_slice = _lead_in_maps(n, nj)
    const = lambda i, j: (0, 0)
    cur_tile = lambda i, j: (cur(i), col(i, j))
    glu_spec = ((w_b, w_b), const)
    wout_spec = ((w_a + w_b, tn), lambda i, j: (0, col(i, j)))
    assert glu_w.dtype == w_out.dtype
    copy_specs, copy_shapes = (a + b for a, b in zip(_weight_copy(glu_w, *glu_spec, (0, 0)),
                                                     _weight_copy(w_out, *wout_spec, (0, nj - 1), first_row=1)))
    return pl.pallas_call(
        functools.partial(_mix_kernel, n_tiles=n, n_cols=nj),
        grid=(n + 1, nj),
        in_specs=[
            pl.BlockSpec((tm // nj, w_a), lambda i, j: (row_slice(i, j), 0)),
            pl.BlockSpec((tm // nj, w_a), lambda i, j: (row_slice(i, j), gate_col)),
            pl.BlockSpec((tm // nj, w_b), lambda i, j: (row_slice(i, j), 0)),
            pl.BlockSpec((tm, tn), cur_tile),
            pl.BlockSpec((1, w_a), const),
            pl.BlockSpec(*glu_spec),
            pl.BlockSpec((1, w_b), const),
            pl.BlockSpec(*wout_spec),
        ],
        out_specs=[pl.BlockSpec((tm, tn), cur_tile)] + copy_specs,
        out_shape=[jax.ShapeDtypeStruct((m, d), F32)] + copy_shapes,
        scratch_shapes=[pltpu.VMEM((tm + MIX_SPARE_ROWS, w_a + w_b), BF16)] * 2,
        compiler_params=_params("arbitrary", "arbitrary"),
        name=name,
    )(o, proj, y, x, norm_g.reshape(1, w_a), glu_w, glu_b.reshape(1, w_b), w_out)


def _mlp_kernel(x_ref, g_ref, up_ref, down_ref, gf_ref, o_ref, *rest, rows, final_norm):
    *copies, h_ref = rest
    up16_ref, down16_ref = copies or (None, None)
    f = pl.program_id(1)

    @pl.when(f == 0)
    def _():
        _rmsnorm_rows(x_ref, g_ref[...], h_ref, rows, copy_ref=o_ref)

    a = jnp.square(jnp.maximum(_dot(h_ref[...], _bf16_weights(up_ref, up16_ref)), 0.0)).astype(BF16)
    o_ref[...] += _dot(a, _bf16_weights(down_ref, down16_ref))

    if final_norm:
        @pl.when(f == pl.num_programs(1) - 1)
        def _():
            _rmsnorm_rows(o_ref, gf_ref[...], o_ref, rows)


def _mlp(x, g, up, down, gf, *, final_norm, name, tf=512):
    m, d = x.shape
    dff = up.shape[1]
    tm = _row_tile(m)
    assert m % tm == 0 and dff % tf == 0 and up.dtype == down.dtype
    up_spec = ((d, tf), lambda i, f: (0, f))
    down_spec = ((tf, d), lambda i, f: (f, 0))
    nf = dff // tf
    copy_specs, copy_shapes = (a + b for a, b in zip(_weight_copy(up, *up_spec, (0, nf - 1)),
                                                     _weight_copy(down, *down_spec, (nf - 1, 0))))
    return pl.pallas_call(
        functools.partial(_mlp_kernel, rows=tm, final_norm=final_norm),
        grid=(m // tm, dff // tf),
        in_specs=[
            pl.BlockSpec((tm, d), lambda i, f: (i, 0)),
            pl.BlockSpec((1, d), lambda i, f: (0, 0)),
            pl.BlockSpec(*up_spec),
            pl.BlockSpec(*down_spec),
            pl.BlockSpec((1, d), lambda i, f: (0, 0)),
        ],
        out_specs=[pl.BlockSpec((tm, d), lambda i, f: (i, 0))] + copy_specs,
        out_shape=[jax.ShapeDtypeStruct((m, d), F32)] + copy_shapes,
        scratch_shapes=[pltpu.VMEM((tm, d), BF16)],
        compiler_params=_params("arbitrary", "arbitrary"),
        name=name,
    )(x, g.reshape(1, d), up, down, gf.reshape(1, d))


def kernel(x_prompt, x_sample, state_hgrn, state_s5_re, state_s5_im, w_in, w_out, norm1_g, norm2_g, hgrn_lb_logits, hgrn_norm_g, s5_a_re, s5_a_im, s5_b_re, s5_b_im, s5_c_re, s5_c_im, s5_d, s5_log_step, glu_w, glu_b, mlp_up, mlp_down, final_norm_g):
    bp, seq, d = x_prompt.shape
    bs, dseq, _ = x_sample.shape
    depth = w_in.shape[0]
    w_a = hgrn_norm_g.shape[1]
    w_b = s5_d.shape[1]
    g_b, p_state = s5_a_re.shape[1], s5_a_re.shape[2]
    mp, ms = bp * seq, bs * dseq
    assert w_in.shape[2] == 4 * w_a + w_b and w_a == w_b and g_b * S5_GROUP == w_b
    gate_col, ucol = 3, (4 * w_a) // w_b

    xp = x_prompt.reshape(mp, d)
    xs = x_sample.reshape(ms, d)
    lb_logits = hgrn_lb_logits.astype(F32)
    new_states = []
    for l in range(depth):
        proj_s, w_in16 = _inproj(xs, norm1_g[l], w_in[l].astype(F32), name="inproj_sample")
        proj_p, = _inproj(xp, norm1_g[l], w_in16, name="inproj_prompt")

        o_p, sh_p, (up16, down16) = _hgrn_prompt(proj_p, lb_logits, layer=l, batch=bp, seq=seq, w_a=w_a,
                                                 round_weights=[(mlp_up[l].astype(F32), 1), (mlp_down[l].astype(F32), 0)])
        o_s, sh_s = _hgrn_sample(proj_s, lb_logits, state_hgrn, layer=l, row0=0, batch=bs, seq=dseq, w_a=w_a)

        ws = (*_s5_weights(*(w[l].astype(F32) for w in (s5_a_re, s5_a_im, s5_log_step, s5_b_re, s5_b_im,
                                                        s5_c_re, s5_c_im))),
              s5_d[l].astype(F32).reshape(1, w_b))
        y_p, sr_p, si_p = _s5_prompt(proj_p, ws, nb=bp, seq=seq, ucol=ucol, w_b=w_b)
        y_s, sr_s, si_s = _s5_sample(
            proj_s, state_s5_re[l].reshape(bs, g_b * p_state).astype(F32), state_s5_im[l].reshape(bs, g_b * p_state).astype(F32),
            ws, row0=0, ucol=ucol, seq=dseq, w_b=w_b)

        last = l == depth - 1
        x1_s, glu16, w_out16 = _mix(o_s, proj_s, y_s, xs, hgrn_norm_g[l], glu_w[l].astype(F32), glu_b[l],
                                    w_out[l].astype(F32), gate_col=gate_col, name="mix_sample")
        xs, = _mlp(x1_s, norm2_g[l], up16, down16, final_norm_g, final_norm=last, name="mlp_sample")
        x1_p, = _mix(o_p, proj_p, y_p, xp, hgrn_norm_g[l], glu16, glu_b[l], w_out16, gate_col=gate_col,
                     name="mix_prompt")
        xp, = _mlp(x1_p, norm2_g[l], up16, down16, final_norm_g, final_norm=last, name="mlp_prompt")
        new_states.append((sh_p, sr_p.reshape(bp, g_b, p_state), si_p.reshape(bp, g_b, p_state),
                           sh_s[0], sr_s.reshape(bs, g_b, p_state), si_s.reshape(bs, g_b, p_state)))

    y_prompt = xp.reshape(bp, seq, d).astype(x_prompt.dtype)
    y_sample = xs.reshape(bs, dseq, d).astype(x_sample.dtype)
    stacked = [jnp.stack([st[i] for st in new_states]) for i in range(6)]
    return (y_prompt, y_sample, *stacked)
```

```python
import functools

import jax
import jax.numpy as jnp
import numpy as np
from jax import lax
from jax.experimental import pallas as pl
from jax.experimental.pallas import tpu as pltpu

F32 = jnp.float32
BF16 = jnp.bfloat16
EPS = 1e-6

LANES = 128
SUBLANES = 8
MXU_DIM = 256
VMEM_LIMIT = 56 * 1024 * 1024

HEAD_DIM = 128
S5_GROUP = 16
HGRN_CHUNK = 64
HGRN_HEAD_GROUP = 2
HGRN_ROUND_EVERY = 4
ROW_TILE = 1024
NEG_BIG = -1e30


def _dot(a, b):
    return jnp.dot(a, b, preferred_element_type=F32)


def _dot_nt(a, b):
    return lax.dot_general(a, b, (((1,), (1,)), ((), ())), preferred_element_type=F32)


def _dot_tn(a, b):
    return lax.dot_general(a, b, (((0,), (0,)), ((), ())), preferred_element_type=F32)


def _split3(x):
    hi = x.astype(BF16)
    r1 = x - hi.astype(F32)
    mid = r1.astype(BF16)
    lo = (r1 - mid.astype(F32)).astype(BF16)
    return hi, mid, lo


def _params(*sem):
    return pltpu.CompilerParams(dimension_semantics=sem, vmem_limit_bytes=VMEM_LIMIT)


def _rmsnorm_rows(src_ref, g, dst_ref, rows, chunk=256, copy_ref=None):
    def body(c, carry):
        r0 = pl.multiple_of(c * chunk, chunk)
        x = src_ref[pl.ds(r0, chunk), :]
        if copy_ref is not None:
            copy_ref[pl.ds(r0, chunk), :] = x
        var = jnp.mean(x * x, axis=-1, keepdims=True)
        dst_ref[pl.ds(r0, chunk), :] = (x * lax.rsqrt(var + EPS) * g).astype(dst_ref.dtype)
        return carry

    lax.fori_loop(0, rows // chunk, body, 0)


def _bf16_weights(w_ref, copy_ref, at=(Ellipsis,), first_row=0):
    w = w_ref[at]
    if copy_ref is None:
        return w
    w = w.astype(BF16)

    @pl.when(pl.program_id(0) == first_row)
    def _():
        copy_ref[at] = w

    return w


def _weight_copy(w, block, index_map, last_index, first_row=0):
    if w.dtype == BF16:
        return [], []

    def once(i, j):
        during = index_map(jnp.full_like(i, first_row), j)
        before = index_map(jnp.full_like(i, first_row), jnp.zeros_like(j))
        pick = lambda b, d, a: jnp.where(i < first_row, b, jnp.where(i == first_row, d, a))
        return jax.tree.map(pick, before, during, last_index)

    return [pl.BlockSpec(block, once)], [jax.ShapeDtypeStruct(w.shape, BF16)]


def _rider_specs(round_weights, steps, slots, step_of):
    def spec(w, axis):
        block = tuple(n // steps if a == axis else n for a, n in enumerate(w.shape))
        assert w.ndim == 2 and w.dtype == F32 and w.shape[axis] % steps == 0
        assert block[0] % (2 * SUBLANES * slots) == 0 and block[1] % LANES == 0
        return pl.BlockSpec(block, lambda *ids: tuple(step_of(*ids) if a == axis else 0 for a in range(2)))

    return [spec(w, axis) for w, axis in round_weights]


def _round_piece(w_refs, w16_refs, k, slots):
    for w_ref, w16_ref in zip(w_refs, w16_refs):
        n = w_ref.shape[0] // slots
        w16_ref[k * n : (k + 1) * n, :] = w_ref[k * n : (k + 1) * n, :].astype(BF16)


def _lead_in_maps(n, nj):
    nxt = lambda i: jnp.minimum(i, n - 1)
    cur = lambda i: jnp.maximum(i - 1, 0)
    col = lambda i, j: jnp.where(i > 0, j, 0)
    row_slice = lambda i, j: nxt(i) * nj + jnp.where(i < n, j, nj - 1)
    return nxt, cur, col, row_slice


def _by_parity(i, n_tiles, step, buf0, buf1):
    even = i % 2 == 0
    inner = jnp.logical_and(i > 0, i < n_tiles)
    pl.when(i == 0)(lambda: step(buf0, None))
    pl.when(jnp.logical_and(inner, even))(lambda: step(buf0, buf1))
    pl.when(jnp.logical_and(inner, jnp.logical_not(even)))(lambda: step(buf1, buf0))
    pl.when(jnp.logical_and(i == n_tiles, even))(lambda: step(None, buf1))
    pl.when(jnp.logical_and(i == n_tiles, jnp.logical_not(even)))(lambda: step(None, buf0))


def _inproj_kernel(x_ref, g_ref, w_ref, o_ref, *rest, n_tiles, n_cols):
    *w16_ref, h0_ref, h1_ref = rest
    j = pl.program_id(1)
    rows_per_step = h0_ref.shape[0] // n_cols
    rows = pl.ds(pl.multiple_of(j * rows_per_step, rows_per_step), rows_per_step)

    def step(h_norm, h_proj):
        if h_norm is not None:
            x = x_ref[...]
            var = jnp.mean(x * x, axis=-1, keepdims=True)
            h_norm[rows, :] = (x * lax.rsqrt(var + EPS) * g_ref[...]).astype(BF16)
        if h_proj is not None:
            o_ref[...] = _dot(h_proj[...], _bf16_weights(w_ref, *(w16_ref or [None]), first_row=1))

    _by_parity(pl.program_id(0), n_tiles, step, h0_ref, h1_ref)


def _row_tile(m):
    return ROW_TILE if m % ROW_TILE == 0 else ROW_TILE // 2


def _inproj(x, g, w, *, name, tn=1280):
    m, d = x.shape
    n_out = w.shape[1]
    tm = _row_tile(m)
    if w.dtype != BF16:
        tn //= 2
    nj = n_out // tn
    assert m % tm == 0 and n_out % tn == 0 and tm % (nj * SUBLANES) == 0
    n = m // tm
    nxt, cur, col, row_slice = _lead_in_maps(n, nj)
    w_spec = ((d, tn), lambda i, j: (0, col(i, j)))
    copy_specs, copy_shapes = _weight_copy(w, *w_spec, (0, nj - 1), first_row=1)
    return pl.pallas_call(
        functools.partial(_inproj_kernel, n_tiles=n, n_cols=nj),
        grid=(n + 1, nj),
        in_specs=[
            pl.BlockSpec((tm // nj, d), lambda i, j: (row_slice(i, j), 0)),
            pl.BlockSpec((1, d), lambda i, j: (0, 0)),
            pl.BlockSpec(*w_spec),
        ],
        out_specs=[pl.BlockSpec((tm, tn), lambda i, j: (cur(i), col(i, j)))] + copy_specs,
        out_shape=[jax.ShapeDtypeStruct((m, n_out), F32)] + copy_shapes,
        scratch_shapes=[pltpu.VMEM((tm, d), BF16)] * 2,
        compiler_params=_params("arbitrary", "arbitrary"),
        name=name,
    )(x, g.reshape(1, d), w)


def _lower_bound(logits, layer):
    m = jnp.max(logits, axis=0, keepdims=True)
    e = jnp.exp(logits - m)
    return jnp.sum(e[: layer + 1], axis=0, keepdims=True) / jnp.sum(e, axis=0, keepdims=True)


def _row_to_col(e_row):
    n = e_row.shape[1]
    hi, mid, lo = (p.astype(F32) for p in _split3(e_row))
    r = lax.broadcasted_iota(jnp.int32, (2 * SUBLANES, n), 0)
    pieces = jnp.where(r == 0, hi, jnp.where(r == 1, mid, jnp.where(r == 2, lo, 0.0)))
    return _dot_tn(pieces.astype(BF16), jnp.ones((2 * SUBLANES, LANES), BF16))


def _hgrn_gates(qp, fp, lb):
    q = qp * jax.nn.sigmoid(qp)
    f = lb + (1.0 - lb) * jax.nn.sigmoid(fp)
    return q, f, 1.0 - f, jnp.log(f)


def _hgrn_tables(c):
    t = np.arange(c)[:, None]
    j = np.arange(c)[None, :]
    sums, pairs = [j <= t], []
    bs = c
    while bs >= 2:
        hs = bs // 2
        ref = t - t % bs + hs - 1
        upper = t % bs >= hs
        sums.append(np.where(upper, (j > ref) & (j <= t), (j > t) & (j <= ref)))
        pairs.append((t // bs == j // bs) & upper & (j % bs < hs))
        bs = hs
    sums.append(j > t)
    pairs.append(t == j)
    sums = np.concatenate(sums, axis=0).astype(np.float32)
    return jnp.asarray(np.concatenate([sums] * 3, axis=1), BF16), jnp.asarray(np.stack(pairs), F32)


def _upper_q_lower_k(q, k, bs):
    c = q.shape[0]
    hs = bs // 2
    if hs % SUBLANES == 0:
        parts = []
        for r0 in range(0, c, bs):
            parts += [k[r0 : r0 + hs], q[r0 + hs : r0 + bs]]
        return jnp.concatenate(parts, axis=0)
    row = lax.broadcasted_iota(jnp.int32, (c, 1), 0)
    return jnp.where(row % bs >= hs, q, k)


def _hgrn_prep(qp, fp, v, lb, sums, levels):
    c = qp.shape[0]
    q = qp * jax.nn.sigmoid(qp)
    f = lb + (1.0 - lb) * jax.nn.sigmoid(fp)
    k = 1.0 - f
    e = jnp.exp2(_dot(sums, jnp.concatenate(_split3(jnp.log2(f)), axis=0)))
    e_b, e_rest = e[:c], e[(levels + 1) * c :]
    ys = [(_upper_q_lower_k(q, k, c >> l) * e[(l + 1) * c : (l + 2) * c]).astype(BF16) for l in range(levels)]
    return dict(ys=ys, q16=q.astype(BF16), k16=k.astype(BF16), v16=v.astype(BF16),
                qe=(q * e_b).astype(BF16), kd=(k * e_rest).astype(BF16), e_last=e_b[c - 1 : c])


def _hgrn_apply(prep, st_ref, pairs_ref, n_seq, heads, emit_between):
    levels = pairs_ref.shape[0] - 1
    pairs = [(n, h) for n in range(n_seq) for h in range(heads)]

    def head(n, h, name, l=None):
        x = prep[n, h // HGRN_HEAD_GROUP][name]
        x = x if l is None else x[l]
        lo = h % HGRN_HEAD_GROUP * HEAD_DIM
        return x[:, lo : lo + HEAD_DIM]

    scores = {}
    for i, (n, h) in enumerate(pairs):
        a = _dot_nt(head(n, h, "q16"), head(n, h, "k16")) * pairs_ref[levels]
        for l in range(levels):
            y = head(n, h, "ys", l)
            a = a + _dot_nt(y, y) * pairs_ref[l]
        scores[n, h] = a.astype(BF16)
        emit_between(i)
    outs = {}
    for n, h in pairs:
        st = st_ref[n, h]
        outs[n, h] = _dot_nt(head(n, h, "qe"), st.astype(BF16)) + _dot(scores[n, h], head(n, h, "v16"))
        st_ref[n, h] = head(n, h, "e_last") * st + _dot_tn(head(n, h, "v16"), head(n, h, "kd"))
    return [jnp.concatenate([outs[n, h] for h in range(heads)], axis=1) for n in range(n_seq)]


def _hgrn_prompt_kernel(q_ref, f_ref, v_ref, lbl_ref, sums_ref, pairs_ref, *rest, layer, heads, n_chunks):
    n_round = (len(rest) - 3) // 2
    w_refs, (o_ref, s_out_ref), w16_refs, st_ref = rest[:n_round], rest[n_round:n_round + 2], rest[n_round + 2:-1], rest[-1]
    t = pl.program_id(1)

    @pl.when(t == 0)
    def _():
        st_ref[...] = jnp.zeros_like(st_ref)

    lb = _lower_bound(lbl_ref[...], layer)
    n_seq = q_ref.shape[0]
    levels = pairs_ref.shape[0] - 1
    pieces = [(n, g) for n in range(n_seq) for g in range(heads // HGRN_HEAD_GROUP)]

    def prep(c, n, g):
        rows = slice(c * HGRN_CHUNK, (c + 1) * HGRN_CHUNK)
        cols = slice(g * HGRN_HEAD_GROUP * HEAD_DIM, (g + 1) * HGRN_HEAD_GROUP * HEAD_DIM)
        return _hgrn_prep(q_ref[n, rows, cols], f_ref[n, rows, cols], v_ref[n, rows, cols], lb[:, cols],
                          sums_ref[...], levels)

    nxt = {p: prep(0, *p) for p in pieces}
    for c in range(n_chunks):
        cur, nxt = nxt, {}

        def emit_between(i, c=c, nxt=nxt):
            if c + 1 < n_chunks and (i + 1) % HGRN_HEAD_GROUP == 0:
                p = pieces[i // HGRN_HEAD_GROUP]
                nxt[p] = prep(c + 1, *p)
            if (i + 1) % HGRN_ROUND_EVERY == 0:
                per_chunk = n_seq * heads // HGRN_ROUND_EVERY
                _round_piece(w_refs, w16_refs, c * per_chunk + i // HGRN_ROUND_EVERY, n_chunks * per_chunk)

        for n, o in enumerate(_hgrn_apply(cur, st_ref, pairs_ref, n_seq, heads, emit_between)):
            o_ref[n, c * HGRN_CHUNK : (c + 1) * HGRN_CHUNK, :] = o

    @pl.when(t == pl.num_programs(1) - 1)
    def _():
        for n in range(n_seq):
            for h in range(heads):
                s_out_ref[n, h] = st_ref[n, h].T


def _hgrn_prompt(proj, lb_logits, *, layer, batch, seq, w_a, round_weights=(), tt=256, n_seq=2):
    heads = w_a // HEAD_DIM
    assert seq % tt == 0 and tt % HGRN_CHUNK == 0 and batch % n_seq == 0
    sums, pairs = _hgrn_tables(HGRN_CHUNK)
    proj3 = proj.reshape(batch, seq, proj.shape[1])
    col = lambda c: pl.BlockSpec((n_seq, tt, w_a), lambda b, t: (b, t, c))
    const = lambda x: pl.BlockSpec(x.shape, lambda b, t: (0,) * x.ndim)
    nt = seq // tt
    slabs = _rider_specs(round_weights, batch // n_seq * nt, tt // HGRN_CHUNK * n_seq * heads // HGRN_ROUND_EVERY,
                         lambda b, t: b * nt + t)
    weights = [w for w, _ in round_weights]
    o, s, *rounded = pl.pallas_call(
        functools.partial(_hgrn_prompt_kernel, layer=layer, heads=heads, n_chunks=tt // HGRN_CHUNK),
        grid=(batch // n_seq, nt),
        in_specs=[col(0), col(1), col(2), const(lb_logits), const(sums), const(pairs)] + slabs,
        out_specs=[
            pl.BlockSpec((n_seq, tt, w_a), lambda b, t: (b, t, 0)),
            pl.BlockSpec((n_seq, heads, HEAD_DIM, HEAD_DIM), lambda b, t: (b, 0, 0, 0)),
        ] + slabs,
        out_shape=[
            jax.ShapeDtypeStruct((batch, seq, w_a), F32),
            jax.ShapeDtypeStruct((batch, heads, HEAD_DIM, HEAD_DIM), F32),
        ] + [jax.ShapeDtypeStruct(w.shape, BF16) for w in weights],
        scratch_shapes=[pltpu.VMEM((n_seq, heads, HEAD_DIM, HEAD_DIM), F32)],
        compiler_params=_params("arbitrary", "arbitrary"),
        name="hgrn_prompt",
    )(proj3, proj3, proj3, lb_logits, sums, pairs, *weights)
    return o.reshape(batch * seq, w_a), s, rounded


def _hgrn_sample_kernel(q_ref, f_ref, v_ref, lbl_ref, s_in_ref, o_ref, s_out_ref, *, layer, heads, seq):
    n_seq = SUBLANES // seq
    lb = _lower_bound(lbl_ref[...], layer)
    row = lax.broadcasted_iota(jnp.int32, (SUBLANES, 1), 0)
    pos = row % seq
    work = []
    for tile, h in ((tile, h) for tile in range(q_ref.shape[0] // SUBLANES) for h in range(heads)):
        cols = slice(h * HEAD_DIM, (h + 1) * HEAD_DIM)
        rows = slice(tile * SUBLANES, (tile + 1) * SUBLANES)
        q, _, k, logf = _hgrn_gates(q_ref[rows, cols], f_ref[rows, cols], lb[:, cols])
        v = v_ref[rows, cols]
        b = logf
        shift = 1
        while shift < seq:
            b = b + jnp.where(pos >= shift, pltpu.roll(b, shift, 0), 0.0)
            shift *= 2
        o = jnp.zeros((SUBLANES, HEAD_DIM), F32)
        for sl in range(SUBLANES):
            lo_t, hi_t = sl, (sl // seq + 1) * seq
            mask = jnp.logical_and(row >= lo_t, row < hi_t)
            z = q * jnp.exp(jnp.where(mask, b - b[sl : sl + 1], NEG_BIG)) * k[sl : sl + 1]
            o = o + jnp.sum(z, axis=-1, keepdims=True) * v[sl : sl + 1]
        per_seq = []
        for n in range(n_seq):
            own = jnp.logical_and(row >= n * seq, row < (n + 1) * seq)
            b_last = b[(n + 1) * seq - 1 : (n + 1) * seq, :]
            qe = jnp.where(own, q * jnp.exp(b), 0.0).astype(BF16)
            kd = jnp.where(own, k * jnp.exp(jnp.where(own, b_last - b, 0.0)), 0.0).astype(BF16)
            per_seq.append((tile * n_seq + n, qe, kd, jnp.exp(b_last)))
        work.append((rows, cols, h, o, v.astype(BF16), per_seq))
    for rows, cols, h, o, _, per_seq in work:
        for i, qe, _, _ in per_seq:
            o = o + _dot(qe, s_in_ref[0, i, h].astype(BF16))
        o_ref[rows, cols] = o
    for _, _, h, _, v16, per_seq in work:
        for i, _, kd, e_last in per_seq:
            s_out_ref[0, i, h] = _row_to_col(e_last) * s_in_ref[0, i, h] + _dot_tn(kd, v16)


def _hgrn_sample(proj, lb_logits, state, *, layer, row0, batch, seq, w_a, tiles=8):
    heads = w_a // HEAD_DIM
    rows = tiles * SUBLANES
    assert SUBLANES % seq == 0 and row0 % rows == 0 and (batch * seq) % rows == 0
    n_seq = rows // seq
    t0 = row0 // rows
    col = lambda c: pl.BlockSpec((rows, w_a), lambda p: (t0 + p, c))
    st = pl.BlockSpec((1, n_seq, heads, HEAD_DIM, HEAD_DIM), lambda p: (layer, p, 0, 0, 0))
    st_out = pl.BlockSpec((1, n_seq, heads, HEAD_DIM, HEAD_DIM), lambda p: (0, p, 0, 0, 0))
    return pl.pallas_call(
        functools.partial(_hgrn_sample_kernel, layer=layer, heads=heads, seq=seq),
        grid=(batch // n_seq,),
        in_specs=[col(0), col(1), col(2), pl.BlockSpec(lb_logits.shape, lambda p: (0, 0)), st],
        out_specs=[pl.BlockSpec((rows, w_a), lambda p: (p, 0)), st_out],
        out_shape=[
            jax.ShapeDtypeStruct((batch * seq, w_a), F32),
            jax.ShapeDtypeStruct((1, batch, heads, HEAD_DIM, HEAD_DIM), F32),
        ],
        compiler_params=_params("arbitrary"),
        name="hgrn_sample",
    )(proj, proj, proj, lb_logits, state)


def _s5_weights_kernel(are_ref, aim_ref, ls_ref, bre_ref, bim_ref, cre_ref, cim_ref, rep_ref, tile_ref, same_ref,
                       lre_ref, lim_ref, bbre_ref, bbim_ref, ccre_ref, ccim_ref):
    a_re = are_ref[...]
    a_im = aim_ref[...]
    dt = jnp.exp(ls_ref[...])
    mag = jnp.exp(a_re * dt)
    lam_re = mag * jnp.cos(a_im * dt)
    lam_im = mag * jnp.sin(a_im * dt)
    den = a_re * a_re + a_im * a_im
    nr, ni = lam_re - 1.0, lam_im
    r_re = (nr * a_re + ni * a_im) / den
    r_im = (ni * a_re - nr * a_im) / den
    lre_ref[...] = lam_re
    lim_ref[...] = lam_im
    b_re = bre_ref[...]
    b_im = bim_ref[...]
    same = same_ref[...]
    bbre_ref[0] = (_dot(rep_ref[...], (r_re * b_re - r_im * b_im).astype(BF16)) * same).astype(BF16)
    bbim_ref[0] = (_dot(rep_ref[...], (r_re * b_im + r_im * b_re).astype(BF16)) * same).astype(BF16)
    ccre_ref[0] = (_dot(cre_ref[...].astype(BF16), tile_ref[...]) * same).astype(BF16)
    ccim_ref[0] = (_dot(cim_ref[...].astype(BF16), tile_ref[...]) * same).astype(BF16)


def _s5_weights(a_re, a_im, log_step, b_re, b_im, c_re, c_im):
    g, p = a_re.shape
    c = b_re.shape[-1]
    gt = MXU_DIM // c
    nj, kc, kp = g // gt, gt * c, gt * p
    flat = lambda x: x.reshape(1, g * p)
    chan = lambda x: jnp.transpose(x, (2, 0, 1)).reshape(c, g * p)
    row_g, lane_g = np.arange(kc)[:, None] // c, np.arange(kp)[None, :] // p
    rep = jnp.asarray(np.arange(kc)[:, None] % c == np.arange(c)[None, :], BF16)
    tile = jnp.asarray(np.arange(p)[:, None] == np.arange(kp)[None, :] % p, BF16)
    same = jnp.asarray(row_g == lane_g, F32)
    lanes = lambda rows: pl.BlockSpec((rows, kp), lambda j: (0, j))
    const = lambda x: pl.BlockSpec(x.shape, lambda j: (0, 0))
    tiles = pl.BlockSpec((1, kc, kp), lambda j: (j, 0, 0))
    tiles_shape = jax.ShapeDtypeStruct((nj, kc, kp), BF16)
    lam_shape = jax.ShapeDtypeStruct((1, g * p), F32)
    lam_re, lam_im, *mats = pl.pallas_call(
        _s5_weights_kernel,
        grid=(nj,),
        in_specs=[lanes(1), lanes(1), lanes(1), lanes(c), lanes(c),
                  pl.BlockSpec((kc, p), lambda j: (j, 0)), pl.BlockSpec((kc, p), lambda j: (j, 0)),
                  const(rep), const(tile), const(same)],
        out_specs=[lanes(1), lanes(1), tiles, tiles, tiles, tiles],
        out_shape=[lam_shape, lam_shape] + [tiles_shape] * 4,
        compiler_params=_params("arbitrary"),
        name="s5_weights",
    )(flat(a_re), flat(a_im), flat(jnp.repeat(log_step, p)), chan(b_re), chan(b_im),
      c_re.reshape(g * c, p), c_im.reshape(g * c, p), rep, tile, same)
    return (lam_re.reshape(-1, 1, LANES), lam_im.reshape(-1, 1, LANES), *mats)


def _s5_lanes(x_ref, lo, hi):
    if len(x_ref.shape) == 2:
        return x_ref[:, lo:hi]
    return jnp.concatenate([x_ref[m] for m in range(lo // LANES, hi // LANES)], axis=1)


def _s5_in_tile(ub, j, bre_ref, bim_ref, xr_ref, xi_ref):
    _, kc, kp = bre_ref.shape
    uj = ub[:, j * kc : (j + 1) * kc]
    for x_ref, w_ref in ((xr_ref, bre_ref), (xi_ref, bim_ref)):
        bu = _dot(uj, w_ref[j])
        if len(x_ref.shape) == 2:
            x_ref[:, j * kp : (j + 1) * kp] = bu
        else:
            for m in range(kp // LANES):
                x_ref[j * kp // LANES + m] = bu[:, m * LANES : (m + 1) * LANES]


def _s5_in(ub, bre_ref, bim_ref, xr_ref, xi_ref):
    for j in range(bre_ref.shape[0]):
        _s5_in_tile(ub, j, bre_ref, bim_ref, xr_ref, xi_ref)


def _s5_out_tile(j, xr_ref, xi_ref, cre_ref, cim_ref):
    kp = cre_ref.shape[2]
    xr = _s5_lanes(xr_ref, j * kp, (j + 1) * kp).astype(BF16)
    xi = _s5_lanes(xi_ref, j * kp, (j + 1) * kp).astype(BF16)
    return _dot_nt(xr, cre_ref[j]) - _dot_nt(xi, cim_ref[j])


def _s5_out(xr_ref, xi_ref, cre_ref, cim_ref):
    return [_s5_out_tile(j, xr_ref, xi_ref, cre_ref, cim_ref) for j in range(cre_ref.shape[0])]


def _s5_step(lr, li, xr, xi, bur, bui):
    return lr * xr - li * xi + bur, lr * xi + li * xr + bui


def _s5_lambda(lre_ref, lim_ref, lo, hi, rows):
    cat = lambda ref: jnp.concatenate([ref[m] for m in range(lo // LANES, hi // LANES)], axis=1)
    return jnp.broadcast_to(cat(lre_ref), (rows, hi - lo)), jnp.broadcast_to(cat(lim_ref), (rows, hi - lo))


def _to_time_major(x):
    n, ts, w = x.shape
    return jnp.swapaxes(x, 0, 1).reshape(ts * n, w)


def _from_time_major(x, n):
    rows, w = x.shape
    return jnp.swapaxes(x.reshape(rows // n, n, w), 0, 1)


def _s5_prompt_kernel(*refs, nv):
    u_refs = refs[:nv]
    (lre_ref, lim_ref, bre_ref, bim_ref, cre_ref, cim_ref, d_ref,
     y_ref, xr_out_ref, xi_out_ref, xr_ref, xi_ref, sr_ref, si_ref) = refs[nv:]
    ts = u_refs[0].shape[0]
    step = pl.program_id(0)

    @pl.when(step == 0)
    def _():
        sr_ref[...] = jnp.zeros_like(sr_ref)
        si_ref[...] = jnp.zeros_like(si_ref)

    u = _to_time_major(jnp.stack([r[...] for r in u_refs], axis=0))
    ub = u.astype(BF16)
    nj, kc, kp = bre_ref.shape
    d = d_ref[...]

    def project(j):
        cols = slice(j * kc, (j + 1) * kc)
        y = _s5_out_tile(j, xr_ref, xi_ref, cre_ref, cim_ref)
        y_ref[:, :, cols] = _from_time_major(y + d[:, cols] * u[:, cols], nv)

    def b_piece(j, n, x_ref, w_ref):
        cols = slice(n * MXU_DIM, (n + 1) * MXU_DIM)
        x_ref[:, j * kp + n * MXU_DIM : j * kp + (n + 1) * MXU_DIM] = _dot(ub[:, j * kc : (j + 1) * kc], w_ref[j][:, cols])

    def scan_piece(j, state, r0, r1):
        ls = slice(j * kp, (j + 1) * kp)
        lr, li, xr, xi = state
        for r in range(r0, r1):
            rows = slice(r * nv, (r + 1) * nv)
            xr, xi = _s5_step(lr, li, xr, xi, xr_ref[rows, ls], xi_ref[rows, ls])
            xr_ref[rows, ls] = xr
            xi_ref[rows, ls] = xi
        return lr, li, xr, xi

    pieces = [(n, x_ref, w_ref) for n in range(kp // MXU_DIM) for x_ref, w_ref in ((xr_ref, bre_ref), (xi_ref, bim_ref))]
    per = ts // len(pieces)
    assert per * len(pieces) == ts
    for s in range(nj + 2):
        state = None
        if 0 <= s - 1 < nj:
            ls = slice((s - 1) * kp, s * kp)
            state = (*_s5_lambda(lre_ref, lim_ref, (s - 1) * kp, s * kp, nv), sr_ref[:, ls], si_ref[:, ls])
        for p, (n, x_ref, w_ref) in enumerate(pieces):
            if s < nj:
                b_piece(s, n, x_ref, w_ref)
            if state is not None:
                state = scan_piece(s - 1, state, p * per, (p + 1) * per)
        if state is not None:
            sr_ref[:, ls] = state[2]
            si_ref[:, ls] = state[3]
        if 0 <= s - 2 < nj:
            project(s - 2)

    @pl.when(step == pl.num_programs(0) - 1)
    def _():
        xr_out_ref[...] = sr_ref[...]
        xi_out_ref[...] = si_ref[...]


def _s5_carry_kernel(y_ref, fr_ref, fi_ref, lre_ref, lim_ref, cre_ref, cim_ref,
                     o_ref, xr_out_ref, xi_out_ref, zr_ref, zi_ref, sr_ref, si_ref, *, half_len, lane_block):
    nrow, gp = fr_ref.shape
    tf = y_ref.shape[2]
    step = pl.program_id(0)
    odd = lax.broadcasted_iota(jnp.int32, (nrow, 1), 0) % 2 == 1

    @pl.when(step == 0)
    def _():
        for lo in range(0, gp, lane_block):
            ls = slice(lo, lo + lane_block)
            pr, pi = _s5_lambda(lre_ref, lim_ref, lo, lo + lane_block, nrow)
            n = 1
            while n < half_len:
                pr, pi = pr * pr - pi * pi, 2.0 * pr * pi
                n *= 2
            fr, fi = fr_ref[:, ls], fi_ref[:, ls]
            fr = jnp.where(odd, pltpu.roll(fr, 1, 0), fr)
            fi = jnp.where(odd, pltpu.roll(fi, 1, 0), fi)
            sr_ref[:, ls] = jnp.where(odd, pr * fr - pi * fi, fr)
            si_ref[:, ls] = jnp.where(odd, pr * fi + pi * fr, fi)

    nj, kc, kp = cre_ref.shape
    nseq = y_ref.shape[0]
    for s in range(nj + 1):
        if s < nj:
            ls = slice(s * kp, (s + 1) * kp)
            lr, li = _s5_lambda(lre_ref, lim_ref, s * kp, (s + 1) * kp, nrow)
            zr, zi = sr_ref[:, ls], si_ref[:, ls]
            for r in range(tf):
                rows = slice(r * nrow, (r + 1) * nrow)
                zr, zi = lr * zr - li * zi, lr * zi + li * zr
                zr_ref[rows, ls] = zr
                zi_ref[rows, ls] = zi
            sr_ref[:, ls] = zr
            si_ref[:, ls] = zi
        if s > 0:
            cols = slice((s - 1) * kc, s * kc)
            y = _from_time_major(_s5_out_tile(s - 1, zr_ref, zi_ref, cre_ref, cim_ref), nrow)
            o_ref[:, :, :, cols] = y_ref[:, :, :, cols] + y.reshape(nseq, nrow // nseq, tf, kc)

    @pl.when(step == pl.num_programs(0) - 1)
    def _():
        xr_out_ref[...] = fr_ref[...] + sr_ref[...]
        xi_out_ref[...] = fi_ref[...] + si_ref[...]


def _s5_weights_specs(ws):
    zero = lambda n: (lambda *_: (0,) * n)
    return [pl.BlockSpec(w.shape, zero(w.ndim)) for w in ws]


def _s5_prompt(proj, ws, *, nb, seq, ucol, w_b, ts=64, tf=64, lane_block=1024):
    lam_re, lam_im, bre, bim, cre, cim, d = ws
    gp = lam_re.shape[0] * LANES
    nv = SUBLANES
    assert nv == 2 * nb, "two pieces per sequence"
    plen, half_len = seq // 2, seq // 4
    assert plen % ts == 0 and half_len % tf == 0 and gp % lane_block == 0
    assert half_len & (half_len - 1) == 0, "repeated squaring needs a power of two"
    nblk = plen // ts
    u_spec = lambda n: pl.BlockSpec((ts, w_b), lambda s: (n * nblk + s, ucol))
    state = pl.BlockSpec((nv, gp), lambda s: (0, 0))
    state_shape = jax.ShapeDtypeStruct((nv, gp), F32)
    scratch = lambda steps: [pltpu.VMEM((nv * steps, gp), F32)] * 2 + [pltpu.VMEM((nv, gp), F32)] * 2
    y, fr, fi = pl.pallas_call(
        functools.partial(_s5_prompt_kernel, nv=nv),
        grid=(nblk,),
        in_specs=[u_spec(n) for n in range(nv)] + _s5_weights_specs(ws),
        out_specs=[pl.BlockSpec((nv, ts, w_b), lambda s: (0, s, 0)), state, state],
        out_shape=[jax.ShapeDtypeStruct((nv, plen, w_b), F32), state_shape, state_shape],
        scratch_shapes=scratch(ts),
        compiler_params=_params("arbitrary"),
        name="s5_prompt",
    )(*([proj] * nv), *ws)

    y5 = y.reshape(nb, 2, 2, half_len, w_b)
    y_spec = pl.BlockSpec((nb, None, 2, tf, w_b), lambda s: (0, 1, 0, s, 0))
    y5, xr, xi = pl.pallas_call(
        functools.partial(_s5_carry_kernel, half_len=half_len, lane_block=lane_block),
        grid=(half_len // tf,),
        in_specs=[y_spec, state, state] + _s5_weights_specs((lam_re, lam_im, cre, cim)),
        out_specs=[y_spec, state, state],
        out_shape=[jax.ShapeDtypeStruct(y5.shape, F32), state_shape, state_shape],
        scratch_shapes=scratch(tf),
        input_output_aliases={0: 0},
        compiler_params=_params("arbitrary"),
        name="s5_prompt_carry",
    )(y5, fr, fi, lam_re, lam_im, cre, cim)
    last = lambda x: x.reshape(nb, 2, gp)[:, 1]
    return y5.reshape(nb * seq, w_b), last(xr), last(xi)


def _s5_sample_kernel(u_ref, x0r_ref, x0i_ref, lre_ref, lim_ref, bre_ref, bim_ref, cre_ref, cim_ref, d_ref,
                      y_ref, xr_out_ref, xi_out_ref, xr_ref, xi_ref, *, seq):
    nb = x0r_ref.shape[0]
    u = u_ref[...]
    _s5_in(u.astype(BF16), bre_ref, bim_ref, xr_ref, xi_ref)
    for m in range(xr_ref.shape[0]):
        ls = slice(m * LANES, (m + 1) * LANES)
        lr, li = lre_ref[m], lim_ref[m]
        xr, xi = x0r_ref[:, ls], x0i_ref[:, ls]
        for r in range(seq):
            rows = pl.ds(r, nb, stride=seq)
            xr, xi = _s5_step(lr, li, xr, xi, xr_ref[m, rows, :], xi_ref[m, rows, :])
            xr_ref[m, rows, :] = xr
            xi_ref[m, rows, :] = xi
        xr_out_ref[:, ls] = xr
        xi_out_ref[:, ls] = xi
    kc = cre_ref.shape[1]
    d = d_ref[...]
    for j, y in enumerate(_s5_out(xr_ref, xi_ref, cre_ref, cim_ref)):
        cols = slice(j * kc, (j + 1) * kc)
        y_ref[:, cols] = y + d[:, cols] * u[:, cols]


def _s5_sample(proj, x0_re, x0_im, ws, *, row0, ucol, seq, w_b, nb=64):
    batch, gp = x0_re.shape
    n_slabs = gp // LANES
    assert batch % nb == 0 and row0 % (nb * seq) == 0
    t0 = row0 // (nb * seq)
    return pl.pallas_call(
        functools.partial(_s5_sample_kernel, seq=seq),
        grid=(batch // nb,),
        in_specs=[
            pl.BlockSpec((nb * seq, w_b), lambda p: (t0 + p, ucol)),
            pl.BlockSpec((nb, gp), lambda p: (p, 0)),
            pl.BlockSpec((nb, gp), lambda p: (p, 0)),
        ] + _s5_weights_specs(ws),
        out_specs=[
            pl.BlockSpec((nb * seq, w_b), lambda p: (p, 0)),
            pl.BlockSpec((nb, gp), lambda p: (p, 0)),
            pl.BlockSpec((nb, gp), lambda p: (p, 0)),
        ],
        out_shape=[
            jax.ShapeDtypeStruct((batch * seq, w_b), F32),
            jax.ShapeDtypeStruct((batch, gp), F32),
            jax.ShapeDtypeStruct((batch, gp), F32),
        ],
        scratch_shapes=[pltpu.VMEM((n_slabs, nb * seq, LANES), F32), pltpu.VMEM((n_slabs, nb * seq, LANES), F32)],
        compiler_params=_params("arbitrary"),
        name="s5_sample",
    )(proj, x0_re, x0_im, *ws)


def _mix_gates(o_ref, gate_ref, y_ref, ng_ref, gluw_ref, glub_ref, a_ref, src, dst):
    w_a = o_ref.shape[1]
    o = o_ref[src, :]
    var = jnp.mean(o * o, axis=-1, keepdims=True)
    gate = gate_ref[src, :]
    a_ref[dst, :w_a] = (o * lax.rsqrt(var + EPS) * ng_ref[...] * (gate * jax.nn.sigmoid(gate))).astype(BF16)
    y = jax.nn.gelu(y_ref[src, :])
    z = _dot(y.astype(BF16), gluw_ref[...]) + glub_ref[...]
    a_ref[dst, w_a:] = (y * jax.nn.sigmoid(z)).astype(BF16)


def _mix_kernel(o_ref, gate_ref, y_ref, x_ref, ng_ref, gluw_ref, glub_ref, wout_ref, out_ref, *rest,
                n_tiles, n_cols):
    *copies, a0_ref, a1_ref = rest
    glu16_ref, wout16_ref = copies or (None, None)
    i = pl.program_id(0)
    j = pl.program_id(1)
    rows_per_step = a0_ref.shape[0] // n_cols

    if glu16_ref is not None:
        @pl.when(jnp.logical_and(i == 0, j == 0))
        def _():
            _bf16_weights(gluw_ref, glu16_ref)
        gluw_ref = glu16_ref

    def step(a_gate, a_proj):
        w_a = o_ref.shape[1]
        tm = rows_per_step * n_cols
        start = pl.multiple_of(j * rows_per_step, rows_per_step)
        dst = pl.ds(start, rows_per_step)
        gated = {}

        def gelu_y():
            gated["y"] = jax.nn.gelu(y_ref[...])

        def glu_dot():
            gated["z"] = _dot(gated["y"].astype(BF16), gluw_ref[...]) + glub_ref[...]

        def glu_out():
            a_gate[dst, w_a:] = (gated["y"] * jax.nn.sigmoid(gated["z"])).astype(BF16)

        def head_out():
            o = o_ref[...]
            var = jnp.mean(o * o, axis=-1, keepdims=True)
            gate = gate_ref[...]
            a_gate[dst, :w_a] = (o * lax.rsqrt(var + EPS) * ng_ref[...] * (gate * jax.nn.sigmoid(gate))).astype(BF16)

        chunks = []
        if a_proj is not None:
            half = tm // 2
            weights = {}

            def chunk(p, h):
                cols = slice(p * MXU_DIM, (p + 1) * MXU_DIM)
                rows = slice(h * half, (h + 1) * half)
                if p not in weights:
                    weights[p] = _bf16_weights(wout_ref, wout16_ref, (slice(None), cols), first_row=1)
                out_ref[rows, cols] = x_ref[rows, cols] + _dot(a_proj[rows, :], weights[p])

            chunks = [functools.partial(chunk, p, h) for p in range(out_ref.shape[1] // MXU_DIM) for h in range(2)]
        phases = [[gelu_y], [glu_dot, head_out], [glu_out]] if a_gate is not None else []
        for k in range(max(len(phases), len(chunks))):
            for phase in phases[k] if k < len(phases) else []:
                phase()
            if k < len(chunks):
                chunks[k]()

    _by_parity(i, n_tiles, step, a0_ref, a1_ref)


def _mix(o, proj, y, x, norm_g, glu_w, glu_b, w_out, *, gate_col, name, tn=512):
    m, w_a = o.shape
    w_b = y.shape[1]
    d = w_out.shape[1]
    tm = _row_tile(m)
    nj = d // tn
    assert m % tm == 0 and d % tn == 0 and tm % (nj * SUBLANES) == 0
    n = m // tm
    nxt, cur, col, row_slice = _lead_in_maps(n, nj)
    const = lambda i, j: (0, 0)
    cur_tile = lambda i, j: (cur(i), col(i, j))
    glu_spec = ((w_b, w_b), const)
    wout_spec = ((w_a + w_b, tn), lambda i, j: (0, col(i, j)))
    assert glu_w.dtype == w_out.dtype
    copy_specs, copy_shapes = (a + b for a, b in zip(_weight_copy(glu_w, *glu_spec, (0, 0)),
                                                     _weight_copy(w_out, *wout_spec, (0, nj - 1), first_row=1)))
    return pl.pallas_call(
        functools.partial(_mix_kernel, n_tiles=n, n_cols=nj),
        grid=(n + 1, nj),
        in_specs=[
            pl.BlockSpec((tm // nj, w_a), lambda i, j: (row_slice(i, j), 0)),
            pl.BlockSpec((tm // nj, w_a), lambda i, j: (row_slice(i, j), gate_col)),
            pl.BlockSpec((tm // nj, w_b), lambda i, j: (row_slice(i, j), 0)),
            pl.BlockSpec((tm, tn), cur_tile),
            pl.BlockSpec((1, w_a), const),
            pl.BlockSpec(*glu_spec),
            pl.BlockSpec((1, w_b), const),
            pl.BlockSpec(*wout_spec),
        ],
        out_specs=[pl.BlockSpec((tm, tn), cur_tile)] + copy_specs,
        out_shape=[jax.ShapeDtypeStruct((m, d), F32)] + copy_shapes,
        scratch_shapes=[pltpu.VMEM((tm, w_a + w_b), BF16)] * 2,
        compiler_params=_params("arbitrary", "arbitrary"),
        name=name,
    )(o, proj, y, x, norm_g.reshape(1, w_a), glu_w, glu_b.reshape(1, w_b), w_out)


def _mlp_kernel(x_ref, g_ref, up_ref, down_ref, gf_ref, o_ref, *rest, rows, final_norm):
    *copies, h_ref = rest
    up16_ref, down16_ref = copies or (None, None)
    f = pl.program_id(1)

    @pl.when(f == 0)
    def _():
        _rmsnorm_rows(x_ref, g_ref[...], h_ref, rows, copy_ref=o_ref)

    a = jnp.square(jnp.maximum(_dot(h_ref[...], _bf16_weights(up_ref, up16_ref)), 0.0)).astype(BF16)
    o_ref[...] += _dot(a, _bf16_weights(down_ref, down16_ref))

    if final_norm:
        @pl.when(f == pl.num_programs(1) - 1)
        def _():
            _rmsnorm_rows(o_ref, gf_ref[...], o_ref, rows)


def _mlp(x, g, up, down, gf, *, final_norm, name, tf=512):
    m, d = x.shape
    dff = up.shape[1]
    tm = _row_tile(m)
    assert m % tm == 0 and dff % tf == 0 and up.dtype == down.dtype
    up_spec = ((d, tf), lambda i, f: (0, f))
    down_spec = ((tf, d), lambda i, f: (f, 0))
    nf = dff // tf
    copy_specs, copy_shapes = (a + b for a, b in zip(_weight_copy(up, *up_spec, (0, nf - 1)),
                                                     _weight_copy(down, *down_spec, (nf - 1, 0))))
    return pl.pallas_call(
        functools.partial(_mlp_kernel, rows=tm, final_norm=final_norm),
        grid=(m // tm, dff // tf),
        in_specs=[
            pl.BlockSpec((tm, d), lambda i, f: (i, 0)),
            pl.BlockSpec((1, d), lambda i, f: (0, 0)),
            pl.BlockSpec(*up_spec),
            pl.BlockSpec(*down_spec),
            pl.BlockSpec((1, d), lambda i, f: (0, 0)),
        ],
        out_specs=[pl.BlockSpec((tm, d), lambda i, f: (i, 0))] + copy_specs,
        out_shape=[jax.ShapeDtypeStruct((m, d), F32)] + copy_shapes,
        scratch_shapes=[pltpu.VMEM((tm, d), BF16)],
        compiler_params=_params("arbitrary", "arbitrary"),
        name=name,
    )(x, g.reshape(1, d), up, down, gf.reshape(1, d))


def kernel(x_prompt, x_sample, state_hgrn, state_s5_re, state_s5_im, w_in, w_out, norm1_g, norm2_g, hgrn_lb_logits, hgrn_norm_g, s5_a_re, s5_a_im, s5_b_re, s5_b_im, s5_c_re, s5_c_im, s5_d, s5_log_step, glu_w, glu_b, mlp_up, mlp_down, final_norm_g):
    bp, seq, d = x_prompt.shape
    bs, dseq, _ = x_sample.shape
    depth = w_in.shape[0]
    w_a = hgrn_norm_g.shape[1]
    w_b = s5_d.shape[1]
    g_b, p_state = s5_a_re.shape[1], s5_a_re.shape[2]
    mp, ms = bp * seq, bs * dseq
    assert w_in.shape[2] == 4 * w_a + w_b and w_a == w_b and g_b * S5_GROUP == w_b
    gate_col, ucol = 3, (4 * w_a) // w_b

    xp = x_prompt.reshape(mp, d)
    xs = x_sample.reshape(ms, d)
    lb_logits = hgrn_lb_logits.astype(F32)
    new_states = []
    for l in range(depth):
        proj_s, w_in16 = _inproj(xs, norm1_g[l], w_in[l].astype(F32), name="inproj_sample")
        proj_p, = _inproj(xp, norm1_g[l], w_in16, name="inproj_prompt")

        o_p, sh_p, (up16, down16) = _hgrn_prompt(proj_p, lb_logits, layer=l, batch=bp, seq=seq, w_a=w_a,
                                                 round_weights=[(mlp_up[l].astype(F32), 1), (mlp_down[l].astype(F32), 0)])
        o_s, sh_s = _hgrn_sample(proj_s, lb_logits, state_hgrn, layer=l, row0=0, batch=bs, seq=dseq, w_a=w_a)

        ws = (*_s5_weights(*(w[l].astype(F32) for w in (s5_a_re, s5_a_im, s5_log_step, s5_b_re, s5_b_im,
                                                        s5_c_re, s5_c_im))),
              s5_d[l].astype(F32).reshape(1, w_b))
        y_p, sr_p, si_p = _s5_prompt(proj_p, ws, nb=bp, seq=seq, ucol=ucol, w_b=w_b)
        y_s, sr_s, si_s = _s5_sample(
            proj_s, state_s5_re[l].reshape(bs, g_b * p_state).astype(F32), state_s5_im[l].reshape(bs, g_b * p_state).astype(F32),
            ws, row0=0, ucol=ucol, seq=dseq, w_b=w_b)

        last = l == depth - 1
        x1_s, glu16, w_out16 = _mix(o_s, proj_s, y_s, xs, hgrn_norm_g[l], glu_w[l].astype(F32), glu_b[l],
                                    w_out[l].astype(F32), gate_col=gate_col, name="mix_sample")
        xs, = _mlp(x1_s, norm2_g[l], up16, down16, final_norm_g, final_norm=last, name="mlp_sample")
        x1_p, = _mix(o_p, proj_p, y_p, xp, hgrn_norm_g[l], glu16, glu_b[l], w_out16, gate_col=gate_col,
                     name="mix_prompt")
        xp, = _mlp(x1_p, norm2_g[l], up16, down16, final_norm_g, final_norm=last, name="mlp_prompt")
        new_states.append((sh_p, sr_p.reshape(bp, g_b, p_state), si_p.reshape(bp, g_b, p_state),
                           sh_s[0], sr_s.reshape(bs, g_b, p_state), si_s.reshape(bs, g_b, p_state)))

    y_prompt = xp.reshape(bp, seq, d).astype(x_prompt.dtype)
    y_sample = xs.reshape(bs, dseq, d).astype(x_sample.dtype)
    stacked = [jnp.stack([st[i] for st in new_states]) for i in range(6)]
    return (y_prompt, y_sample, *stacked)
```

```python
import functools

import jax
import jax.numpy as jnp
import numpy as np
from jax import lax
from jax.experimental import pallas as pl
from jax.experimental.pallas import tpu as pltpu

F32 = jnp.float32
BF16 = jnp.bfloat16
EPS = 1e-6

LANES = 128
SUBLANES = 8
MXU_DIM = 256
VMEM_LIMIT = 56 * 1024 * 1024

HEAD_DIM = 128
S5_GROUP = 16
HGRN_CHUNK = 64
HGRN_HEAD_GROUP = 2
HGRN_ROUND_EVERY = 4
ROW_TILE = 1024
NEG_BIG = -1e30


def _dot(a, b):
    return jnp.dot(a, b, preferred_element_type=F32)


def _dot_nt(a, b):
    return lax.dot_general(a, b, (((1,), (1,)), ((), ())), preferred_element_type=F32)


def _dot_tn(a, b):
    return lax.dot_general(a, b, (((0,), (0,)), ((), ())), preferred_element_type=F32)


def _split3(x):
    hi = x.astype(BF16)
    r1 = x - hi.astype(F32)
    mid = r1.astype(BF16)
    lo = (r1 - mid.astype(F32)).astype(BF16)
    return hi, mid, lo


def _params(*sem):
    return pltpu.CompilerParams(dimension_semantics=sem, vmem_limit_bytes=VMEM_LIMIT)


def _rmsnorm_rows(src_ref, g, dst_ref, rows, chunk=256, copy_ref=None):
    def body(c, carry):
        r0 = pl.multiple_of(c * chunk, chunk)
        x = src_ref[pl.ds(r0, chunk), :]
        if copy_ref is not None:
            copy_ref[pl.ds(r0, chunk), :] = x
        var = jnp.mean(x * x, axis=-1, keepdims=True)
        dst_ref[pl.ds(r0, chunk), :] = (x * lax.rsqrt(var + EPS) * g).astype(dst_ref.dtype)
        return carry

    lax.fori_loop(0, rows // chunk, body, 0)


def _bf16_weights(w_ref, copy_ref, at=(Ellipsis,), first_row=0):
    w = w_ref[at]
    if copy_ref is None:
        return w
    w = w.astype(BF16)

    @pl.when(pl.program_id(0) == first_row)
    def _():
        copy_ref[at] = w

    return w


def _weight_copy(w, block, index_map, last_index, first_row=0):
    if w.dtype == BF16:
        return [], []

    def once(i, j):
        during = index_map(jnp.full_like(i, first_row), j)
        before = index_map(jnp.full_like(i, first_row), jnp.zeros_like(j))
        pick = lambda b, d, a: jnp.where(i < first_row, b, jnp.where(i == first_row, d, a))
        return jax.tree.map(pick, before, during, last_index)

    return [pl.BlockSpec(block, once)], [jax.ShapeDtypeStruct(w.shape, BF16)]


def _rider_specs(round_weights, steps, slots, step_of):
    def spec(w, axis):
        block = tuple(n // steps if a == axis else n for a, n in enumerate(w.shape))
        assert w.ndim == 2 and w.dtype == F32 and w.shape[axis] % steps == 0
        assert block[0] % (2 * SUBLANES * slots) == 0 and block[1] % LANES == 0
        return pl.BlockSpec(block, lambda *ids: tuple(step_of(*ids) if a == axis else 0 for a in range(2)))

    return [spec(w, axis) for w, axis in round_weights]


def _round_piece(w_refs, w16_refs, k, slots):
    for w_ref, w16_ref in zip(w_refs, w16_refs):
        n = w_ref.shape[0] // slots
        w16_ref[k * n : (k + 1) * n, :] = w_ref[k * n : (k + 1) * n, :].astype(BF16)


def _lead_in_maps(n, nj):
    nxt = lambda i: jnp.minimum(i, n - 1)
    cur = lambda i: jnp.maximum(i - 1, 0)
    col = lambda i, j: jnp.where(i > 0, j, 0)
    row_slice = lambda i, j: nxt(i) * nj + jnp.where(i < n, j, nj - 1)
    return nxt, cur, col, row_slice


def _by_parity(i, n_tiles, step, buf0, buf1):
    even = i % 2 == 0
    inner = jnp.logical_and(i > 0, i < n_tiles)
    pl.when(i == 0)(lambda: step(buf0, None))
    pl.when(jnp.logical_and(inner, even))(lambda: step(buf0, buf1))
    pl.when(jnp.logical_and(inner, jnp.logical_not(even)))(lambda: step(buf1, buf0))
    pl.when(jnp.logical_and(i == n_tiles, even))(lambda: step(None, buf1))
    pl.when(jnp.logical_and(i == n_tiles, jnp.logical_not(even)))(lambda: step(None, buf0))


def _inproj_kernel(x_ref, g_ref, w_ref, o_ref, *rest, n_tiles, n_cols):
    *w16_ref, h0_ref, h1_ref = rest
    j = pl.program_id(1)
    rows_per_step = h0_ref.shape[0] // n_cols
    rows = pl.ds(pl.multiple_of(j * rows_per_step, rows_per_step), rows_per_step)

    def step(h_norm, h_proj):
        if h_norm is not None:
            x = x_ref[...]
            var = jnp.mean(x * x, axis=-1, keepdims=True)
            h_norm[rows, :] = (x * lax.rsqrt(var + EPS) * g_ref[...]).astype(BF16)
        if h_proj is not None:
            o_ref[...] = _dot(h_proj[...], _bf16_weights(w_ref, *(w16_ref or [None]), first_row=1))

    _by_parity(pl.program_id(0), n_tiles, step, h0_ref, h1_ref)


def _row_tile(m):
    return ROW_TILE if m % ROW_TILE == 0 else ROW_TILE // 2


def _inproj(x, g, w, *, name, tn=1280):
    m, d = x.shape
    n_out = w.shape[1]
    tm = _row_tile(m)
    if w.dtype != BF16:
        tn //= 2
    nj = n_out // tn
    assert m % tm == 0 and n_out % tn == 0 and tm % (nj * SUBLANES) == 0
    n = m // tm
    nxt, cur, col, row_slice = _lead_in_maps(n, nj)
    w_spec = ((d, tn), lambda i, j: (0, col(i, j)))
    copy_specs, copy_shapes = _weight_copy(w, *w_spec, (0, nj - 1), first_row=1)
    return pl.pallas_call(
        functools.partial(_inproj_kernel, n_tiles=n, n_cols=nj),
        grid=(n + 1, nj),
        in_specs=[
            pl.BlockSpec((tm // nj, d), lambda i, j: (row_slice(i, j), 0)),
            pl.BlockSpec((1, d), lambda i, j: (0, 0)),
            pl.BlockSpec(*w_spec),
        ],
        out_specs=[pl.BlockSpec((tm, tn), lambda i, j: (cur(i), col(i, j)))] + copy_specs,
        out_shape=[jax.ShapeDtypeStruct((m, n_out), F32)] + copy_shapes,
        scratch_shapes=[pltpu.VMEM((tm, d), BF16)] * 2,
        compiler_params=_params("arbitrary", "arbitrary"),
        name=name,
    )(x, g.reshape(1, d), w)


def _lower_bound(logits, layer):
    m = jnp.max(logits, axis=0, keepdims=True)
    e = jnp.exp(logits - m)
    return jnp.sum(e[: layer + 1], axis=0, keepdims=True) / jnp.sum(e, axis=0, keepdims=True)


def _row_to_col(e_row):
    n = e_row.shape[1]
    hi, mid, lo = (p.astype(F32) for p in _split3(e_row))
    r = lax.broadcasted_iota(jnp.int32, (2 * SUBLANES, n), 0)
    pieces = jnp.where(r == 0, hi, jnp.where(r == 1, mid, jnp.where(r == 2, lo, 0.0)))
    return _dot_tn(pieces.astype(BF16), jnp.ones((2 * SUBLANES, LANES), BF16))


def _hgrn_gates(qp, fp, lb):
    q = qp * jax.nn.sigmoid(qp)
    f = lb + (1.0 - lb) * jax.nn.sigmoid(fp)
    return q, f, 1.0 - f, jnp.log(f)


def _hgrn_tables(c):
    t = np.arange(c)[:, None]
    j = np.arange(c)[None, :]
    sums, pairs = [j <= t], []
    bs = c
    while bs >= 2:
        hs = bs // 2
        ref = t - t % bs + hs - 1
        upper = t % bs >= hs
        sums.append(np.where(upper, (j > ref) & (j <= t), (j > t) & (j <= ref)))
        pairs.append((t // bs == j // bs) & upper & (j % bs < hs))
        bs = hs
    sums.append(j > t)
    pairs.append(t == j)
    sums = np.concatenate(sums, axis=0).astype(np.float32)
    return jnp.asarray(np.concatenate([sums] * 3, axis=1), BF16), jnp.asarray(np.stack(pairs), F32)


def _upper_q_lower_k(q, k, bs):
    c = q.shape[0]
    hs = bs // 2
    if hs % SUBLANES == 0:
        parts = []
        for r0 in range(0, c, bs):
            parts += [k[r0 : r0 + hs], q[r0 + hs : r0 + bs]]
        return jnp.concatenate(parts, axis=0)
    row = lax.broadcasted_iota(jnp.int32, (c, 1), 0)
    return jnp.where(row % bs >= hs, q, k)


def _hgrn_prep(qp, fp, v, lb, sums, levels):
    c = qp.shape[0]
    q = qp * jax.nn.sigmoid(qp)
    f = lb + (1.0 - lb) * jax.nn.sigmoid(fp)
    k = 1.0 - f
    e = jnp.exp2(_dot(sums, jnp.concatenate(_split3(jnp.log2(f)), axis=0)))
    e_b, e_rest = e[:c], e[(levels + 1) * c :]
    ys = [(_upper_q_lower_k(q, k, c >> l) * e[(l + 1) * c : (l + 2) * c]).astype(BF16) for l in range(levels)]
    return dict(ys=ys, q16=q.astype(BF16), k16=k.astype(BF16), v16=v.astype(BF16),
                qe=(q * e_b).astype(BF16), kd=(k * e_rest).astype(BF16), e_last=e_b[c - 1 : c])


def _hgrn_apply(prep, st_ref, pairs_ref, n_seq, heads, emit_between):
    levels = pairs_ref.shape[0] - 1
    pairs = [(n, h) for n in range(n_seq) for h in range(heads)]

    def head(n, h, name, l=None):
        x = prep[n, h // HGRN_HEAD_GROUP][name]
        x = x if l is None else x[l]
        lo = h % HGRN_HEAD_GROUP * HEAD_DIM
        return x[:, lo : lo + HEAD_DIM]

    scores = {}
    for i, (n, h) in enumerate(pairs):
        a = _dot_nt(head(n, h, "q16"), head(n, h, "k16")) * pairs_ref[levels]
        for l in range(levels):
            y = head(n, h, "ys", l)
            a = a + _dot_nt(y, y) * pairs_ref[l]
        scores[n, h] = a.astype(BF16)
        emit_between(i)
    outs = {}
    for n, h in pairs:
        st = st_ref[n, h]
        outs[n, h] = _dot_nt(head(n, h, "qe"), st.astype(BF16)) + _dot(scores[n, h], head(n, h, "v16"))
        st_ref[n, h] = head(n, h, "e_last") * st + _dot_tn(head(n, h, "v16"), head(n, h, "kd"))
    return [jnp.concatenate([outs[n, h] for h in range(heads)], axis=1) for n in range(n_seq)]


def _hgrn_prompt_kernel(q_ref, f_ref, v_ref, lbl_ref, sums_ref, pairs_ref, *rest, layer, heads, n_chunks):
    n_round = (len(rest) - 3) // 2
    w_refs, (o_ref, s_out_ref), w16_refs, st_ref = rest[:n_round], rest[n_round:n_round + 2], rest[n_round + 2:-1], rest[-1]
    t = pl.program_id(1)

    @pl.when(t == 0)
    def _():
        st_ref[...] = jnp.zeros_like(st_ref)

    lb = _lower_bound(lbl_ref[...], layer)
    n_seq = q_ref.shape[0]
    levels = pairs_ref.shape[0] - 1
    pieces = [(n, g) for n in range(n_seq) for g in range(heads // HGRN_HEAD_GROUP)]

    def prep(c, n, g):
        rows = slice(c * HGRN_CHUNK, (c + 1) * HGRN_CHUNK)
        cols = slice(g * HGRN_HEAD_GROUP * HEAD_DIM, (g + 1) * HGRN_HEAD_GROUP * HEAD_DIM)
        return _hgrn_prep(q_ref[n, rows, cols], f_ref[n, rows, cols], v_ref[n, rows, cols], lb[:, cols],
                          sums_ref[...], levels)

    nxt = {p: prep(0, *p) for p in pieces}
    for c in range(n_chunks):
        cur, nxt = nxt, {}

        def emit_between(i, c=c, nxt=nxt):
            if c + 1 < n_chunks and (i + 1) % HGRN_HEAD_GROUP == 0:
                p = pieces[i // HGRN_HEAD_GROUP]
                nxt[p] = prep(c + 1, *p)
            if (i + 1) % HGRN_ROUND_EVERY == 0:
                per_chunk = n_seq * heads // HGRN_ROUND_EVERY
                _round_piece(w_refs, w16_refs, c * per_chunk + i // HGRN_ROUND_EVERY, n_chunks * per_chunk)

        for n, o in enumerate(_hgrn_apply(cur, st_ref, pairs_ref, n_seq, heads, emit_between)):
            o_ref[n, c * HGRN_CHUNK : (c + 1) * HGRN_CHUNK, :] = o

    @pl.when(t == pl.num_programs(1) - 1)
    def _():
        for n in range(n_seq):
            for h in range(heads):
                s_out_ref[n, h] = st_ref[n, h].T


def _hgrn_prompt(proj, lb_logits, *, layer, batch, seq, w_a, round_weights=(), tt=256, n_seq=2):
    heads = w_a // HEAD_DIM
    assert seq % tt == 0 and tt % HGRN_CHUNK == 0 and batch % n_seq == 0
    sums, pairs = _hgrn_tables(HGRN_CHUNK)
    proj3 = proj.reshape(batch, seq, proj.shape[1])
    col = lambda c: pl.BlockSpec((n_seq, tt, w_a), lambda b, t: (b, t, c))
    const = lambda x: pl.BlockSpec(x.shape, lambda b, t: (0,) * x.ndim)
    nt = seq // tt
    slabs = _rider_specs(round_weights, batch // n_seq * nt, tt // HGRN_CHUNK * n_seq * heads // HGRN_ROUND_EVERY,
                         lambda b, t: b * nt + t)
    weights = [w for w, _ in round_weights]
    o, s, *rounded = pl.pallas_call(
        functools.partial(_hgrn_prompt_kernel, layer=layer, heads=heads, n_chunks=tt // HGRN_CHUNK),
        grid=(batch // n_seq, nt),
        in_specs=[col(0), col(1), col(2), const(lb_logits), const(sums), const(pairs)] + slabs,
        out_specs=[
            pl.BlockSpec((n_seq, tt, w_a), lambda b, t: (b, t, 0)),
            pl.BlockSpec((n_seq, heads, HEAD_DIM, HEAD_DIM), lambda b, t: (b, 0, 0, 0)),
        ] + slabs,
        out_shape=[
            jax.ShapeDtypeStruct((batch, seq, w_a), F32),
            jax.ShapeDtypeStruct((batch, heads, HEAD_DIM, HEAD_DIM), F32),
        ] + [jax.ShapeDtypeStruct(w.shape, BF16) for w in weights],
        scratch_shapes=[pltpu.VMEM((n_seq, heads, HEAD_DIM, HEAD_DIM), F32)],
        compiler_params=_params("arbitrary", "arbitrary"),
        name="hgrn_prompt",
    )(proj3, proj3, proj3, lb_logits, sums, pairs, *weights)
    return o.reshape(batch * seq, w_a), s, rounded


def _hgrn_sample_kernel(q_ref, f_ref, v_ref, lbl_ref, s_in_ref, o_ref, s_out_ref, *, layer, heads, seq):
    n_seq = SUBLANES // seq
    lb = _lower_bound(lbl_ref[...], layer)
    row = lax.broadcasted_iota(jnp.int32, (SUBLANES, 1), 0)
    pos = row % seq
    work = []
    for tile, h in ((tile, h) for tile in range(q_ref.shape[0] // SUBLANES) for h in range(heads)):
        cols = slice(h * HEAD_DIM, (h + 1) * HEAD_DIM)
        rows = slice(tile * SUBLANES, (tile + 1) * SUBLANES)
        q, _, k, logf = _hgrn_gates(q_ref[rows, cols], f_ref[rows, cols], lb[:, cols])
        v = v_ref[rows, cols]
        b = logf
        shift = 1
        while shift < seq:
            b = b + jnp.where(pos >= shift, pltpu.roll(b, shift, 0), 0.0)
            shift *= 2
        o = jnp.zeros((SUBLANES, HEAD_DIM), F32)
        for sl in range(SUBLANES):
            lo_t, hi_t = sl, (sl // seq + 1) * seq
            mask = jnp.logical_and(row >= lo_t, row < hi_t)
            z = q * jnp.exp(jnp.where(mask, b - b[sl : sl + 1], NEG_BIG)) * k[sl : sl + 1]
            o = o + jnp.sum(z, axis=-1, keepdims=True) * v[sl : sl + 1]
        per_seq = []
        for n in range(n_seq):
            own = jnp.logical_and(row >= n * seq, row < (n + 1) * seq)
            b_last = b[(n + 1) * seq - 1 : (n + 1) * seq, :]
            qe = jnp.where(own, q * jnp.exp(b), 0.0).astype(BF16)
            kd = jnp.where(own, k * jnp.exp(jnp.where(own, b_last - b, 0.0)), 0.0).astype(BF16)
            per_seq.append((tile * n_seq + n, qe, kd, jnp.exp(b_last)))
        work.append((rows, cols, h, o, v.astype(BF16), per_seq))
    for rows, cols, h, o, _, per_seq in work:
        for i, qe, _, _ in per_seq:
            o = o + _dot(qe, s_in_ref[0, i, h].astype(BF16))
        o_ref[rows, cols] = o
    for _, _, h, _, v16, per_seq in work:
        for i, _, kd, e_last in per_seq:
            s_out_ref[0, i, h] = _row_to_col(e_last) * s_in_ref[0, i, h] + _dot_tn(kd, v16)


def _hgrn_sample(proj, lb_logits, state, *, layer, row0, batch, seq, w_a, tiles=8):
    heads = w_a // HEAD_DIM
    rows = tiles * SUBLANES
    assert SUBLANES % seq == 0 and row0 % rows == 0 and (batch * seq) % rows == 0
    n_seq = rows // seq
    t0 = row0 // rows
    col = lambda c: pl.BlockSpec((rows, w_a), lambda p: (t0 + p, c))
    st = pl.BlockSpec((1, n_seq, heads, HEAD_DIM, HEAD_DIM), lambda p: (layer, p, 0, 0, 0))
    st_out = pl.BlockSpec((1, n_seq, heads, HEAD_DIM, HEAD_DIM), lambda p: (0, p, 0, 0, 0))
    return pl.pallas_call(
        functools.partial(_hgrn_sample_kernel, layer=layer, heads=heads, seq=seq),
        grid=(batch // n_seq,),
        in_specs=[col(0), col(1), col(2), pl.BlockSpec(lb_logits.shape, lambda p: (0, 0)), st],
        out_specs=[pl.BlockSpec((rows, w_a), lambda p: (p, 0)), st_out],
        out_shape=[
            jax.ShapeDtypeStruct((batch * seq, w_a), F32),
            jax.ShapeDtypeStruct((1, batch, heads, HEAD_DIM, HEAD_DIM), F32),
        ],
        compiler_params=_params("arbitrary"),
        name="hgrn_sample",
    )(proj, proj, proj, lb_logits, state)


def _s5_weights_kernel(are_ref, aim_ref, ls_ref, bre_ref, bim_ref, cre_ref, cim_ref, rep_ref, tile_ref, same_ref,
                       lre_ref, lim_ref, bbre_ref, bbim_ref, ccre_ref, ccim_ref):
    a_re = are_ref[...]
    a_im = aim_ref[...]
    dt = jnp.exp(ls_ref[...])
    mag = jnp.exp(a_re * dt)
    lam_re = mag * jnp.cos(a_im * dt)
    lam_im = mag * jnp.sin(a_im * dt)
    den = a_re * a_re + a_im * a_im
    nr, ni = lam_re - 1.0, lam_im
    r_re = (nr * a_re + ni * a_im) / den
    r_im = (ni * a_re - nr * a_im) / den
    lre_ref[...] = lam_re
    lim_ref[...] = lam_im
    b_re = bre_ref[...]
    b_im = bim_ref[...]
    same = same_ref[...]
    bbre_ref[0] = (_dot(rep_ref[...], (r_re * b_re - r_im * b_im).astype(BF16)) * same).astype(BF16)
    bbim_ref[0] = (_dot(rep_ref[...], (r_re * b_im + r_im * b_re).astype(BF16)) * same).astype(BF16)
    ccre_ref[0] = (_dot(cre_ref[...].astype(BF16), tile_ref[...]) * same).astype(BF16)
    ccim_ref[0] = (_dot(cim_ref[...].astype(BF16), tile_ref[...]) * same).astype(BF16)


def _s5_weights(a_re, a_im, log_step, b_re, b_im, c_re, c_im):
    g, p = a_re.shape
    c = b_re.shape[-1]
    gt = MXU_DIM // c
    nj, kc, kp = g // gt, gt * c, gt * p
    flat = lambda x: x.reshape(1, g * p)
    chan = lambda x: jnp.transpose(x, (2, 0, 1)).reshape(c, g * p)
    row_g, lane_g = np.arange(kc)[:, None] // c, np.arange(kp)[None, :] // p
    rep = jnp.asarray(np.arange(kc)[:, None] % c == np.arange(c)[None, :], BF16)
    tile = jnp.asarray(np.arange(p)[:, None] == np.arange(kp)[None, :] % p, BF16)
    same = jnp.asarray(row_g == lane_g, F32)
    lanes = lambda rows: pl.BlockSpec((rows, kp), lambda j: (0, j))
    const = lambda x: pl.BlockSpec(x.shape, lambda j: (0, 0))
    tiles = pl.BlockSpec((1, kc, kp), lambda j: (j, 0, 0))
    tiles_shape = jax.ShapeDtypeStruct((nj, kc, kp), BF16)
    lam_shape = jax.ShapeDtypeStruct((1, g * p), F32)
    lam_re, lam_im, *mats = pl.pallas_call(
        _s5_weights_kernel,
        grid=(nj,),
        in_specs=[lanes(1), lanes(1), lanes(1), lanes(c), lanes(c),
                  pl.BlockSpec((kc, p), lambda j: (j, 0)), pl.BlockSpec((kc, p), lambda j: (j, 0)),
                  const(rep), const(tile), const(same)],
        out_specs=[lanes(1), lanes(1), tiles, tiles, tiles, tiles],
        out_shape=[lam_shape, lam_shape] + [tiles_shape] * 4,
        compiler_params=_params("arbitrary"),
        name="s5_weights",
    )(flat(a_re), flat(a_im), flat(jnp.repeat(log_step, p)), chan(b_re), chan(b_im),
      c_re.reshape(g * c, p), c_im.reshape(g * c, p), rep, tile, same)
    return (lam_re.reshape(-1, 1, LANES), lam_im.reshape(-1, 1, LANES), *mats)


def _s5_lanes(x_ref, lo, hi):
    if len(x_ref.shape) == 2:
        return x_ref[:, lo:hi]
    return jnp.concatenate([x_ref[m] for m in range(lo // LANES, hi // LANES)], axis=1)


def _s5_in_tile(ub, j, bre_ref, bim_ref, xr_ref, xi_ref):
    _, kc, kp = bre_ref.shape
    uj = ub[:, j * kc : (j + 1) * kc]
    for x_ref, w_ref in ((xr_ref, bre_ref), (xi_ref, bim_ref)):
        bu = _dot(uj, w_ref[j])
        if len(x_ref.shape) == 2:
            x_ref[:, j * kp : (j + 1) * kp] = bu
        else:
            for m in range(kp // LANES):
                x_ref[j * kp // LANES + m] = bu[:, m * LANES : (m + 1) * LANES]


def _s5_in(ub, bre_ref, bim_ref, xr_ref, xi_ref):
    for j in range(bre_ref.shape[0]):
        _s5_in_tile(ub, j, bre_ref, bim_ref, xr_ref, xi_ref)


def _s5_out_tile(j, xr_ref, xi_ref, cre_ref, cim_ref):
    kp = cre_ref.shape[2]
    xr = _s5_lanes(xr_ref, j * kp, (j + 1) * kp).astype(BF16)
    xi = _s5_lanes(xi_ref, j * kp, (j + 1) * kp).astype(BF16)
    return _dot_nt(xr, cre_ref[j]) - _dot_nt(xi, cim_ref[j])


def _s5_out(xr_ref, xi_ref, cre_ref, cim_ref):
    return [_s5_out_tile(j, xr_ref, xi_ref, cre_ref, cim_ref) for j in range(cre_ref.shape[0])]


def _s5_step(lr, li, xr, xi, bur, bui):
    return lr * xr - li * xi + bur, lr * xi + li * xr + bui


def _s5_lambda(lre_ref, lim_ref, lo, hi, rows):
    cat = lambda ref: jnp.concatenate([ref[m] for m in range(lo // LANES, hi // LANES)], axis=1)
    return jnp.broadcast_to(cat(lre_ref), (rows, hi - lo)), jnp.broadcast_to(cat(lim_ref), (rows, hi - lo))


def _to_time_major(x):
    n, ts, w = x.shape
    return jnp.swapaxes(x, 0, 1).reshape(ts * n, w)


def _from_time_major(x, n):
    rows, w = x.shape
    return jnp.swapaxes(x.reshape(rows // n, n, w), 0, 1)


def _s5_prompt_kernel(*refs, nv):
    u_refs = refs[:nv]
    (lre_ref, lim_ref, bre_ref, bim_ref, cre_ref, cim_ref, d_ref,
     y_ref, xr_out_ref, xi_out_ref, xr_ref, xi_ref, sr_ref, si_ref) = refs[nv:]
    ts = u_refs[0].shape[0]
    step = pl.program_id(0)

    @pl.when(step == 0)
    def _():
        sr_ref[...] = jnp.zeros_like(sr_ref)
        si_ref[...] = jnp.zeros_like(si_ref)

    u = _to_time_major(jnp.stack([r[...] for r in u_refs], axis=0))
    ub = u.astype(BF16)
    nj, kc, kp = bre_ref.shape
    d = d_ref[...]

    def project(j):
        cols = slice(j * kc, (j + 1) * kc)
        y = _s5_out_tile(j, xr_ref, xi_ref, cre_ref, cim_ref)
        y_ref[:, :, cols] = _from_time_major(y + d[:, cols] * u[:, cols], nv)

    def b_piece(j, n, x_ref, w_ref):
        cols = slice(n * MXU_DIM, (n + 1) * MXU_DIM)
        x_ref[:, j * kp + n * MXU_DIM : j * kp + (n + 1) * MXU_DIM] = _dot(ub[:, j * kc : (j + 1) * kc], w_ref[j][:, cols])

    def scan_piece(j, state, r0, r1):
        ls = slice(j * kp, (j + 1) * kp)
        lr, li, xr, xi = state
        for r in range(r0, r1):
            rows = slice(r * nv, (r + 1) * nv)
            xr, xi = _s5_step(lr, li, xr, xi, xr_ref[rows, ls], xi_ref[rows, ls])
            xr_ref[rows, ls] = xr
            xi_ref[rows, ls] = xi
        return lr, li, xr, xi

    pieces = [(n, x_ref, w_ref) for n in range(kp // MXU_DIM) for x_ref, w_ref in ((xr_ref, bre_ref), (xi_ref, bim_ref))]
    per = ts // len(pieces)
    assert per * len(pieces) == ts
    for s in range(nj + 2):
        state = None
        if 0 <= s - 1 < nj:
            ls = slice((s - 1) * kp, s * kp)
            state = (*_s5_lambda(lre_ref, lim_ref, (s - 1) * kp, s * kp, nv), sr_ref[:, ls], si_ref[:, ls])
        for p, (n, x_ref, w_ref) in enumerate(pieces):
            if s < nj:
                b_piece(s, n, x_ref, w_ref)
            if state is not None:
                state = scan_piece(s - 1, state, p * per, (p + 1) * per)
        if state is not None:
            sr_ref[:, ls] = state[2]
            si_ref[:, ls] = state[3]
        if 0 <= s - 2 < nj:
            project(s - 2)

    @pl.when(step == pl.num_programs(0) - 1)
    def _():
        xr_out_ref[...] = sr_ref[...]
        xi_out_ref[...] = si_ref[...]


def _s5_carry_kernel(y_ref, fr_ref, fi_ref, lre_ref, lim_ref, cre_ref, cim_ref,
                     o_ref, xr_out_ref, xi_out_ref, zr_ref, zi_ref, sr_ref, si_ref, *, half_len, lane_block):
    nrow, gp = fr_ref.shape
    tf = y_ref.shape[2]
    step = pl.program_id(0)
    odd = lax.broadcasted_iota(jnp.int32, (nrow, 1), 0) % 2 == 1

    @pl.when(step == 0)
    def _():
        for lo in range(0, gp, lane_block):
            ls = slice(lo, lo + lane_block)
            pr, pi = _s5_lambda(lre_ref, lim_ref, lo, lo + lane_block, nrow)
            n = 1
            while n < half_len:
                pr, pi = pr * pr - pi * pi, 2.0 * pr * pi
                n *= 2
            fr, fi = fr_ref[:, ls], fi_ref[:, ls]
            fr = jnp.where(odd, pltpu.roll(fr, 1, 0), fr)
            fi = jnp.where(odd, pltpu.roll(fi, 1, 0), fi)
            sr_ref[:, ls] = jnp.where(odd, pr * fr - pi * fi, fr)
            si_ref[:, ls] = jnp.where(odd, pr * fi + pi * fr, fi)

    nj, kc, kp = cre_ref.shape
    nseq = y_ref.shape[0]
    for s in range(nj + 1):
        if s < nj:
            ls = slice(s * kp, (s + 1) * kp)
            lr, li = _s5_lambda(lre_ref, lim_ref, s * kp, (s + 1) * kp, nrow)
            zr, zi = sr_ref[:, ls], si_ref[:, ls]
            for r in range(tf):
                rows = slice(r * nrow, (r + 1) * nrow)
                zr, zi = lr * zr - li * zi, lr * zi + li * zr
                zr_ref[rows, ls] = zr
                zi_ref[rows, ls] = zi
            sr_ref[:, ls] = zr
            si_ref[:, ls] = zi
        if s > 0:
            cols = slice((s - 1) * kc, s * kc)
            y = _from_time_major(_s5_out_tile(s - 1, zr_ref, zi_ref, cre_ref, cim_ref), nrow)
            o_ref[:, :, :, cols] = y_ref[:, :, :, cols] + y.reshape(nseq, nrow // nseq, tf, kc)

    @pl.when(step == pl.num_programs(0) - 1)
    def _():
        xr_out_ref[...] = fr_ref[...] + sr_ref[...]
        xi_out_ref[...] = fi_ref[...] + si_ref[...]


def _s5_weights_specs(ws):
    zero = lambda n: (lambda *_: (0,) * n)
    return [pl.BlockSpec(w.shape, zero(w.ndim)) for w in ws]


def _s5_prompt(proj, ws, *, nb, seq, ucol, w_b, ts=64, tf=64, lane_block=1024):
    lam_re, lam_im, bre, bim, cre, cim, d = ws
    gp = lam_re.shape[0] * LANES
    nv = SUBLANES
    assert nv == 2 * nb, "two pieces per sequence"
    plen, half_len = seq // 2, seq // 4
    assert plen % ts == 0 and half_len % tf == 0 and gp % lane_block == 0
    assert half_len & (half_len - 1) == 0, "repeated squaring needs a power of two"
    nblk = plen // ts
    u_spec = lambda n: pl.BlockSpec((ts, w_b), lambda s: (n * nblk + s, ucol))
    state = pl.BlockSpec((nv, gp), lambda s: (0, 0))
    state_shape = jax.ShapeDtypeStruct((nv, gp), F32)
    scratch = lambda steps: [pltpu.VMEM((nv * steps, gp), F32)] * 2 + [pltpu.VMEM((nv, gp), F32)] * 2
    y, fr, fi = pl.pallas_call(
        functools.partial(_s5_prompt_kernel, nv=nv),
        grid=(nblk,),
        in_specs=[u_spec(n) for n in range(nv)] + _s5_weights_specs(ws),
        out_specs=[pl.BlockSpec((nv, ts, w_b), lambda s: (0, s, 0)), state, state],
        out_shape=[jax.ShapeDtypeStruct((nv, plen, w_b), F32), state_shape, state_shape],
        scratch_shapes=scratch(ts),
        compiler_params=_params("arbitrary"),
        name="s5_prompt",
    )(*([proj] * nv), *ws)

    y5 = y.reshape(nb, 2, 2, half_len, w_b)
    y_spec = pl.BlockSpec((nb, None, 2, tf, w_b), lambda s: (0, 1, 0, s, 0))
    y5, xr, xi = pl.pallas_call(
        functools.partial(_s5_carry_kernel, half_len=half_len, lane_block=lane_block),
        grid=(half_len // tf,),
        in_specs=[y_spec, state, state] + _s5_weights_specs((lam_re, lam_im, cre, cim)),
        out_specs=[y_spec, state, state],
        out_shape=[jax.ShapeDtypeStruct(y5.shape, F32), state_shape, state_shape],
        scratch_shapes=scratch(tf),
        input_output_aliases={0: 0},
        compiler_params=_params("arbitrary"),
        name="s5_prompt_carry",
    )(y5, fr, fi, lam_re, lam_im, cre, cim)
    last = lambda x: x.reshape(nb, 2, gp)[:, 1]
    return y5.reshape(nb * seq, w_b), last(xr), last(xi)


def _s5_sample_kernel(u_ref, x0r_ref, x0i_ref, lre_ref, lim_ref, bre_ref, bim_ref, cre_ref, cim_ref, d_ref,
                      y_ref, xr_out_ref, xi_out_ref, xr_ref, xi_ref, *, seq):
    nb = x0r_ref.shape[0]
    u = u_ref[...]
    _s5_in(u.astype(BF16), bre_ref, bim_ref, xr_ref, xi_ref)
    for m in range(xr_ref.shape[0]):
        ls = slice(m * LANES, (m + 1) * LANES)
        lr, li = lre_ref[m], lim_ref[m]
        xr, xi = x0r_ref[:, ls], x0i_ref[:, ls]
        for r in range(seq):
            rows = pl.ds(r, nb, stride=seq)
            xr, xi = _s5_step(lr, li, xr, xi, xr_ref[m, rows, :], xi_ref[m, rows, :])
            xr_ref[m, rows, :] = xr
            xi_ref[m, rows, :] = xi
        xr_out_ref[:, ls] = xr
        xi_out_ref[:, ls] = xi
    kc = cre_ref.shape[1]
    d = d_ref[...]
    for j, y in enumerate(_s5_out(xr_ref, xi_ref, cre_ref, cim_ref)):
        cols = slice(j * kc, (j + 1) * kc)
        y_ref[:, cols] = y + d[:, cols] * u[:, cols]


def _s5_sample(proj, x0_re, x0_im, ws, *, row0, ucol, seq, w_b, nb=64):
    batch, gp = x0_re.shape
    n_slabs = gp // LANES
    assert batch % nb == 0 and row0 % (nb * seq) == 0
    t0 = row0 // (nb * seq)
    return pl.pallas_call(
        functools.partial(_s5_sample_kernel, seq=seq),
        grid=(batch // nb,),
        in_specs=[
            pl.BlockSpec((nb * seq, w_b), lambda p: (t0 + p, ucol)),
            pl.BlockSpec((nb, gp), lambda p: (p, 0)),
            pl.BlockSpec((nb, gp), lambda p: (p, 0)),
        ] + _s5_weights_specs(ws),
        out_specs=[
            pl.BlockSpec((nb * seq, w_b), lambda p: (p, 0)),
            pl.BlockSpec((nb, gp), lambda p: (p, 0)),
            pl.BlockSpec((nb, gp), lambda p: (p, 0)),
        ],
        out_shape=[
            jax.ShapeDtypeStruct((batch * seq, w_b), F32),
            jax.ShapeDtypeStruct((batch, gp), F32),
            jax.ShapeDtypeStruct((batch, gp), F32),
        ],
        scratch_shapes=[pltpu.VMEM((n_slabs, nb * seq, LANES), F32), pltpu.VMEM((n_slabs, nb * seq, LANES), F32)],
        compiler_params=_params("arbitrary"),
        name="s5_sample",
    )(proj, x0_re, x0_im, *ws)


def _mix_kernel(o_ref, gate_ref, y_ref, x_ref, ng_ref, gluw_ref, glub_ref, wout_ref, out_ref, *rest,
                n_tiles, n_cols):
    *copies, a0_ref, a1_ref = rest
    glu16_ref, wout16_ref = copies or (None, None)
    i = pl.program_id(0)
    j = pl.program_id(1)
    rows_per_step = a0_ref.shape[0] // n_cols

    if glu16_ref is not None:
        @pl.when(jnp.logical_and(i == 0, j == 0))
        def _():
            _bf16_weights(gluw_ref, glu16_ref)
        gluw_ref = glu16_ref

    def step(a_gate, a_proj):
        w_a = o_ref.shape[1]
        tm = rows_per_step * n_cols
        start = pl.multiple_of(j * rows_per_step, rows_per_step)
        dst = pl.ds(start, rows_per_step)
        gated = {}

        def gelu_y():
            gated["y"] = jax.nn.gelu(y_ref[...])

        def glu_dot():
            gated["z"] = _dot(gated["y"].astype(BF16), gluw_ref[...]) + glub_ref[...]

        def glu_out():
            a_gate[dst, w_a:] = (gated["y"] * jax.nn.sigmoid(gated["z"])).astype(BF16)

        def head_out():
            o = o_ref[...]
            var = jnp.mean(o * o, axis=-1, keepdims=True)
            gate = gate_ref[...]
            a_gate[dst, :w_a] = (o * lax.rsqrt(var + EPS) * ng_ref[...] * (gate * jax.nn.sigmoid(gate))).astype(BF16)

        chunks = []
        if a_proj is not None:
            half = tm // 2
            weights = {}

            def chunk(p, h):
                cols = slice(p * MXU_DIM, (p + 1) * MXU_DIM)
                rows = slice(h * half, (h + 1) * half)
                if p not in weights:
                    weights[p] = _bf16_weights(wout_ref, wout16_ref, (slice(None), cols), first_row=1)
                out_ref[rows, cols] = x_ref[rows, cols] + _dot(a_proj[rows, :], weights[p])

            chunks = [functools.partial(chunk, p, h) for p in range(out_ref.shape[1] // MXU_DIM) for h in range(2)]
        phases = [[gelu_y], [glu_dot, head_out], [glu_out]] if a_gate is not None else []
        for k in range(max(len(phases), len(chunks))):
            for phase in phases[k] if k < len(phases) else []:
                phase()
            if k < len(chunks):
                chunks[k]()

    _by_parity(i, n_tiles, step, a0_ref, a1_ref)


def _mix(o, proj, y, x, norm_g, glu_w, glu_b, w_out, *, gate_col, name, tn=512):
    m, w_a = o.shape
    w_b = y.shape[1]
    d = w_out.shape[1]
    tm = _row_tile(m)
    nj = d // tn
    assert m % tm == 0 and d % tn == 0 and tm % (nj * SUBLANES) == 0
    n = m // tm
    nxt, cur, col, row_slice = _lead_in_maps(n, nj)
    const = lambda i, j: (0, 0)
    cur_tile = lambda i, j: (cur(i), col(i, j))
    glu_spec = ((w_b, w_b), const)
    wout_spec = ((w_a + w_b, tn), lambda i, j: (0, col(i, j)))
    assert glu_w.dtype == w_out.dtype
    copy_specs, copy_shapes = (a + b for a, b in zip(_weight_copy(glu_w, *glu_spec, (0, 0)),
                                                     _weight_copy(w_out, *wout_spec, (0, nj - 1), first_row=1)))
    return pl.pallas_call(
        functools.partial(_mix_kernel, n_tiles=n, n_cols=nj),
        grid=(n + 1, nj),
        in_specs=[
            pl.BlockSpec((tm // nj, w_a), lambda i, j: (row_slice(i, j), 0)),
            pl.BlockSpec((tm // nj, w_a), lambda i, j: (row_slice(i, j), gate_col)),
            pl.BlockSpec((tm // nj, w_b), lambda i, j: (row_slice(i, j), 0)),
            pl.BlockSpec((tm, tn), cur_tile),
            pl.BlockSpec((1, w_a), const),
            pl.BlockSpec(*glu_spec),
            pl.BlockSpec((1, w_b), const),
            pl.BlockSpec(*wout_spec),
        ],
        out_specs=[pl.BlockSpec((tm, tn), cur_tile)] + copy_specs,
        out_shape=[jax.ShapeDtypeStruct((m, d), F32)] + copy_shapes,
        scratch_shapes=[pltpu.VMEM((tm, w_a + w_b), BF16)] * 2,
        compiler_params=_params("arbitrary", "arbitrary"),
        name=name,
    )(o, proj, y, x, norm_g.reshape(1, w_a), glu_w, glu_b.reshape(1, w_b), w_out)


def _mlp_kernel(x_ref, g_ref, up_ref, down_ref, gf_ref, o_ref, *rest, rows, final_norm):
    *copies, h_ref = rest
    up16_ref, down16_ref = copies or (None, None)
    f = pl.program_id(1)

    @pl.when(f == 0)
    def _():
        _rmsnorm_rows(x_ref, g_ref[...], h_ref, rows, copy_ref=o_ref)

    a = jnp.square(jnp.maximum(_dot(h_ref[...], _bf16_weights(up_ref, up16_ref)), 0.0)).astype(BF16)
    o_ref[...] += _dot(a, _bf16_weights(down_ref, down16_ref))

    if final_norm:
        @pl.when(f == pl.num_programs(1) - 1)
        def _():
            _rmsnorm_rows(o_ref, gf_ref[...], o_ref, rows)


def _mlp(x, g, up, down, gf, *, final_norm, name, tf=512):
    m, d = x.shape
    dff = up.shape[1]
    tm = _row_tile(m)
    assert m % tm == 0 and dff % tf == 0 and up.dtype == down.dtype
    up_spec = ((d, tf), lambda i, f: (0, f))
    down_spec = ((tf, d), lambda i, f: (f, 0))
    nf = dff // tf
    copy_specs, copy_shapes = (a + b for a, b in zip(_weight_copy(up, *up_spec, (0, nf - 1)),
                                                     _weight_copy(down, *down_spec, (nf - 1, 0))))
    return pl.pallas_call(
        functools.partial(_mlp_kernel, rows=tm, final_norm=final_norm),
        grid=(m // tm, dff // tf),
        in_specs=[
            pl.BlockSpec((tm, d), lambda i, f: (i, 0)),
            pl.BlockSpec((1, d), lambda i, f: (0, 0)),
            pl.BlockSpec(*up_spec),
            pl.BlockSpec(*down_spec),
            pl.BlockSpec((1, d), lambda i, f: (0, 0)),
        ],
        out_specs=[pl.BlockSpec((tm, d), lambda i, f: (i, 0))] + copy_specs,
        out_shape=[jax.ShapeDtypeStruct((m, d), F32)] + copy_shapes,
        scratch_shapes=[pltpu.VMEM((tm, d), BF16)],
        compiler_params=_params("arbitrary", "arbitrary"),
        name=name,
    )(x, g.reshape(1, d), up, down, gf.reshape(1, d))


def kernel(x_prompt, x_sample, state_hgrn, state_s5_re, state_s5_im, w_in, w_out, norm1_g, norm2_g, hgrn_lb_logits, hgrn_norm_g, s5_a_re, s5_a_im, s5_b_re, s5_b_im, s5_c_re, s5_c_im, s5_d, s5_log_step, glu_w, glu_b, mlp_up, mlp_down, final_norm_g):
    bp, seq, d = x_prompt.shape
    bs, dseq, _ = x_sample.shape
    depth = w_in.shape[0]
    w_a = hgrn_norm_g.shape[1]
    w_b = s5_d.shape[1]
    g_b, p_state = s5_a_re.shape[1], s5_a_re.shape[2]
    mp, ms = bp * seq, bs * dseq
    assert w_in.shape[2] == 4 * w_a + w_b and w_a == w_b and g_b * S5_GROUP == w_b
    gate_col, ucol = 3, (4 * w_a) // w_b

    xp = x_prompt.reshape(mp, d)
    xs = x_sample.reshape(ms, d)
    lb_logits = hgrn_lb_logits.astype(F32)
    new_states = []
    for l in range(depth):
        proj_s, w_in16 = _inproj(xs, norm1_g[l], w_in[l].astype(F32), name="inproj_sample")
        proj_p, = _inproj(xp, norm1_g[l], w_in16, name="inproj_prompt")

        o_p, sh_p, (up16, down16) = _hgrn_prompt(proj_p, lb_logits, layer=l, batch=bp, seq=seq, w_a=w_a,
                                                 round_weights=[(mlp_up[l].astype(F32), 1), (mlp_down[l].astype(F32), 0)])
        o_s, sh_s = _hgrn_sample(proj_s, lb_logits, state_hgrn, layer=l, row0=0, batch=bs, seq=dseq, w_a=w_a)

        ws = (*_s5_weights(*(w[l].astype(F32) for w in (s5_a_re, s5_a_im, s5_log_step, s5_b_re, s5_b_im,
                                                        s5_c_re, s5_c_im))),
              s5_d[l].astype(F32).reshape(1, w_b))
        y_p, sr_p, si_p = _s5_prompt(proj_p, ws, nb=bp, seq=seq, ucol=ucol, w_b=w_b)
        y_s, sr_s, si_s = _s5_sample(
            proj_s, state_s5_re[l].reshape(bs, g_b * p_state).astype(F32), state_s5_im[l].reshape(bs, g_b * p_state).astype(F32),
            ws, row0=0, ucol=ucol, seq=dseq, w_b=w_b)

        last = l == depth - 1
        x1_s, glu16, w_out16 = _mix(o_s, proj_s, y_s, xs, hgrn_norm_g[l], glu_w[l].astype(F32), glu_b[l],
                                    w_out[l].astype(F32), gate_col=gate_col, name="mix_sample")
        xs, = _mlp(x1_s, norm2_g[l], up16, down16, final_norm_g, final_norm=last, name="mlp_sample")
        x1_p, = _mix(o_p, proj_p, y_p, xp, hgrn_norm_g[l], glu16, glu_b[l], w_out16, gate_col=gate_col,
                     name="mix_prompt")
        xp, = _mlp(x1_p, norm2_g[l], up16, down16, final_norm_g, final_norm=last, name="mlp_prompt")
        new_states.append((sh_p, sr_p.reshape(bp, g_b, p_state), si_p.reshape(bp, g_b, p_state),
                           sh_s[0], sr_s.reshape(bs, g_b, p_state), si_s.reshape(bs, g_b, p_state)))

    y_prompt = xp.reshape(bp, seq, d).astype(x_prompt.dtype)
    y_sample = xs.reshape(bs, dseq, d).astype(x_sample.dtype)
    stacked = [jnp.stack([st[i] for st in new_states]) for i in range(6)]
    return (y_prompt, y_sample, *stacked)
```
